```python
import functools
import math
import jax
import jax.numpy as jnp
from jax import lax
import numpy as np

D_MODEL = 1024
BATCH = 8
SEQ = 2048
DEPTH = 2
DEC_BATCH = 32
DEC_SEQ = 4
PAST_LEN = 8192
PAGE_SIZE = 128

CONV_CH = 512
CONV_W = 31
GDN_HEADS = 4
GDN_DK = 128
GDN_DV = 128
GDN_CONV_W = 4
GDN_CHUNK = 64
NSA_HEADS = 8
NSA_KV = 2
NSA_GQ = NSA_HEADS // NSA_KV
NSA_DH = 64
L_CMP = 32
L_SEL = 64
N_SEL = 16
WINDOW = 512
Q_BLOCK = 128
FORCE_BONUS = 1e4
N_MEM = 256
X_HEADS = 4
X_DH = D_MODEL // X_HEADS
D_FF = 4 * D_MODEL
N_BUCKETS = 32
MAX_DIST = 128
EPS = 1e-6
F32 = jnp.float32

A_COLS = 2 * CONV_CH
GDN_QK = GDN_HEADS * GDN_DK
GDN_V = GDN_HEADS * GDN_DV
GDN_QKV = 2 * GDN_QK + GDN_V
B_COLS = GDN_QKV + GDN_V + 2 * GDN_HEADS
NSA_Q = NSA_HEADS * NSA_DH
NSA_KVW = NSA_KV * NSA_DH
C_COLS = NSA_Q + 6 * NSA_KVW + 3 * NSA_HEADS
G_COLS = 3 * D_MODEL
N_IN = A_COLS + B_COLS + C_COLS + G_COLS

kernel_name = 'hybrid_conv_gdn_nsa_decoder_step'


def rmsnorm(x, g):
    xf = x.astype(F32)
    y = xf * lax.rsqrt(jnp.mean(xf * xf, axis=-1, keepdims=True) + EPS)
    return (y * g.astype(F32)).astype(x.dtype)


def layernorm(x, g, b):
    xf = x.astype(F32)
    mu = jnp.mean(xf, axis=-1, keepdims=True)
    var = jnp.mean(jnp.square(xf - mu), axis=-1, keepdims=True)
    return ((xf - mu) * lax.rsqrt(var + EPS) * g.astype(F32) + b.astype(F32)).astype(x.dtype)


def l2norm(x):
    xf = x.astype(F32)
    return xf * lax.rsqrt(jnp.sum(xf * xf, axis=-1, keepdims=True) + EPS)


def masked_softmax(s, mask):
    s = jnp.where(mask, s.astype(F32), -1e30)
    m = jnp.max(s, axis=-1, keepdims=True)
    e = jnp.exp(s - m) * mask
    return e / jnp.maximum(jnp.sum(e, axis=-1, keepdims=True), 1e-30)


def t5_bucket(rel):
    n = jnp.maximum(rel, 0)
    max_exact = N_BUCKETS // 2
    nf = jnp.maximum(n, 1).astype(F32)
    large = max_exact + (jnp.log(nf / max_exact) / math.log(MAX_DIST / max_exact) * (N_BUCKETS - max_exact)).astype(jnp.int32)
    large = jnp.minimum(large, N_BUCKETS - 1)
    return jnp.where(n < max_exact, n, large)


def head_bias(rel_bias, rel):
    b = rel_bias[t5_bucket(rel)].astype(F32)
    b = b.reshape(b.shape[:-1] + (NSA_KV, NSA_GQ))
    return jnp.moveaxis(b, (-2, -1), (-4, -3))


def causal_dwconv(x, buf, w):
    xcat = jnp.concatenate([buf.astype(x.dtype), x], axis=1)
    y = lax.conv_general_dilated(xcat, w.astype(x.dtype)[:, None, :], window_strides=(1,), padding='VALID',
                                 dimension_numbers=('NWC', 'WIO', 'NWC'), feature_group_count=x.shape[-1])
    return y, xcat[:, xcat.shape[1] - (w.shape[0] - 1):]


def conformer_conv(u, buf, w_dw, b_dw, ln_g, ln_b, w_p):
    a, gate = jnp.split(u, 2, axis=-1)
    glu = a * jax.nn.sigmoid(gate)
    y, new_buf = causal_dwconv(glu, buf, w_dw)
    y = jax.nn.silu(layernorm(y + b_dw.astype(y.dtype), ln_g, ln_b))
    return y @ w_p, new_buf


def gdn_chunked(q, k, v, g, beta, s0):
    B, T, H, DK = q.shape
    DV = v.shape[-1]
    C = min(GDN_CHUNK, T)
    N = T // C

    def to_chunks(a):
        a = a.astype(F32).reshape((B, N, C, H) + a.shape[3:])
        return jnp.moveaxis(a, (1, 3), (0, 2))

    qc, kc, vc, gc, bc = map(to_chunks, (q, k, v, g, beta))
    gc = jnp.cumsum(gc, axis=-1)
    i = jnp.arange(C)
    incl = i[:, None] >= i[None, :]
    strict = i[:, None] > i[None, :]
    decay = jnp.exp(jnp.where(incl, gc[..., :, None] - gc[..., None, :], -jnp.inf))
    kb = kc * bc[..., None]
    a_low = jnp.where(strict, jnp.einsum('nbhid,nbhjd->nbhij', kb, kc) * decay, 0.0)
    lhs = a_low + jnp.eye(C, dtype=F32)
    rhs = jnp.concatenate([vc * bc[..., None], kb * jnp.exp(gc)[..., None]], axis=-1)
    sol = lax.linalg.triangular_solve(lhs, rhs, left_side=True, lower=True, unit_diagonal=True)
    v_w, k_cd = sol[..., :DV], sol[..., DV:]
    qk = jnp.where(incl, jnp.einsum('nbhid,nbhjd->nbhij', qc, kc) * decay, 0.0)

    def step(S, xs):
        q_i, k_i, vw_i, kcd_i, qk_i, g_i = xs
        v_new = vw_i - jnp.einsum('bhck,bhkv->bhcv', kcd_i, S)
        o_i = jnp.einsum('bhck,bhkv->bhcv', q_i * jnp.exp(g_i)[..., None], S) + jnp.einsum('bhij,bhjv->bhiv', qk_i, v_new)
        g_end = g_i[..., -1]
        S = S * jnp.exp(g_end)[..., None, None] + jnp.einsum('bhck,bhcv->bhkv', k_i * jnp.exp(g_end[..., None] - g_i)[..., None], v_new)
        return S, o_i

    s_fin, o = lax.scan(step, s0.astype(F32), (qc, kc, v_w, k_cd, qk, gc))
    o = jnp.moveaxis(o, (0, 2), (1, 3)).reshape(B, T, H, DV)
    return o, s_fin


def gated_deltanet(u, qkv_buf, s0, w_conv, a_log, dt_bias, norm_g, w_p):
    B, T, _ = u.shape
    qkv, z, a, b = jnp.split(u, [GDN_QKV, GDN_QKV + GDN_V, GDN_QKV + GDN_V + GDN_HEADS], axis=-1)
    qkv_c, qkv_new = causal_dwconv(qkv, qkv_buf, w_conv)
    qkv_c = jax.nn.silu(qkv_c)
    q, k, v = jnp.split(qkv_c, [GDN_QK, 2 * GDN_QK], axis=-1)
    q = l2norm(q.reshape(B, T, GDN_HEADS, GDN_DK)) * (GDN_DK ** -0.5)
    k = l2norm(k.reshape(B, T, GDN_HEADS, GDN_DK))
    v = v.reshape(B, T, GDN_HEADS, GDN_DV)
    g = -jnp.exp(a_log.astype(F32)) * jax.nn.softplus(a.astype(F32) + dt_bias.astype(F32))
    beta = jax.nn.sigmoid(b.astype(F32))
    o, s_new = gdn_chunked(q, k, v, g, beta, s0)
    o = rmsnorm(o, norm_g) * jax.nn.silu(z.astype(F32)).reshape(B, T, GDN_HEADS, GDN_DV)
    return o.reshape(B, T, GDN_V).astype(u.dtype) @ w_p, qkv_new, s_new


def nsa_compress(kv2, w_pos):
    B, Tk = kv2.shape[:2]
    nc = Tk // L_CMP
    blocks = kv2[:, :nc * L_CMP].reshape(B, nc, L_CMP, 2, NSA_KV, NSA_DH)
    kvc = jnp.einsum('bnlsgd,sl->bnsgd', blocks, w_pos.astype(kv2.dtype))
    end_pos = jnp.arange(nc) * L_CMP + (L_CMP - 1)
    return kvc[:, :, 0], kvc[:, :, 1], end_pos


def nsa_cmp_slc(q, q_pos, kc, vc, c_end, ksb, vsb, rel_bias):
    B, Tq = q.shape[:2]
    scale = NSA_DH ** -0.5
    rel_c = q_pos[:, None] - c_end[None, :]
    s_c = jnp.einsum('bqgrd,bngd->bgrqn', q, kc).astype(F32) * scale + head_bias(rel_bias, rel_c)
    p_c = masked_softmax(s_c, rel_c >= 0)
    o_c = jnp.einsum('bgrqn,bngd->bqgrd', p_c.astype(vc.dtype), vc)
    nsb = ksb.shape[2]
    ratio = L_SEL // L_CMP
    imp = p_c.sum(axis=2)
    nc = imp.shape[-1]
    imp = jnp.pad(imp, ((0, 0), (0, 0), (0, 0), (0, nsb * ratio - nc))).reshape(B, NSA_KV, Tq, nsb, ratio).sum(-1)
    blk = jnp.arange(nsb)[None, :]
    cur = (q_pos // L_SEL)[:, None]
    valid = blk <= cur
    forced = (blk == 0) | (blk == cur) | (blk == cur - 1)
    score = jnp.where(valid, imp + jnp.where(forced, FORCE_BONUS, 0.0), -1.0)
    n_sel = min(N_SEL, nsb)
    _, idx = lax.top_k(score, n_sel)
    gather = jax.vmap(jax.vmap(lambda blocks, ix: blocks[ix]))
    ks = gather(ksb, idx).reshape(B, NSA_KV, Tq, n_sel * L_SEL, NSA_DH)
    vs = gather(vsb, idx).reshape(B, NSA_KV, Tq, n_sel * L_SEL, NSA_DH)
    pos = (idx[..., None] * L_SEL + jnp.arange(L_SEL)).reshape(B, NSA_KV, Tq, n_sel * L_SEL)
    rel_s = q_pos[:, None] - pos
    tb = rel_bias.astype(F32).reshape(N_BUCKETS, NSA_KV, NSA_GQ)
    bias_s = jnp.moveaxis(tb[t5_bucket(rel_s), jnp.arange(NSA_KV)[None, :, None, None]], -1, 2)
    s_s = jnp.einsum('bqgrd,bgqkd->bgrqk', q, ks).astype(F32) * scale + bias_s
    p_s = masked_softmax(s_s, (rel_s >= 0)[:, :, None])
    o_s = jnp.einsum('bgrqk,bgqkd->bqgrd', p_s.astype(vs.dtype), vs)
    return o_c, o_s


def nsa_window(q, q_pos, k, v, k_pos, rel_bias):
    rel = q_pos[:, :, None] - k_pos[:, None, :]
    mask = (rel >= 0) & (rel < WINDOW) & (k_pos[:, None, :] >= 0)
    s = jnp.einsum('bnqgrd,bnkgd->bngrqk', q, k).astype(F32) * (NSA_DH ** -0.5) + head_bias(rel_bias, rel)
    p = masked_softmax(s, mask[:, None, None])
    return jnp.einsum('bngrqk,bnkgd->bnqgrd', p.astype(v.dtype), v)


def nsa_prompt(q, kv, rel_bias, w_pos):
    B, T = q.shape[:2]
    q_pos = jnp.arange(T)
    kc, vc, c_end = nsa_compress(kv[:, :, 0:2], w_pos)
    nsb = T // L_SEL
    ksb = kv[:, :, 2].reshape(B, nsb, L_SEL, NSA_KV, NSA_DH).transpose(0, 3, 1, 2, 4)
    vsb = kv[:, :, 3].reshape(B, nsb, L_SEL, NSA_KV, NSA_DH).transpose(0, 3, 1, 2, 4)
    nqb = T // Q_BLOCK
    qb = jnp.moveaxis(q.reshape(B, nqb, Q_BLOCK, NSA_KV, NSA_GQ, NSA_DH), 1, 0)
    pb = q_pos.reshape(nqb, Q_BLOCK)
    o_c, o_s = lax.map(lambda xs: nsa_cmp_slc(xs[0], xs[1], kc, vc, c_end, ksb, vsb, rel_bias), (qb, pb))
    o_c = jnp.moveaxis(o_c, 0, 1).reshape(B, T, NSA_KV, NSA_GQ, NSA_DH)
    o_s = jnp.moveaxis(o_s, 0, 1).reshape(B, T, NSA_KV, NSA_GQ, NSA_DH)
    n_prev = WINDOW // Q_BLOCK
    kvw = jnp.pad(kv[:, :, 4:6], ((0, 0), (WINDOW, 0), (0, 0), (0, 0), (0, 0)))
    kvw = kvw.reshape(B, nqb + n_prev, Q_BLOCK, 2, NSA_KV, NSA_DH)
    band = jnp.concatenate([kvw[:, j:j + nqb] for j in range(n_prev + 1)], axis=2)
    k_pos = (jnp.arange(nqb)[:, None] - n_prev) * Q_BLOCK + jnp.arange((n_prev + 1) * Q_BLOCK)[None, :]
    o_w = nsa_window(q.reshape(B, nqb, Q_BLOCK, NSA_KV, NSA_GQ, NSA_DH), pb, band[:, :, :, 0], band[:, :, :, 1], k_pos, rel_bias)
    o_w = o_w.reshape(B, T, NSA_KV, NSA_GQ, NSA_DH)
    wb = min(WINDOW, T)
    return o_c, o_s, o_w, kv[:, :, :4], kv[:, T - wb:, 4:6]


def nsa_sample(q, kv, rel_bias, w_pos, past_kv, win_buf):
    B, T = q.shape[:2]
    past = past_kv.shape[1]
    q_pos = past + jnp.arange(T)
    full = jnp.concatenate([past_kv.astype(kv.dtype), kv[:, :, :4]], axis=1)
    Tk = past + T
    kc, vc, c_end = nsa_compress(full[:, :, 0:2], w_pos)
    nsb = -(-Tk // L_SEL)
    sel = jnp.pad(full[:, :, 2:4], ((0, 0), (0, nsb * L_SEL - Tk), (0, 0), (0, 0), (0, 0)))
    sel = sel.reshape(B, nsb, L_SEL, 2, NSA_KV, NSA_DH)
    ksb = sel[:, :, :, 0].transpose(0, 3, 1, 2, 4)
    vsb = sel[:, :, :, 1].transpose(0, 3, 1, 2, 4)
    o_c, o_s = nsa_cmp_slc(q, q_pos, kc, vc, c_end, ksb, vsb, rel_bias)
    wb = win_buf.shape[1]
    wkv = jnp.concatenate([win_buf.astype(kv.dtype), kv[:, :, 4:6]], axis=1)
    k_pos = past - wb + jnp.arange(wb + T)
    o_w = nsa_window(q[:, None], q_pos[None], wkv[:, None, :, 0], wkv[:, None, :, 1], k_pos[None], rel_bias)[:, 0]
    return o_c, o_s, o_w, kv[:, :, :4], wkv[:, T:]


def parallel_mixers(h, p, l, conv_buf, qkv_buf, s0, nsa_attend):
    Bh, Th, _ = h.shape
    u = h @ p['w_in'][l]
    u_a, u_b, u_c, u_g = jnp.split(u, [A_COLS, A_COLS + B_COLS, A_COLS + B_COLS + C_COLS], axis=-1)
    y_a, conv_new = conformer_conv(u_a, conv_buf, p['conv_a_w'][l], p['conv_a_b'][l], p['ln_a_g'][l], p['ln_a_b'][l], p['w_pa'][l])
    y_b, qkv_new, s_new = gated_deltanet(u_b, qkv_buf, s0, p['gdn_conv_w'][l], p['gdn_a_log'][l], p['gdn_dt_bias'][l],
                                         p['gdn_norm_g'][l], p['w_pb'][l])
    q, kv, gl = jnp.split(u_c, [NSA_Q, NSA_Q + 6 * NSA_KVW], axis=-1)
    q = q.reshape(Bh, Th, NSA_KV, NSA_GQ, NSA_DH)
    kv = kv.reshape(Bh, Th, 6, NSA_KV, NSA_DH)
    gt = jax.nn.sigmoid(gl).reshape(Bh, Th, 3, NSA_KV, NSA_GQ, 1)
    o_c, o_s, o_w, rows, win = nsa_attend(q, kv, p['rel_bias'], p['nsa_cmp_w'][l])
    o = gt[:, :, 0] * o_c + gt[:, :, 1] * o_s + gt[:, :, 2] * o_w
    y_c = o.reshape(Bh, Th, NSA_Q) @ p['w_pc'][l]
    g_a, g_b, g_c = jnp.split(jax.nn.sigmoid(u_g), 3, axis=-1)
    y = (g_a * y_a + g_b * y_b + g_c * y_c) @ p['w_o'][l]
    return y, conv_new, qkv_new, s_new, rows, win


def cross_attn(h, mem_k, mem_v, w_q, w_o):
    B, T, _ = h.shape
    q = (h @ w_q).reshape(B, T, X_HEADS, X_DH)
    s = jnp.einsum('bqhd,bmhd->bhqm', q, mem_k).astype(F32) * (X_DH ** -0.5)
    pr = jax.nn.softmax(s, axis=-1)
    o = jnp.einsum('bhqm,bmhd->bqhd', pr.astype(mem_v.dtype), mem_v).reshape(B, T, D_MODEL)
    return o @ w_o


def sq_relu_mlp(h, w1, w2):
    return jnp.square(jax.nn.relu(h @ w1)) @ w2


def setup_inputs(seed: int = 0) -> dict:
    key = jax.random.key(seed)
    ks = iter(jax.random.split(key, 48))

    def nrm(shape, scale):
        return jax.random.normal(next(ks), shape, F32) * scale

    def gain(shape):
        return 1.0 + nrm(shape, 0.1)

    n_pages = PAST_LEN // PAGE_SIZE
    n_used = DEC_BATCH * n_pages
    n_pool = n_used + max(1, n_used // 4)
    w_buf = min(WINDOW, PAST_LEN)
    x_prompt = nrm((BATCH, SEQ, D_MODEL), 1.0)
    x_sample = nrm((DEC_BATCH, DEC_SEQ, D_MODEL), 1.0)
    cache_nsa_kv = nrm((DEPTH, n_pool, PAGE_SIZE, 4, NSA_KV, NSA_DH), 1.0)
    cache_win_kv = nrm((DEPTH, DEC_BATCH, w_buf, 2, NSA_KV, NSA_DH), 1.0)
    state_conv_a = nrm((DEPTH, DEC_BATCH, CONV_W - 1, CONV_CH), 0.5)
    state_conv_qkv = nrm((DEPTH, DEC_BATCH, GDN_CONV_W - 1, GDN_QKV), 1.0)
    state_gdn = nrm((DEPTH, DEC_BATCH, GDN_HEADS, GDN_DK, GDN_DV), 0.1)
    cache_mem_kv = nrm((DEPTH, DEC_BATCH, N_MEM, 2, X_HEADS, X_DH), 1.0)
    page_table = jax.random.permutation(next(ks), n_pool)[:n_used].reshape(DEC_BATCH, n_pages).astype(jnp.int32)
    mem_prompt = nrm((BATCH, N_MEM, D_MODEL), 1.0)
    dt = jax.random.uniform(next(ks), (DEPTH, GDN_HEADS), F32, 0.001, 0.1)
    return {
        'x_prompt': x_prompt,
        'x_sample': x_sample,
        'cache_nsa_kv': cache_nsa_kv,
        'cache_win_kv': cache_win_kv,
        'state_conv_a': state_conv_a,
        'state_conv_qkv': state_conv_qkv,
        'state_gdn': state_gdn,
        'cache_mem_kv': cache_mem_kv,
        'page_table': page_table,
        'mem_prompt': mem_prompt,
        'rel_bias': nrm((N_BUCKETS, NSA_HEADS), 0.5),
        'norm_mix': gain((DEPTH, D_MODEL)),
        'w_in': nrm((DEPTH, D_MODEL, N_IN), D_MODEL ** -0.5),
        'conv_a_w': nrm((DEPTH, CONV_W, CONV_CH), CONV_W ** -0.5),
        'conv_a_b': nrm((DEPTH, CONV_CH), 0.02),
        'ln_a_g': gain((DEPTH, CONV_CH)),
        'ln_a_b': nrm((DEPTH, CONV_CH), 0.02),
        'w_pa': nrm((DEPTH, CONV_CH, D_MODEL), CONV_CH ** -0.5),
        'gdn_conv_w': nrm((DEPTH, GDN_CONV_W, GDN_QKV), GDN_CONV_W ** -0.5),
        'gdn_a_log': jnp.log(jax.random.uniform(next(ks), (DEPTH, GDN_HEADS), F32, 1.0, 16.0)),
        'gdn_dt_bias': jnp.log(jnp.expm1(dt)),
        'gdn_norm_g': gain((DEPTH, GDN_DV)),
        'w_pb': nrm((DEPTH, GDN_V, D_MODEL), GDN_V ** -0.5),
        'nsa_cmp_w': (1.0 + nrm((DEPTH, 2, L_CMP), 0.1)) / L_CMP,
        'w_pc': nrm((DEPTH, NSA_Q, D_MODEL), NSA_Q ** -0.5),
        'w_o': nrm((DEPTH, D_MODEL, D_MODEL), D_MODEL ** -0.5),
        'norm_x': gain((DEPTH, D_MODEL)),
        'w_xq': nrm((DEPTH, D_MODEL, D_MODEL), D_MODEL ** -0.5),
        'w_xk': nrm((DEPTH, D_MODEL, D_MODEL), D_MODEL ** -0.5),
        'w_xv': nrm((DEPTH, D_MODEL, D_MODEL), D_MODEL ** -0.5),
        'w_xo': nrm((DEPTH, D_MODEL, D_MODEL), D_MODEL ** -0.5),
        'norm_mlp': gain((DEPTH, D_MODEL)),
        'w_ff1': nrm((DEPTH, D_MODEL, D_FF), D_MODEL ** -0.5),
        'w_ff2': nrm((DEPTH, D_FF, D_MODEL), D_FF ** -0.5),
        'norm_final': gain((D_MODEL,)),
    }


def reference(x_prompt, x_sample, cache_nsa_kv, cache_win_kv, state_conv_a, state_conv_qkv, state_gdn, cache_mem_kv,
              page_table, mem_prompt, rel_bias, norm_mix, w_in, conv_a_w, conv_a_b, ln_a_g, ln_a_b, w_pa, gdn_conv_w,
              gdn_a_log, gdn_dt_bias, gdn_norm_g, w_pb, nsa_cmp_w, w_pc, w_o, norm_x, w_xq, w_xk, w_xv, w_xo,
              norm_mlp, w_ff1, w_ff2, norm_final):
    p = {'rel_bias': rel_bias, 'w_in': w_in, 'conv_a_w': conv_a_w, 'conv_a_b': conv_a_b, 'ln_a_g': ln_a_g,
         'ln_a_b': ln_a_b, 'w_pa': w_pa, 'gdn_conv_w': gdn_conv_w, 'gdn_a_log': gdn_a_log, 'gdn_dt_bias': gdn_dt_bias,
         'gdn_norm_g': gdn_norm_g, 'w_pb': w_pb, 'nsa_cmp_w': nsa_cmp_w, 'w_pc': w_pc, 'w_o': w_o}
    xp, xs = x_prompt, x_sample
    Bp = xp.shape[0]
    Bs = xs.shape[0]
    n_pages = page_table.shape[1]
    p_rows, p_win, p_conv, p_qkv, p_gdn, p_mem = [], [], [], [], [], []
    s_rows, s_win, s_conv, s_qkv, s_gdn = [], [], [], [], []
    for l in range(DEPTH):
        h = rmsnorm(xp, norm_mix[l])
        conv0 = jnp.zeros((Bp, CONV_W - 1, CONV_CH), xp.dtype)
        qkv0 = jnp.zeros((Bp, GDN_CONV_W - 1, GDN_QKV), xp.dtype)
        s0 = jnp.zeros((Bp, GDN_HEADS, GDN_DK, GDN_DV), F32)
        y, conv_n, qkv_n, s_n, rows, win = parallel_mixers(h, p, l, conv0, qkv0, s0, nsa_prompt)
        xp = xp + y
        mem_kv = jnp.stack([mem_prompt @ w_xk[l], mem_prompt @ w_xv[l]], axis=2).reshape(Bp, N_MEM, 2, X_HEADS, X_DH)
        xp = xp + cross_attn(rmsnorm(xp, norm_x[l]), mem_kv[:, :, 0], mem_kv[:, :, 1], w_xq[l], w_xo[l])
        xp = xp + sq_relu_mlp(rmsnorm(xp, norm_mlp[l]), w_ff1[l], w_ff2[l])
        p_rows.append(rows); p_win.append(win); p_conv.append(conv_n); p_qkv.append(qkv_n); p_gdn.append(s_n); p_mem.append(mem_kv)
        h = rmsnorm(xs, norm_mix[l])
        past = cache_nsa_kv[l][page_table].reshape(Bs, n_pages * PAGE_SIZE, 4, NSA_KV, NSA_DH)
        nsa_fn = functools.partial(nsa_sample, past_kv=past, win_buf=cache_win_kv[l])
        y, conv_n, qkv_n, s_n, rows, win = parallel_mixers(h, p, l, state_conv_a[l], state_conv_qkv[l], state_gdn[l], nsa_fn)
        xs = xs + y
        mkv = cache_mem_kv[l].astype(xs.dtype)
        xs = xs + cross_attn(rmsnorm(xs, norm_x[l]), mkv[:, :, 0], mkv[:, :, 1], w_xq[l], w_xo[l])
        xs = xs + sq_relu_mlp(rmsnorm(xs, norm_mlp[l]), w_ff1[l], w_ff2[l])
        s_rows.append(rows); s_win.append(win); s_conv.append(conv_n); s_qkv.append(qkv_n); s_gdn.append(s_n)
    y_prompt = rmsnorm(xp, norm_final)
    y_sample = rmsnorm(xs, norm_final)
    new_nsa_kv_prompt = jnp.stack(p_rows, axis=0)
    new_win_kv_prompt = jnp.stack(p_win, axis=0)
    new_conv_a_prompt = jnp.stack(p_conv, axis=0)
    new_conv_qkv_prompt = jnp.stack(p_qkv, axis=0)
    new_gdn_prompt = jnp.stack(p_gdn, axis=0)
    new_mem_kv_prompt = jnp.stack(p_mem, axis=0)
    new_nsa_kv_sample = jnp.stack(s_rows, axis=0)
    new_win_kv_sample = jnp.stack(s_win, axis=0)
    new_conv_a_sample = jnp.stack(s_conv, axis=0)
    new_conv_qkv_sample = jnp.stack(s_qkv, axis=0)
    new_gdn_sample = jnp.stack(s_gdn, axis=0)
    return (y_prompt, y_sample, new_nsa_kv_prompt, new_win_kv_prompt, new_conv_a_prompt, new_conv_qkv_prompt,
            new_gdn_prompt, new_mem_kv_prompt, new_nsa_kv_sample, new_win_kv_sample, new_conv_a_sample,
            new_conv_qkv_sample, new_gdn_sample)
```

```python
import functools
import math

import jax
import jax.numpy as jnp
import numpy as np
from jax import lax
from jax.experimental import pallas as pl
from jax.experimental.pallas import tpu as pltpu

F32 = jnp.float32
BF16 = jnp.bfloat16
HI = lax.Precision.HIGHEST

D_MODEL = 1024
CONV_CH = 512
CONV_W = 31
GDN_HEADS = 4
GDN_D = 128
GDN_CHUNK = 64
GDN_QKV = 3 * GDN_HEADS * GDN_D
NSA_HEADS = 8
NSA_KV = 2
NSA_GQ = 4
NSA_DH = 64
L_CMP = 32
L_SEL = 64
N_SEL = 16
WINDOW = 512
Q_BLOCK = 128
FORCE_BONUS = 1e4
PAGE = 128
N_MEM = 256
X_HEADS = 4
X_DH = 256
D_FF = 4096
N_BUCKETS = 32
EPS = 1e-6
NEG = -1e30

LANES = 128
HALO = 32
VMEM_LIMIT = 48 * 1024 * 1024


def _bf(x):
    return x.astype(BF16)


def _dot(a, b):
    return jnp.dot(a, b, preferred_element_type=F32)


def _dot_nt(a, b):
    return lax.dot_general(a, b, (((1,), (1,)), ((), ())), preferred_element_type=F32)


def _dot_hi(a, b):
    return jnp.dot(a, b, precision=HI, preferred_element_type=F32)


def _sigmoid(x):
    return 1.0 / (1.0 + jnp.exp(-x))


def _silu(x):
    return x * _sigmoid(x)


def _params(*sem):
    return pltpu.CompilerParams(dimension_semantics=sem, vmem_limit_bytes=VMEM_LIMIT)


def _rms_kernel(x_ref, g_ref, o_ref):
    x = x_ref[...]
    y = x * lax.rsqrt(jnp.mean(x * x, axis=-1, keepdims=True) + EPS)
    o_ref[...] = (y * g_ref[...]).astype(o_ref.dtype)


def _rmsnorm(x, g, out_dtype):
    m, d = x.shape
    tm = min(m, 512)
    return pl.pallas_call(
        _rms_kernel,
        grid=(m // tm,),
        in_specs=[pl.BlockSpec((tm, d), lambda i: (i, 0)), pl.BlockSpec((1, d), lambda i: (0, 0))],
        out_specs=pl.BlockSpec((tm, d), lambda i: (i, 0)),
        out_shape=jax.ShapeDtypeStruct((m, d), out_dtype),
        compiler_params=_params("parallel"),
        name="rmsnorm",
    )(x, g.reshape(1, d))


def _mm_kernel(a_ref, w_ref, o_ref):
    o_ref[...] = _dot(a_ref[...], w_ref[...])


def _col_tile(n):
    for tn in (1024, 768, 512, 384, 256, 128):
        if n % tn == 0:
            return tn
    raise ValueError(n)


def _matmul(a, w):
    m, k = a.shape
    n = w.shape[1]
    tm = min(m, 1024)
    tn = _col_tile(n)
    return pl.pallas_call(
        _mm_kernel,
        grid=(m // tm, n // tn),
        in_specs=[pl.BlockSpec((tm, k), lambda i, j: (i, 0)), pl.BlockSpec((k, tn), lambda i, j: (0, j))],
        out_specs=pl.BlockSpec((tm, tn), lambda i, j: (i, j)),
        out_shape=jax.ShapeDtypeStruct((m, n), F32),
        compiler_params=_params("parallel", "parallel"),
        name="matmul",
    )(a, w)


def _conf_kernel(n_t, tt, tv, u_ref, halo_ref, st_ref, w_ref, b_ref, g_ref, lb_ref, o_ref, nb_ref, xc_ref):
    t = pl.program_id(1)
    u = u_ref[0]
    xc_ref[HALO:HALO + tt, :] = u[:, :CONV_CH] * _sigmoid(u[:, CONV_CH:])
    if n_t > 1:
        uh = halo_ref[0]
        gh = uh[:, :CONV_CH] * _sigmoid(uh[:, CONV_CH:])
        xc_ref[0:HALO, :] = jnp.where(t > 0, gh, st_ref[0])
    else:
        xc_ref[0:HALO, :] = st_ref[0]
    off = HALO - (CONV_W - 1)
    acc = xc_ref[off:off + tt, :] * w_ref[0:1, :]
    for i in range(1, CONV_W):
        acc = acc + xc_ref[off + i:off + i + tt, :] * w_ref[i:i + 1, :]
    y = acc + b_ref[...]
    mu = jnp.mean(y, axis=-1, keepdims=True)
    yc = y - mu
    var = jnp.mean(yc * yc, axis=-1, keepdims=True)
    ln = yc * lax.rsqrt(var + EPS) * g_ref[...] + lb_ref[...]
    o_ref[0] = _silu(ln)

    @pl.when(t == n_t - 1)
    def _():
        nb_ref[0] = xc_ref[tv:tv + HALO, :]


def _conformer(u_a, state_pad, w_dw, b_dw, ln_g, ln_b, n_valid_last):
    bsz, t_len, _ = u_a.shape
    tt = min(t_len, 256)
    n_t = t_len // tt
    hb = tt // HALO if n_t > 1 else 1
    halo_rows = HALO if n_t > 1 else tt
    w_pad = jnp.pad(w_dw, ((0, HALO - CONV_W), (0, 0)))
    row = lambda v: v.reshape(1, CONV_CH)
    kern = functools.partial(_conf_kernel, n_t, tt, n_valid_last)
    return pl.pallas_call(
        kern,
        grid=(bsz, n_t),
        in_specs=[
            pl.BlockSpec((1, tt, 2 * CONV_CH), lambda b, t: (b, t, 0)),
            pl.BlockSpec((1, halo_rows, 2 * CONV_CH), lambda b, t: (b, jnp.maximum(t * hb - 1, 0), 0)),
            pl.BlockSpec((1, HALO, CONV_CH), lambda b, t: (b, 0, 0)),
            pl.BlockSpec((HALO, CONV_CH), lambda b, t: (0, 0)),
            pl.BlockSpec((1, CONV_CH), lambda b, t: (0, 0)),
            pl.BlockSpec((1, CONV_CH), lambda b, t: (0, 0)),
            pl.BlockSpec((1, CONV_CH), lambda b, t: (0, 0)),
        ],
        out_specs=[
            pl.BlockSpec((1, tt, CONV_CH), lambda b, t: (b, t, 0)),
            pl.BlockSpec((1, HALO, CONV_CH), lambda b, t: (b, 0, 0)),
        ],
        out_shape=[
            jax.ShapeDtypeStruct((bsz, t_len, CONV_CH), F32),
            jax.ShapeDtypeStruct((bsz, HALO, CONV_CH), F32),
        ],
        scratch_shapes=[pltpu.VMEM((HALO + tt, CONV_CH), F32)],
        compiler_params=_params("parallel", "arbitrary"),
        name="conformer_conv",
    )(u_a, u_a, state_pad, w_pad, row(b_dw), row(ln_g), row(ln_b))


def _tri_inverse(a, ii, jj):
    eye = (ii == jj).astype(F32)
    a0 = jnp.where((ii >> 3) == (jj >> 3), a, 0.0)
    a2 = _dot_hi(a0, a0)
    a4 = _dot_hi(a2, a2)
    x = _dot_hi(_dot_hi(eye - a0, eye + a2), eye + a4)
    for sh in (3, 4, 5):
        off = jnp.where(((ii >> (sh + 1)) == (jj >> (sh + 1))) & ((ii >> sh) != (jj >> sh)), a, 0.0)
        x = x - _dot_hi(x, _dot_hi(off, x))
    return x


def _softplus(x):
    return jnp.maximum(x, 0.0) + jnp.log1p(jnp.exp(-jnp.abs(x)))


def _gdn_kernel(t_len, n_valid, alog_ref, dtb_ref, q_ref, k_ref, v_ref, z_ref, a_ref, b_ref,
                sq_ref, sk_ref, sv_ref, wq_ref, wk_ref, wv_ref, s0_ref, ng_ref,
                o_ref, sn_ref,
                xp_s, qn_s, kn_s, vn_s, g_s, be_s, vw_s, kcd_s, qg_s, kdt_s, qk_s, ge_s, oo_s):
    h = pl.program_id(1)
    n_chunks = t_len // GDN_CHUNK
    c_len = GDN_CHUNK

    def conv(x_ref, st_ref, w_ref):
        xp_s[0:8, :] = st_ref[0]
        xp_s[8:8 + t_len, :] = x_ref[0]
        acc = xp_s[5:5 + t_len, :] * w_ref[0:1, :]
        for i in range(1, 4):
            acc = acc + xp_s[5 + i:5 + i + t_len, :] * w_ref[i:i + 1, :]
        return _silu(acc)

    qc = conv(q_ref, sq_ref, wq_ref)
    qn_s[...] = qc * lax.rsqrt(jnp.sum(qc * qc, axis=-1, keepdims=True) + EPS) * (GDN_D ** -0.5)
    kc = conv(k_ref, sk_ref, wk_ref)
    kn_s[...] = kc * lax.rsqrt(jnp.sum(kc * kc, axis=-1, keepdims=True) + EPS)
    vn_s[...] = conv(v_ref, sv_ref, wv_ref)

    a_exp = jnp.exp(jnp.full((1, LANES), alog_ref[h], F32))
    g = -a_exp * _softplus(a_ref[0] + dtb_ref[h])
    beta = _sigmoid(b_ref[0])
    if n_valid < t_len:
        live = lax.broadcasted_iota(jnp.int32, (t_len, LANES), 0) < n_valid
        g = jnp.where(live, g, 0.0)
        beta = jnp.where(live, beta, 0.0)
    g_s[...] = g
    be_s[...] = beta

    ii = lax.broadcasted_iota(jnp.int32, (c_len, c_len), 0)
    jj = lax.broadcasted_iota(jnp.int32, (c_len, c_len), 1)
    incl = ii >= jj
    strict = ii > jj
    ltri = incl.astype(F32)

    def prep(c, carry):
        sl = pl.ds(pl.multiple_of(c * c_len, c_len), c_len)
        gc = _dot_hi(ltri, g_s[sl, :])
        k_c = kn_s[sl, :]
        q_c = qn_s[sl, :]
        b_c = be_s[sl, :]
        dlog = gc[:, 0:c_len] - gc.T[0:c_len, :]
        dec = jnp.where(incl, jnp.exp(jnp.minimum(dlog, 0.0)), 0.0)
        kb = k_c * b_c
        a_low = jnp.where(strict, _dot_nt(_bf(kb), _bf(k_c)) * dec, 0.0)
        t_inv = _tri_inverse(a_low, ii, jj)
        eg = jnp.exp(gc)
        sol = _dot_hi(t_inv, jnp.concatenate([vn_s[sl, :] * b_c, kb * eg], axis=1))
        vw_s[sl, :] = sol[:, :GDN_D]
        kcd_s[sl, :] = sol[:, GDN_D:]
        qk_s[c] = jnp.where(incl, _dot_nt(_bf(q_c), _bf(k_c)) * dec, 0.0)
        qg_s[sl, :] = q_c * eg
        g_end = gc[c_len - 1:c_len, :]
        kdt_s[c] = (k_c * jnp.exp(g_end - gc)).T
        ge_s[c] = jnp.broadcast_to(jnp.exp(g_end), (8, LANES))
        return carry

    lax.fori_loop(0, n_chunks, prep, 0)

    def step(c, s):
        sl = pl.ds(pl.multiple_of(c * c_len, c_len), c_len)
        sb = _bf(s)
        v_new = vw_s[sl, :] - _dot(_bf(kcd_s[sl, :]), sb)
        vb = _bf(v_new)
        oo_s[sl, :] = _dot(_bf(qg_s[sl, :]), sb) + _dot(_bf(qk_s[c]), vb)
        return s * ge_s[c][0:1, :] + _dot(_bf(kdt_s[c]), vb)

    s_fin = lax.fori_loop(0, n_chunks, step, s0_ref[0, 0])
    sn_ref[0, 0] = s_fin

    o = oo_s[...]
    y = o * lax.rsqrt(jnp.mean(o * o, axis=-1, keepdims=True) + EPS) * ng_ref[...]
    o_ref[0] = y * _silu(z_ref[0])


def _gated_deltanet(qkv, zab, state_pad, s0, w_conv_pad, a_log, dt_bias, norm_g, n_valid):
    bsz, t_len, _ = qkv.shape
    nh = GDN_HEADS
    blk = lambda off: pl.BlockSpec((1, t_len, GDN_D), lambda b, h, o=off: (b, 0, o + h))
    stb = lambda off: pl.BlockSpec((1, 8, GDN_D), lambda b, h, o=off: (b, 0, o + h))
    wb = lambda off: pl.BlockSpec((8, GDN_D), lambda b, h, o=off: (0, o + h))
    smem = pl.BlockSpec(memory_space=pltpu.SMEM)
    n_chunks = t_len // GDN_CHUNK
    seq = lambda: pltpu.VMEM((t_len, GDN_D), F32)
    kern = functools.partial(_gdn_kernel, t_len, n_valid)
    return pl.pallas_call(
        kern,
        grid=(bsz, nh),
        in_specs=[smem, smem, blk(0), blk(nh), blk(2 * nh), blk(0), blk(nh), blk(2 * nh),
                  stb(0), stb(nh), stb(2 * nh), wb(0), wb(nh), wb(2 * nh),
                  pl.BlockSpec((1, 1, GDN_D, GDN_D), lambda b, h: (b, h, 0, 0)),
                  pl.BlockSpec((1, GDN_D), lambda b, h: (0, 0))],
        out_specs=[pl.BlockSpec((1, t_len, GDN_D), lambda b, h: (b, 0, h)),
                   pl.BlockSpec((1, 1, GDN_D, GDN_D), lambda b, h: (b, h, 0, 0))],
        out_shape=[jax.ShapeDtypeStruct((bsz, t_len, nh * GDN_D), F32),
                   jax.ShapeDtypeStruct((bsz, nh, GDN_D, GDN_D), F32)],
        scratch_shapes=[pltpu.VMEM((8 + t_len, GDN_D), F32), seq(), seq(), seq(), seq(), seq(), seq(), seq(), seq(),
                        pltpu.VMEM((n_chunks, GDN_D, GDN_CHUNK), F32),
                        pltpu.VMEM((n_chunks, GDN_CHUNK, GDN_CHUNK), F32),
                        pltpu.VMEM((n_chunks, 8, LANES), F32), seq()],
        compiler_params=_params("parallel", "parallel"),
        name="gated_deltanet",
    )(a_log, dt_bias, qkv, qkv, qkv, zab, zab, zab, state_pad, state_pad, state_pad,
      w_conv_pad, w_conv_pad, w_conv_pad, s0, norm_g.reshape(1, GDN_D))


def _heads_to_rows(x, g, nt):
    lane = lax.broadcasted_iota(jnp.int32, (nt, LANES), 1)
    keep = (lane >= NSA_DH * g) & (lane < NSA_DH * (g + 1))
    parts = []
    for r in range(NSA_GQ):
        hh = NSA_GQ * g + r
        blk = x[:, (hh // 2) * LANES:(hh // 2 + 1) * LANES]
        if hh % 2 != g:
            blk = pltpu.roll(blk, NSA_DH, axis=1)
        parts.append(jnp.where(keep, blk, 0.0))
    return jnp.concatenate(parts, axis=0)


def _rows_to_heads(y, g, nt):
    outs = []
    for m in range(2):
        x0 = y[(2 * m) * nt:(2 * m + 1) * nt]
        x1 = y[(2 * m + 1) * nt:(2 * m + 2) * nt]
        if g == 1:
            x0 = pltpu.roll(x0, NSA_DH, axis=1)
        else:
            x1 = pltpu.roll(x1, NSA_DH, axis=1)
        outs.append(x0 + x1)
    return outs


def _masked_softmax_parts(parts, masks, axis):
    sm = [jnp.where(m, s, NEG) for s, m in zip(parts, masks)]
    mx = functools.reduce(jnp.maximum, [jnp.max(s, axis=axis, keepdims=True) for s in sm])
    es = [jnp.where(m, jnp.exp(s - mx), 0.0) for s, m in zip(sm, masks)]
    den = functools.reduce(lambda p, q: p + q, [jnp.sum(e, axis=axis, keepdims=True) for e in es])
    inv = 1.0 / jnp.maximum(den, 1e-30)
    return [e * inv for e in es]


def _bucket_np(rel):
    n = np.maximum(rel, 0)
    nf = np.maximum(n, 1).astype(np.float32)
    large = 16 + (np.log(nf / np.float32(16)) / np.float32(math.log(8.0)) * np.float32(16)).astype(np.int32)
    return np.where(n < 16, n, np.minimum(large, N_BUCKETS - 1)).astype(np.int32)


def _head_rows(tab):
    lead = tab.shape[:-3]
    nt, nk = tab.shape[-3], tab.shape[-2]
    x = jnp.moveaxis(tab, -1, -3)
    return x.reshape(lead + (NSA_KV, NSA_GQ * nt, nk))


def _nsa_prompt_kernel(t_len, q_ref, kcmp_ref, vcmp_ref, kslc_ref, vslc_ref, kwin_ref, vwin_ref, gl_ref,
                       wk_ref, wv_ref, bce_ref, bco_ref, bcet_ref, bcot_ref, bnear_ref, bwin_ref,
                       o_ref, kce_s, kco_s, vce_s, vco_s):
    i = pl.program_id(1)
    nsb = t_len // L_SEL
    qb = Q_BLOCK
    rows = NSA_GQ * qb

    @pl.when(i == 0)
    def _():
        n2 = 2 * lax.broadcasted_iota(jnp.int32, (nsb, t_len), 0)
        cb = lax.broadcasted_iota(jnp.int32, (nsb, t_len), 1) >> 5
        kc = _bf(kcmp_ref[0])
        vc = _bf(vcmp_ref[0])
        wk = wk_ref[...]
        wv = wv_ref[...]
        kce_s[...] = _dot(_bf(jnp.where(cb == n2, wk, 0.0)), kc)
        kco_s[...] = _dot(_bf(jnp.where(cb == n2 + 1, wk, 0.0)), kc)
        vce_s[...] = _dot(_bf(jnp.where(cb == n2, wv, 0.0)), vc)
        vco_s[...] = _dot(_bf(jnp.where(cb == n2 + 1, wv, 0.0)), vc)

    q_all = q_ref[0] * (NSA_DH ** -0.5)
    gl = gl_ref[0]
    t0 = i * qb
    tq = t0 + (lax.broadcasted_iota(jnp.int32, (rows, 1), 0) & (qb - 1))
    eye_q = _bf((lax.broadcasted_iota(jnp.int32, (qb, qb), 0) == lax.broadcasted_iota(jnp.int32, (qb, qb), 1)).astype(F32))
    far_end = jnp.maximum(t0 - qb, 0)
    n_far = (far_end + 511) >> 9

    for g in range(NSA_KV):
        qg = _bf(_heads_to_rows(q_all, g, qb))
        kce = _bf(kce_s[...])
        kco = _bf(kco_s[...])

        ncol = lax.broadcasted_iota(jnp.int32, (rows, nsb), 1)
        me = (ncol * L_SEL + (L_CMP - 1)) <= tq
        mo = (ncol * L_SEL + (L_SEL - 1)) <= tq
        pe, po = _masked_softmax_parts(
            [_dot_nt(qg, kce) + bce_ref[0, g], _dot_nt(qg, kco) + bco_ref[0, g]], [me, mo], 1)
        o_c = _dot(_bf(pe), _bf(vce_s[...])) + _dot(_bf(po), _bf(vco_s[...]))

        nrow = lax.broadcasted_iota(jnp.int32, (nsb, rows), 0)
        tl = t0 + (lax.broadcasted_iota(jnp.int32, (nsb, rows), 1) & (qb - 1))
        pet, pot = _masked_softmax_parts(
            [_dot_nt(kce, qg) + bcet_ref[0, g], _dot_nt(kco, qg) + bcot_ref[0, g]],
            [(nrow * L_SEL + (L_CMP - 1)) <= tl, (nrow * L_SEL + (L_SEL - 1)) <= tl], 0)
        imp_e = pet[:, 0:qb] + pet[:, qb:2 * qb] + pet[:, 2 * qb:3 * qb] + pet[:, 3 * qb:4 * qb]
        imp_o = pot[:, 0:qb] + pot[:, qb:2 * qb] + pot[:, 2 * qb:3 * qb] + pot[:, 3 * qb:4 * qb]
        blk = lax.broadcasted_iota(jnp.int32, (nsb, qb), 0)
        cur = (t0 + lax.broadcasted_iota(jnp.int32, (nsb, qb), 1)) >> 6
        forced = (blk == 0) | (blk == cur) | (blk == cur - 1)
        score = jnp.where(blk <= cur, (imp_e + imp_o) + jnp.where(forced, FORCE_BONUS, 0.0), -1.0)
        rank = jnp.zeros((nsb, qb), F32)
        for j in range(nsb):
            sj = score[j:j + 1, :]
            ahead = (sj > score) | ((sj == score) & (blk > j))
            rank = rank + jnp.where(ahead, 1.0, 0.0)
        sel_t = jnp.where(rank < float(min(N_SEL, nsb)), 1.0, 0.0)
        sel = _bf(_dot_nt(eye_q, _bf(sel_t)))

        def far_tile(kt, carry):
            m_i, l_i, acc = carry
            k0 = pl.multiple_of(kt * 512, 512)
            s = _dot_nt(qg, _bf(kslc_ref[0, pl.ds(k0, 512), :]))
            ej = lax.broadcasted_iota(jnp.int32, (nsb, 512), 0)
            ec = k0 + lax.broadcasted_iota(jnp.int32, (nsb, 512), 1)
            expand = jnp.where((ej == (ec >> 6)) & (ec < far_end), 1.0, 0.0)
            mk = jnp.concatenate([_dot(sel, _bf(expand))] * NSA_GQ, axis=0) > 0.5
            s = jnp.where(mk, s, NEG)
            m_n = jnp.maximum(m_i, jnp.max(s, axis=1, keepdims=True))
            p = jnp.where(mk, jnp.exp(s - m_n), 0.0)
            alpha = jnp.exp(m_i - m_n)
            l_n = alpha * l_i + jnp.sum(p, axis=1, keepdims=True)
            acc_n = alpha * acc + _dot(_bf(p), _bf(vslc_ref[0, pl.ds(k0, 512), :]))
            return m_n, l_n, acc_n

        init = (jnp.full((rows, 1), NEG, F32), jnp.zeros((rows, 1), F32), jnp.zeros((rows, LANES), F32))
        m_i, l_i, acc = lax.fori_loop(0, n_far, far_tile, init)

        p0 = pl.multiple_of(jnp.maximum(t0 - qb, 0), qb)
        d0 = pl.multiple_of(t0, qb)
        s_near = jnp.concatenate([_dot_nt(qg, _bf(kslc_ref[0, pl.ds(p0, qb), :])),
                                  _dot_nt(qg, _bf(kslc_ref[0, pl.ds(d0, qb), :]))], axis=1) + bnear_ref[g]
        cn = lax.broadcasted_iota(jnp.int32, (nsb, 2 * qb), 1)
        jn = lax.broadcasted_iota(jnp.int32, (nsb, 2 * qb), 0)
        expand_n = jnp.where(jn == ((t0 - qb + cn) >> 6), 1.0, 0.0)
        mk = jnp.concatenate([_dot(sel, _bf(expand_n))] * NSA_GQ, axis=0) > 0.5
        cn_r = lax.broadcasted_iota(jnp.int32, (rows, 2 * qb), 1)
        mk = mk & ((t0 - qb + cn_r) <= tq) & ((t0 - qb + cn_r) >= 0)
        s_near = jnp.where(mk, s_near, NEG)
        m_n = jnp.maximum(m_i, jnp.max(s_near, axis=1, keepdims=True))
        p = jnp.where(mk, jnp.exp(s_near - m_n), 0.0)
        alpha = jnp.exp(m_i - m_n)
        l_n = alpha * l_i + jnp.sum(p, axis=1, keepdims=True)
        acc = alpha * acc + _dot(_bf(p[:, 0:qb]), _bf(vslc_ref[0, pl.ds(p0, qb), :])) \
            + _dot(_bf(p[:, qb:2 * qb]), _bf(vslc_ref[0, pl.ds(d0, qb), :]))
        o_s = acc * (1.0 / jnp.maximum(l_n, 1e-30))

        n_prev = WINDOW // qb
        starts = [pl.multiple_of(jnp.maximum(t0 + (j - n_prev) * qb, 0), qb) for j in range(n_prev + 1)]
        s_w = jnp.concatenate([_dot_nt(qg, _bf(kwin_ref[0, pl.ds(st, qb), :])) for st in starts], axis=1) + bwin_ref[g]
        cw = t0 - WINDOW + lax.broadcasted_iota(jnp.int32, (rows, WINDOW + qb), 1)
        mw = (cw <= tq) & (cw > tq - WINDOW) & (cw >= 0)
        (pw,) = _masked_softmax_parts([s_w], [mw], 1)
        o_w = _dot(_bf(pw[:, 0:qb]), _bf(vwin_ref[0, pl.ds(starts[0], qb), :]))
        for j in range(1, n_prev + 1):
            o_w = o_w + _dot(_bf(pw[:, j * qb:(j + 1) * qb]), _bf(vwin_ref[0, pl.ds(starts[j], qb), :]))

        gates = [_heads_to_rows(_sigmoid(gl[:, br * 512:(br + 1) * 512]), g, qb) for br in range(3)]
        comb = gates[0] * o_c + gates[1] * o_s + gates[2] * o_w
        blocks = _rows_to_heads(comb, g, qb)
        o_ref[0, :, (2 * g) * LANES:(2 * g + 1) * LANES] = blocks[0]
        o_ref[0, :, (2 * g + 1) * LANES:(2 * g + 2) * LANES] = blocks[1]


def _nsa_prompt_tables(rel_bias, t_len):
    nqb = t_len // Q_BLOCK
    nsb = t_len // L_SEL
    tb = rel_bias.astype(F32)
    t = np.arange(Q_BLOCK)
    b31 = tb[N_BUCKETS - 1].reshape(NSA_KV, NSA_GQ, 1, 1)
    tq = (np.arange(nqb)[:, None] * Q_BLOCK + t[None, :])[:, :, None]
    n = np.arange(nsb)[None, None, :]
    bce = _head_rows(tb[_bucket_np(tq - (n * L_SEL + L_CMP - 1))])
    bco = _head_rows(tb[_bucket_np(tq - (n * L_SEL + L_SEL - 1))])
    bcet = jnp.swapaxes(bce, -1, -2)
    bcot = jnp.swapaxes(bco, -1, -2)
    near = tb[_bucket_np(Q_BLOCK + t[:, None] - np.arange(2 * Q_BLOCK)[None, :])]
    near = (jnp.moveaxis(near, -1, 0).reshape(NSA_KV, NSA_GQ, Q_BLOCK, 2 * Q_BLOCK) - b31)
    near = near.reshape(NSA_KV, NSA_GQ * Q_BLOCK, 2 * Q_BLOCK)
    win = _head_rows(tb[_bucket_np(WINDOW + t[:, None] - np.arange(WINDOW + Q_BLOCK)[None, :])])
    return bce, bco, bcet, bcot, near, win


def _nsa_prompt(qc, rows, win, glr, w_pos, tables):
    bsz, t_len, _ = qc.shape
    nqb = t_len // Q_BLOCK
    nsb = t_len // L_SEL
    bce, bco, bcet, bcot, near, wtab = tables
    wk = jnp.tile(w_pos[0], t_len // L_CMP).reshape(1, t_len)
    wv = jnp.tile(w_pos[1], t_len // L_CMP).reshape(1, t_len)
    seq = lambda c: pl.BlockSpec((1, t_len, LANES), lambda b, i, c=c: (b, 0, c))
    full = lambda a: pl.BlockSpec(a.shape, lambda b, i, nd=a.ndim: (0,) * nd)
    per_i = lambda a: pl.BlockSpec((1,) + a.shape[1:], lambda b, i, nd=a.ndim: (i,) + (0,) * (nd - 1))
    kern = functools.partial(_nsa_prompt_kernel, t_len)
    return pl.pallas_call(
        kern,
        grid=(bsz, nqb),
        in_specs=[pl.BlockSpec((1, Q_BLOCK, 512), lambda b, i: (b, i, 0)),
                  seq(0), seq(1), seq(2), seq(3), seq(0), seq(1),
                  pl.BlockSpec((1, Q_BLOCK, 1536), lambda b, i: (b, i, 0)),
                  full(wk), full(wv), per_i(bce), per_i(bco), per_i(bcet), per_i(bcot), full(near), full(wtab)],
        out_specs=pl.BlockSpec((1, Q_BLOCK, 512), lambda b, i: (b, i, 0)),
        out_shape=jax.ShapeDtypeStruct((bsz, t_len, 512), F32),
        scratch_shapes=[pltpu.VMEM((nsb, LANES), F32)] * 4,
        compiler_params=_params("parallel", "arbitrary"),
        name="nsa_prompt",
    )(qc, rows, rows, rows, rows, win, win, glr, wk, wv, bce, bco, bcet, bcot, near, wtab)


def _nsa_sample_kernel(n_pages, n_new, pt_ref, cmp_ref, slc_ref, q_ref, rows_ref, wnew_ref, wbuf_ref, gl_ref,
                       pool_ref, bce_ref, bco_ref, blast_ref, bnew_ref, bwin_ref, bwnew_ref,
                       o_ref, kce_s, kco_s, vce_s, vco_s, sel_s, oc_s, m_s, l_s, acc_s, pad_s):
    ph = pl.program_id(1)
    p = pl.program_id(2)
    nt = 8
    rows = NSA_GQ * nt
    slots = n_pages * 8
    last = n_pages - 1
    q_all = q_ref[0] * (NSA_DH ** -0.5)
    tr = lax.broadcasted_iota(jnp.int32, (rows, 1), 0) & (nt - 1)

    @pl.when(ph == 0)
    def _():
        pg = cmp_ref[0]
        kc = _bf(pg[:, 0:LANES])
        vc = _bf(pg[:, LANES:2 * LANES])
        kce_s[p] = _dot(_bf(pool_ref[0]), kc)
        kco_s[p] = _dot(_bf(pool_ref[1]), kc)
        vce_s[p] = _dot(_bf(pool_ref[2]), vc)
        vco_s[p] = _dot(_bf(pool_ref[3]), vc)

    @pl.when((ph == 0) & (p == last))
    def _():
        kce = _bf(kce_s[...].reshape(slots, LANES))
        kco = _bf(kco_s[...].reshape(slots, LANES))
        vce = _bf(vce_s[...].reshape(slots, LANES))
        vco = _bf(vco_s[...].reshape(slots, LANES))
        scol = lax.broadcasted_iota(jnp.int32, (rows, slots), 1)
        live = (scol & 7) < 2
        scol8 = lax.broadcasted_iota(jnp.int32, (nt, slots), 1)
        live8 = (scol8 & 7) < 2
        for g in range(NSA_KV):
            qg = _bf(_heads_to_rows(q_all, g, nt))
            pe, po = _masked_softmax_parts(
                [_dot_nt(qg, kce) + bce_ref[g], _dot_nt(qg, kco) + bco_ref[g]], [live, live], 1)
            oc_s[g] = _dot(_bf(pe), vce) + _dot(_bf(po), vco)
            imp_e = pe[0:nt] + pe[nt:2 * nt] + pe[2 * nt:3 * nt] + pe[3 * nt:4 * nt]
            imp_o = po[0:nt] + po[nt:2 * nt] + po[2 * nt:3 * nt] + po[3 * nt:4 * nt]
            forced = (scol8 == 0) | (scol8 == (last * 8 + 1))
            score = jnp.where(live8, (imp_e + imp_o) + jnp.where(forced, FORCE_BONUS, 0.0), -2.0)
            rank = jnp.where(FORCE_BONUS > score, 1.0, 0.0)
            for pg_i in range(n_pages):
                for e in range(2):
                    s_i = pg_i * 8 + e
                    sj = score[:, s_i:s_i + 1]
                    ahead = (sj > score) | ((sj == score) & (scol8 > s_i))
                    rank = rank + jnp.where(ahead, 1.0, 0.0)
            sel = jnp.where((rank < float(N_SEL)) & live8, 1.0, 0.0)
            sel_s[g] = jnp.concatenate([sel] * NSA_GQ, axis=0)
            m_s[g] = jnp.full((rows, LANES), NEG, F32)
            l_s[g] = jnp.zeros((rows, LANES), F32)
            acc_s[g] = jnp.zeros((rows, LANES), F32)

    def online(g, s, mk, v):
        m_i = m_s[g]
        l_i = l_s[g]
        s = jnp.where(mk, s, NEG)
        m_n = jnp.maximum(m_i, jnp.max(s, axis=1, keepdims=True))
        pr = jnp.where(mk, jnp.exp(s - m_n), 0.0)
        alpha = jnp.exp(m_i - m_n)
        m_s[g] = m_n
        l_s[g] = alpha * l_i + jnp.sum(pr, axis=1, keepdims=True)
        acc_s[g] = alpha * acc_s[g] + _dot(_bf(pr), v)

    @pl.when(ph == 1)
    def _():
        pg = slc_ref[0]
        ks = _bf(pg[:, 0:LANES])
        vs = _bf(pg[:, LANES:2 * LANES])
        ej = lax.broadcasted_iota(jnp.int32, (slots, PAGE), 0)
        ec = lax.broadcasted_iota(jnp.int32, (slots, PAGE), 1)
        expand = _bf(jnp.where(ej == (p * 8 + (ec >> 6)), 1.0, 0.0))
        is_last = p == last
        for g in range(NSA_KV):
            qg = _bf(_heads_to_rows(q_all, g, nt))
            s = _dot_nt(qg, ks) + jnp.where(is_last, blast_ref[g], 0.0)
            mk = _dot(_bf(sel_s[g]), expand) > 0.5
            online(g, s, mk, vs)

    @pl.when((ph == 1) & (p == last))
    def _():
        gl = gl_ref[0]
        new = rows_ref[0]
        wnew = wnew_ref[0]
        wbuf = wbuf_ref[0]
        tc = lax.broadcasted_iota(jnp.int32, (rows, LANES), 1)
        mnew = (tc <= tr) & (tc < n_new)
        cw = lax.broadcasted_iota(jnp.int32, (rows, WINDOW), 1)
        mwin = cw > tr
        pad_s[...] = jnp.zeros((4, LANES, LANES), F32)
        pad_s[0, 0:nt, :] = new[:, 2 * LANES:3 * LANES]
        pad_s[1, 0:nt, :] = new[:, 3 * LANES:4 * LANES]
        pad_s[2, 0:nt, :] = wnew[:, 0:LANES]
        pad_s[3, 0:nt, :] = wnew[:, LANES:2 * LANES]
        for g in range(NSA_KV):
            qg = _bf(_heads_to_rows(q_all, g, nt))
            online(g, _dot_nt(qg, _bf(pad_s[0])) + bnew_ref[g], mnew, _bf(pad_s[1]))
            o_s = acc_s[g] * (1.0 / jnp.maximum(l_s[g], 1e-30))
            pw, pn = _masked_softmax_parts(
                [_dot_nt(qg, _bf(wbuf[:, 0:LANES])) + bwin_ref[g], _dot_nt(qg, _bf(pad_s[2])) + bwnew_ref[g]],
                [mwin, mnew], 1)
            o_w = _dot(_bf(pw), _bf(wbuf[:, LANES:2 * LANES])) + _dot(_bf(pn), _bf(pad_s[3]))
            gates = [_heads_to_rows(_sigmoid(gl[:, br * 512:(br + 1) * 512]), g, nt) for br in range(3)]
            comb = gates[0] * oc_s[g] + gates[1] * o_s + gates[2] * o_w
            blocks = _rows_to_heads(comb, g, nt)
            o_ref[0, :, (2 * g) * LANES:(2 * g + 1) * LANES] = blocks[0]
            o_ref[0, :, (2 * g + 1) * LANES:(2 * g + 2) * LANES] = blocks[1]


def _nsa_sample_tables(rel_bias, n_pages, nt):
    tb = rel_bias.astype(F32)
    past = n_pages * PAGE
    t = np.arange(nt)[:, None]
    b31 = tb[N_BUCKETS - 1].reshape(NSA_KV, NSA_GQ, 1, 1)
    s = np.arange(n_pages * 8)[None, :]
    n_e = 4 * (s // 8) + 2 * (s % 8)
    bce = _head_rows(tb[_bucket_np(past + t - (n_e * L_CMP + L_CMP - 1))])
    bco = _head_rows(tb[_bucket_np(past + t - ((n_e + 1) * L_CMP + L_CMP - 1))])
    shifted = lambda idx: (jnp.moveaxis(tb[idx], -1, 0).reshape(NSA_KV, NSA_GQ, nt, idx.shape[1]) - b31).reshape(
        NSA_KV, NSA_GQ * nt, idx.shape[1])
    c = np.arange(PAGE)[None, :]
    blast = shifted(_bucket_np(PAGE + t - c))
    bnew_idx = _bucket_np(t - c)
    bnew = shifted(bnew_idx)
    bwin = _head_rows(tb[_bucket_np(WINDOW + t - np.arange(WINDOW)[None, :])])
    bwnew = _head_rows(tb[bnew_idx])
    return bce, bco, blast, bnew, bwin, bwnew


def _nsa_sample(cache, page_table, qc, rows, wnew, wbuf, glr, w_pos, tables, n_new):
    bsz, n_pages = page_table.shape
    nt = qc.shape[1]
    bce, bco, blast, bnew, bwin, bwnew = tables
    c = np.arange(PAGE)
    r = np.arange(8)[:, None]
    even = jnp.asarray(((c[None, :] // L_CMP) == 2 * r) & (r < 2), F32)
    odd = jnp.asarray(((c[None, :] // L_CMP) == 2 * r + 1) & (r < 2), F32)
    wk = jnp.tile(w_pos[0], PAGE // L_CMP)[None, :]
    wv = jnp.tile(w_pos[1], PAGE // L_CMP)[None, :]
    pool = jnp.stack([even * wk, odd * wk, even * wv, odd * wv])
    last = n_pages - 1
    full = lambda a: pl.BlockSpec(a.shape, lambda b, ph, p, pt, nd=a.ndim: (0,) * nd)
    per_b = lambda a: pl.BlockSpec((1,) + a.shape[1:], lambda b, ph, p, pt, nd=a.ndim: (b,) + (0,) * (nd - 1))
    rows_f = NSA_GQ * nt
    kern = functools.partial(_nsa_sample_kernel, n_pages, n_new)
    grid_spec = pltpu.PrefetchScalarGridSpec(
        num_scalar_prefetch=1,
        grid=(bsz, 2, n_pages),
        in_specs=[
            pl.BlockSpec((1, PAGE, 2 * LANES), lambda b, ph, p, pt: (pt[b, jnp.where(ph == 0, p, last)], 0, 0)),
            pl.BlockSpec((1, PAGE, 2 * LANES), lambda b, ph, p, pt: (pt[b, jnp.where(ph == 1, p, 0)], 0, 1)),
            per_b(qc), per_b(rows), per_b(wnew), per_b(wbuf), per_b(glr),
            full(pool), full(bce), full(bco), full(blast), full(bnew), full(bwin), full(bwnew)],
        out_specs=pl.BlockSpec((1, nt, 512), lambda b, ph, p, pt: (b, 0, 0)),
        scratch_shapes=[pltpu.VMEM((n_pages, 8, LANES), F32)] * 4 + [
            pltpu.VMEM((NSA_KV, rows_f, n_pages * 8), F32),
            pltpu.VMEM((NSA_KV, rows_f, LANES), F32),
            pltpu.VMEM((NSA_KV, rows_f, LANES), F32),
            pltpu.VMEM((NSA_KV, rows_f, LANES), F32),
            pltpu.VMEM((NSA_KV, rows_f, LANES), F32),
            pltpu.VMEM((4, LANES, LANES), F32)],
    )
    return pl.pallas_call(
        kern,
        grid_spec=grid_spec,
        out_shape=jax.ShapeDtypeStruct((bsz, nt, 512), F32),
        compiler_params=_params("parallel", "arbitrary", "arbitrary"),
        name="nsa_sample",
    )(page_table, cache, cache, qc, rows, wnew, wbuf, glr, pool, bce, bco, blast, bnew, bwin, bwnew)


def _mixout_kernel(x_ref, ca_ref, ob_ref, oc_ref, ug_ref, wpa_ref, wpb_ref, wpc_ref, wo_ref, o_ref):
    ug = ug_ref[...]
    y = _sigmoid(ug[:, 0:D_MODEL]) * _dot(_bf(ca_ref[...]), wpa_ref[...])
    y = y + _sigmoid(ug[:, D_MODEL:2 * D_MODEL]) * _dot(_bf(ob_ref[...]), wpb_ref[...])
    y = y + _sigmoid(ug[:, 2 * D_MODEL:3 * D_MODEL]) * _dot(_bf(oc_ref[...]), wpc_ref[...])
    o_ref[...] = x_ref[...] + _dot(_bf(y), wo_ref[...])


def _mixout(x, ca, ob, oc, ug, wpa, wpb, wpc, wo):
    m = x.shape[0]
    tm = min(m, 512)
    rowblk = lambda n: pl.BlockSpec((tm, n), lambda i: (i, 0))
    full = lambda a: pl.BlockSpec(a.shape, lambda i: (0, 0))
    return pl.pallas_call(
        _mixout_kernel,
        grid=(m // tm,),
        in_specs=[rowblk(D_MODEL), rowblk(512), rowblk(512), rowblk(512), rowblk(3 * D_MODEL),
                  full(wpa), full(wpb), full(wpc), full(wo)],
        out_specs=rowblk(D_MODEL),
        out_shape=jax.ShapeDtypeStruct((m, D_MODEL), F32),
        compiler_params=_params("parallel"),
        name="mixer_out",
    )(x, ca, ob, oc, ug, wpa, wpb, wpc, wo)


def _xattn_kernel(x_ref, g_ref, kv_ref, wq_ref, wo_ref, o_ref):
    x = x_ref[0]
    h = _bf(x * lax.rsqrt(jnp.mean(x * x, axis=-1, keepdims=True) + EPS) * g_ref[...])
    q = _dot(h, wq_ref[...])
    kv = kv_ref[0]
    outs = []
    for hd in range(X_HEADS):
        qh = _bf(q[:, hd * X_DH:(hd + 1) * X_DH])
        kh = _bf(kv[:, hd * X_DH:(hd + 1) * X_DH])
        vh = _bf(kv[:, D_MODEL + hd * X_DH:D_MODEL + (hd + 1) * X_DH])
        s = _dot_nt(qh, kh) * (X_DH ** -0.5)
        e = jnp.exp(s - jnp.max(s, axis=-1, keepdims=True))
        pr = e * (1.0 / jnp.sum(e, axis=-1, keepdims=True))
        outs.append(_dot(_bf(pr), vh))
    o = jnp.concatenate(outs, axis=1)
    o_ref[0] = x + _dot(_bf(o), wo_ref[...])


def _cross_attn(x, g, mem_kv, wq, wo):
    bsz, t_len, d = x.shape
    tt = min(t_len, 512)
    full = lambda a: pl.BlockSpec(a.shape, lambda b, t: (0, 0))
    return pl.pallas_call(
        _xattn_kernel,
        grid=(bsz, t_len // tt),
        in_specs=[pl.BlockSpec((1, tt, d), lambda b, t: (b, t, 0)),
                  pl.BlockSpec((1, d), lambda b, t: (0, 0)),
                  pl.BlockSpec((1, N_MEM, 2 * d), lambda b, t: (b, 0, 0)),
                  full(wq), full(wo)],
        out_specs=pl.BlockSpec((1, tt, d), lambda b, t: (b, t, 0)),
        out_shape=jax.ShapeDtypeStruct((bsz, t_len, d), F32),
        compiler_params=_params("parallel", "parallel"),
        name="cross_attn",
    )(x, g.reshape(1, d), mem_kv, wq, wo)


FF_CHUNK = 1024


def _mlp_kernel(n_k, x_ref, g_ref, w1_ref, w2_ref, o_ref, h_s, acc_s):
    k = pl.program_id(1)

    @pl.when(k == 0)
    def _():
        x = x_ref[...]
        h_s[...] = _bf(x * lax.rsqrt(jnp.mean(x * x, axis=-1, keepdims=True) + EPS) * g_ref[...])
        acc_s[...] = x

    a = jnp.maximum(_dot(h_s[...], w1_ref[...]), 0.0)
    acc_s[...] += _dot(_bf(a * a), w2_ref[...])

    @pl.when(k == n_k - 1)
    def _():
        o_ref[...] = acc_s[...]


def _mlp(x, g, w1, w2):
    m, d = x.shape
    tm = min(m, 1024)
    n_k = D_FF // FF_CHUNK
    return pl.pallas_call(
        functools.partial(_mlp_kernel, n_k),
        grid=(m // tm, n_k),
        in_specs=[pl.BlockSpec((tm, d), lambda i, k: (i, 0)),
                  pl.BlockSpec((1, d), lambda i, k: (0, 0)),
                  pl.BlockSpec((d, FF_CHUNK), lambda i, k: (0, k)),
                  pl.BlockSpec((FF_CHUNK, d), lambda i, k: (k, 0))],
        out_specs=pl.BlockSpec((tm, d), lambda i, k: (i, 0)),
        out_shape=jax.ShapeDtypeStruct((m, d), F32),
        scratch_shapes=[pltpu.VMEM((tm, d), BF16), pltpu.VMEM((tm, d), F32)],
        compiler_params=_params("parallel", "arbitrary"),
        name="sq_relu_mlp",
    )(x, g.reshape(1, d), w1, w2)


A_COLS = 2 * CONV_CH
B0 = A_COLS
Z0 = B0 + GDN_QKV
AB0 = Z0 + GDN_HEADS * GDN_D
C0 = AB0 + 2 * GDN_HEADS
KV0 = C0 + NSA_HEADS * NSA_DH
GL0 = KV0 + 6 * NSA_KV * NSA_DH
G0 = GL0 + 3 * NSA_HEADS
N_IN = G0 + 3 * D_MODEL


def _layer_weights(l, w_in, w_pa, w_pb, w_pc, w_o, w_xq, w_xk, w_xv, w_xo, w_ff1, w_ff2):
    w = w_in[l]
    rep = lambda cols, n: jnp.repeat(cols, n, axis=1)
    zab = jnp.concatenate([w[:, Z0:AB0], rep(w[:, AB0:AB0 + GDN_HEADS], GDN_D),
                           rep(w[:, AB0 + GDN_HEADS:C0], GDN_D)], axis=1)
    return {
        "a": _bf(w[:, 0:A_COLS]),
        "qkv": _bf(w[:, B0:Z0]),
        "zab": _bf(zab),
        "q": _bf(w[:, C0:KV0]),
        "rows": _bf(w[:, KV0:KV0 + 4 * LANES]),
        "win": _bf(w[:, KV0 + 4 * LANES:GL0]),
        "glr": _bf(rep(w[:, GL0:G0], NSA_DH)),
        "g": _bf(w[:, G0:N_IN]),
        "pa": _bf(w_pa[l]), "pb": _bf(w_pb[l]), "pc": _bf(w_pc[l]), "o": _bf(w_o[l]),
        "xq": _bf(w_xq[l]), "xo": _bf(w_xo[l]),
        "xkv": _bf(jnp.concatenate([w_xk[l], w_xv[l]], axis=1)),
        "ff1": _bf(w_ff1[l]), "ff2": _bf(w_ff2[l]),
    }


def _mixers(x, lw, p, l, conv_state_pad, qkv_state_pad, s0, n_valid, gdn_len, nsa_fn):
    bsz, t_len, d = x.shape
    m = bsz * t_len
    x2 = x.reshape(m, d)
    h = _rmsnorm(x2, p["norm_mix"][l], BF16)
    proj = lambda name: _matmul(h, lw[name])
    u_a = proj("a").reshape(bsz, t_len, -1)
    qkv = proj("qkv").reshape(bsz, t_len, -1)
    zab = proj("zab").reshape(bsz, t_len, -1)
    qc = proj("q").reshape(bsz, t_len, -1)
    rows = proj("rows").reshape(bsz, t_len, -1)
    win = proj("win").reshape(bsz, t_len, -1)
    glr = proj("glr").reshape(bsz, t_len, -1)
    ug = proj("g")

    ca, conv_new = _conformer(u_a, conv_state_pad, p["conv_a_w"][l], p["conv_a_b"][l], p["ln_a_g"][l],
                              p["ln_a_b"][l], n_valid if n_valid < t_len else min(t_len, 256))
    pad_t = ((0, 0), (0, gdn_len - t_len), (0, 0))
    w_conv_pad = jnp.pad(p["gdn_conv_w"][l], ((0, 4), (0, 0)))
    ob, s_new = _gated_deltanet(jnp.pad(qkv, pad_t), jnp.pad(zab, pad_t), qkv_state_pad, s0, w_conv_pad,
                                p["gdn_a_log"][l], p["gdn_dt_bias"][l], p["gdn_norm_g"][l],
                                n_valid if n_valid < t_len else gdn_len)
    ob = ob[:, :t_len]
    oc = nsa_fn(qc, rows, win, glr)
    x_new = _mixout(x2, ca.reshape(m, -1), ob.reshape(m, -1), oc.reshape(m, -1), ug,
                    lw["pa"], lw["pb"], lw["pc"], lw["o"])
    return x_new.reshape(bsz, t_len, d), conv_new[:, HALO - (CONV_W - 1):], qkv, s_new, rows, win


def kernel(x_prompt, x_sample, cache_nsa_kv, cache_win_kv, state_conv_a, state_conv_qkv, state_gdn, cache_mem_kv,
           page_table, mem_prompt, rel_bias, norm_mix, w_in, conv_a_w, conv_a_b, ln_a_g, ln_a_b, w_pa, gdn_conv_w,
           gdn_a_log, gdn_dt_bias, gdn_norm_g, w_pb, nsa_cmp_w, w_pc, w_o, norm_x, w_xq, w_xk, w_xv, w_xo,
           norm_mlp, w_ff1, w_ff2, norm_final):
    p = {"norm_mix": norm_mix, "conv_a_w": conv_a_w, "conv_a_b": conv_a_b, "ln_a_g": ln_a_g, "ln_a_b": ln_a_b,
         "gdn_conv_w": gdn_conv_w, "gdn_a_log": gdn_a_log, "gdn_dt_bias": gdn_dt_bias, "gdn_norm_g": gdn_norm_g}
    depth = w_in.shape[0]
    bp, tp, d = x_prompt.shape
    bs, ts, _ = x_sample.shape
    ts_pad = 8
    n_pages = page_table.shape[1]
    wb = cache_win_kv.shape[2]
    xp = x_prompt
    xs = jnp.pad(x_sample, ((0, 0), (0, ts_pad - ts), (0, 0)))
    ptab = _nsa_prompt_tables(rel_bias, tp)
    stab = _nsa_sample_tables(rel_bias, n_pages, ts_pad)
    outs = {k: [] for k in ("p_rows", "p_win", "p_conv", "p_qkv", "p_gdn", "p_mem",
                            "s_rows", "s_win", "s_conv", "s_qkv", "s_gdn")}
    for l in range(depth):
        lw = _layer_weights(l, w_in, w_pa, w_pb, w_pc, w_o, w_xq, w_xk, w_xv, w_xo, w_ff1, w_ff2)
        nsa_p = lambda qc, rows, win, glr: _nsa_prompt(qc, rows, win, glr, nsa_cmp_w[l], ptab)
        xp, conv_n, qkv_raw, s_n, rows, win = _mixers(
            xp, lw, p, l, jnp.zeros((bp, HALO, CONV_CH), F32), jnp.zeros((bp, 8, GDN_QKV), F32),
            jnp.zeros((bp, GDN_HEADS, GDN_D, GDN_D), F32), tp, tp, nsa_p)
        mem_kv = _matmul(_bf(mem_prompt.reshape(bp * N_MEM, d)), lw["xkv"]).reshape(bp, N_MEM, 2 * d)
        xp = _cross_attn(xp, norm_x[l], mem_kv, lw["xq"], lw["xo"])
        xp = _mlp(xp.reshape(bp * tp, d), norm_mlp[l], lw["ff1"], lw["ff2"]).reshape(bp, tp, d)
        outs["p_rows"].append(rows.reshape(bp, tp, 4, NSA_KV, NSA_DH))
        outs["p_win"].append(win[:, tp - min(WINDOW, tp):].reshape(bp, min(WINDOW, tp), 2, NSA_KV, NSA_DH))
        outs["p_conv"].append(conv_n)
        outs["p_qkv"].append(qkv_raw[:, tp - 3:])
        outs["p_gdn"].append(s_n)
        outs["p_mem"].append(mem_kv.reshape(bp, N_MEM, 2, X_HEADS, X_DH))
        cache_l = cache_nsa_kv[l].reshape(-1, PAGE, 4 * LANES)
        wbuf = cache_win_kv[l].reshape(bs, wb, 2 * LANES)
        nsa_s = lambda qc, rows, win, glr: _nsa_sample(cache_l, page_table, qc, rows, win, wbuf, glr,
                                                       nsa_cmp_w[l], stab, ts)
        conv_pad = jnp.pad(state_conv_a[l], ((0, 0), (HALO - (CONV_W - 1), 0), (0, 0)))
        qkv_pad = jnp.pad(state_conv_qkv[l], ((0, 0), (5, 0), (0, 0)))
        xs, conv_n, qkv_raw, s_n, rows, win = _mixers(
            xs, lw, p, l, conv_pad, qkv_pad, state_gdn[l], ts, GDN_CHUNK, nsa_s)
        mkv = cache_mem_kv[l].reshape(bs, N_MEM, 2 * d)
        xs = _cross_attn(xs, norm_x[l], mkv, lw["xq"], lw["xo"])
        xs = _mlp(xs.reshape(bs * ts_pad, d), norm_mlp[l], lw["ff1"], lw["ff2"]).reshape(bs, ts_pad, d)
        outs["s_rows"].append(rows[:, :ts].reshape(bs, ts, 4, NSA_KV, NSA_DH))
        new_win = jnp.concatenate([wbuf, win[:, :ts]], axis=1)[:, ts:]
        outs["s_win"].append(new_win.reshape(bs, wb, 2, NSA_KV, NSA_DH))
        outs["s_conv"].append(conv_n)
        outs["s_qkv"].append(qkv_raw[:, ts - 3:ts])
        outs["s_gdn"].append(s_n)
    y_prompt = _rmsnorm(xp.reshape(bp * tp, d), norm_final, F32).reshape(bp, tp, d)
    y_sample = _rmsnorm(xs.reshape(bs * ts_pad, d), norm_final, F32).reshape(bs, ts_pad, d)[:, :ts]
    st = lambda k: jnp.stack(outs[k], axis=0)
    return (y_prompt, y_sample, st("p_rows"), st("p_win"), st("p_conv"), st("p_qkv"), st("p_gdn"), st("p_mem"),
            st("s_rows"), st("s_win"), st("s_conv"), st("s_qkv"), st("s_gdn"))
```

```python
import functools
import math

import jax
import jax.numpy as jnp
import numpy as np
from jax import lax
from jax.experimental import pallas as pl
from jax.experimental.pallas import tpu as pltpu

F32 = jnp.float32
BF16 = jnp.bfloat16

D_MODEL = 1024
CONV_CH = 512
CONV_W = 31
GDN_HEADS = 4
GDN_D = 128
GDN_CHUNK = 64
GDN_QKV = 3 * GDN_HEADS * GDN_D
NSA_HEADS = 8
NSA_KV = 2
NSA_GQ = 4
NSA_DH = 64
L_CMP = 32
L_SEL = 64
N_SEL = 16
WINDOW = 512
Q_BLOCK = 128
FORCE_BONUS = 1e4
PAGE = 128
N_MEM = 256
X_HEADS = 4
X_DH = 256
D_FF = 4096
N_BUCKETS = 32
EPS = 1e-6
NEG = -1e30

LANES = 128
HALO = 32
VMEM_LIMIT = 48 * 1024 * 1024


def _bf(x):
    return x.astype(BF16)


def _dot(a, b):
    return jnp.dot(a, b, preferred_element_type=F32)


def _dot_nt(a, b):
    return lax.dot_general(a, b, (((1,), (1,)), ((), ())), preferred_element_type=F32)


def _split2(x):
    hi = _bf(x)
    return hi, _bf(x - hi.astype(F32))


def _dot3(a, b):
    ah, al = _split2(a)
    bh, bl = _split2(b)
    return (_dot(ah, bl) + _dot(al, bh)) + _dot(ah, bh)


def _sigmoid(x):
    return 1.0 / (1.0 + jnp.exp(-x))


def _silu(x):
    return x * _sigmoid(x)


def _params(*sem):
    return pltpu.CompilerParams(dimension_semantics=sem, vmem_limit_bytes=VMEM_LIMIT)


def _rms_kernel(x_ref, g_ref, o_ref):
    x = x_ref[...]
    y = x * lax.rsqrt(jnp.mean(x * x, axis=-1, keepdims=True) + EPS)
    o_ref[...] = (y * g_ref[...]).astype(o_ref.dtype)


def _rmsnorm(x, g, out_dtype):
    m, d = x.shape
    tm = min(m, 512)
    return pl.pallas_call(
        _rms_kernel,
        grid=(m // tm,),
        in_specs=[pl.BlockSpec((tm, d), lambda i: (i, 0)), pl.BlockSpec((1, d), lambda i: (0, 0))],
        out_specs=pl.BlockSpec((tm, d), lambda i: (i, 0)),
        out_shape=jax.ShapeDtypeStruct((m, d), out_dtype),
        compiler_params=_params("parallel"),
        name="rmsnorm",
    )(x, g.reshape(1, d))


def _mm_kernel(a_ref, w_ref, o_ref):
    o_ref[...] = _dot(a_ref[...], w_ref[...])


def _col_tile(n):
    for tn in (1024, 768, 512, 384, 256, 128):
        if n % tn == 0:
            return tn
    raise ValueError(n)


def _matmul(a, w):
    m, k = a.shape
    n = w.shape[1]
    tm = min(m, 1024)
    tn = _col_tile(n)
    return pl.pallas_call(
        _mm_kernel,
        grid=(m // tm, n // tn),
        in_specs=[pl.BlockSpec((tm, k), lambda i, j: (i, 0)), pl.BlockSpec((k, tn), lambda i, j: (0, j))],
        out_specs=pl.BlockSpec((tm, tn), lambda i, j: (i, j)),
        out_shape=jax.ShapeDtypeStruct((m, n), F32),
        compiler_params=_params("parallel", "parallel"),
        name="matmul",
    )(a, w)


def _conf_kernel(n_t, tt, tv, u_ref, halo_ref, st_ref, w_ref, b_ref, g_ref, lb_ref, o_ref, nb_ref, xc_ref):
    t = pl.program_id(1)
    u = u_ref[0]
    xc_ref[HALO:HALO + tt, :] = u[:, :CONV_CH] * _sigmoid(u[:, CONV_CH:])
    if n_t > 1:
        uh = halo_ref[0]
        gh = uh[:, :CONV_CH] * _sigmoid(uh[:, CONV_CH:])
        xc_ref[0:HALO, :] = jnp.where(t > 0, gh, st_ref[0])
    else:
        xc_ref[0:HALO, :] = st_ref[0]
    off = HALO - (CONV_W - 1)
    acc = xc_ref[off:off + tt, :] * w_ref[0:1, :]
    for i in range(1, CONV_W):
        acc = acc + xc_ref[off + i:off + i + tt, :] * w_ref[i:i + 1, :]
    y = acc + b_ref[...]
    mu = jnp.mean(y, axis=-1, keepdims=True)
    yc = y - mu
    var = jnp.mean(yc * yc, axis=-1, keepdims=True)
    ln = yc * lax.rsqrt(var + EPS) * g_ref[...] + lb_ref[...]
    o_ref[0] = _silu(ln)

    @pl.when(t == n_t - 1)
    def _():
        nb_ref[0] = xc_ref[tv:tv + HALO, :]


def _conformer(u_a, state_pad, w_dw, b_dw, ln_g, ln_b, n_valid_last):
    bsz, t_len, _ = u_a.shape
    tt = min(t_len, 256)
    n_t = t_len // tt
    hb = tt // HALO if n_t > 1 else 1
    halo_rows = HALO if n_t > 1 else tt
    w_pad = jnp.pad(w_dw, ((0, HALO - CONV_W), (0, 0)))
    row = lambda v: v.reshape(1, CONV_CH)
    kern = functools.partial(_conf_kernel, n_t, tt, n_valid_last)
    return pl.pallas_call(
        kern,
        grid=(bsz, n_t),
        in_specs=[
            pl.BlockSpec((1, tt, 2 * CONV_CH), lambda b, t: (b, t, 0)),
            pl.BlockSpec((1, halo_rows, 2 * CONV_CH), lambda b, t: (b, jnp.maximum(t * hb - 1, 0), 0)),
            pl.BlockSpec((1, HALO, CONV_CH), lambda b, t: (b, 0, 0)),
            pl.BlockSpec((HALO, CONV_CH), lambda b, t: (0, 0)),
            pl.BlockSpec((1, CONV_CH), lambda b, t: (0, 0)),
            pl.BlockSpec((1, CONV_CH), lambda b, t: (0, 0)),
            pl.BlockSpec((1, CONV_CH), lambda b, t: (0, 0)),
        ],
        out_specs=[
            pl.BlockSpec((1, tt, CONV_CH), lambda b, t: (b, t, 0)),
            pl.BlockSpec((1, HALO, CONV_CH), lambda b, t: (b, 0, 0)),
        ],
        out_shape=[
            jax.ShapeDtypeStruct((bsz, t_len, CONV_CH), F32),
            jax.ShapeDtypeStruct((bsz, HALO, CONV_CH), F32),
        ],
        scratch_shapes=[pltpu.VMEM((HALO + tt, CONV_CH), F32)],
        compiler_params=_params("parallel", "arbitrary"),
        name="conformer_conv",
    )(u_a, u_a, state_pad, w_pad, row(b_dw), row(ln_g), row(ln_b))


def _tri_inverse(a_list, ii, jj):
    eye = (ii == jj).astype(F32)
    a0 = [jnp.where((ii >> 3) == (jj >> 3), a, 0.0) for a in a_list]
    a2 = [_dot3(p, p) for p in a0]
    a4 = [_dot3(p, p) for p in a2]
    x = [_dot3(eye - p, eye + q) for p, q in zip(a0, a2)]
    x = [_dot3(p, eye + q) for p, q in zip(x, a4)]
    for sh in (3, 4, 5):
        mask = ((ii >> (sh + 1)) == (jj >> (sh + 1))) & ((ii >> sh) != (jj >> sh))
        t = [_dot3(jnp.where(mask, a, 0.0), p) for a, p in zip(a_list, x)]
        x = [p - _dot3(p, q) for p, q in zip(x, t)]
    return x


def _softplus(x):
    return jnp.maximum(x, 0.0) + jnp.log1p(jnp.exp(-jnp.abs(x)))


def _gdn_kernel(t_len, n_valid, alog_ref, dtb_ref, q_ref, k_ref, v_ref, z_ref, a_ref, b_ref,
                sq_ref, sk_ref, sv_ref, wq_ref, wk_ref, wv_ref, s0_ref, ng_ref,
                o_ref, sn_ref,
                xp_s, qn_s, kn_s, vn_s, g_s, be_s, vw_s, kcd_s, qg_s, kdt_s, qk_s, ge_s, oo_s):
    h = pl.program_id(1)
    n_chunks = t_len // GDN_CHUNK
    c_len = GDN_CHUNK

    def conv(x_ref, st_ref, w_ref):
        xp_s[0:8, :] = st_ref[0]
        xp_s[8:8 + t_len, :] = x_ref[0]
        acc = xp_s[5:5 + t_len, :] * w_ref[0:1, :]
        for i in range(1, 4):
            acc = acc + xp_s[5 + i:5 + i + t_len, :] * w_ref[i:i + 1, :]
        return _silu(acc)

    qc = conv(q_ref, sq_ref, wq_ref)
    qn_s[...] = qc * lax.rsqrt(jnp.sum(qc * qc, axis=-1, keepdims=True) + EPS) * (GDN_D ** -0.5)
    kc = conv(k_ref, sk_ref, wk_ref)
    kn_s[...] = kc * lax.rsqrt(jnp.sum(kc * kc, axis=-1, keepdims=True) + EPS)
    vn_s[...] = conv(v_ref, sv_ref, wv_ref)

    a_exp = jnp.exp(jnp.full((1, LANES), alog_ref[h], F32))
    g = -a_exp * _softplus(a_ref[0] + dtb_ref[h])
    beta = _sigmoid(b_ref[0])
    if n_valid < t_len:
        live = lax.broadcasted_iota(jnp.int32, (t_len, LANES), 0) < n_valid
        g = jnp.where(live, g, 0.0)
        beta = jnp.where(live, beta, 0.0)
    g_s[...] = g
    be_s[...] = beta

    ii = lax.broadcasted_iota(jnp.int32, (c_len, c_len), 0)
    jj = lax.broadcasted_iota(jnp.int32, (c_len, c_len), 1)
    incl = ii >= jj
    strict = ii > jj
    ltri = _bf(incl.astype(F32))
    unroll = 8 if n_chunks % 8 == 0 else 1

    def cumdecay(g_c):
        g_hi, g_lo = _split2(g_c)
        g_lo2 = _bf((g_c - g_hi.astype(F32)) - g_lo.astype(F32))
        return (_dot(ltri, g_lo2) + _dot(ltri, g_lo)) + _dot(ltri, g_hi)

    def prep(cu, carry):
        cs = [cu * unroll + u for u in range(unroll)]
        sls = [pl.ds(pl.multiple_of(c * c_len, c_len), c_len) for c in cs]
        each = lambda f, *ls: [f(*a) for a in zip(*ls)]
        q_l = [qn_s[sl, :] for sl in sls]
        k_l = [kn_s[sl, :] for sl in sls]
        b_l = [be_s[sl, :] for sl in sls]
        gc_l = [cumdecay(g_s[sl, :]) for sl in sls]
        dec_l = each(lambda gc: jnp.where(
            incl, jnp.exp(jnp.minimum(gc[:, 0:c_len] - gc.T[0:c_len, :], 0.0)), 0.0), gc_l)
        kb_l = each(lambda k, b: k * b, k_l, b_l)
        a_l = each(lambda kb, k, dec: jnp.where(strict, _dot_nt(_bf(kb), _bf(k)) * dec, 0.0), kb_l, k_l, dec_l)
        t_l = _tri_inverse(a_l, ii, jj)
        eg_l = each(jnp.exp, gc_l)
        rhs_l = [jnp.concatenate([vn_s[sl, :] * b, kb * eg], axis=1) for sl, b, kb, eg in zip(sls, b_l, kb_l, eg_l)]
        sol_l = each(_dot3, t_l, rhs_l)
        qk_l = each(lambda q, k, dec: jnp.where(incl, _dot_nt(_bf(q), _bf(k)) * dec, 0.0), q_l, k_l, dec_l)
        for c, sl, sol, qk, q, k, gc, eg in zip(cs, sls, sol_l, qk_l, q_l, k_l, gc_l, eg_l):
            g_end = gc[c_len - 1:c_len, :]
            vw_s[sl, :] = sol[:, :GDN_D]
            kcd_s[sl, :] = sol[:, GDN_D:]
            qk_s[c] = qk
            qg_s[sl, :] = q * eg
            kdt_s[c] = (k * jnp.exp(g_end - gc)).T
            ge_s[c] = jnp.broadcast_to(jnp.exp(g_end), (8, LANES))
        return carry

    lax.fori_loop(0, n_chunks // unroll, prep, 0)

    def step(c, s):
        sl = pl.ds(pl.multiple_of(c * c_len, c_len), c_len)
        sb = _bf(s)
        v_new = vw_s[sl, :] - _dot(_bf(kcd_s[sl, :]), sb)
        vb = _bf(v_new)
        oo_s[sl, :] = _dot(_bf(qg_s[sl, :]), sb) + _dot(_bf(qk_s[c]), vb)
        return s * ge_s[c][0:1, :] + _dot(_bf(kdt_s[c]), vb)

    s_fin = lax.fori_loop(0, n_chunks, step, s0_ref[0, 0])
    sn_ref[0, 0] = s_fin

    o = oo_s[...]
    y = o * lax.rsqrt(jnp.mean(o * o, axis=-1, keepdims=True) + EPS) * ng_ref[...]
    o_ref[0] = y * _silu(z_ref[0])


def _gated_deltanet(qkv, zab, state_pad, s0, w_conv_pad, a_log, dt_bias, norm_g, n_valid):
    bsz, t_len, _ = qkv.shape
    nh = GDN_HEADS
    blk = lambda off: pl.BlockSpec((1, t_len, GDN_D), lambda b, h, o=off: (b, 0, o + h))
    stb = lambda off: pl.BlockSpec((1, 8, GDN_D), lambda b, h, o=off: (b, 0, o + h))
    wb = lambda off: pl.BlockSpec((8, GDN_D), lambda b, h, o=off: (0, o + h))
    smem = pl.BlockSpec(memory_space=pltpu.SMEM)
    n_chunks = t_len // GDN_CHUNK
    seq = lambda: pltpu.VMEM((t_len, GDN_D), F32)
    kern = functools.partial(_gdn_kernel, t_len, n_valid)
    return pl.pallas_call(
        kern,
        grid=(bsz, nh),
        in_specs=[smem, smem, blk(0), blk(nh), blk(2 * nh), blk(0), blk(nh), blk(2 * nh),
                  stb(0), stb(nh), stb(2 * nh), wb(0), wb(nh), wb(2 * nh),
                  pl.BlockSpec((1, 1, GDN_D, GDN_D), lambda b, h: (b, h, 0, 0)),
                  pl.BlockSpec((1, GDN_D), lambda b, h: (0, 0))],
        out_specs=[pl.BlockSpec((1, t_len, GDN_D), lambda b, h: (b, 0, h)),
                   pl.BlockSpec((1, 1, GDN_D, GDN_D), lambda b, h: (b, h, 0, 0))],
        out_shape=[jax.ShapeDtypeStruct((bsz, t_len, nh * GDN_D), F32),
                   jax.ShapeDtypeStruct((bsz, nh, GDN_D, GDN_D), F32)],
        scratch_shapes=[pltpu.VMEM((8 + t_len, GDN_D), F32), seq(), seq(), seq(), seq(), seq(), seq(), seq(), seq(),
                        pltpu.VMEM((n_chunks, GDN_D, GDN_CHUNK), F32),
                        pltpu.VMEM((n_chunks, GDN_CHUNK, GDN_CHUNK), F32),
                        pltpu.VMEM((n_chunks, 8, LANES), F32), seq()],
        compiler_params=_params("parallel", "parallel"),
        name="gated_deltanet",
    )(a_log, dt_bias, qkv, qkv, qkv, zab, zab, zab, state_pad, state_pad, state_pad,
      w_conv_pad, w_conv_pad, w_conv_pad, s0, norm_g.reshape(1, GDN_D))


def _heads_to_rows(x, g, nt):
    lane = lax.broadcasted_iota(jnp.int32, (nt, LANES), 1)
    keep = (lane >= NSA_DH * g) & (lane < NSA_DH * (g + 1))
    parts = []
    for r in range(NSA_GQ):
        hh = NSA_GQ * g + r
        blk = x[:, (hh // 2) * LANES:(hh // 2 + 1) * LANES]
        if hh % 2 != g:
            blk = pltpu.roll(blk, NSA_DH, axis=1)
        parts.append(jnp.where(keep, blk, 0.0))
    return jnp.concatenate(parts, axis=0)


def _rows_to_heads(y, g, nt):
    outs = []
    for m in range(2):
        x0 = y[(2 * m) * nt:(2 * m + 1) * nt]
        x1 = y[(2 * m + 1) * nt:(2 * m + 2) * nt]
        if g == 1:
            x0 = pltpu.roll(x0, NSA_DH, axis=1)
        else:
            x1 = pltpu.roll(x1, NSA_DH, axis=1)
        outs.append(x0 + x1)
    return outs


def _masked_softmax_parts(parts, masks, axis):
    sm = [jnp.where(m, s, NEG) for s, m in zip(parts, masks)]
    mx = functools.reduce(jnp.maximum, [jnp.max(s, axis=axis, keepdims=True) for s in sm])
    es = [jnp.where(m, jnp.exp(s - mx), 0.0) for s, m in zip(sm, masks)]
    den = functools.reduce(lambda p, q: p + q, [jnp.sum(e, axis=axis, keepdims=True) for e in es])
    inv = 1.0 / jnp.maximum(den, 1e-30)
    return [e * inv for e in es]


def _bucket_np(rel):
    n = np.maximum(rel, 0)
    nf = np.maximum(n, 1).astype(np.float32)
    large = 16 + (np.log(nf / np.float32(16)) / np.float32(math.log(8.0)) * np.float32(16)).astype(np.int32)
    return np.where(n < 16, n, np.minimum(large, N_BUCKETS - 1)).astype(np.int32)


LOOKUP_TILE = 8192


def _lookup_kernel(idx_ref, tb_ref, o_ref):
    idx = idx_ref[...]
    acc = jnp.zeros(o_ref.shape, F32)
    for k in range(N_BUCKETS):
        acc = jnp.where(idx == k, tb_ref[:, k:k + 1], acc)
    o_ref[...] = acc


def _bias_lookup(rel_bias, idx_list):
    sizes = [int(np.prod(a.shape)) for a in idx_list]
    total = sum(sizes)
    padded = -(-total // LOOKUP_TILE) * LOOKUP_TILE
    flat = np.zeros((1, padded), np.int32)
    flat[0, :total] = np.concatenate([np.asarray(a, np.int32).reshape(-1) for a in idx_list])
    tab = pl.pallas_call(
        _lookup_kernel,
        grid=(padded // LOOKUP_TILE,),
        in_specs=[pl.BlockSpec((1, LOOKUP_TILE), lambda i: (0, i)),
                  pl.BlockSpec((NSA_HEADS, N_BUCKETS), lambda i: (0, 0))],
        out_specs=pl.BlockSpec((NSA_HEADS, LOOKUP_TILE), lambda i: (0, i)),
        out_shape=jax.ShapeDtypeStruct((NSA_HEADS, padded), F32),
        compiler_params=_params("parallel"),
        name="bias_lookup",
    )(jnp.asarray(flat), rel_bias.astype(F32).T)
    outs, off = [], 0
    for a, n in zip(idx_list, sizes):
        outs.append(tab[:, off:off + n].reshape((NSA_HEADS,) + tuple(a.shape)))
        off += n
    return outs


def _head_rows(tab):
    return tab.reshape(NSA_KV, NSA_GQ * tab.shape[1], tab.shape[2])


def _nsa_prompt_kernel(t_len, q_ref, kcmp_ref, vcmp_ref, kslc_ref, vslc_ref, kwin_ref, vwin_ref, gl_ref,
                       wk_ref, wv_ref, bce_ref, bco_ref, bcet_ref, bcot_ref, bnear_ref, bwin_ref,
                       o_ref, kce_s, kco_s, vce_s, vco_s):
    i = pl.program_id(1)
    nsb = t_len // L_SEL
    qb = Q_BLOCK
    rows = NSA_GQ * qb

    @pl.when(i == 0)
    def _():
        n2 = 2 * lax.broadcasted_iota(jnp.int32, (nsb, t_len), 0)
        cb = lax.broadcasted_iota(jnp.int32, (nsb, t_len), 1) >> 5
        kc = _bf(kcmp_ref[0])
        vc = _bf(vcmp_ref[0])
        wk = wk_ref[...]
        wv = wv_ref[...]
        kce_s[...] = _dot(_bf(jnp.where(cb == n2, wk, 0.0)), kc)
        kco_s[...] = _dot(_bf(jnp.where(cb == n2 + 1, wk, 0.0)), kc)
        vce_s[...] = _dot(_bf(jnp.where(cb == n2, wv, 0.0)), vc)
        vco_s[...] = _dot(_bf(jnp.where(cb == n2 + 1, wv, 0.0)), vc)

    q_all = q_ref[0] * (NSA_DH ** -0.5)
    gl = gl_ref[0]
    t0 = i * qb
    tq = t0 + (lax.broadcasted_iota(jnp.int32, (rows, 1), 0) & (qb - 1))
    eye_q = _bf((lax.broadcasted_iota(jnp.int32, (qb, qb), 0) == lax.broadcasted_iota(jnp.int32, (qb, qb), 1)).astype(F32))
    far_end = jnp.maximum(t0 - qb, 0)
    n_far = (far_end + 511) >> 9

    for g in range(NSA_KV):
        qg = _bf(_heads_to_rows(q_all, g, qb))
        kce = _bf(kce_s[...])
        kco = _bf(kco_s[...])

        ncol = lax.broadcasted_iota(jnp.int32, (rows, nsb), 1)
        me = (ncol * L_SEL + (L_CMP - 1)) <= tq
        mo = (ncol * L_SEL + (L_SEL - 1)) <= tq
        pe, po = _masked_softmax_parts(
            [_dot_nt(qg, kce) + bce_ref[0, g], _dot_nt(qg, kco) + bco_ref[0, g]], [me, mo], 1)
        o_c = _dot(_bf(pe), _bf(vce_s[...])) + _dot(_bf(po), _bf(vco_s[...]))

        nrow = lax.broadcasted_iota(jnp.int32, (nsb, rows), 0)
        tl = t0 + (lax.broadcasted_iota(jnp.int32, (nsb, rows), 1) & (qb - 1))
        pet, pot = _masked_softmax_parts(
            [_dot_nt(kce, qg) + bcet_ref[0, g], _dot_nt(kco, qg) + bcot_ref[0, g]],
            [(nrow * L_SEL + (L_CMP - 1)) <= tl, (nrow * L_SEL + (L_SEL - 1)) <= tl], 0)
        imp_e = pet[:, 0:qb] + pet[:, qb:2 * qb] + pet[:, 2 * qb:3 * qb] + pet[:, 3 * qb:4 * qb]
        imp_o = pot[:, 0:qb] + pot[:, qb:2 * qb] + pot[:, 2 * qb:3 * qb] + pot[:, 3 * qb:4 * qb]
        blk = lax.broadcasted_iota(jnp.int32, (nsb, qb), 0)
        cur = (t0 + lax.broadcasted_iota(jnp.int32, (nsb, qb), 1)) >> 6
        forced = (blk == 0) | (blk == cur) | (blk == cur - 1)
        score = jnp.where(blk <= cur, (imp_e + imp_o) + jnp.where(forced, FORCE_BONUS, 0.0), -1.0)
        rank = jnp.zeros((nsb, qb), F32)
        for j in range(nsb):
            sj = score[j:j + 1, :]
            ahead = (sj > score) | ((sj == score) & (blk > j))
            rank = rank + jnp.where(ahead, 1.0, 0.0)
        sel_t = jnp.where(rank < float(min(N_SEL, nsb)), 1.0, 0.0)
        sel = _bf(_dot_nt(eye_q, _bf(sel_t)))

        def far_tile(kt, carry):
            m_i, l_i, acc = carry
            k0 = pl.multiple_of(kt * 512, 512)
            s = _dot_nt(qg, _bf(kslc_ref[0, pl.ds(k0, 512), :]))
            ej = lax.broadcasted_iota(jnp.int32, (nsb, 512), 0)
            ec = k0 + lax.broadcasted_iota(jnp.int32, (nsb, 512), 1)
            expand = jnp.where((ej == (ec >> 6)) & (ec < far_end), 1.0, 0.0)
            mk = jnp.concatenate([_dot(sel, _bf(expand))] * NSA_GQ, axis=0) > 0.5
            s = jnp.where(mk, s, NEG)
            m_n = jnp.maximum(m_i, jnp.max(s, axis=1, keepdims=True))
            p = jnp.where(mk, jnp.exp(s - m_n), 0.0)
            alpha = jnp.exp(m_i - m_n)
            l_n = alpha * l_i + jnp.sum(p, axis=1, keepdims=True)
            acc_n = alpha * acc + _dot(_bf(p), _bf(vslc_ref[0, pl.ds(k0, 512), :]))
            return m_n, l_n, acc_n

        init = (jnp.full((rows, 1), NEG, F32), jnp.zeros((rows, 1), F32), jnp.zeros((rows, LANES), F32))
        m_i, l_i, acc = lax.fori_loop(0, n_far, far_tile, init)

        p0 = pl.multiple_of(jnp.maximum(t0 - qb, 0), qb)
        d0 = pl.multiple_of(t0, qb)
        s_near = jnp.concatenate([_dot_nt(qg, _bf(kslc_ref[0, pl.ds(p0, qb), :])),
                                  _dot_nt(qg, _bf(kslc_ref[0, pl.ds(d0, qb), :]))], axis=1) + bnear_ref[g]
        cn = lax.broadcasted_iota(jnp.int32, (nsb, 2 * qb), 1)
        jn = lax.broadcasted_iota(jnp.int32, (nsb, 2 * qb), 0)
        expand_n = jnp.where(jn == ((t0 - qb + cn) >> 6), 1.0, 0.0)
        mk = jnp.concatenate([_dot(sel, _bf(expand_n))] * NSA_GQ, axis=0) > 0.5
        cn_r = lax.broadcasted_iota(jnp.int32, (rows, 2 * qb), 1)
        mk = mk & ((t0 - qb + cn_r) <= tq) & ((t0 - qb + cn_r) >= 0)
        s_near = jnp.where(mk, s_near, NEG)
        m_n = jnp.maximum(m_i, jnp.max(s_near, axis=1, keepdims=True))
        p = jnp.where(mk, jnp.exp(s_near - m_n), 0.0)
        alpha = jnp.exp(m_i - m_n)
        l_n = alpha * l_i + jnp.sum(p, axis=1, keepdims=True)
        acc = alpha * acc + _dot(_bf(p[:, 0:qb]), _bf(vslc_ref[0, pl.ds(p0, qb), :])) \
            + _dot(_bf(p[:, qb:2 * qb]), _bf(vslc_ref[0, pl.ds(d0, qb), :]))
        o_s = acc * (1.0 / jnp.maximum(l_n, 1e-30))

        n_prev = WINDOW // qb
        starts = [pl.multiple_of(jnp.maximum(t0 + (j - n_prev) * qb, 0), qb) for j in range(n_prev + 1)]
        s_w = jnp.concatenate([_dot_nt(qg, _bf(kwin_ref[0, pl.ds(st, qb), :])) for st in starts], axis=1) + bwin_ref[g]
        cw = t0 - WINDOW + lax.broadcasted_iota(jnp.int32, (rows, WINDOW + qb), 1)
        mw = (cw <= tq) & (cw > tq - WINDOW) & (cw >= 0)
        (pw,) = _masked_softmax_parts([s_w], [mw], 1)
        o_w = _dot(_bf(pw[:, 0:qb]), _bf(vwin_ref[0, pl.ds(starts[0], qb), :]))
        for j in range(1, n_prev + 1):
            o_w = o_w + _dot(_bf(pw[:, j * qb:(j + 1) * qb]), _bf(vwin_ref[0, pl.ds(starts[j], qb), :]))

        gates = [_heads_to_rows(_sigmoid(gl[:, br * 512:(br + 1) * 512]), g, qb) for br in range(3)]
        comb = gates[0] * o_c + gates[1] * o_s + gates[2] * o_w
        blocks = _rows_to_heads(comb, g, qb)
        o_ref[0, :, (2 * g) * LANES:(2 * g + 1) * LANES] = blocks[0]
        o_ref[0, :, (2 * g + 1) * LANES:(2 * g + 2) * LANES] = blocks[1]


def _nsa_tables(rel_bias, t_len, n_pages, nt):
    nqb = t_len // Q_BLOCK
    nsb = t_len // L_SEL
    past = n_pages * PAGE
    t = np.arange(Q_BLOCK)
    tq = (np.arange(nqb)[:, None] * Q_BLOCK + t[None, :])[:, :, None]
    n = np.arange(nsb)[None, None, :]
    ts = np.arange(nt)[:, None]
    j = np.arange(2 * n_pages)[None, :]
    c = np.arange(PAGE)[None, :]
    idx = [
        _bucket_np(tq - (n * L_SEL + L_CMP - 1)),
        _bucket_np(tq - (n * L_SEL + L_SEL - 1)),
        _bucket_np(Q_BLOCK + t[:, None] - np.arange(2 * Q_BLOCK)[None, :]),
        _bucket_np(WINDOW + t[:, None] - np.arange(WINDOW + Q_BLOCK)[None, :]),
        _bucket_np(past + ts - (j * L_SEL + L_CMP - 1)),
        _bucket_np(past + ts - (j * L_SEL + L_SEL - 1)),
        _bucket_np(PAGE + ts - c),
        _bucket_np(ts - c),
        _bucket_np(WINDOW + ts - np.arange(WINDOW)[None, :]),
    ]
    ce, co, near, win, sce, sco, slast, snew, swin = _bias_lookup(rel_bias, idx)
    b31 = rel_bias.astype(F32)[N_BUCKETS - 1].reshape(NSA_KV, NSA_GQ, 1, 1)
    shift = lambda tab: (tab.reshape(NSA_KV, NSA_GQ, tab.shape[1], tab.shape[2]) - b31).reshape(
        NSA_KV, NSA_GQ * tab.shape[1], tab.shape[2])
    blocked = lambda tab: jnp.swapaxes(tab, 0, 1).reshape(nqb, NSA_KV, NSA_GQ * Q_BLOCK, nsb)
    bce, bco = blocked(ce), blocked(co)
    ptab = (bce, bco, jnp.swapaxes(bce, -1, -2), jnp.swapaxes(bco, -1, -2), shift(near), _head_rows(win))
    rows64 = lambda tab: tab.reshape(NSA_KV * NSA_GQ * nt, tab.shape[-1])
    stab = (rows64(_head_rows(sce)), rows64(_head_rows(sco)), rows64(shift(slast)), rows64(shift(snew)),
            rows64(_head_rows(swin)), rows64(_head_rows(snew)))
    return ptab, stab


def _nsa_prompt(qc, rows, win, glr, w_pos, tables):
    bsz, t_len, _ = qc.shape
    nqb = t_len // Q_BLOCK
    nsb = t_len // L_SEL
    bce, bco, bcet, bcot, near, wtab = tables
    wk = jnp.tile(w_pos[0], t_len // L_CMP).reshape(1, t_len)
    wv = jnp.tile(w_pos[1], t_len // L_CMP).reshape(1, t_len)
    seq = lambda c: pl.BlockSpec((1, t_len, LANES), lambda b, i, c=c: (b, 0, c))
    full = lambda a: pl.BlockSpec(a.shape, lambda b, i, nd=a.ndim: (0,) * nd)
    per_i = lambda a: pl.BlockSpec((1,) + a.shape[1:], lambda b, i, nd=a.ndim: (i,) + (0,) * (nd - 1))
    kern = functools.partial(_nsa_prompt_kernel, t_len)
    return pl.pallas_call(
        kern,
        grid=(bsz, nqb),
        in_specs=[pl.BlockSpec((1, Q_BLOCK, 512), lambda b, i: (b, i, 0)),
                  seq(0), seq(1), seq(2), seq(3), seq(0), seq(1),
                  pl.BlockSpec((1, Q_BLOCK, 1536), lambda b, i: (b, i, 0)),
                  full(wk), full(wv), per_i(bce), per_i(bco), per_i(bcet), per_i(bcot), full(near), full(wtab)],
        out_specs=pl.BlockSpec((1, Q_BLOCK, 512), lambda b, i: (b, i, 0)),
        out_shape=jax.ShapeDtypeStruct((bsz, t_len, 512), F32),
        scratch_shapes=[pltpu.VMEM((nsb, LANES), F32)] * 4,
        compiler_params=_params("parallel", "arbitrary"),
        name="nsa_prompt",
    )(qc, rows, rows, rows, rows, win, win, glr, wk, wv, bce, bco, bcet, bcot, near, wtab)


def _nsa_sample_kernel(layer, n_pages, n_new, pt_ref, cache_ref, q_ref, rows_ref, wnew_ref, wbuf_ref, gl_ref,
                       wpool_ref, bce_ref, bco_ref, blast_ref, bnew_ref, bwin_ref, bwnew_ref,
                       o_ref, cmp_s, slc_s, pool_s, exp_s, pad_s, sem):
    b = pl.program_id(0)
    nb = pl.num_programs(0)
    nt = 8
    past = n_pages * PAGE
    nblk = 2 * n_pages
    rows = NSA_KV * NSA_GQ * nt
    half = 2 * LANES

    def page_copy(seq, p, part, buf, s):
        return pltpu.make_async_copy(
            cache_ref.at[layer, pt_ref[seq, p], pl.ds(part * half, half), :],
            buf.at[:, pl.ds(pl.multiple_of(p * PAGE, PAGE), PAGE)], s)

    def start_gather(seq, part, buf, s):
        def body(p, c):
            page_copy(seq, p, part, buf, s).start()
            return c
        lax.fori_loop(0, n_pages, body, 0)

    def wait_gather(seq, part, buf, s):
        def body(p, c):
            page_copy(seq, p, part, buf, s).wait()
            return c
        lax.fori_loop(0, n_pages, body, 0)

    @pl.when(b == 0)
    def _():
        start_gather(0, 0, cmp_s, sem.at[0])
        start_gather(0, 1, slc_s, sem.at[1])
        cb = lax.broadcasted_iota(jnp.int32, (past, nblk), 0) >> 5
        j2 = 2 * lax.broadcasted_iota(jnp.int32, (past, nblk), 1)
        pool_s[0] = _bf(jnp.where(cb == j2, 1.0, 0.0))
        pool_s[1] = _bf(jnp.where(cb == j2 + 1, 1.0, 0.0))
        ej = lax.broadcasted_iota(jnp.int32, (nblk, past), 0)
        ec = lax.broadcasted_iota(jnp.int32, (nblk, past), 1) >> 6
        exp_s[...] = _bf(jnp.where(ej == ec, 1.0, 0.0))
        pad_s[...] = jnp.zeros(pad_s.shape, F32)

    q_all = q_ref[0] * (NSA_DH ** -0.5)
    qq = _bf(jnp.concatenate([_heads_to_rows(q_all, g, nt) for g in range(NSA_KV)], axis=0))
    tr = lax.broadcasted_iota(jnp.int32, (rows, 1), 0) & (nt - 1)

    wait_gather(b, 0, cmp_s, sem.at[0])
    wp = wpool_ref[...]
    ks = _bf(cmp_s[0:LANES, :] * wp[0:1, :])
    vs = _bf(cmp_s[LANES:half, :] * wp[1:2, :])
    kce = _bf(_dot(ks, pool_s[0]))
    kco = _bf(_dot(ks, pool_s[1]))
    vce = _bf(_dot(vs, pool_s[0]))
    vco = _bf(_dot(vs, pool_s[1]))

    @pl.when(b + 1 < nb)
    def _():
        start_gather(b + 1, 0, cmp_s, sem.at[0])

    se = _dot(qq, kce) + bce_ref[...]
    so = _dot(qq, kco) + bco_ref[...]
    mx = jnp.maximum(jnp.max(se, axis=1, keepdims=True), jnp.max(so, axis=1, keepdims=True))
    ee = jnp.exp(se - mx)
    eo = jnp.exp(so - mx)
    inv = 1.0 / (jnp.sum(ee, axis=1, keepdims=True) + jnp.sum(eo, axis=1, keepdims=True))
    pe = ee * inv
    po = eo * inv
    o_c = _dot_nt(_bf(pe), vce) + _dot_nt(_bf(po), vco)

    def head_sum(pr):
        return jnp.concatenate(
            [pr[g * 4 * nt:g * 4 * nt + nt] + pr[g * 4 * nt + nt:g * 4 * nt + 2 * nt]
             + pr[g * 4 * nt + 2 * nt:g * 4 * nt + 3 * nt] + pr[g * 4 * nt + 3 * nt:g * 4 * nt + 4 * nt]
             for g in range(NSA_KV)], axis=0)

    jcol = lax.broadcasted_iota(jnp.int32, (NSA_KV * nt, nblk), 1)
    forced = (jcol == 0) | (jcol == nblk - 1)
    score = (head_sum(pe) + head_sum(po)) + jnp.where(forced, FORCE_BONUS, 0.0)
    rank = jnp.where(FORCE_BONUS > score, 1.0, 0.0)
    for j in range(nblk):
        sj = score[:, j:j + 1]
        ahead = (sj > score) | ((sj == score) & (jcol > j))
        rank = rank + jnp.where(ahead, 1.0, 0.0)
    sel = jnp.where(rank < float(N_SEL), 1.0, 0.0)
    sel_rows = jnp.concatenate([sel[g * nt:(g + 1) * nt] for g in range(NSA_KV) for _ in range(NSA_GQ)], axis=0)

    new = rows_ref[0]
    wnew = wnew_ref[0]
    pad_s[0, 0:nt, :] = new[:, 2 * LANES:3 * LANES]
    pad_s[1, 0:nt, :] = new[:, 3 * LANES:4 * LANES]
    pad_s[2, 0:nt, :] = wnew[:, 0:LANES]
    pad_s[3, 0:nt, :] = wnew[:, LANES:2 * LANES]
    tc = lax.broadcasted_iota(jnp.int32, (rows, LANES), 1)
    mnew = (tc <= tr) & (tc < n_new)

    wait_gather(b, 1, slc_s, sem.at[1])
    s_all = _dot(qq, _bf(slc_s[0:LANES, :]))
    mk = _dot(_bf(sel_rows), exp_s[...]) > 0.5
    far = past - PAGE
    p_far, p_last, p_new = _masked_softmax_parts(
        [s_all[:, :far], s_all[:, far:] + blast_ref[...], _dot_nt(qq, _bf(pad_s[0])) + bnew_ref[...]],
        [mk[:, :far], mk[:, far:], mnew], 1)
    o_s = _dot_nt(_bf(jnp.concatenate([p_far, p_last], axis=1)), _bf(slc_s[LANES:half, :])) \
        + _dot(_bf(p_new), _bf(pad_s[1]))

    wb = wbuf_ref[0, 0]
    cw = lax.broadcasted_iota(jnp.int32, (rows, WINDOW), 1)
    pw, pn = _masked_softmax_parts(
        [_dot(qq, _bf(wb[0:LANES, :])) + bwin_ref[...], _dot_nt(qq, _bf(pad_s[2])) + bwnew_ref[...]],
        [cw > tr, mnew], 1)
    o_w = _dot_nt(_bf(pw), _bf(wb[LANES:half, :])) + _dot(_bf(pn), _bf(pad_s[3]))

    gl = gl_ref[0]
    gates = [jnp.concatenate([_heads_to_rows(_sigmoid(gl[:, br * 512:(br + 1) * 512]), g, nt)
                              for g in range(NSA_KV)], axis=0) for br in range(3)]
    comb = gates[0] * o_c + gates[1] * o_s + gates[2] * o_w
    for g in range(NSA_KV):
        blocks = _rows_to_heads(comb[g * 4 * nt:(g + 1) * 4 * nt], g, nt)
        o_ref[0, :, (2 * g) * LANES:(2 * g + 1) * LANES] = blocks[0]
        o_ref[0, :, (2 * g + 1) * LANES:(2 * g + 2) * LANES] = blocks[1]

    @pl.when(b + 1 < nb)
    def _():
        start_gather(b + 1, 1, slc_s, sem.at[1])


def _nsa_sample(layer, cache_t, page_table, qc, rows, wnew, wbuf_t, glr, w_pos, tables, n_new):
    bsz, n_pages = page_table.shape
    nt = qc.shape[1]
    past = n_pages * PAGE
    wpool = jnp.tile(w_pos, (1, past // L_CMP))
    full = lambda a: pl.BlockSpec(a.shape, lambda b, pt, nd=a.ndim: (0,) * nd)
    per_b = lambda a: pl.BlockSpec((1,) + a.shape[1:], lambda b, pt, nd=a.ndim: (b,) + (0,) * (nd - 1))
    kern = functools.partial(_nsa_sample_kernel, layer, n_pages, n_new)
    grid_spec = pltpu.PrefetchScalarGridSpec(
        num_scalar_prefetch=1,
        grid=(bsz,),
        in_specs=[pl.BlockSpec(memory_space=pl.ANY),
                  per_b(qc), per_b(rows), per_b(wnew),
                  pl.BlockSpec((1, 1) + wbuf_t.shape[2:], lambda b, pt: (layer, b, 0, 0)),
                  per_b(glr), full(wpool)] + [full(t) for t in tables],
        out_specs=pl.BlockSpec((1, nt, 512), lambda b, pt: (b, 0, 0)),
        scratch_shapes=[pltpu.VMEM((2 * LANES, past), F32), pltpu.VMEM((2 * LANES, past), F32),
                        pltpu.VMEM((2, past, 2 * n_pages), BF16), pltpu.VMEM((2 * n_pages, past), BF16),
                        pltpu.VMEM((4, LANES, LANES), F32), pltpu.SemaphoreType.DMA((2,))],
    )
    return pl.pallas_call(
        kern,
        grid_spec=grid_spec,
        out_shape=jax.ShapeDtypeStruct((bsz, nt, 512), F32),
        compiler_params=_params("arbitrary"),
        name="nsa_sample",
    )(page_table, cache_t, qc, rows, wnew, wbuf_t, glr, wpool, *tables)


def _mixout_kernel(x_ref, ca_ref, ob_ref, oc_ref, ug_ref, wpa_ref, wpb_ref, wpc_ref, wo_ref, o_ref):
    ug = ug_ref[...]
    y = _sigmoid(ug[:, 0:D_MODEL]) * _dot(_bf(ca_ref[...]), wpa_ref[...])
    y = y + _sigmoid(ug[:, D_MODEL:2 * D_MODEL]) * _dot(_bf(ob_ref[...]), wpb_ref[...])
    y = y + _sigmoid(ug[:, 2 * D_MODEL:3 * D_MODEL]) * _dot(_bf(oc_ref[...]), wpc_ref[...])
    o_ref[...] = x_ref[...] + _dot(_bf(y), wo_ref[...])


def _mixout(x, ca, ob, oc, ug, wpa, wpb, wpc, wo):
    m = x.shape[0]
    tm = min(m, 512)
    rowblk = lambda n: pl.BlockSpec((tm, n), lambda i: (i, 0))
    full = lambda a: pl.BlockSpec(a.shape, lambda i: (0, 0))
    return pl.pallas_call(
        _mixout_kernel,
        grid=(m // tm,),
        in_specs=[rowblk(D_MODEL), rowblk(512), rowblk(512), rowblk(512), rowblk(3 * D_MODEL),
                  full(wpa), full(wpb), full(wpc), full(wo)],
        out_specs=rowblk(D_MODEL),
        out_shape=jax.ShapeDtypeStruct((m, D_MODEL), F32),
        compiler_params=_params("parallel"),
        name="mixer_out",
    )(x, ca, ob, oc, ug, wpa, wpb, wpc, wo)


def _xattn_kernel(x_ref, g_ref, kv_ref, wq_ref, wo_ref, o_ref):
    x = x_ref[0]
    h = _bf(x * lax.rsqrt(jnp.mean(x * x, axis=-1, keepdims=True) + EPS) * g_ref[...])
    q = _dot(h, wq_ref[...])
    kv = kv_ref[0]
    outs = []
    for hd in range(X_HEADS):
        qh = _bf(q[:, hd * X_DH:(hd + 1) * X_DH])
        kh = _bf(kv[:, hd * X_DH:(hd + 1) * X_DH])
        vh = _bf(kv[:, D_MODEL + hd * X_DH:D_MODEL + (hd + 1) * X_DH])
        s = _dot_nt(qh, kh) * (X_DH ** -0.5)
        e = jnp.exp(s - jnp.max(s, axis=-1, keepdims=True))
        pr = e * (1.0 / jnp.sum(e, axis=-1, keepdims=True))
        outs.append(_dot(_bf(pr), vh))
    o = jnp.concatenate(outs, axis=1)
    o_ref[0] = x + _dot(_bf(o), wo_ref[...])


def _cross_attn(x, g, mem_kv, wq, wo):
    bsz, t_len, d = x.shape
    tt = min(t_len, 512)
    full = lambda a: pl.BlockSpec(a.shape, lambda b, t: (0, 0))
    return pl.pallas_call(
        _xattn_kernel,
        grid=(bsz, t_len // tt),
        in_specs=[pl.BlockSpec((1, tt, d), lambda b, t: (b, t, 0)),
                  pl.BlockSpec((1, d), lambda b, t: (0, 0)),
                  pl.BlockSpec((1, N_MEM, 2 * d), lambda b, t: (b, 0, 0)),
                  full(wq), full(wo)],
        out_specs=pl.BlockSpec((1, tt, d), lambda b, t: (b, t, 0)),
        out_shape=jax.ShapeDtypeStruct((bsz, t_len, d), F32),
        compiler_params=_params("parallel", "parallel"),
        name="cross_attn",
    )(x, g.reshape(1, d), mem_kv, wq, wo)


FF_CHUNK = 1024


def _mlp_kernel(n_k, x_ref, g_ref, w1_ref, w2_ref, o_ref, h_s, acc_s):
    k = pl.program_id(1)

    @pl.when(k == 0)
    def _():
        x = x_ref[...]
        h_s[...] = _bf(x * lax.rsqrt(jnp.mean(x * x, axis=-1, keepdims=True) + EPS) * g_ref[...])
        acc_s[...] = x

    a = jnp.maximum(_dot(h_s[...], w1_ref[...]), 0.0)
    acc_s[...] += _dot(_bf(a * a), w2_ref[...])

    @pl.when(k == n_k - 1)
    def _():
        o_ref[...] = acc_s[...]


def _mlp(x, g, w1, w2):
    m, d = x.shape
    tm = min(m, 1024)
    n_k = D_FF // FF_CHUNK
    return pl.pallas_call(
        functools.partial(_mlp_kernel, n_k),
        grid=(m // tm, n_k),
        in_specs=[pl.BlockSpec((tm, d), lambda i, k: (i, 0)),
                  pl.BlockSpec((1, d), lambda i, k: (0, 0)),
                  pl.BlockSpec((d, FF_CHUNK), lambda i, k: (0, k)),
                  pl.BlockSpec((FF_CHUNK, d), lambda i, k: (k, 0))],
        out_specs=pl.BlockSpec((tm, d), lambda i, k: (i, 0)),
        out_shape=jax.ShapeDtypeStruct((m, d), F32),
        scratch_shapes=[pltpu.VMEM((tm, d), BF16), pltpu.VMEM((tm, d), F32)],
        compiler_params=_params("parallel", "arbitrary"),
        name="sq_relu_mlp",
    )(x, g.reshape(1, d), w1, w2)


A_COLS = 2 * CONV_CH
B0 = A_COLS
Z0 = B0 + GDN_QKV
AB0 = Z0 + GDN_HEADS * GDN_D
C0 = AB0 + 2 * GDN_HEADS
KV0 = C0 + NSA_HEADS * NSA_DH
GL0 = KV0 + 6 * NSA_KV * NSA_DH
G0 = GL0 + 3 * NSA_HEADS
N_IN = G0 + 3 * D_MODEL


def _layer_weights(l, w_in, w_pa, w_pb, w_pc, w_o, w_xq, w_xk, w_xv, w_xo, w_ff1, w_ff2):
    w = w_in[l]
    rep = lambda cols, n: jnp.repeat(cols, n, axis=1)
    zab = jnp.concatenate([w[:, Z0:AB0], rep(w[:, AB0:AB0 + GDN_HEADS], GDN_D),
                           rep(w[:, AB0 + GDN_HEADS:C0], GDN_D)], axis=1)
    return {
        "a": _bf(w[:, 0:A_COLS]),
        "qkv": _bf(w[:, B0:Z0]),
        "zab": _bf(zab),
        "q": _bf(w[:, C0:KV0]),
        "rows": _bf(w[:, KV0:KV0 + 4 * LANES]),
        "win": _bf(w[:, KV0 + 4 * LANES:GL0]),
        "glr": _bf(rep(w[:, GL0:G0], NSA_DH)),
        "g": _bf(w[:, G0:N_IN]),
        "pa": _bf(w_pa[l]), "pb": _bf(w_pb[l]), "pc": _bf(w_pc[l]), "o": _bf(w_o[l]),
        "xq": _bf(w_xq[l]), "xo": _bf(w_xo[l]),
        "xkv": _bf(jnp.concatenate([w_xk[l], w_xv[l]], axis=1)),
        "ff1": _bf(w_ff1[l]), "ff2": _bf(w_ff2[l]),
    }


def _mixers(x, lw, p, l, conv_state_pad, qkv_state_pad, s0, n_valid, gdn_len, nsa_fn):
    bsz, t_len, d = x.shape
    m = bsz * t_len
    x2 = x.reshape(m, d)
    h = _rmsnorm(x2, p["norm_mix"][l], BF16)
    proj = lambda name: _matmul(h, lw[name])
    u_a = proj("a").reshape(bsz, t_len, -1)
    qkv = proj("qkv").reshape(bsz, t_len, -1)
    zab = proj("zab").reshape(bsz, t_len, -1)
    qc = proj("q").reshape(bsz, t_len, -1)
    rows = proj("rows").reshape(bsz, t_len, -1)
    win = proj("win").reshape(bsz, t_len, -1)
    glr = proj("glr").reshape(bsz, t_len, -1)
    ug = proj("g")

    ca, conv_new = _conformer(u_a, conv_state_pad, p["conv_a_w"][l], p["conv_a_b"][l], p["ln_a_g"][l],
                              p["ln_a_b"][l], n_valid if n_valid < t_len else min(t_len, 256))
    pad_t = ((0, 0), (0, gdn_len - t_len), (0, 0))
    w_conv_pad = jnp.pad(p["gdn_conv_w"][l], ((0, 4), (0, 0)))
    ob, s_new = _gated_deltanet(jnp.pad(qkv, pad_t), jnp.pad(zab, pad_t), qkv_state_pad, s0, w_conv_pad,
                                p["gdn_a_log"][l], p["gdn_dt_bias"][l], p["gdn_norm_g"][l],
                                n_valid if n_valid < t_len else gdn_len)
    ob = ob[:, :t_len]
    oc = nsa_fn(qc, rows, win, glr)
    x_new = _mixout(x2, ca.reshape(m, -1), ob.reshape(m, -1), oc.reshape(m, -1), ug,
                    lw["pa"], lw["pb"], lw["pc"], lw["o"])
    return x_new.reshape(bsz, t_len, d), conv_new[:, HALO - (CONV_W - 1):], qkv, s_new, rows, win


def kernel(x_prompt, x_sample, cache_nsa_kv, cache_win_kv, state_conv_a, state_conv_qkv, state_gdn, cache_mem_kv,
           page_table, mem_prompt, rel_bias, norm_mix, w_in, conv_a_w, conv_a_b, ln_a_g, ln_a_b, w_pa, gdn_conv_w,
           gdn_a_log, gdn_dt_bias, gdn_norm_g, w_pb, nsa_cmp_w, w_pc, w_o, norm_x, w_xq, w_xk, w_xv, w_xo,
           norm_mlp, w_ff1, w_ff2, norm_final):
    p = {"norm_mix": norm_mix, "conv_a_w": conv_a_w, "conv_a_b": conv_a_b, "ln_a_g": ln_a_g, "ln_a_b": ln_a_b,
         "gdn_conv_w": gdn_conv_w, "gdn_a_log": gdn_a_log, "gdn_dt_bias": gdn_dt_bias, "gdn_norm_g": gdn_norm_g}
    depth = w_in.shape[0]
    bp, tp, d = x_prompt.shape
    bs, ts, _ = x_sample.shape
    ts_pad = 8
    n_pages = page_table.shape[1]
    wb = cache_win_kv.shape[2]
    xp = x_prompt
    xs = jnp.pad(x_sample, ((0, 0), (0, ts_pad - ts), (0, 0)))
    ptab, stab = _nsa_tables(rel_bias, tp, n_pages, ts_pad)
    cache_t = jnp.transpose(cache_nsa_kv, (0, 1, 3, 4, 5, 2)).reshape(depth, -1, 4 * LANES, PAGE)
    wbuf_t = jnp.transpose(cache_win_kv, (0, 1, 3, 4, 5, 2)).reshape(depth, bs, 2 * LANES, wb)
    outs = {k: [] for k in ("p_rows", "p_win", "p_conv", "p_qkv", "p_gdn", "p_mem",
                            "s_rows", "s_win", "s_conv", "s_qkv", "s_gdn")}
    for l in range(depth):
        lw = _layer_weights(l, w_in, w_pa, w_pb, w_pc, w_o, w_xq, w_xk, w_xv, w_xo, w_ff1, w_ff2)
        nsa_p = lambda qc, rows, win, glr: _nsa_prompt(qc, rows, win, glr, nsa_cmp_w[l], ptab)
        xp, conv_n, qkv_raw, s_n, rows, win = _mixers(
            xp, lw, p, l, jnp.zeros((bp, HALO, CONV_CH), F32), jnp.zeros((bp, 8, GDN_QKV), F32),
            jnp.zeros((bp, GDN_HEADS, GDN_D, GDN_D), F32), tp, tp, nsa_p)
        mem_kv = _matmul(_bf(mem_prompt.reshape(bp * N_MEM, d)), lw["xkv"]).reshape(bp, N_MEM, 2 * d)
        xp = _cross_attn(xp, norm_x[l], mem_kv, lw["xq"], lw["xo"])
        xp = _mlp(xp.reshape(bp * tp, d), norm_mlp[l], lw["ff1"], lw["ff2"]).reshape(bp, tp, d)
        outs["p_rows"].append(rows.reshape(bp, tp, 4, NSA_KV, NSA_DH))
        outs["p_win"].append(win[:, tp - min(WINDOW, tp):].reshape(bp, min(WINDOW, tp), 2, NSA_KV, NSA_DH))
        outs["p_conv"].append(conv_n)
        outs["p_qkv"].append(qkv_raw[:, tp - 3:])
        outs["p_gdn"].append(s_n)
        outs["p_mem"].append(mem_kv.reshape(bp, N_MEM, 2, X_HEADS, X_DH))
        nsa_s =lambda qc, rows, win, glr: _nsa_sample(l, cache_t, page_table, qc, rows, win, wbuf_t, glr,
                                                       nsa_cmp_w[l], stab, ts)
        conv_pad = jnp.pad(state_conv_a[l], ((0, 0), (HALO - (CONV_W - 1), 0), (0, 0)))
        qkv_pad = jnp.pad(state_conv_qkv[l], ((0, 0), (5, 0), (0, 0)))
        xs, conv_n, qkv_raw, s_n, rows, win = _mixers(
            xs, lw, p, l, conv_pad, qkv_pad, state_gdn[l], ts, GDN_CHUNK, nsa_s)
        mkv = cache_mem_kv[l].reshape(bs, N_MEM, 2 * d)
        xs = _cross_attn(xs, norm_x[l], mkv, lw["xq"], lw["xo"])
        xs = _mlp(xs.reshape(bs * ts_pad, d), norm_mlp[l], lw["ff1"], lw["ff2"]).reshape(bs, ts_pad, d)
        outs["s_rows"].append(rows[:, :ts].reshape(bs, ts, 4, NSA_KV, NSA_DH))
        win_new = win[:, :ts].reshape(bs, ts, 2, NSA_KV, NSA_DH)
        outs["s_win"].append(jnp.concatenate([cache_win_kv[l], win_new], axis=1)[:, ts:])
        outs["s_conv"].append(conv_n)
        outs["s_qkv"].append(qkv_raw[:, ts - 3:ts])
        outs["s_gdn"].append(s_n)
    y_prompt = _rmsnorm(xp.reshape(bp * tp, d), norm_final, F32).reshape(bp, tp, d)
    y_sample = _rmsnorm(xs.reshape(bs * ts_pad, d), norm_final, F32).reshape(bs, ts_pad, d)[:, :ts]
    st = lambda k: jnp.stack(outs[k], axis=0)
    return (y_prompt, y_sample, st("p_rows"), st("p_win"), st("p_conv"), st("p_qkv"), st("p_gdn"), st("p_mem"),
            st("s_rows"), st("s_win"), st("s_conv"), st("s_qkv"), st("s_gdn"))
```

```python
import functools
import math

import jax
import jax.numpy as jnp
import numpy as np
from jax import lax
from jax.experimental import pallas as pl
from jax.experimental.pallas import tpu as pltpu

F32 = jnp.float32
BF16 = jnp.bfloat16

D_MODEL = 1024
CONV_CH = 512
CONV_W = 31
GDN_HEADS = 4
GDN_D = 128
GDN_CHUNK = 64
GDN_QKV = 3 * GDN_HEADS * GDN_D
NSA_HEADS = 8
NSA_KV = 2
NSA_GQ = 4
NSA_DH = 64
L_CMP = 32
L_SEL = 64
N_SEL = 16
WINDOW = 512
Q_BLOCK = 128
FORCE_BONUS = 1e4
PAGE = 128
N_MEM = 256
X_HEADS = 4
X_DH = 256
D_FF = 4096
N_BUCKETS = 32
EPS = 1e-6
NEG = -1e30

LANES = 128
HALO = 32
VMEM_LIMIT = 48 * 1024 * 1024


def _bf(x):
    return x.astype(BF16)


def _dot(a, b):
    return jnp.dot(a, b, preferred_element_type=F32)


def _dot_nt(a, b):
    return lax.dot_general(a, b, (((1,), (1,)), ((), ())), preferred_element_type=F32)


def _split2(x):
    hi = _bf(x)
    return hi, _bf(x - hi.astype(F32))


def _dot3(a, b):
    ah, al = _split2(a)
    bh, bl = _split2(b)
    return (_dot(ah, bl) + _dot(al, bh)) + _dot(ah, bh)


def _sigmoid(x):
    return 1.0 / (1.0 + jnp.exp(-x))


def _silu(x):
    return x * _sigmoid(x)


def _params(*sem):
    return pltpu.CompilerParams(dimension_semantics=sem, vmem_limit_bytes=VMEM_LIMIT)


def _rms_kernel(x_ref, g_ref, o_ref):
    x = x_ref[...]
    y = x * lax.rsqrt(jnp.mean(x * x, axis=-1, keepdims=True) + EPS)
    o_ref[...] = (y * g_ref[...]).astype(o_ref.dtype)


def _rmsnorm(x, g, out_dtype):
    m, d = x.shape
    tm = min(m, 512)
    return pl.pallas_call(
        _rms_kernel,
        grid=(m // tm,),
        in_specs=[pl.BlockSpec((tm, d), lambda i: (i, 0)), pl.BlockSpec((1, d), lambda i: (0, 0))],
        out_specs=pl.BlockSpec((tm, d), lambda i: (i, 0)),
        out_shape=jax.ShapeDtypeStruct((m, d), out_dtype),
        compiler_params=_params("parallel"),
        name="rmsnorm",
    )(x, g.reshape(1, d))


def _mm_kernel(a_ref, w_ref, o_ref):
    o_ref[...] = _dot(a_ref[...], w_ref[...])


def _col_tile(n):
    for tn in (1024, 768, 512, 384, 256, 128):
        if n % tn == 0:
            return tn
    raise ValueError(n)


def _matmul(a, w):
    m, k = a.shape
    n = w.shape[1]
    tm = min(m, 1024)
    tn = _col_tile(n)
    return pl.pallas_call(
        _mm_kernel,
        grid=(m // tm, n // tn),
        in_specs=[pl.BlockSpec((tm, k), lambda i, j: (i, 0)), pl.BlockSpec((k, tn), lambda i, j: (0, j))],
        out_specs=pl.BlockSpec((tm, tn), lambda i, j: (i, j)),
        out_shape=jax.ShapeDtypeStruct((m, n), F32),
        compiler_params=_params("parallel", "parallel"),
        name="matmul",
    )(a, w)


def _conf_kernel(n_t, tt, tv, u_ref, halo_ref, st_ref, w_ref, b_ref, g_ref, lb_ref, o_ref, nb_ref, xc_ref):
    t = pl.program_id(1)
    u = u_ref[0]
    xc_ref[HALO:HALO + tt, :] = u[:, :CONV_CH] * _sigmoid(u[:, CONV_CH:])
    if n_t > 1:
        uh = halo_ref[0]
        gh = uh[:, :CONV_CH] * _sigmoid(uh[:, CONV_CH:])
        xc_ref[0:HALO, :] = jnp.where(t > 0, gh, st_ref[0])
    else:
        xc_ref[0:HALO, :] = st_ref[0]
    off = HALO - (CONV_W - 1)
    acc = xc_ref[off:off + tt, :] * w_ref[0:1, :]
    for i in range(1, CONV_W):
        acc = acc + xc_ref[off + i:off + i + tt, :] * w_ref[i:i + 1, :]
    y = acc + b_ref[...]
    mu = jnp.mean(y, axis=-1, keepdims=True)
    yc = y - mu
    var = jnp.mean(yc * yc, axis=-1, keepdims=True)
    ln = yc * lax.rsqrt(var + EPS) * g_ref[...] + lb_ref[...]
    o_ref[0] = _silu(ln)

    @pl.when(t == n_t - 1)
    def _():
        nb_ref[0] = xc_ref[tv:tv + HALO, :]


def _conformer(u_a, state_pad, w_dw, b_dw, ln_g, ln_b, n_valid_last):
    bsz, t_len, _ = u_a.shape
    tt = min(t_len, 256)
    n_t = t_len // tt
    hb = tt // HALO if n_t > 1 else 1
    halo_rows = HALO if n_t > 1 else tt
    w_pad = jnp.pad(w_dw, ((0, HALO - CONV_W), (0, 0)))
    row = lambda v: v.reshape(1, CONV_CH)
    kern = functools.partial(_conf_kernel, n_t, tt, n_valid_last)
    return pl.pallas_call(
        kern,
        grid=(bsz, n_t),
        in_specs=[
            pl.BlockSpec((1, tt, 2 * CONV_CH), lambda b, t: (b, t, 0)),
            pl.BlockSpec((1, halo_rows, 2 * CONV_CH), lambda b, t: (b, jnp.maximum(t * hb - 1, 0), 0)),
            pl.BlockSpec((1, HALO, CONV_CH), lambda b, t: (b, 0, 0)),
            pl.BlockSpec((HALO, CONV_CH), lambda b, t: (0, 0)),
            pl.BlockSpec((1, CONV_CH), lambda b, t: (0, 0)),
            pl.BlockSpec((1, CONV_CH), lambda b, t: (0, 0)),
            pl.BlockSpec((1, CONV_CH), lambda b, t: (0, 0)),
        ],
        out_specs=[
            pl.BlockSpec((1, tt, CONV_CH), lambda b, t: (b, t, 0)),
            pl.BlockSpec((1, HALO, CONV_CH), lambda b, t: (b, 0, 0)),
        ],
        out_shape=[
            jax.ShapeDtypeStruct((bsz, t_len, CONV_CH), F32),
            jax.ShapeDtypeStruct((bsz, HALO, CONV_CH), F32),
        ],
        scratch_shapes=[pltpu.VMEM((HALO + tt, CONV_CH), F32)],
        compiler_params=_params("parallel", "arbitrary"),
        name="conformer_conv",
    )(u_a, u_a, state_pad, w_pad, row(b_dw), row(ln_g), row(ln_b))


def _tri_inverse(a_list, ii, jj, merge_shifts):
    eye = (ii == jj).astype(F32)
    a0 = [jnp.where((ii >> 3) == (jj >> 3), a, 0.0) for a in a_list]
    a2 = [_dot3(p, p) for p in a0]
    a4 = [_dot3(p, p) for p in a2]
    x = [_dot3(eye - p, eye + q) for p, q in zip(a0, a2)]
    x = [_dot3(p, eye + q) for p, q in zip(x, a4)]
    for sh in merge_shifts:
        mask = ((ii >> (sh + 1)) == (jj >> (sh + 1))) & ((ii >> sh) != (jj >> sh))
        t = [_dot3(jnp.where(mask, a, 0.0), p) for a, p in zip(a_list, x)]
        x = [p - _dot3(p, q) for p, q in zip(x, t)]
    return x


def _softplus(x):
    return jnp.maximum(x, 0.0) + jnp.log1p(jnp.exp(-jnp.abs(x)))


def _gdn_kernel(t_len, n_valid, alog_ref, dtb_ref, q_ref, k_ref, v_ref, z_ref, a_ref, b_ref,
                sq_ref, sk_ref, sv_ref, wq_ref, wk_ref, wv_ref, s0_ref, ng_ref,
                o_ref, sn_ref,
                xp_s, qn_s, kn_s, vn_s, g_s, be_s, vw_s, kcd_s, qg_s, kdt_s, qk_s, ge_s, oo_s):
    h = pl.program_id(1)
    n_chunks = t_len // GDN_CHUNK
    c_len = GDN_CHUNK

    def conv(x_ref, st_ref, w_ref):
        xp_s[0:8, :] = st_ref[0]
        xp_s[8:8 + t_len, :] = x_ref[0]
        acc = xp_s[5:5 + t_len, :] * w_ref[0:1, :]
        for i in range(1, 4):
            acc = acc + xp_s[5 + i:5 + i + t_len, :] * w_ref[i:i + 1, :]
        return _silu(acc)

    qc = conv(q_ref, sq_ref, wq_ref)
    qn_s[...] = qc * lax.rsqrt(jnp.sum(qc * qc, axis=-1, keepdims=True) + EPS) * (GDN_D ** -0.5)
    kc = conv(k_ref, sk_ref, wk_ref)
    kn_s[...] = kc * lax.rsqrt(jnp.sum(kc * kc, axis=-1, keepdims=True) + EPS)
    vn_s[...] = conv(v_ref, sv_ref, wv_ref)

    a_exp = jnp.exp(jnp.full((1, LANES), alog_ref[h], F32))
    g = -a_exp * _softplus(a_ref[0] + dtb_ref[h])
    beta = _sigmoid(b_ref[0])
    if n_valid < t_len:
        live = lax.broadcasted_iota(jnp.int32, (t_len, LANES), 0) < n_valid
        g = jnp.where(live, g, 0.0)
        beta = jnp.where(live, beta, 0.0)
    g_s[...] = g
    be_s[...] = beta

    ii = lax.broadcasted_iota(jnp.int32, (c_len, c_len), 0)
    jj = lax.broadcasted_iota(jnp.int32, (c_len, c_len), 1)
    incl = ii >= jj
    strict = ii > jj
    ltri = _bf(incl.astype(F32))
    unroll = 8 if n_chunks % 8 == 0 else 1

    def cumdecay(g_c):
        g_hi, g_lo = _split2(g_c)
        g_lo2 = _bf((g_c - g_hi.astype(F32)) - g_lo.astype(F32))
        return (_dot(ltri, g_lo2) + _dot(ltri, g_lo)) + _dot(ltri, g_hi)

    def prep(cu, carry):
        cs = [cu * unroll + u for u in range(unroll)]
        sls = [pl.ds(pl.multiple_of(c * c_len, c_len), c_len) for c in cs]
        each = lambda f, *ls: [f(*a) for a in zip(*ls)]
        q_l = [qn_s[sl, :] for sl in sls]
        k_l = [kn_s[sl, :] for sl in sls]
        b_l = [be_s[sl, :] for sl in sls]
        gc_l = [cumdecay(g_s[sl, :]) for sl in sls]
        dec_l = each(lambda gc: jnp.where(
            incl, jnp.exp(jnp.minimum(gc[:, 0:c_len] - gc.T[0:c_len, :], 0.0)), 0.0), gc_l)
        kb_l = each(lambda k, b: k * b, k_l, b_l)
        a_l = each(lambda kb, k, dec: jnp.where(strict, _dot_nt(_bf(kb), _bf(k)) * dec, 0.0), kb_l, k_l, dec_l)
        t_l = _tri_inverse(a_l, ii, jj, () if n_valid <= 8 else (3, 4, 5))
        eg_l = each(jnp.exp, gc_l)
        rhs_l = [jnp.concatenate([vn_s[sl, :] * b, kb * eg], axis=1) for sl, b, kb, eg in zip(sls, b_l, kb_l, eg_l)]
        sol_l = each(_dot3, t_l, rhs_l)
        qk_l = each(lambda q, k, dec: jnp.where(incl, _dot_nt(_bf(q), _bf(k)) * dec, 0.0), q_l, k_l, dec_l)
        for c, sl, sol, qk, q, k, gc, eg in zip(cs, sls, sol_l, qk_l, q_l, k_l, gc_l, eg_l):
            g_end = gc[c_len - 1:c_len, :]
            vw_s[sl, :] = sol[:, :GDN_D]
            kcd_s[sl, :] = sol[:, GDN_D:]
            qk_s[c] = qk
            qg_s[sl, :] = q * eg
            kdt_s[c] = (k * jnp.exp(g_end - gc)).T
            ge_s[c] = jnp.broadcast_to(jnp.exp(g_end), (8, LANES))
        return carry

    lax.fori_loop(0, n_chunks // unroll, prep, 0)

    def step(c, s):
        sl = pl.ds(pl.multiple_of(c * c_len, c_len), c_len)
        sb = _bf(s)
        v_new = vw_s[sl, :] - _dot(_bf(kcd_s[sl, :]), sb)
        vb = _bf(v_new)
        oo_s[sl, :] = _dot(_bf(qg_s[sl, :]), sb) + _dot(_bf(qk_s[c]), vb)
        return s * ge_s[c][0:1, :] + _dot(_bf(kdt_s[c]), vb)

    s_fin = lax.fori_loop(0, n_chunks, step, s0_ref[0, 0])
    sn_ref[0, 0] = s_fin

    o = oo_s[...]
    y = o * lax.rsqrt(jnp.mean(o * o, axis=-1, keepdims=True) + EPS) * ng_ref[...]
    o_ref[0] = y * _silu(z_ref[0])


def _gated_deltanet(qkv, zab, state_pad, s0, w_conv_pad, a_log, dt_bias, norm_g, n_valid):
    bsz, t_len, _ = qkv.shape
    nh = GDN_HEADS
    blk = lambda off: pl.BlockSpec((1, t_len, GDN_D), lambda b, h, o=off: (b, 0, o + h))
    stb = lambda off: pl.BlockSpec((1, 8, GDN_D), lambda b, h, o=off: (b, 0, o + h))
    wb = lambda off: pl.BlockSpec((8, GDN_D), lambda b, h, o=off: (0, o + h))
    smem = pl.BlockSpec(memory_space=pltpu.SMEM)
    n_chunks = t_len // GDN_CHUNK
    seq = lambda: pltpu.VMEM((t_len, GDN_D), F32)
    kern = functools.partial(_gdn_kernel, t_len, n_valid)
    return pl.pallas_call(
        kern,
        grid=(bsz, nh),
        in_specs=[smem, smem, blk(0), blk(nh), blk(2 * nh), blk(0), blk(nh), blk(2 * nh),
                  stb(0), stb(nh), stb(2 * nh), wb(0), wb(nh), wb(2 * nh),
                  pl.BlockSpec((1, 1, GDN_D, GDN_D), lambda b, h: (b, h, 0, 0)),
                  pl.BlockSpec((1, GDN_D), lambda b, h: (0, 0))],
        out_specs=[pl.BlockSpec((1, t_len, GDN_D), lambda b, h: (b, 0, h)),
                   pl.BlockSpec((1, 1, GDN_D, GDN_D), lambda b, h: (b, h, 0, 0))],
        out_shape=[jax.ShapeDtypeStruct((bsz, t_len, nh * GDN_D), F32),
                   jax.ShapeDtypeStruct((bsz, nh, GDN_D, GDN_D), F32)],
        scratch_shapes=[pltpu.VMEM((8 + t_len, GDN_D), F32), seq(), seq(), seq(), seq(), seq(), seq(), seq(), seq(),
                        pltpu.VMEM((n_chunks, GDN_D, GDN_CHUNK), F32),
                        pltpu.VMEM((n_chunks, GDN_CHUNK, GDN_CHUNK), F32),
                        pltpu.VMEM((n_chunks, 8, LANES), F32), seq()],
        compiler_params=_params("parallel", "parallel"),
        name="gated_deltanet",
    )(a_log, dt_bias, qkv, qkv, qkv, zab, zab, zab, state_pad, state_pad, state_pad,
      w_conv_pad, w_conv_pad, w_conv_pad, s0, norm_g.reshape(1, GDN_D))


def _heads_to_rows(x, g, nt):
    lane = lax.broadcasted_iota(jnp.int32, (nt, LANES), 1)
    keep = (lane >= NSA_DH * g) & (lane < NSA_DH * (g + 1))
    parts = []
    for r in range(NSA_GQ):
        hh = NSA_GQ * g + r
        blk = x[:, (hh // 2) * LANES:(hh // 2 + 1) * LANES]
        if hh % 2 != g:
            blk = pltpu.roll(blk, NSA_DH, axis=1)
        parts.append(jnp.where(keep, blk, 0.0))
    return jnp.concatenate(parts, axis=0)


def _rows_to_heads(y, g, nt):
    outs = []
    for m in range(2):
        x0 = y[(2 * m) * nt:(2 * m + 1) * nt]
        x1 = y[(2 * m + 1) * nt:(2 * m + 2) * nt]
        if g == 1:
            x0 = pltpu.roll(x0, NSA_DH, axis=1)
        else:
            x1 = pltpu.roll(x1, NSA_DH, axis=1)
        outs.append(x0 + x1)
    return outs


def _masked_softmax_parts(parts, masks, axis):
    sm = [jnp.where(m, s, NEG) for s, m in zip(parts, masks)]
    mx = functools.reduce(jnp.maximum, [jnp.max(s, axis=axis, keepdims=True) for s in sm])
    es = [jnp.where(m, jnp.exp(s - mx), 0.0) for s, m in zip(sm, masks)]
    den = functools.reduce(lambda p, q: p + q, [jnp.sum(e, axis=axis, keepdims=True) for e in es])
    inv = 1.0 / jnp.maximum(den, 1e-30)
    return [e * inv for e in es]


def _bucket_np(rel):
    n = np.maximum(rel, 0)
    nf = np.maximum(n, 1).astype(np.float32)
    large = 16 + (np.log(nf / np.float32(16)) / np.float32(math.log(8.0)) * np.float32(16)).astype(np.int32)
    return np.where(n < 16, n, np.minimum(large, N_BUCKETS - 1)).astype(np.int32)


LOOKUP_TILE = 8192


def _lookup_kernel(idx_ref, tb_ref, o_ref):
    idx = idx_ref[...]
    acc = jnp.zeros(o_ref.shape, F32)
    for k in range(N_BUCKETS):
        acc = jnp.where(idx == k, tb_ref[:, k:k + 1], acc)
    o_ref[...] = acc


def _bias_lookup(rel_bias, idx_list):
    sizes = [int(np.prod(a.shape)) for a in idx_list]
    total = sum(sizes)
    padded = -(-total // LOOKUP_TILE) * LOOKUP_TILE
    flat = np.zeros((1, padded), np.int32)
    flat[0, :total] = np.concatenate([np.asarray(a, np.int32).reshape(-1) for a in idx_list])
    tab = pl.pallas_call(
        _lookup_kernel,
        grid=(padded // LOOKUP_TILE,),
        in_specs=[pl.BlockSpec((1, LOOKUP_TILE), lambda i: (0, i)),
                  pl.BlockSpec((NSA_HEADS, N_BUCKETS), lambda i: (0, 0))],
        out_specs=pl.BlockSpec((NSA_HEADS, LOOKUP_TILE), lambda i: (0, i)),
        out_shape=jax.ShapeDtypeStruct((NSA_HEADS, padded), F32),
        compiler_params=_params("parallel"),
        name="bias_lookup",
    )(jnp.asarray(flat), rel_bias.astype(F32).T)
    outs, off = [], 0
    for a, n in zip(idx_list, sizes):
        outs.append(tab[:, off:off + n].reshape((NSA_HEADS,) + tuple(a.shape)))
        off += n
    return outs


def _head_rows(tab):
    return tab.reshape(NSA_KV, NSA_GQ * tab.shape[1], tab.shape[2])


def _nsa_prompt_kernel(t_len, q_ref, kcmp_ref, vcmp_ref, kslc_ref, vslc_ref, kwin_ref, vwin_ref, gl_ref,
                       wk_ref, wv_ref, bc_ref, bct_ref, bnear_ref, bwin_ref, o_ref, kc_s, vc_s):
    i = pl.program_id(1)
    nsb = t_len // L_SEL
    qb = Q_BLOCK
    rows = NSA_GQ * qb

    @pl.when(i == 0)
    def _():
        n2 = 2 * lax.broadcasted_iota(jnp.int32, (nsb, t_len), 0)
        cb = lax.broadcasted_iota(jnp.int32, (nsb, t_len), 1) >> 5
        kc = _bf(kcmp_ref[0])
        vc = _bf(vcmp_ref[0])
        wk = wk_ref[...]
        wv = wv_ref[...]
        kc_s[0:nsb, :] = _dot(_bf(jnp.where(cb == n2, wk, 0.0)), kc)
        kc_s[nsb:2 * nsb, :] = _dot(_bf(jnp.where(cb == n2 + 1, wk, 0.0)), kc)
        vc_s[0:nsb, :] = _dot(_bf(jnp.where(cb == n2, wv, 0.0)), vc)
        vc_s[nsb:2 * nsb, :] = _dot(_bf(jnp.where(cb == n2 + 1, wv, 0.0)), vc)

    q_all = q_ref[0] * (NSA_DH ** -0.5)
    gl = gl_ref[0]
    gate_all = [_sigmoid(gl[:, br * 512:(br + 1) * 512]) for br in range(3)]
    t0 = i * qb
    tq = t0 + (lax.broadcasted_iota(jnp.int32, (rows, 1), 0) & (qb - 1))
    tl = t0 + (lax.broadcasted_iota(jnp.int32, (1, rows), 1) & (qb - 1))
    eye_q = _bf((lax.broadcasted_iota(jnp.int32, (qb, qb), 0) == lax.broadcasted_iota(jnp.int32, (qb, qb), 1)).astype(F32))
    far_end = jnp.maximum(t0 - qb, 0)
    n_far = (far_end + 511) >> 9
    kc = _bf(kc_s[...])
    vc = _bf(vc_s[...])

    def key_aug(k0, n_keys, limit):
        kpos = k0 + lax.broadcasted_iota(jnp.int32, (n_keys, LANES), 0)
        lane = lax.broadcasted_iota(jnp.int32, (n_keys, LANES), 1)
        hit = (lane == (kpos >> 6)) | ((lane == nsb) & ((kpos >= limit) | (kpos < 0)))
        return _bf(jnp.where(hit, NEG, 0.0))

    gs = range(NSA_KV)
    qg = [_bf(_heads_to_rows(q_all, g, qb)) for g in gs]

    n_prev = WINDOW // qb
    starts = [pl.multiple_of(jnp.maximum(t0 + (j - n_prev) * qb, 0), qb) for j in range(n_prev + 1)]
    kwin = [_bf(kwin_ref[0, pl.ds(st, qb), :]) for st in starts]
    pens = [jnp.where(i + (j - n_prev) >= 0, 0.0, NEG) for j in range(n_prev)] + [0.0]
    s_w = [jnp.concatenate([_dot_nt(qg[g], kwin[j]) + pens[j] for j in range(n_prev + 1)], axis=1) + bwin_ref[g]
           for g in gs]

    s_c = [_dot_nt(qg[g], kc) + bc_ref[0, g] for g in gs]
    s_t = [_dot_nt(kc, qg[g]) + bct_ref[0, g] for g in gs]
    e_c = [jnp.where(tq >= L_CMP - 1, jnp.exp(s - jnp.max(s, axis=1, keepdims=True)), 0.0) for s in s_c]
    p_c = [e * (1.0 / jnp.maximum(jnp.sum(e, axis=1, keepdims=True), 1e-30)) for e in e_c]
    o_c = [_dot(_bf(p), vc) for p in p_c]

    e_t = [jnp.where(tl >= L_CMP - 1, jnp.exp(s - jnp.max(s, axis=0, keepdims=True)), 0.0) for s in s_t]
    p_t = [e * (1.0 / jnp.maximum(jnp.sum(e, axis=0, keepdims=True), 1e-30)) for e in e_t]
    head_sum = lambda x: x[:, 0:qb] + x[:, qb:2 * qb] + x[:, 2 * qb:3 * qb] + x[:, 3 * qb:4 * qb]
    blk = lax.broadcasted_iota(jnp.int32, (nsb, qb), 0)
    cur = (t0 + lax.broadcasted_iota(jnp.int32, (nsb, qb), 1)) >> 6
    bonus = jnp.where((blk == 0) | (blk == cur) | (blk == cur - 1), FORCE_BONUS, 0.0)
    score = [jnp.where(blk <= cur, (head_sum(p[0:nsb]) + head_sum(p[nsb:2 * nsb])) + bonus, -1.0) for p in p_t]
    rank = [jnp.zeros((nsb, qb), F32) for _ in gs]
    for j in range(nsb):
        for g in gs:
            sj = score[g][j:j + 1, :]
            ahead = (sj > score[g]) | ((sj == score[g]) & (blk > j))
            rank[g] = rank[g] + jnp.where(ahead, 1.0, 0.0)
    pen_rows = jnp.where(lax.broadcasted_iota(jnp.int32, (LANES - nsb, qb), 0) == 0, 1.0, 0.0)
    not_sel_t = [jnp.where(r < float(min(N_SEL, nsb)), 0.0, 1.0) for r in rank]
    q_aug = [_bf(_dot_nt(eye_q, _bf(jnp.concatenate([ns, pen_rows], axis=0)))) for ns in not_sel_t]
    qa = [jnp.concatenate([qg[g], jnp.concatenate([q_aug[g]] * NSA_GQ, axis=0)], axis=1) for g in gs]

    def online(carry, s, pv):
        m_i, l_i, acc = carry
        m_n = jnp.maximum(m_i, jnp.max(s, axis=1, keepdims=True))
        p = jnp.exp(s - m_n)
        alpha = jnp.exp(m_i - m_n)
        return m_n, alpha * l_i + jnp.sum(p, axis=1, keepdims=True), alpha * acc + pv(_bf(p))

    def far_tile(kt, carry):
        k0 = pl.multiple_of(kt * 512, 512)
        ka = jnp.concatenate([_bf(kslc_ref[0, pl.ds(k0, 512), :]), key_aug(k0, 512, far_end)], axis=1)
        vt = _bf(vslc_ref[0, pl.ds(k0, 512), :])
        s = [_dot_nt(qa[g], ka) for g in gs]
        return tuple(online(carry[g], s[g], lambda p: _dot(p, vt)) for g in gs)

    init = (jnp.full((rows, 1), NEG, F32), jnp.zeros((rows, 1), F32), jnp.zeros((rows, LANES), F32))
    far = lax.fori_loop(0, n_far, far_tile, tuple(init for _ in gs))

    p0 = pl.multiple_of(jnp.maximum(t0 - qb, 0), qb)
    d0 = pl.multiple_of(t0, qb)
    ka = jnp.concatenate([
        jnp.concatenate([_bf(kslc_ref[0, pl.ds(p0, qb), :]), _bf(kslc_ref[0, pl.ds(d0, qb), :])], axis=0),
        key_aug(t0 - qb, 2 * qb, t_len)], axis=1)
    vp = _bf(vslc_ref[0, pl.ds(p0, qb), :])
    vd = _bf(vslc_ref[0, pl.ds(d0, qb), :])
    s_near = [_dot_nt(qa[g], ka) + bnear_ref[g] for g in gs]
    fin = [online(far[g], s_near[g], lambda p: _dot(p[:, 0:qb], vp) + _dot(p[:, qb:2 * qb], vd)) for g in gs]
    o_s = [acc * (1.0 / l_n) for _, l_n, acc in fin]

    vwin = [_bf(vwin_ref[0, pl.ds(st, qb), :]) for st in starts]
    e_w = [jnp.exp(s - jnp.max(s, axis=1, keepdims=True)) for s in s_w]
    o_w = []
    for g in gs:
        ew = _bf(e_w[g])
        acc = _dot(ew[:, 0:qb], vwin[0])
        for j in range(1, n_prev + 1):
            acc = acc + _dot(ew[:, j * qb:(j + 1) * qb], vwin[j])
        o_w.append(acc * (1.0 / jnp.sum(e_w[g], axis=1, keepdims=True)))

    for g in gs:
        gates = [_heads_to_rows(gate_all[br], g, qb) for br in range(3)]
        comb = gates[0] * o_c[g] + gates[1] * o_s[g] + gates[2] * o_w[g]
        blocks = _rows_to_heads(comb, g, qb)
        o_ref[0, :, (2 * g) * LANES:(2 * g + 1) * LANES] = blocks[0]
        o_ref[0, :, (2 * g + 1) * LANES:(2 * g + 2) * LANES] = blocks[1]


def _nsa_tables(rel_bias, t_len, n_pages, nt):
    nqb = t_len // Q_BLOCK
    nsb = t_len // L_SEL
    past = n_pages * PAGE
    t = np.arange(Q_BLOCK)
    tq = (np.arange(nqb)[:, None] * Q_BLOCK + t[None, :])[:, :, None]
    n = np.arange(nsb)[None, None, :]
    ts = np.arange(nt)[:, None]
    j = np.arange(2 * n_pages)[None, :]
    c = np.arange(PAGE)[None, :]
    idx = [
        _bucket_np(tq - (n * L_SEL + L_CMP - 1)),
        _bucket_np(tq - (n * L_SEL + L_SEL - 1)),
        _bucket_np(Q_BLOCK + t[:, None] - np.arange(2 * Q_BLOCK)[None, :]),
        _bucket_np(WINDOW + t[:, None] - np.arange(WINDOW + Q_BLOCK)[None, :]),
        _bucket_np(past + ts - (j * L_SEL + L_CMP - 1)),
        _bucket_np(past + ts - (j * L_SEL + L_SEL - 1)),
        _bucket_np(PAGE + ts - c),
        _bucket_np(ts - c),
        _bucket_np(WINDOW + ts - np.arange(WINDOW)[None, :]),
    ]
    ce, co, near, win, sce, sco, slast, snew, swin = _bias_lookup(rel_bias, idx)
    b31 = rel_bias.astype(F32)[N_BUCKETS - 1].reshape(NSA_KV, NSA_GQ, 1, 1)
    shift = lambda tab: (tab.reshape(NSA_KV, NSA_GQ, tab.shape[1], tab.shape[2]) - b31).reshape(
        NSA_KV, NSA_GQ * tab.shape[1], tab.shape[2])
    vis = lambda m: jnp.asarray(np.tile(m, (1,) * (m.ndim - 2) + (NSA_GQ, 1)))
    blocked = lambda tab: jnp.swapaxes(tab, 0, 1).reshape(nqb, NSA_KV, NSA_GQ * Q_BLOCK, nsb)
    bc = jnp.concatenate([blocked(ce), blocked(co)], axis=-1)
    vis_c = np.concatenate([n * L_SEL + L_CMP - 1 <= tq, n * L_SEL + L_SEL - 1 <= tq], axis=-1)
    bc = jnp.where(vis(vis_c)[:, None], bc, NEG)
    c_near = np.arange(2 * Q_BLOCK)[None, :]
    near_m = jnp.where(vis(c_near <= Q_BLOCK + t[:, None])[None], shift(near), NEG)
    c_win = np.arange(WINDOW + Q_BLOCK)[None, :]
    win_m = jnp.where(vis((c_win > t[:, None]) & (c_win <= WINDOW + t[:, None]))[None], _head_rows(win), NEG)
    ptab = (bc, jnp.swapaxes(bc, -1, -2), near_m, win_m)
    rows64 = lambda tab: tab.reshape(NSA_KV * NSA_GQ * nt, tab.shape[-1])
    stab = (rows64(_head_rows(sce)), rows64(_head_rows(sco)), rows64(shift(slast)), rows64(shift(snew)),
            rows64(_head_rows(swin)), rows64(_head_rows(snew)))
    return ptab, stab


def _nsa_prompt(qc, rows, win, glr, w_pos, tables):
    bsz, t_len, _ = qc.shape
    nqb = t_len // Q_BLOCK
    nsb = t_len // L_SEL
    assert nsb < LANES
    bc, bct, near, wtab = tables
    wk = jnp.tile(w_pos[0], t_len // L_CMP).reshape(1, t_len)
    wv = jnp.tile(w_pos[1], t_len // L_CMP).reshape(1, t_len)
    seq = lambda c: pl.BlockSpec((1, t_len, LANES), lambda b, i, c=c: (b, 0, c))
    full = lambda a: pl.BlockSpec(a.shape, lambda b, i, nd=a.ndim: (0,) * nd)
    per_i = lambda a: pl.BlockSpec((1,) + a.shape[1:], lambda b, i, nd=a.ndim: (i,) + (0,) * (nd - 1))
    kern = functools.partial(_nsa_prompt_kernel, t_len)
    return pl.pallas_call(
        kern,
        grid=(bsz, nqb),
        in_specs=[pl.BlockSpec((1, Q_BLOCK, 512), lambda b, i: (b, i, 0)),
                  seq(0), seq(1), seq(2), seq(3), seq(0), seq(1),
                  pl.BlockSpec((1, Q_BLOCK, 1536), lambda b, i: (b, i, 0)),
                  full(wk), full(wv), per_i(bc), per_i(bct), full(near), full(wtab)],
        out_specs=pl.BlockSpec((1, Q_BLOCK, 512), lambda b, i: (b, i, 0)),
        out_shape=jax.ShapeDtypeStruct((bsz, t_len, 512), F32),
        scratch_shapes=[pltpu.VMEM((2 * nsb, LANES), F32)] * 2,
        compiler_params=_params("parallel", "arbitrary"),
        name="nsa_prompt",
    )(qc, rows, rows, rows, rows, win, win, glr, wk, wv, bc, bct, near, wtab)


def _nsa_sample_kernel(layer, n_pages, n_new, pt_ref, cache_ref, q_ref, rows_ref, wnew_ref, wbuf_ref, gl_ref,
                       wpool_ref, bce_ref, bco_ref, blast_ref, bnew_ref, bwin_ref, bwnew_ref,
                       o_ref, cmp_s, slc_s, pool_s, exp_s, pad_s, sem):
    b = pl.program_id(0)
    nb = pl.num_programs(0)
    nt = 8
    past = n_pages * PAGE
    nblk = 2 * n_pages
    rows = NSA_KV * NSA_GQ * nt
    half = 2 * LANES

    def page_copy(seq, p, part, buf, s):
        return pltpu.make_async_copy(
            cache_ref.at[layer, pt_ref[seq, p], pl.ds(part * half, half), :],
            buf.at[:, pl.ds(pl.multiple_of(p * PAGE, PAGE), PAGE)], s)

    def start_gather(seq, part, buf, s):
        def body(p, c):
            page_copy(seq, p, part, buf, s).start()
            return c
        lax.fori_loop(0, n_pages, body, 0)

    def wait_gather(seq, part, buf, s):
        def body(p, c):
            page_copy(seq, p, part, buf, s).wait()
            return c
        lax.fori_loop(0, n_pages, body, 0)

    @pl.when(b == 0)
    def _():
        start_gather(0, 0, cmp_s, sem.at[0])
        start_gather(0, 1, slc_s, sem.at[1])
        cb = lax.broadcasted_iota(jnp.int32, (past, nblk), 0) >> 5
        j2 = 2 * lax.broadcasted_iota(jnp.int32, (past, nblk), 1)
        pool_s[0] = _bf(jnp.where(cb == j2, 1.0, 0.0))
        pool_s[1] = _bf(jnp.where(cb == j2 + 1, 1.0, 0.0))
        ej = lax.broadcasted_iota(jnp.int32, (nblk, past), 0)
        ec = lax.broadcasted_iota(jnp.int32, (nblk, past), 1) >> 6
        exp_s[...] = _bf(jnp.where(ej == ec, 1.0, 0.0))
        pad_s[...] = jnp.zeros(pad_s.shape, F32)

    q_all = q_ref[0] * (NSA_DH ** -0.5)
    qq = _bf(jnp.concatenate([_heads_to_rows(q_all, g, nt) for g in range(NSA_KV)], axis=0))
    tr = lax.broadcasted_iota(jnp.int32, (rows, 1), 0) & (nt - 1)

    wait_gather(b, 0, cmp_s, sem.at[0])
    wp = wpool_ref[...]
    ks = _bf(cmp_s[0:LANES, :] * wp[0:1, :])
    vs = _bf(cmp_s[LANES:half, :] * wp[1:2, :])
    kce = _bf(_dot(ks, pool_s[0]))
    kco = _bf(_dot(ks, pool_s[1]))
    vce = _bf(_dot(vs, pool_s[0]))
    vco = _bf(_dot(vs, pool_s[1]))

    @pl.when(b + 1 < nb)
    def _():
        start_gather(b + 1, 0, cmp_s, sem.at[0])

    se = _dot(qq, kce) + bce_ref[...]
    so = _dot(qq, kco) + bco_ref[...]
    mx = jnp.maximum(jnp.max(se, axis=1, keepdims=True), jnp.max(so, axis=1, keepdims=True))
    ee = jnp.exp(se - mx)
    eo = jnp.exp(so - mx)
    inv = 1.0 / (jnp.sum(ee, axis=1, keepdims=True) + jnp.sum(eo, axis=1, keepdims=True))
    pe = ee * inv
    po = eo * inv
    o_c = _dot_nt(_bf(pe), vce) + _dot_nt(_bf(po), vco)

    def head_sum(pr):
        return jnp.concatenate(
            [pr[g * 4 * nt:g * 4 * nt + nt] + pr[g * 4 * nt + nt:g * 4 * nt + 2 * nt]
             + pr[g * 4 * nt + 2 * nt:g * 4 * nt + 3 * nt] + pr[g * 4 * nt + 3 * nt:g * 4 * nt + 4 * nt]
             for g in range(NSA_KV)], axis=0)

    jcol = lax.broadcasted_iota(jnp.int32, (NSA_KV * nt, nblk), 1)
    forced = (jcol == 0) | (jcol == nblk - 1)
    score = (head_sum(pe) + head_sum(po)) + jnp.where(forced, FORCE_BONUS, 0.0)
    rank = jnp.where(FORCE_BONUS > score, 1.0, 0.0)
    for j in range(nblk):
        sj = score[:, j:j + 1]
        ahead = (sj > score) | ((sj == score) & (jcol > j))
        rank = rank + jnp.where(ahead, 1.0, 0.0)
    sel = jnp.where(rank < float(N_SEL), 1.0, 0.0)
    sel_rows = jnp.concatenate([sel[g * nt:(g + 1) * nt] for g in range(NSA_KV) for _ in range(NSA_GQ)], axis=0)

    new = rows_ref[0]
    wnew = wnew_ref[0]
    pad_s[0, 0:nt, :] = new[:, 2 * LANES:3 * LANES]
    pad_s[1, 0:nt, :] = new[:, 3 * LANES:4 * LANES]
    pad_s[2, 0:nt, :] = wnew[:, 0:LANES]
    pad_s[3, 0:nt, :] = wnew[:, LANES:2 * LANES]
    tc = lax.broadcasted_iota(jnp.int32, (rows, LANES), 1)
    mnew = (tc <= tr) & (tc < n_new)

    wait_gather(b, 1, slc_s, sem.at[1])
    s_all = _dot(qq, _bf(slc_s[0:LANES, :]))
    mk = _dot(_bf(sel_rows), exp_s[...]) > 0.5
    far = past - PAGE
    p_far, p_last, p_new = _masked_softmax_parts(
        [s_all[:, :far], s_all[:, far:] + blast_ref[...], _dot_nt(qq, _bf(pad_s[0])) + bnew_ref[...]],
        [mk[:, :far], mk[:, far:], mnew], 1)
    o_s = _dot_nt(_bf(jnp.concatenate([p_far, p_last], axis=1)), _bf(slc_s[LANES:half, :])) \
        + _dot(_bf(p_new), _bf(pad_s[1]))

    wb = wbuf_ref[0, 0]
    cw = lax.broadcasted_iota(jnp.int32, (rows, WINDOW), 1)
    pw, pn = _masked_softmax_parts(
        [_dot(qq, _bf(wb[0:LANES, :])) + bwin_ref[...], _dot_nt(qq, _bf(pad_s[2])) + bwnew_ref[...]],
        [cw > tr, mnew], 1)
    o_w = _dot_nt(_bf(pw), _bf(wb[LANES:half, :])) + _dot(_bf(pn), _bf(pad_s[3]))

    gl = gl_ref[0]
    gates = [jnp.concatenate([_heads_to_rows(_sigmoid(gl[:, br * 512:(br + 1) * 512]), g, nt)
                              for g in range(NSA_KV)], axis=0) for br in range(3)]
    comb = gates[0] * o_c + gates[1] * o_s + gates[2] * o_w
    for g in range(NSA_KV):
        blocks = _rows_to_heads(comb[g * 4 * nt:(g + 1) * 4 * nt], g, nt)
        o_ref[0, :, (2 * g) * LANES:(2 * g + 1) * LANES] = blocks[0]
        o_ref[0, :, (2 * g + 1) * LANES:(2 * g + 2) * LANES] = blocks[1]

    @pl.when(b + 1 < nb)
    def _():
        start_gather(b + 1, 1, slc_s, sem.at[1])


def _nsa_sample(layer, cache_t, page_table, qc, rows, wnew, wbuf_t, glr, w_pos, tables, n_new):
    bsz, n_pages = page_table.shape
    nt = qc.shape[1]
    past = n_pages * PAGE
    wpool = jnp.tile(w_pos, (1, past // L_CMP))
    full = lambda a: pl.BlockSpec(a.shape, lambda b, pt, nd=a.ndim: (0,) * nd)
    per_b = lambda a: pl.BlockSpec((1,) + a.shape[1:], lambda b, pt, nd=a.ndim: (b,) + (0,) * (nd - 1))
    kern = functools.partial(_nsa_sample_kernel, layer, n_pages, n_new)
    grid_spec = pltpu.PrefetchScalarGridSpec(
        num_scalar_prefetch=1,
        grid=(bsz,),
        in_specs=[pl.BlockSpec(memory_space=pl.ANY),
                  per_b(qc), per_b(rows), per_b(wnew),
                  pl.BlockSpec((1, 1) + wbuf_t.shape[2:], lambda b, pt: (layer, b, 0, 0)),
                  per_b(glr), full(wpool)] + [full(t) for t in tables],
        out_specs=pl.BlockSpec((1, nt, 512), lambda b, pt: (b, 0, 0)),
        scratch_shapes=[pltpu.VMEM((2 * LANES, past), F32), pltpu.VMEM((2 * LANES, past), F32),
                        pltpu.VMEM((2, past, 2 * n_pages), BF16), pltpu.VMEM((2 * n_pages, past), BF16),
                        pltpu.VMEM((4, LANES, LANES), F32), pltpu.SemaphoreType.DMA((2,))],
    )
    return pl.pallas_call(
        kern,
        grid_spec=grid_spec,
        out_shape=jax.ShapeDtypeStruct((bsz, nt, 512), F32),
        compiler_params=_params("arbitrary"),
        name="nsa_sample",
    )(page_table, cache_t, qc, rows, wnew, wbuf_t, glr, wpool, *tables)


def _mixout_kernel(x_ref, ca_ref, ob_ref, oc_ref, ug_ref, wpa_ref, wpb_ref, wpc_ref, wo_ref, o_ref):
    ug = ug_ref[...]
    y = _sigmoid(ug[:, 0:D_MODEL]) * _dot(_bf(ca_ref[...]), wpa_ref[...])
    y = y + _sigmoid(ug[:, D_MODEL:2 * D_MODEL]) * _dot(_bf(ob_ref[...]), wpb_ref[...])
    y = y + _sigmoid(ug[:, 2 * D_MODEL:3 * D_MODEL]) * _dot(_bf(oc_ref[...]), wpc_ref[...])
    o_ref[...] = x_ref[...] + _dot(_bf(y), wo_ref[...])


def _mixout(x, ca, ob, oc, ug, wpa, wpb, wpc, wo):
    m = x.shape[0]
    tm = min(m, 512)
    rowblk = lambda n: pl.BlockSpec((tm, n), lambda i: (i, 0))
    full = lambda a: pl.BlockSpec(a.shape, lambda i: (0, 0))
    return pl.pallas_call(
        _mixout_kernel,
        grid=(m // tm,),
        in_specs=[rowblk(D_MODEL), rowblk(512), rowblk(512), rowblk(512), rowblk(3 * D_MODEL),
                  full(wpa), full(wpb), full(wpc), full(wo)],
        out_specs=rowblk(D_MODEL),
        out_shape=jax.ShapeDtypeStruct((m, D_MODEL), F32),
        compiler_params=_params("parallel"),
        name="mixer_out",
    )(x, ca, ob, oc, ug, wpa, wpb, wpc, wo)


def _xattn_kernel(x_ref, g_ref, kv_ref, wq_ref, wo_ref, o_ref):
    x = x_ref[0]
    h = _bf(x * lax.rsqrt(jnp.mean(x * x, axis=-1, keepdims=True) + EPS) * g_ref[...])
    q = _dot(h, wq_ref[...])
    kv = kv_ref[0]
    outs = []
    for hd in range(X_HEADS):
        qh = _bf(q[:, hd * X_DH:(hd + 1) * X_DH])
        kh = _bf(kv[:, hd * X_DH:(hd + 1) * X_DH])
        vh = _bf(kv[:, D_MODEL + hd * X_DH:D_MODEL + (hd + 1) * X_DH])
        s = _dot_nt(qh, kh) * (X_DH ** -0.5)
        e = jnp.exp(s - jnp.max(s, axis=-1, keepdims=True))
        pr = e * (1.0 / jnp.sum(e, axis=-1, keepdims=True))
        outs.append(_dot(_bf(pr), vh))
    o = jnp.concatenate(outs, axis=1)
    o_ref[0] = x + _dot(_bf(o), wo_ref[...])


def _cross_attn(x, g, mem_kv, wq, wo):
    bsz, t_len, d = x.shape
    tt = min(t_len, 512)
    full = lambda a: pl.BlockSpec(a.shape, lambda b, t: (0, 0))
    return pl.pallas_call(
        _xattn_kernel,
        grid=(bsz, t_len // tt),
        in_specs=[pl.BlockSpec((1, tt, d), lambda b, t: (b, t, 0)),
                  pl.BlockSpec((1, d), lambda b, t: (0, 0)),
                  pl.BlockSpec((1, N_MEM, 2 * d), lambda b, t: (b, 0, 0)),
                  full(wq), full(wo)],
        out_specs=pl.BlockSpec((1, tt, d), lambda b, t: (b, t, 0)),
        out_shape=jax.ShapeDtypeStruct((bsz, t_len, d), F32),
        compiler_params=_params("parallel", "parallel"),
        name="cross_attn",
    )(x, g.reshape(1, d), mem_kv, wq, wo)


FF_CHUNK = 1024


def _mlp_kernel(n_k, x_ref, g_ref, w1_ref, w2_ref, o_ref, h_s, acc_s):
    k = pl.program_id(1)

    @pl.when(k == 0)
    def _():
        x = x_ref[...]
        h_s[...] = _bf(x * lax.rsqrt(jnp.mean(x * x, axis=-1, keepdims=True) + EPS) * g_ref[...])
        acc_s[...] = x

    a = jnp.maximum(_dot(h_s[...], w1_ref[...]), 0.0)
    acc_s[...] += _dot(_bf(a * a), w2_ref[...])

    @pl.when(k == n_k - 1)
    def _():
        o_ref[...] = acc_s[...]


def _mlp(x, g, w1, w2):
    m, d = x.shape
    tm = min(m, 1024)
    n_k = D_FF // FF_CHUNK
    return pl.pallas_call(
        functools.partial(_mlp_kernel, n_k),
        grid=(m // tm, n_k),
        in_specs=[pl.BlockSpec((tm, d), lambda i, k: (i, 0)),
                  pl.BlockSpec((1, d), lambda i, k: (0, 0)),
                  pl.BlockSpec((d, FF_CHUNK), lambda i, k: (0, k)),
                  pl.BlockSpec((FF_CHUNK, d), lambda i, k: (k, 0))],
        out_specs=pl.BlockSpec((tm, d), lambda i, k: (i, 0)),
        out_shape=jax.ShapeDtypeStruct((m, d), F32),
        scratch_shapes=[pltpu.VMEM((tm, d), BF16), pltpu.VMEM((tm, d), F32)],
        compiler_params=_params("parallel", "arbitrary"),
        name="sq_relu_mlp",
    )(x, g.reshape(1, d), w1, w2)


A_COLS = 2 * CONV_CH
B0 = A_COLS
Z0 = B0 + GDN_QKV
AB0 = Z0 + GDN_HEADS * GDN_D
C0 = AB0 + 2 * GDN_HEADS
KV0 = C0 + NSA_HEADS * NSA_DH
GL0 = KV0 + 6 * NSA_KV * NSA_DH
G0 = GL0 + 3 * NSA_HEADS
N_IN = G0 + 3 * D_MODEL


def _layer_weights(l, w_in, w_pa, w_pb, w_pc, w_o, w_xq, w_xk, w_xv, w_xo, w_ff1, w_ff2):
    w = w_in[l]
    rep = lambda cols, n: jnp.repeat(cols, n, axis=1)
    zab = jnp.concatenate([w[:, Z0:AB0], rep(w[:, AB0:AB0 + GDN_HEADS], GDN_D),
                           rep(w[:, AB0 + GDN_HEADS:C0], GDN_D)], axis=1)
    return {
        "a": _bf(w[:, 0:A_COLS]),
        "qkv": _bf(w[:, B0:Z0]),
        "zab": _bf(zab),
        "q": _bf(w[:, C0:KV0]),
        "rows": _bf(w[:, KV0:KV0 + 4 * LANES]),
        "win": _bf(w[:, KV0 + 4 * LANES:GL0]),
        "glr": _bf(rep(w[:, GL0:G0], NSA_DH)),
        "g": _bf(w[:, G0:N_IN]),
        "pa": _bf(w_pa[l]), "pb": _bf(w_pb[l]), "pc": _bf(w_pc[l]), "o": _bf(w_o[l]),
        "xq": _bf(w_xq[l]), "xo": _bf(w_xo[l]),
        "xkv": _bf(jnp.concatenate([w_xk[l], w_xv[l]], axis=1)),
        "ff1": _bf(w_ff1[l]), "ff2": _bf(w_ff2[l]),
    }


def _mixers(x, lw, p, l, conv_state_pad, qkv_state_pad, s0, n_valid, gdn_len, nsa_fn):
    bsz, t_len, d = x.shape
    m = bsz * t_len
    x2 = x.reshape(m, d)
    h = _rmsnorm(x2, p["norm_mix"][l], BF16)
    proj = lambda name: _matmul(h, lw[name])
    u_a = proj("a").reshape(bsz, t_len, -1)
    qkv = proj("qkv").reshape(bsz, t_len, -1)
    zab = proj("zab").reshape(bsz, t_len, -1)
    qc = proj("q").reshape(bsz, t_len, -1)
    rows = proj("rows").reshape(bsz, t_len, -1)
    win = proj("win").reshape(bsz, t_len, -1)
    glr = proj("glr").reshape(bsz, t_len, -1)
    ug = proj("g")

    ca, conv_new = _conformer(u_a, conv_state_pad, p["conv_a_w"][l], p["conv_a_b"][l], p["ln_a_g"][l],
                              p["ln_a_b"][l], n_valid if n_valid < t_len else min(t_len, 256))
    pad_t = ((0, 0), (0, gdn_len - t_len), (0, 0))
    w_conv_pad = jnp.pad(p["gdn_conv_w"][l], ((0, 4), (0, 0)))
    ob, s_new = _gated_deltanet(jnp.pad(qkv, pad_t), jnp.pad(zab, pad_t), qkv_state_pad, s0, w_conv_pad,
                                p["gdn_a_log"][l], p["gdn_dt_bias"][l], p["gdn_norm_g"][l],
                                n_valid if n_valid < t_len else gdn_len)
    ob = ob[:, :t_len]
    oc = nsa_fn(qc, rows, win, glr)
    x_new = _mixout(x2, ca.reshape(m, -1), ob.reshape(m, -1), oc.reshape(m, -1), ug,
                    lw["pa"], lw["pb"], lw["pc"], lw["o"])
    return x_new.reshape(bsz, t_len, d), conv_new[:, HALO - (CONV_W - 1):], qkv, s_new, rows, win


def kernel(x_prompt, x_sample, cache_nsa_kv, cache_win_kv, state_conv_a, state_conv_qkv, state_gdn, cache_mem_kv,
           page_table, mem_prompt, rel_bias, norm_mix, w_in, conv_a_w, conv_a_b, ln_a_g, ln_a_b, w_pa, gdn_conv_w,
           gdn_a_log, gdn_dt_bias, gdn_norm_g, w_pb, nsa_cmp_w, w_pc, w_o, norm_x, w_xq, w_xk, w_xv, w_xo,
           norm_mlp, w_ff1, w_ff2, norm_final):
    p = {"norm_mix": norm_mix, "conv_a_w": conv_a_w, "conv_a_b": conv_a_b, "ln_a_g": ln_a_g, "ln_a_b": ln_a_b,
         "gdn_conv_w": gdn_conv_w, "gdn_a_log": gdn_a_log, "gdn_dt_bias": gdn_dt_bias, "gdn_norm_g": gdn_norm_g}
    depth = w_in.shape[0]
    bp, tp, d = x_prompt.shape
    bs, ts, _ = x_sample.shape
    ts_pad = 8
    n_pages = page_table.shape[1]
    wb = cache_win_kv.shape[2]
    xp = x_prompt
    xs = jnp.pad(x_sample, ((0, 0), (0, ts_pad - ts), (0, 0)))
    ptab, stab = _nsa_tables(rel_bias, tp, n_pages, ts_pad)
    cache_t = jnp.transpose(cache_nsa_kv, (0, 1, 3, 4, 5, 2)).reshape(depth, -1, 4 * LANES, PAGE)
    wbuf_t = jnp.transpose(cache_win_kv, (0, 1, 3, 4, 5, 2)).reshape(depth, bs, 2 * LANES, wb)
    outs = {k: [] for k in ("p_rows", "p_win", "p_conv", "p_qkv", "p_gdn", "p_mem",
                            "s_rows", "s_win", "s_conv", "s_qkv", "s_gdn")}
    for l in range(depth):
        lw = _layer_weights(l, w_in, w_pa, w_pb, w_pc, w_o, w_xq, w_xk, w_xv, w_xo, w_ff1, w_ff2)
        nsa_p = lambda qc, rows, win, glr: _nsa_prompt(qc, rows, win, glr, nsa_cmp_w[l], ptab)
        xp, conv_n, qkv_raw, s_n, rows, win = _mixers(
            xp, lw, p, l, jnp.zeros((bp, HALO, CONV_CH), F32), jnp.zeros((bp, 8, GDN_QKV), F32),
            jnp.zeros((bp, GDN_HEADS, GDN_D, GDN_D), F32), tp, tp, nsa_p)
        mem_kv = _matmul(_bf(mem_prompt.reshape(bp * N_MEM, d)), lw["xkv"]).reshape(bp, N_MEM, 2 * d)
        xp = _cross_attn(xp, norm_x[l], mem_kv, lw["xq"], lw["xo"])
        xp = _mlp(xp.reshape(bp * tp, d), norm_mlp[l], lw["ff1"], lw["ff2"]).reshape(bp, tp, d)
        outs["p_rows"].append(rows.reshape(bp, tp, 4, NSA_KV, NSA_DH))
        outs["p_win"].append(win[:, tp - min(WINDOW, tp):].reshape(bp, min(WINDOW, tp), 2, NSA_KV, NSA_DH))
        outs["p_conv"].append(conv_n)
        outs["p_qkv"].append(qkv_raw[:, tp - 3:])
        outs["p_gdn"].append(s_n)
        outs["p_mem"].append(mem_kv.reshape(bp, N_MEM, 2, X_HEADS, X_DH))
        nsa_s =lambda qc, rows, win, glr: _nsa_sample(l, cache_t, page_table, qc, rows, win, wbuf_t, glr,
                                                       nsa_cmp_w[l], stab, ts)
        conv_pad = jnp.pad(state_conv_a[l], ((0, 0), (HALO - (CONV_W - 1), 0), (0, 0)))
        qkv_pad = jnp.pad(state_conv_qkv[l], ((0, 0), (5, 0), (0, 0)))
        xs, conv_n, qkv_raw, s_n, rows, win = _mixers(
            xs, lw, p, l, conv_pad, qkv_pad, state_gdn[l], ts, GDN_CHUNK, nsa_s)
        mkv = cache_mem_kv[l].reshape(bs, N_MEM, 2 * d)
        xs = _cross_attn(xs, norm_x[l], mkv, lw["xq"], lw["xo"])
        xs = _mlp(xs.reshape(bs * ts_pad, d), norm_mlp[l], lw["ff1"], lw["ff2"]).reshape(bs, ts_pad, d)
        outs["s_rows"].append(rows[:, :ts].reshape(bs, ts, 4, NSA_KV, NSA_DH))
        win_new = win[:, :ts].reshape(bs, ts, 2, NSA_KV, NSA_DH)
        outs["s_win"].append(jnp.concatenate([cache_win_kv[l], win_new], axis=1)[:, ts:])
        outs["s_conv"].append(conv_n)
        outs["s_qkv"].append(qkv_raw[:, ts - 3:ts])
        outs["s_gdn"].append(s_n)
    y_prompt = _rmsnorm(xp.reshape(bp * tp, d), norm_final, F32).reshape(bp, tp, d)
    y_sample = _rmsnorm(xs.reshape(bs * ts_pad, d), norm_final, F32).reshape(bs, ts_pad, d)[:, :ts]
    st = lambda k: jnp.stack(outs[k], axis=0)
    return (y_prompt, y_sample, st("p_rows"), st("p_win"), st("p_conv"), st("p_qkv"), st("p_gdn"), st("p_mem"),
            st("s_rows"), st("s_win"), st("s_conv"), st("s_qkv"), st("s_gdn"))
```

```python
import functools
import math

import jax
import jax.numpy as jnp
import numpy as np
from jax import lax
from jax.experimental import pallas as pl
from jax.experimental.pallas import tpu as pltpu

F32 = jnp.float32
BF16 = jnp.bfloat16

D_MODEL = 1024
CONV_CH = 512
CONV_W = 31
GDN_HEADS = 4
GDN_D = 128
GDN_CHUNK = 64
GDN_QKV = 3 * GDN_HEADS * GDN_D
NSA_HEADS = 8
NSA_KV = 2
NSA_GQ = 4
NSA_DH = 64
L_CMP = 32
L_SEL = 64
N_SEL = 16
WINDOW = 512
Q_BLOCK = 128
FORCE_BONUS = 1e4
PAGE = 128
N_MEM = 256
X_HEADS = 4
X_DH = 256
D_FF = 4096
N_BUCKETS = 32
EPS = 1e-6
NEG = -1e30

LANES = 128
HALO = 32
VMEM_LIMIT = 48 * 1024 * 1024
GDN_VMEM_LIMIT = 56 * 1024 * 1024


def _bf(x):
    return x.astype(BF16)


def _dot(a, b):
    return jnp.dot(a, b, preferred_element_type=F32)


def _dot_nt(a, b):
    return lax.dot_general(a, b, (((1,), (1,)), ((), ())), preferred_element_type=F32)


def _split2(x):
    hi = _bf(x)
    return hi, _bf(x - hi.astype(F32))


def _dot3(a, b):
    ah, al = _split2(a)
    bh, bl = _split2(b)
    return (_dot(ah, bl) + _dot(al, bh)) + _dot(ah, bh)


def _sigmoid(x):
    return 1.0 / (1.0 + jnp.exp(-x))


def _silu(x):
    return x * _sigmoid(x)


def _params(*sem):
    return pltpu.CompilerParams(dimension_semantics=sem, vmem_limit_bytes=VMEM_LIMIT)


def _rms_kernel(x_ref, g_ref, o_ref):
    x = x_ref[...]
    y = x * lax.rsqrt(jnp.mean(x * x, axis=-1, keepdims=True) + EPS)
    o_ref[...] = (y * g_ref[...]).astype(o_ref.dtype)


def _rmsnorm(x, g, out_dtype):
    m, d = x.shape
    tm = min(m, 512)
    return pl.pallas_call(
        _rms_kernel,
        grid=(m // tm,),
        in_specs=[pl.BlockSpec((tm, d), lambda i: (i, 0)), pl.BlockSpec((1, d), lambda i: (0, 0))],
        out_specs=pl.BlockSpec((tm, d), lambda i: (i, 0)),
        out_shape=jax.ShapeDtypeStruct((m, d), out_dtype),
        compiler_params=_params("parallel"),
        name="rmsnorm",
    )(x, g.reshape(1, d))


def _mm_kernel(a_ref, w_ref, o_ref):
    o_ref[...] = _dot(a_ref[...], w_ref[...])


def _col_tile(n):
    for tn in (1024, 768, 512, 384, 256, 128):
        if n % tn == 0:
            return tn
    raise ValueError(n)


def _matmul(a, w):
    m, k = a.shape
    n = w.shape[1]
    tm = min(m, 1024)
    tn = _col_tile(n)
    return pl.pallas_call(
        _mm_kernel,
        grid=(m // tm, n // tn),
        in_specs=[pl.BlockSpec((tm, k), lambda i, j: (i, 0)), pl.BlockSpec((k, tn), lambda i, j: (0, j))],
        out_specs=pl.BlockSpec((tm, tn), lambda i, j: (i, j)),
        out_shape=jax.ShapeDtypeStruct((m, n), F32),
        compiler_params=_params("parallel", "parallel"),
        name="matmul",
    )(a, w)


U_A = 0
U_Q = 1024
U_ROWS = 1536
U_QKV = 2048
U_Z = 3584
U_G = 4096
U_WIN = 7168
U_AB = 7424
U_GL = 7552
U_N = 7680
PROJ_TN = 1536


def _proj_kernel(x_ref, g_ref, w_ref, o_ref, h_s):
    @pl.when(pl.program_id(1) == 0)
    def _():
        x = x_ref[...]
        h_s[...] = _bf(x * lax.rsqrt(jnp.mean(x * x, axis=-1, keepdims=True) + EPS) * g_ref[...])

    o_ref[...] = _dot(h_s[...], w_ref[...])


def _proj_in(x, g, w):
    m, d = x.shape
    n = w.shape[1]
    tm = min(m, 1024)
    return pl.pallas_call(
        _proj_kernel,
        grid=(m // tm, n // PROJ_TN),
        in_specs=[pl.BlockSpec((tm, d), lambda i, j: (i, 0)), pl.BlockSpec((1, d), lambda i, j: (0, 0)),
                  pl.BlockSpec((d, PROJ_TN), lambda i, j: (0, j))],
        out_specs=pl.BlockSpec((tm, PROJ_TN), lambda i, j: (i, j)),
        out_shape=jax.ShapeDtypeStruct((m, n), F32),
        scratch_shapes=[pltpu.VMEM((tm, d), BF16)],
        compiler_params=_params("parallel", "arbitrary"),
        name="proj_in",
    )(x, g.reshape(1, d), w)


def _split3(x):
    hi = _bf(x)
    r = x - hi.astype(F32)
    mid = _bf(r)
    return hi, mid, _bf(r - mid.astype(F32))


def _pick_columns(x, onehot):
    hi, mid, lo = _split3(x)
    return (_dot(lo, onehot) + _dot(mid, onehot)) + _dot(hi, onehot)


def _conf_kernel(n_t, tt, tv, u_ref, halo_ref, st_ref, w_ref, b_ref, g_ref, lb_ref, o_ref, nb_ref, xc_ref):
    t = pl.program_id(1)
    u = u_ref[0]
    xc_ref[HALO:HALO + tt, :] = u[:, :CONV_CH] * _sigmoid(u[:, CONV_CH:])
    if n_t > 1:
        uh = halo_ref[0]
        gh = uh[:, :CONV_CH] * _sigmoid(uh[:, CONV_CH:])
        xc_ref[0:HALO, :] = jnp.where(t > 0, gh, st_ref[0])
    else:
        xc_ref[0:HALO, :] = st_ref[0]
    off = HALO - (CONV_W - 1)
    acc = xc_ref[off:off + tt, :] * w_ref[0:1, :]
    for i in range(1, CONV_W):
        acc = acc + xc_ref[off + i:off + i + tt, :] * w_ref[i:i + 1, :]
    y = acc + b_ref[...]
    mu = jnp.mean(y, axis=-1, keepdims=True)
    yc = y - mu
    var = jnp.mean(yc * yc, axis=-1, keepdims=True)
    ln = yc * lax.rsqrt(var + EPS) * g_ref[...] + lb_ref[...]
    o_ref[0] = _silu(ln)

    @pl.when(t == n_t - 1)
    def _():
        nb_ref[0] = xc_ref[tv:tv + HALO, :]


def _conformer(u_a, state_pad, w_dw, b_dw, ln_g, ln_b, n_valid_last):
    bsz, t_len, _ = u_a.shape
    tt = min(t_len, 256)
    n_t = t_len // tt
    hb = tt // HALO if n_t > 1 else 1
    halo_rows = HALO if n_t > 1 else tt
    w_pad = jnp.pad(w_dw, ((0, HALO - CONV_W), (0, 0)))
    row = lambda v: v.reshape(1, CONV_CH)
    kern = functools.partial(_conf_kernel, n_t, tt, n_valid_last)
    return pl.pallas_call(
        kern,
        grid=(bsz, n_t),
        in_specs=[
            pl.BlockSpec((1, tt, 2 * CONV_CH), lambda b, t: (b, t, 0)),
            pl.BlockSpec((1, halo_rows, 2 * CONV_CH), lambda b, t: (b, jnp.maximum(t * hb - 1, 0), 0)),
            pl.BlockSpec((1, HALO, CONV_CH), lambda b, t: (b, 0, 0)),
            pl.BlockSpec((HALO, CONV_CH), lambda b, t: (0, 0)),
            pl.BlockSpec((1, CONV_CH), lambda b, t: (0, 0)),
            pl.BlockSpec((1, CONV_CH), lambda b, t: (0, 0)),
            pl.BlockSpec((1, CONV_CH), lambda b, t: (0, 0)),
        ],
        out_specs=[
            pl.BlockSpec((1, tt, CONV_CH), lambda b, t: (b, t, 0)),
            pl.BlockSpec((1, HALO, CONV_CH), lambda b, t: (b, 0, 0)),
        ],
        out_shape=[
            jax.ShapeDtypeStruct((bsz, t_len, CONV_CH), F32),
            jax.ShapeDtypeStruct((bsz, HALO, CONV_CH), F32),
        ],
        scratch_shapes=[pltpu.VMEM((HALO + tt, CONV_CH), F32)],
        compiler_params=_params("parallel", "arbitrary"),
        name="conformer_conv",
    )(u_a, u_a, state_pad, w_pad, row(b_dw), row(ln_g), row(ln_b))


def _tri_inverse(a_list, ii, jj, merge_shifts):
    eye = (ii == jj).astype(F32)
    a0 = [jnp.where((ii >> 3) == (jj >> 3), a, 0.0) for a in a_list]
    a2 = [_dot3(p, p) for p in a0]
    a4 = [_dot3(p, p) for p in a2]
    x = [_dot3(eye - p, eye + q) for p, q in zip(a0, a2)]
    x = [_dot3(p, eye + q) for p, q in zip(x, a4)]
    for sh in merge_shifts:
        mask = ((ii >> (sh + 1)) == (jj >> (sh + 1))) & ((ii >> sh) != (jj >> sh))
        t = [_dot3(jnp.where(mask, a, 0.0), p) for a, p in zip(a_list, x)]
        x = [p - _dot3(p, q) for p, q in zip(x, t)]
    return x


def _softplus(x):
    return jnp.maximum(x, 0.0) + jnp.log1p(jnp.exp(-jnp.abs(x)))


def _gdn_kernel(t_len, n_valid, hp, alog_ref, dtb_ref, q_ref, k_ref, v_ref, z_ref, ab_ref,
                sq_ref, sk_ref, sv_ref, wq_ref, wk_ref, wv_ref, s0_ref, ng_ref,
                o_ref, sn_ref,
                xp_s, qn_s, kn_s, vn_s, g_s, be_s, vw_s, kcd_s, qg_s, kdt_s, qk_s, ge_s):
    h0 = pl.program_id(1) * hp
    n_chunks = t_len // GDN_CHUNK
    c_len = GDN_CHUNK

    heads = range(hp)
    hcols = lambda hh: slice(hh * GDN_D, (hh + 1) * GDN_D)

    def conv(x_ref, st_ref, w_ref, hh):
        xp_s[0:8, :] = st_ref[0, :, hcols(hh)]
        xp_s[8:8 + t_len, :] = x_ref[0, :, hcols(hh)]
        acc = xp_s[5:5 + t_len, :] * w_ref[0:1, hcols(hh)]
        for i in range(1, 4):
            acc = acc + xp_s[5 + i:5 + i + t_len, :] * w_ref[i:i + 1, hcols(hh)]
        return _silu(acc)

    col = lax.broadcasted_iota(jnp.int32, (LANES, LANES), 0)
    ab = ab_ref[0]
    for hh in heads:
        h = h0 + hh
        qc = conv(q_ref, sq_ref, wq_ref, hh)
        qn_s[hh] = qc * lax.rsqrt(jnp.sum(qc * qc, axis=-1, keepdims=True) + EPS) * (GDN_D ** -0.5)
        kc = conv(k_ref, sk_ref, wk_ref, hh)
        kn_s[hh] = kc * lax.rsqrt(jnp.sum(kc * kc, axis=-1, keepdims=True) + EPS)
        vn_s[hh] = conv(v_ref, sv_ref, wv_ref, hh)
        a_rep = _pick_columns(ab, _bf(jnp.where(col == h, 1.0, 0.0)))
        b_rep = _pick_columns(ab, _bf(jnp.where(col == GDN_HEADS + h, 1.0, 0.0)))
        a_exp = jnp.exp(jnp.full((1, LANES), alog_ref[h], F32))
        g = -a_exp * _softplus(a_rep + dtb_ref[h])
        beta = _sigmoid(b_rep)
        if n_valid < t_len:
            live = lax.broadcasted_iota(jnp.int32, (t_len, LANES), 0) < n_valid
            g = jnp.where(live, g, 0.0)
            beta = jnp.where(live, beta, 0.0)
        g_s[hh] = g
        be_s[hh] = beta

    ii = lax.broadcasted_iota(jnp.int32, (c_len, c_len), 0)
    jj = lax.broadcasted_iota(jnp.int32, (c_len, c_len), 1)
    incl = ii >= jj
    strict = ii > jj
    ltri = _bf(incl.astype(F32))
    unroll = max(u for u in (1, 2, 4, 8) if n_chunks % u == 0 and u * hp <= 8)

    def cumdecay(g_c):
        g_hi, g_mid, g_lo = _split3(g_c)
        return (_dot(ltri, g_lo) + _dot(ltri, g_mid)) + _dot(ltri, g_hi)

    def prep(cu, carry):
        pairs = [(hh, cu * unroll + u) for hh in heads for u in range(unroll)]
        sls = [pl.ds(pl.multiple_of(c * c_len, c_len), c_len) for _, c in pairs]
        each = lambda f, *ls: [f(*a) for a in zip(*ls)]
        q_l = [qn_s[hh, sl, :] for (hh, _), sl in zip(pairs, sls)]
        k_l = [kn_s[hh, sl, :] for (hh, _), sl in zip(pairs, sls)]
        b_l = [be_s[hh, sl, :] for (hh, _), sl in zip(pairs, sls)]
        gc_l = [cumdecay(g_s[hh, sl, :]) for (hh, _), sl in zip(pairs, sls)]
        dec_l = each(lambda gc: jnp.where(
            incl, jnp.exp(jnp.minimum(gc[:, 0:c_len] - gc.T[0:c_len, :], 0.0)), 0.0), gc_l)
        kb_l = each(lambda k, b: k * b, k_l, b_l)
        a_l = each(lambda kb, k, dec: jnp.where(strict, _dot_nt(_bf(kb), _bf(k)) * dec, 0.0), kb_l, k_l, dec_l)
        t_l = _tri_inverse(a_l, ii, jj, () if n_valid <= 8 else (3, 4, 5))
        eg_l = each(jnp.exp, gc_l)
        rhs_l = [jnp.concatenate([vn_s[hh, sl, :] * b, kb * eg], axis=1)
                 for (hh, _), sl, b, kb, eg in zip(pairs, sls, b_l, kb_l, eg_l)]
        sol_l = each(_dot3, t_l, rhs_l)
        qk_l = each(lambda q, k, dec: jnp.where(incl, _dot_nt(_bf(q), _bf(k)) * dec, 0.0), q_l, k_l, dec_l)
        for (hh, c), sl, sol, qk, q, k, gc, eg in zip(pairs, sls, sol_l, qk_l, q_l, k_l, gc_l, eg_l):
            g_end = gc[c_len - 1:c_len, :]
            vw_s[hh, sl, :] = sol[:, :GDN_D]
            kcd_s[hh, sl, :] = _bf(sol[:, GDN_D:])
            qk_s[hh, c] = _bf(qk)
            qg_s[hh, sl, :] = _bf(q * eg)
            kdt_s[hh, c] = _bf((k * jnp.exp(g_end - gc)).T)
            ge_s[hh, c] = jnp.broadcast_to(jnp.exp(g_end), (8, LANES))
        return carry

    lax.fori_loop(0, n_chunks // unroll, prep, 0)

    def step(c, states):
        sl = pl.ds(pl.multiple_of(c * c_len, c_len), c_len)
        sb = [_bf(s) for s in states]
        v_new = [vw_s[hh, sl, :] - _dot(kcd_s[hh, sl, :], sb[hh]) for hh in heads]
        vb = [_bf(v) for v in v_new]
        for hh in heads:
            o_ref[0, sl, hcols(hh)] = _dot(qg_s[hh, sl, :], sb[hh]) + _dot(qk_s[hh, c], vb[hh])
        return tuple(states[hh] * ge_s[hh, c][0:1, :] + _dot(kdt_s[hh, c], vb[hh]) for hh in heads)

    s_fin = lax.fori_loop(0, n_chunks, step, tuple(s0_ref[0, hh] for hh in heads))
    for hh in heads:
        sn_ref[0, hh] = s_fin[hh]
        o = o_ref[0, :, hcols(hh)]
        y = o * lax.rsqrt(jnp.mean(o * o, axis=-1, keepdims=True) + EPS) * ng_ref[...]
        o_ref[0, :, hcols(hh)] = y * _silu(z_ref[0, :, hcols(hh)])


def _gated_deltanet(u, qkv_blk, z_blk, ab_blk, state_pad, s0, w_conv_pad, a_log, dt_bias, norm_g, n_valid):
    bsz, t_len, _ = u.shape
    nh = GDN_HEADS
    hp = nh if t_len <= 4 * GDN_CHUNK else 2
    wide = hp * GDN_D
    assert (qkv_blk * GDN_D) % wide == 0 and (z_blk * GDN_D) % wide == 0
    ublk = lambda blk: pl.BlockSpec((1, t_len, wide), lambda b, j, o=blk * GDN_D // wide: (b, 0, o + j))
    stb = lambda off: pl.BlockSpec((1, 8, wide), lambda b, j, o=off * GDN_D // wide: (b, 0, o + j))
    wb = lambda off: pl.BlockSpec((8, wide), lambda b, j, o=off * GDN_D // wide: (0, o + j))
    smem = pl.BlockSpec(memory_space=pltpu.SMEM)
    n_chunks = t_len // GDN_CHUNK
    seq = lambda dt: pltpu.VMEM((hp, t_len, GDN_D), dt)
    kern = functools.partial(_gdn_kernel, t_len, n_valid, hp)
    return pl.pallas_call(
        kern,
        grid=(bsz, nh // hp),
        in_specs=[smem, smem, ublk(qkv_blk), ublk(qkv_blk + nh), ublk(qkv_blk + 2 * nh), ublk(z_blk),
                  pl.BlockSpec((1, t_len, GDN_D), lambda b, j: (b, 0, ab_blk)),
                  stb(0), stb(nh), stb(2 * nh), wb(0), wb(nh), wb(2 * nh),
                  pl.BlockSpec((1, hp, GDN_D, GDN_D), lambda b, j: (b, j, 0, 0)),
                  pl.BlockSpec((1, GDN_D), lambda b, j: (0, 0))],
        out_specs=[pl.BlockSpec((1, t_len, wide), lambda b, j: (b, 0, j)),
                   pl.BlockSpec((1, hp, GDN_D, GDN_D), lambda b, j: (b, j, 0, 0))],
        out_shape=[jax.ShapeDtypeStruct((bsz, t_len, nh * GDN_D), F32),
                   jax.ShapeDtypeStruct((bsz, nh, GDN_D, GDN_D), F32)],
        scratch_shapes=[pltpu.VMEM((8 + t_len, GDN_D), F32), seq(F32), seq(F32), seq(F32), seq(F32), seq(F32),
                        seq(F32), seq(BF16), seq(BF16),
                        pltpu.VMEM((hp, n_chunks, GDN_D, GDN_CHUNK), BF16),
                        pltpu.VMEM((hp, n_chunks, GDN_CHUNK, GDN_CHUNK), BF16),
                        pltpu.VMEM((hp, n_chunks, 8, LANES), F32)],
        compiler_params=pltpu.CompilerParams(dimension_semantics=("parallel", "parallel"),
                                             vmem_limit_bytes=GDN_VMEM_LIMIT),
        name="gated_deltanet",
    )(a_log, dt_bias, u, u, u, u, u, state_pad, state_pad, state_pad,
      w_conv_pad, w_conv_pad, w_conv_pad, s0, norm_g.reshape(1, GDN_D))


def _heads_to_rows(x, g, nt):
    lane = lax.broadcasted_iota(jnp.int32, (nt, LANES), 1)
    keep = (lane >= NSA_DH * g) & (lane < NSA_DH * (g + 1))
    parts = []
    for r in range(NSA_GQ):
        hh = NSA_GQ * g + r
        blk = x[:, (hh // 2) * LANES:(hh // 2 + 1) * LANES]
        if hh % 2 != g:
            blk = pltpu.roll(blk, NSA_DH, axis=1)
        parts.append(jnp.where(keep, blk, 0.0))
    return jnp.concatenate(parts, axis=0)


def _rows_to_heads(y, g, nt):
    outs = []
    for m in range(2):
        x0 = y[(2 * m) * nt:(2 * m + 1) * nt]
        x1 = y[(2 * m + 1) * nt:(2 * m + 2) * nt]
        if g == 1:
            x0 = pltpu.roll(x0, NSA_DH, axis=1)
        else:
            x1 = pltpu.roll(x1, NSA_DH, axis=1)
        outs.append(x0 + x1)
    return outs


def _masked_softmax_parts(parts, masks, axis):
    sm = [jnp.where(m, s, NEG) for s, m in zip(parts, masks)]
    mx = functools.reduce(jnp.maximum, [jnp.max(s, axis=axis, keepdims=True) for s in sm])
    es = [jnp.where(m, jnp.exp(s - mx), 0.0) for s, m in zip(sm, masks)]
    den = functools.reduce(lambda p, q: p + q, [jnp.sum(e, axis=axis, keepdims=True) for e in es])
    inv = 1.0 / jnp.maximum(den, 1e-30)
    return [e * inv for e in es]


def _bucket_np(rel):
    n = np.maximum(rel, 0)
    nf = np.maximum(n, 1).astype(np.float32)
    large = 16 + (np.log(nf / np.float32(16)) / np.float32(math.log(8.0)) * np.float32(16)).astype(np.int32)
    return np.where(n < 16, n, np.minimum(large, N_BUCKETS - 1)).astype(np.int32)


LOOKUP_TILE = 8192


def _lookup_kernel(idx_ref, tb_ref, o_ref):
    idx = idx_ref[...]
    acc = jnp.zeros(o_ref.shape, F32)
    for k in range(N_BUCKETS):
        acc = jnp.where(idx == k, tb_ref[:, k:k + 1], acc)
    o_ref[...] = acc


def _bias_lookup(rel_bias, idx_list):
    sizes = [int(np.prod(a.shape)) for a in idx_list]
    total = sum(sizes)
    padded = -(-total // LOOKUP_TILE) * LOOKUP_TILE
    flat = np.zeros((1, padded), np.int32)
    flat[0, :total] = np.concatenate([np.asarray(a, np.int32).reshape(-1) for a in idx_list])
    tab = pl.pallas_call(
        _lookup_kernel,
        grid=(padded // LOOKUP_TILE,),
        in_specs=[pl.BlockSpec((1, LOOKUP_TILE), lambda i: (0, i)),
                  pl.BlockSpec((NSA_HEADS, N_BUCKETS), lambda i: (0, 0))],
        out_specs=pl.BlockSpec((NSA_HEADS, LOOKUP_TILE), lambda i: (0, i)),
        out_shape=jax.ShapeDtypeStruct((NSA_HEADS, padded), F32),
        compiler_params=_params("parallel"),
        name="bias_lookup",
    )(jnp.asarray(flat), rel_bias.astype(F32).T)
    outs, off = [], 0
    for a, n in zip(idx_list, sizes):
        outs.append(tab[:, off:off + n].reshape((NSA_HEADS,) + tuple(a.shape)))
        off += n
    return outs


def _head_rows(tab):
    return tab.reshape(NSA_KV, NSA_GQ * tab.shape[1], tab.shape[2])


def _nsa_prompt_kernel(t_len, q_ref, kcmp_ref, vcmp_ref, kslc_ref, vslc_ref, kwin_ref, vwin_ref, gl_ref, rep_ref,
                       wk_ref, wv_ref, bc_ref, bct_ref, bnear_ref, bwin_ref, o_ref, kc_s, vc_s):
    i = pl.program_id(1)
    nsb = t_len // L_SEL
    qb = Q_BLOCK
    rows = NSA_GQ * qb

    @pl.when(i == 0)
    def _():
        n2 = 2 * lax.broadcasted_iota(jnp.int32, (nsb, t_len), 0)
        cb = lax.broadcasted_iota(jnp.int32, (nsb, t_len), 1) >> 5
        kc = _bf(kcmp_ref[0])
        vc = _bf(vcmp_ref[0])
        wk = wk_ref[...]
        wv = wv_ref[...]
        kc_s[0:nsb, :] = _dot(_bf(jnp.where(cb == n2, wk, 0.0)), kc)
        kc_s[nsb:2 * nsb, :] = _dot(_bf(jnp.where(cb == n2 + 1, wk, 0.0)), kc)
        vc_s[0:nsb, :] = _dot(_bf(jnp.where(cb == n2, wv, 0.0)), vc)
        vc_s[nsb:2 * nsb, :] = _dot(_bf(jnp.where(cb == n2 + 1, wv, 0.0)), vc)

    q_all = q_ref[0] * (NSA_DH ** -0.5)
    gl = gl_ref[0]
    gate_all = [_sigmoid(_pick_columns(gl, rep_ref[br])) for br in range(3)]
    t0 = i * qb
    tq = t0 + (lax.broadcasted_iota(jnp.int32, (rows, 1), 0) & (qb - 1))
    tl = t0 + (lax.broadcasted_iota(jnp.int32, (1, rows), 1) & (qb - 1))
    eye_q = _bf((lax.broadcasted_iota(jnp.int32, (qb, qb), 0) == lax.broadcasted_iota(jnp.int32, (qb, qb), 1)).astype(F32))
    far_end = jnp.maximum(t0 - qb, 0)
    n_far = (far_end + 511) >> 9
    kc = _bf(kc_s[...])
    vc = _bf(vc_s[...])

    def key_aug(k0, n_keys, limit):
        kpos = k0 + lax.broadcasted_iota(jnp.int32, (n_keys, LANES), 0)
        lane = lax.broadcasted_iota(jnp.int32, (n_keys, LANES), 1)
        hit = (lane == (kpos >> 6)) | ((lane == nsb) & ((kpos >= limit) | (kpos < 0)))
        return _bf(jnp.where(hit, NEG, 0.0))

    gs = range(NSA_KV)
    qg = [_bf(_heads_to_rows(q_all, g, qb)) for g in gs]

    n_prev = WINDOW // qb
    starts = [pl.multiple_of(jnp.maximum(t0 + (j - n_prev) * qb, 0), qb) for j in range(n_prev + 1)]
    kwin = [_bf(kwin_ref[0, pl.ds(st, qb), :]) for st in starts]
    pens = [jnp.where(i + (j - n_prev) >= 0, 0.0, NEG) for j in range(n_prev)] + [0.0]
    s_w = [jnp.concatenate([_dot_nt(qg[g], kwin[j]) + pens[j] for j in range(n_prev + 1)], axis=1) + bwin_ref[g]
           for g in gs]

    s_c = [_dot_nt(qg[g], kc) + bc_ref[0, g] for g in gs]
    s_t = [_dot_nt(kc, qg[g]) + bct_ref[0, g] for g in gs]
    e_c = [jnp.where(tq >= L_CMP - 1, jnp.exp(s - jnp.max(s, axis=1, keepdims=True)), 0.0) for s in s_c]
    p_c = [e * (1.0 / jnp.maximum(jnp.sum(e, axis=1, keepdims=True), 1e-30)) for e in e_c]
    o_c = [_dot(_bf(p), vc) for p in p_c]

    e_t = [jnp.where(tl >= L_CMP - 1, jnp.exp(s - jnp.max(s, axis=0, keepdims=True)), 0.0) for s in s_t]
    p_t = [e * (1.0 / jnp.maximum(jnp.sum(e, axis=0, keepdims=True), 1e-30)) for e in e_t]
    head_sum = lambda x: x[:, 0:qb] + x[:, qb:2 * qb] + x[:, 2 * qb:3 * qb] + x[:, 3 * qb:4 * qb]
    blk = lax.broadcasted_iota(jnp.int32, (nsb, qb), 0)
    cur = (t0 + lax.broadcasted_iota(jnp.int32, (nsb, qb), 1)) >> 6
    bonus = jnp.where((blk == 0) | (blk == cur) | (blk == cur - 1), FORCE_BONUS, 0.0)
    score = [jnp.where(blk <= cur, (head_sum(p[0:nsb]) + head_sum(p[nsb:2 * nsb])) + bonus, -1.0) for p in p_t]
    rank = [jnp.zeros((nsb, qb), F32) for _ in gs]
    for j in range(nsb):
        for g in gs:
            sj = score[g][j:j + 1, :]
            ahead = (sj > score[g]) | ((sj == score[g]) & (blk > j))
            rank[g] = rank[g] + jnp.where(ahead, 1.0, 0.0)
    pen_rows = jnp.where(lax.broadcasted_iota(jnp.int32, (LANES - nsb, qb), 0) == 0, 1.0, 0.0)
    not_sel_t = [jnp.where(r < float(min(N_SEL, nsb)), 0.0, 1.0) for r in rank]
    q_aug = [_bf(_dot_nt(eye_q, _bf(jnp.concatenate([ns, pen_rows], axis=0)))) for ns in not_sel_t]
    qa = [jnp.concatenate([qg[g], jnp.concatenate([q_aug[g]] * NSA_GQ, axis=0)], axis=1) for g in gs]

    def online(carry, s, pv):
        m_i, l_i, acc = carry
        m_n = jnp.maximum(m_i, jnp.max(s, axis=1, keepdims=True))
        p = jnp.exp(s - m_n)
        alpha = jnp.exp(m_i - m_n)
        return m_n, alpha * l_i + jnp.sum(p, axis=1, keepdims=True), alpha * acc + pv(_bf(p))

    def far_tile(kt, carry):
        k0 = pl.multiple_of(kt * 512, 512)
        ka = jnp.concatenate([_bf(kslc_ref[0, pl.ds(k0, 512), :]), key_aug(k0, 512, far_end)], axis=1)
        vt = _bf(vslc_ref[0, pl.ds(k0, 512), :])
        s = [_dot_nt(qa[g], ka) for g in gs]
        return tuple(online(carry[g], s[g], lambda p: _dot(p, vt)) for g in gs)

    init = (jnp.full((rows, 1), NEG, F32), jnp.zeros((rows, 1), F32), jnp.zeros((rows, LANES), F32))
    far = lax.fori_loop(0, n_far, far_tile, tuple(init for _ in gs))

    p0 = pl.multiple_of(jnp.maximum(t0 - qb, 0), qb)
    d0 = pl.multiple_of(t0, qb)
    ka = jnp.concatenate([
        jnp.concatenate([_bf(kslc_ref[0, pl.ds(p0, qb), :]), _bf(kslc_ref[0, pl.ds(d0, qb), :])], axis=0),
        key_aug(t0 - qb, 2 * qb, t_len)], axis=1)
    vp = _bf(vslc_ref[0, pl.ds(p0, qb), :])
    vd = _bf(vslc_ref[0, pl.ds(d0, qb), :])
    s_near = [_dot_nt(qa[g], ka) + bnear_ref[g] for g in gs]
    fin = [online(far[g], s_near[g], lambda p: _dot(p[:, 0:qb], vp) + _dot(p[:, qb:2 * qb], vd)) for g in gs]
    o_s = [acc * (1.0 / l_n) for _, l_n, acc in fin]

    vwin = [_bf(vwin_ref[0, pl.ds(st, qb), :]) for st in starts]
    e_w = [jnp.exp(s - jnp.max(s, axis=1, keepdims=True)) for s in s_w]
    o_w = []
    for g in gs:
        ew = _bf(e_w[g])
        acc = _dot(ew[:, 0:qb], vwin[0])
        for j in range(1, n_prev + 1):
            acc = acc + _dot(ew[:, j * qb:(j + 1) * qb], vwin[j])
        o_w.append(acc * (1.0 / jnp.sum(e_w[g], axis=1, keepdims=True)))

    for g in gs:
        gates = [_heads_to_rows(gate_all[br], g, qb) for br in range(3)]
        comb = gates[0] * o_c[g] + gates[1] * o_s[g] + gates[2] * o_w[g]
        blocks = _rows_to_heads(comb, g, qb)
        o_ref[0, :, (2 * g) * LANES:(2 * g + 1) * LANES] = blocks[0]
        o_ref[0, :, (2 * g + 1) * LANES:(2 * g + 2) * LANES] = blocks[1]


def _nsa_tables(rel_bias, t_len, n_pages, nt):
    nqb = t_len // Q_BLOCK
    nsb = t_len // L_SEL
    past = n_pages * PAGE
    t = np.arange(Q_BLOCK)
    tq = (np.arange(nqb)[:, None] * Q_BLOCK + t[None, :])[:, :, None]
    n = np.arange(nsb)[None, None, :]
    ts = np.arange(nt)[:, None]
    j = np.arange(2 * n_pages)[None, :]
    c = np.arange(PAGE)[None, :]
    idx = [
        _bucket_np(tq - (n * L_SEL + L_CMP - 1)),
        _bucket_np(tq - (n * L_SEL + L_SEL - 1)),
        _bucket_np(Q_BLOCK + t[:, None] - np.arange(2 * Q_BLOCK)[None, :]),
        _bucket_np(WINDOW + t[:, None] - np.arange(WINDOW + Q_BLOCK)[None, :]),
        _bucket_np(past + ts - (j * L_SEL + L_CMP - 1)),
        _bucket_np(past + ts - (j * L_SEL + L_SEL - 1)),
        _bucket_np(PAGE + ts - c),
        _bucket_np(ts - c),
        _bucket_np(WINDOW + ts - np.arange(WINDOW)[None, :]),
    ]
    ce, co, near, win, sce, sco, slast, snew, swin = _bias_lookup(rel_bias, idx)
    b31 = rel_bias.astype(F32)[N_BUCKETS - 1].reshape(NSA_KV, NSA_GQ, 1, 1)
    shift = lambda tab: (tab.reshape(NSA_KV, NSA_GQ, tab.shape[1], tab.shape[2]) - b31).reshape(
        NSA_KV, NSA_GQ * tab.shape[1], tab.shape[2])
    vis = lambda m: jnp.asarray(np.tile(m, (1,) * (m.ndim - 2) + (NSA_GQ, 1)))
    blocked = lambda tab: jnp.swapaxes(tab, 0, 1).reshape(nqb, NSA_KV, NSA_GQ * Q_BLOCK, nsb)
    bc = jnp.concatenate([blocked(ce), blocked(co)], axis=-1)
    vis_c = np.concatenate([n * L_SEL + L_CMP - 1 <= tq, n * L_SEL + L_SEL - 1 <= tq], axis=-1)
    bc = jnp.where(vis(vis_c)[:, None], bc, NEG)
    c_near = np.arange(2 * Q_BLOCK)[None, :]
    near_m = jnp.where(vis(c_near <= Q_BLOCK + t[:, None])[None], shift(near), NEG)
    c_win = np.arange(WINDOW + Q_BLOCK)[None, :]
    win_m = jnp.where(vis((c_win > t[:, None]) & (c_win <= WINDOW + t[:, None]))[None], _head_rows(win), NEG)
    ptab = (bc, jnp.swapaxes(bc, -1, -2), near_m, win_m)
    rows64 = lambda tab: tab.reshape(NSA_KV * NSA_GQ * nt, tab.shape[-1])
    stab = (rows64(_head_rows(sce)), rows64(_head_rows(sco)), rows64(shift(slast)), rows64(shift(snew)),
            rows64(_head_rows(swin)), rows64(_head_rows(snew)))
    return ptab, stab


def _gate_rep():
    j = np.arange(LANES)[None, :, None]
    c = np.arange(NSA_HEADS * NSA_DH)[None, None, :]
    br = np.arange(3)[:, None, None]
    return jnp.asarray(j == br * NSA_HEADS + c // NSA_DH, BF16)


def _nsa_prompt(u, w_pos, tables):
    bsz, t_len, _ = u.shape
    nqb = t_len // Q_BLOCK
    nsb = t_len // L_SEL
    assert nsb < LANES
    bc, bct, near, wtab = tables
    wk = jnp.tile(w_pos[0], t_len // L_CMP).reshape(1, t_len)
    wv = jnp.tile(w_pos[1], t_len // L_CMP).reshape(1, t_len)
    rep = _gate_rep()
    seq = lambda c: pl.BlockSpec((1, t_len, LANES), lambda b, i, c=c: (b, 0, c))
    full = lambda a: pl.BlockSpec(a.shape, lambda b, i, nd=a.ndim: (0,) * nd)
    per_i = lambda a: pl.BlockSpec((1,) + a.shape[1:], lambda b, i, nd=a.ndim: (i,) + (0,) * (nd - 1))
    kern = functools.partial(_nsa_prompt_kernel, t_len)
    kv0 = U_ROWS // LANES
    w0 = U_WIN // LANES
    return pl.pallas_call(
        kern,
        grid=(bsz, nqb),
        in_specs=[pl.BlockSpec((1, Q_BLOCK, 512), lambda b, i: (b, i, U_Q // 512)),
                  seq(kv0), seq(kv0 + 1), seq(kv0 + 2), seq(kv0 + 3), seq(w0), seq(w0 + 1),
                  pl.BlockSpec((1, Q_BLOCK, LANES), lambda b, i: (b, i, U_GL // LANES)), full(rep),
                  full(wk), full(wv), per_i(bc), per_i(bct), full(near), full(wtab)],
        out_specs=pl.BlockSpec((1, Q_BLOCK, 512), lambda b, i: (b, i, 0)),
        out_shape=jax.ShapeDtypeStruct((bsz, t_len, 512), F32),
        scratch_shapes=[pltpu.VMEM((2 * nsb, LANES), F32)] * 2,
        compiler_params=_params("parallel", "arbitrary"),
        name="nsa_prompt",
    )(u, u, u, u, u, u, u, u, rep, wk, wv, bc, bct, near, wtab)


def _nsa_sample_kernel(layer, n_pages, n_new, pt_ref, cache_ref, q_ref, rows_ref, wnew_ref, wbuf_ref, gl_ref,
                       rep_ref, wpool_ref, bce_ref, bco_ref, blast_ref, bnew_ref, bwin_ref, bwnew_ref,
                       o_ref, cmp_s, slc_s, pool_s, exp_s, pad_s, sem):
    b = pl.program_id(0)
    nb = pl.num_programs(0)
    nt = 8
    past = n_pages * PAGE
    nblk = 2 * n_pages
    rows = NSA_KV * NSA_GQ * nt
    half = 2 * LANES

    def page_copy(seq, p, part, buf, s):
        return pltpu.make_async_copy(
            cache_ref.at[layer, pt_ref[seq, p], pl.ds(part * half, half), :],
            buf.at[:, pl.ds(pl.multiple_of(p * PAGE, PAGE), PAGE)], s)

    def start_gather(seq, part, buf, s):
        def body(p, c):
            page_copy(seq, p, part, buf, s).start()
            return c
        lax.fori_loop(0, n_pages, body, 0)

    def wait_gather(seq, part, buf, s):
        def body(p, c):
            page_copy(seq, p, part, buf, s).wait()
            return c
        lax.fori_loop(0, n_pages, body, 0)

    @pl.when(b == 0)
    def _():
        start_gather(0, 0, cmp_s, sem.at[0])
        start_gather(0, 1, slc_s, sem.at[1])
        cb = lax.broadcasted_iota(jnp.int32, (past, nblk), 0) >> 5
        j2 = 2 * lax.broadcasted_iota(jnp.int32, (past, nblk), 1)
        pool_s[0] = _bf(jnp.where(cb == j2, 1.0, 0.0))
        pool_s[1] = _bf(jnp.where(cb == j2 + 1, 1.0, 0.0))
        ej = lax.broadcasted_iota(jnp.int32, (nblk, past), 0)
        ec = lax.broadcasted_iota(jnp.int32, (nblk, past), 1) >> 6
        exp_s[...] = _bf(jnp.where(ej == ec, 1.0, 0.0))
        pad_s[...] = jnp.zeros(pad_s.shape, F32)

    q_all = q_ref[0] * (NSA_DH ** -0.5)
    qq = _bf(jnp.concatenate([_heads_to_rows(q_all, g, nt) for g in range(NSA_KV)], axis=0))
    tr = lax.broadcasted_iota(jnp.int32, (rows, 1), 0) & (nt - 1)

    wait_gather(b, 0, cmp_s, sem.at[0])
    wp = wpool_ref[...]
    ks = _bf(cmp_s[0:LANES, :] * wp[0:1, :])
    vs = _bf(cmp_s[LANES:half, :] * wp[1:2, :])
    kce = _bf(_dot(ks, pool_s[0]))
    kco = _bf(_dot(ks, pool_s[1]))
    vce = _bf(_dot(vs, pool_s[0]))
    vco = _bf(_dot(vs, pool_s[1]))

    @pl.when(b + 1 < nb)
    def _():
        start_gather(b + 1, 0, cmp_s, sem.at[0])

    se = _dot(qq, kce) + bce_ref[...]
    so = _dot(qq, kco) + bco_ref[...]
    mx = jnp.maximum(jnp.max(se, axis=1, keepdims=True), jnp.max(so, axis=1, keepdims=True))
    ee = jnp.exp(se - mx)
    eo = jnp.exp(so - mx)
    inv = 1.0 / (jnp.sum(ee, axis=1, keepdims=True) + jnp.sum(eo, axis=1, keepdims=True))
    pe = ee * inv
    po = eo * inv
    o_c = _dot_nt(_bf(pe), vce) + _dot_nt(_bf(po), vco)

    def head_sum(pr):
        return jnp.concatenate(
            [pr[g * 4 * nt:g * 4 * nt + nt] + pr[g * 4 * nt + nt:g * 4 * nt + 2 * nt]
             + pr[g * 4 * nt + 2 * nt:g * 4 * nt + 3 * nt] + pr[g * 4 * nt + 3 * nt:g * 4 * nt + 4 * nt]
             for g in range(NSA_KV)], axis=0)

    jcol = lax.broadcasted_iota(jnp.int32, (NSA_KV * nt, nblk), 1)
    forced = (jcol == 0) | (jcol == nblk - 1)
    score = (head_sum(pe) + head_sum(po)) + jnp.where(forced, FORCE_BONUS, 0.0)
    rank = jnp.where(FORCE_BONUS > score, 1.0, 0.0)
    for j in range(nblk):
        sj = score[:, j:j + 1]
        ahead = (sj > score) | ((sj == score) & (jcol > j))
        rank = rank + jnp.where(ahead, 1.0, 0.0)
    sel = jnp.where(rank < float(N_SEL), 1.0, 0.0)
    sel_rows = jnp.concatenate([sel[g * nt:(g + 1) * nt] for g in range(NSA_KV) for _ in range(NSA_GQ)], axis=0)

    new = rows_ref[0]
    wnew = wnew_ref[0]
    pad_s[0, 0:nt, :] = new[:, 2 * LANES:3 * LANES]
    pad_s[1, 0:nt, :] = new[:, 3 * LANES:4 * LANES]
    pad_s[2, 0:nt, :] = wnew[:, 0:LANES]
    pad_s[3, 0:nt, :] = wnew[:, LANES:2 * LANES]
    tc = lax.broadcasted_iota(jnp.int32, (rows, LANES), 1)
    mnew = (tc <= tr) & (tc < n_new)

    wait_gather(b, 1, slc_s, sem.at[1])
    s_all = _dot(qq, _bf(slc_s[0:LANES, :]))
    mk = _dot(_bf(sel_rows), exp_s[...]) > 0.5
    far = past - PAGE
    p_far, p_last, p_new = _masked_softmax_parts(
        [s_all[:, :far], s_all[:, far:] + blast_ref[...], _dot_nt(qq, _bf(pad_s[0])) + bnew_ref[...]],
        [mk[:, :far], mk[:, far:], mnew], 1)
    o_s = _dot_nt(_bf(jnp.concatenate([p_far, p_last], axis=1)), _bf(slc_s[LANES:half, :])) \
        + _dot(_bf(p_new), _bf(pad_s[1]))

    wb = wbuf_ref[0, 0]
    cw = lax.broadcasted_iota(jnp.int32, (rows, WINDOW), 1)
    pw, pn = _masked_softmax_parts(
        [_dot(qq, _bf(wb[0:LANES, :])) + bwin_ref[...], _dot_nt(qq, _bf(pad_s[2])) + bwnew_ref[...]],
        [cw > tr, mnew], 1)
    o_w = _dot_nt(_bf(pw), _bf(wb[LANES:half, :])) + _dot(_bf(pn), _bf(pad_s[3]))

    gl = gl_ref[0]
    gate_all = [_sigmoid(_pick_columns(gl, rep_ref[br])) for br in range(3)]
    gates = [jnp.concatenate([_heads_to_rows(ga, g, nt) for g in range(NSA_KV)], axis=0) for ga in gate_all]
    comb = gates[0] * o_c + gates[1] * o_s + gates[2] * o_w
    for g in range(NSA_KV):
        blocks = _rows_to_heads(comb[g * 4 * nt:(g + 1) * 4 * nt], g, nt)
        o_ref[0, :, (2 * g) * LANES:(2 * g + 1) * LANES] = blocks[0]
        o_ref[0, :, (2 * g + 1) * LANES:(2 * g + 2) * LANES] = blocks[1]

    @pl.when(b + 1 < nb)
    def _():
        start_gather(b + 1, 1, slc_s, sem.at[1])


def _nsa_sample(layer, cache_t, page_table, u, wbuf_t, w_pos, tables, n_new):
    bsz, n_pages = page_table.shape
    nt = u.shape[1]
    past = n_pages * PAGE
    wpool = jnp.tile(w_pos, (1, past // L_CMP))
    rep = _gate_rep()
    full = lambda a: pl.BlockSpec(a.shape, lambda b, pt, nd=a.ndim: (0,) * nd)
    ucols = lambda width, off: pl.BlockSpec((1, nt, width), lambda b, pt: (b, 0, off // width))
    kern = functools.partial(_nsa_sample_kernel, layer, n_pages, n_new)
    grid_spec = pltpu.PrefetchScalarGridSpec(
        num_scalar_prefetch=1,
        grid=(bsz,),
        in_specs=[pl.BlockSpec(memory_space=pl.ANY),
                  ucols(512, U_Q), ucols(512, U_ROWS), ucols(2 * LANES, U_WIN),
                  pl.BlockSpec((1, 1) + wbuf_t.shape[2:], lambda b, pt: (layer, b, 0, 0)),
                  ucols(LANES, U_GL), full(rep), full(wpool)] + [full(t) for t in tables],
        out_specs=pl.BlockSpec((1, nt, 512), lambda b, pt: (b, 0, 0)),
        scratch_shapes=[pltpu.VMEM((2 * LANES, past), F32), pltpu.VMEM((2 * LANES, past), F32),
                        pltpu.VMEM((2, past, 2 * n_pages), BF16), pltpu.VMEM((2 * n_pages, past), BF16),
                        pltpu.VMEM((4, LANES, LANES), F32), pltpu.SemaphoreType.DMA((2,))],
    )
    return pl.pallas_call(
        kern,
        grid_spec=grid_spec,
        out_shape=jax.ShapeDtypeStruct((bsz, nt, 512), F32),
        compiler_params=_params("arbitrary"),
        name="nsa_sample",
    )(page_table, cache_t, u, u, u, wbuf_t, u, rep, wpool, *tables)


def _mixout_kernel(x_ref, ca_ref, ob_ref, oc_ref, ga_ref, gb_ref, gc_ref, wpa_ref, wpb_ref, wpc_ref, wo_ref, o_ref):
    y = _sigmoid(ga_ref[...]) * _dot(_bf(ca_ref[...]), wpa_ref[...])
    y = y + _sigmoid(gb_ref[...]) * _dot(_bf(ob_ref[...]), wpb_ref[...])
    y = y + _sigmoid(gc_ref[...]) * _dot(_bf(oc_ref[...]), wpc_ref[...])
    o_ref[...] = x_ref[...] + _dot(_bf(y), wo_ref[...])


def _mixout(x, ca, ob, oc, u, wpa, wpb, wpc, wo):
    m = x.shape[0]
    tm = min(m, 512)
    rowblk = lambda n: pl.BlockSpec((tm, n), lambda i: (i, 0))
    gate = lambda k: pl.BlockSpec((tm, D_MODEL), lambda i, k=k: (i, U_G // D_MODEL + k))
    full = lambda a: pl.BlockSpec(a.shape, lambda i: (0, 0))
    return pl.pallas_call(
        _mixout_kernel,
        grid=(m // tm,),
        in_specs=[rowblk(D_MODEL), rowblk(512), rowblk(512), rowblk(512), gate(0), gate(1), gate(2),
                  full(wpa), full(wpb), full(wpc), full(wo)],
        out_specs=rowblk(D_MODEL),
        out_shape=jax.ShapeDtypeStruct((m, D_MODEL), F32),
        compiler_params=_params("parallel"),
        name="mixer_out",
    )(x, ca, ob, oc, u, u, u, wpa, wpb, wpc, wo)


def _xattn_kernel(x_ref, g_ref, kv_ref, wq_ref, wo_ref, o_ref):
    x = x_ref[0]
    h = _bf(x * lax.rsqrt(jnp.mean(x * x, axis=-1, keepdims=True) + EPS) * g_ref[...])
    q = _dot(h, wq_ref[...])
    kv = kv_ref[0]
    outs = []
    for hd in range(X_HEADS):
        qh = _bf(q[:, hd * X_DH:(hd + 1) * X_DH])
        kh = _bf(kv[:, hd * X_DH:(hd + 1) * X_DH])
        vh = _bf(kv[:, D_MODEL + hd * X_DH:D_MODEL + (hd + 1) * X_DH])
        s = _dot_nt(qh, kh) * (X_DH ** -0.5)
        e = jnp.exp(s - jnp.max(s, axis=-1, keepdims=True))
        pr = e * (1.0 / jnp.sum(e, axis=-1, keepdims=True))
        outs.append(_dot(_bf(pr), vh))
    o = jnp.concatenate(outs, axis=1)
    o_ref[0] = x + _dot(_bf(o), wo_ref[...])


def _cross_attn(x, g, mem_kv, wq, wo):
    bsz, t_len, d = x.shape
    tt = min(t_len, 512)
    full = lambda a: pl.BlockSpec(a.shape, lambda b, t: (0, 0))
    return pl.pallas_call(
        _xattn_kernel,
        grid=(bsz, t_len // tt),
        in_specs=[pl.BlockSpec((1, tt, d), lambda b, t: (b, t, 0)),
                  pl.BlockSpec((1, d), lambda b, t: (0, 0)),
                  pl.BlockSpec((1, N_MEM, 2 * d), lambda b, t: (b, 0, 0)),
                  full(wq), full(wo)],
        out_specs=pl.BlockSpec((1, tt, d), lambda b, t: (b, t, 0)),
        out_shape=jax.ShapeDtypeStruct((bsz, t_len, d), F32),
        compiler_params=_params("parallel", "parallel"),
        name="cross_attn",
    )(x, g.reshape(1, d), mem_kv, wq, wo)


FF_CHUNK = 1024


def _mlp_kernel(n_k, x_ref, g_ref, w1_ref, w2_ref, o_ref, h_s, acc_s):
    k = pl.program_id(1)

    @pl.when(k == 0)
    def _():
        x = x_ref[...]
        h_s[...] = _bf(x * lax.rsqrt(jnp.mean(x * x, axis=-1, keepdims=True) + EPS) * g_ref[...])
        acc_s[...] = x

    a = jnp.maximum(_dot(h_s[...], w1_ref[...]), 0.0)
    acc_s[...] += _dot(_bf(a * a), w2_ref[...])

    @pl.when(k == n_k - 1)
    def _():
        o_ref[...] = acc_s[...]


def _mlp(x, g, w1, w2):
    m, d = x.shape
    tm = min(m, 1024)
    n_k = D_FF // FF_CHUNK
    return pl.pallas_call(
        functools.partial(_mlp_kernel, n_k),
        grid=(m // tm, n_k),
        in_specs=[pl.BlockSpec((tm, d), lambda i, k: (i, 0)),
                  pl.BlockSpec((1, d), lambda i, k: (0, 0)),
                  pl.BlockSpec((d, FF_CHUNK), lambda i, k: (0, k)),
                  pl.BlockSpec((FF_CHUNK, d), lambda i, k: (k, 0))],
        out_specs=pl.BlockSpec((tm, d), lambda i, k: (i, 0)),
        out_shape=jax.ShapeDtypeStruct((m, d), F32),
        scratch_shapes=[pltpu.VMEM((tm, d), BF16), pltpu.VMEM((tm, d), F32)],
        compiler_params=_params("parallel", "arbitrary"),
        name="sq_relu_mlp",
    )(x, g.reshape(1, d), w1, w2)


A_COLS = 2 * CONV_CH
B0 = A_COLS
Z0 = B0 + GDN_QKV
AB0 = Z0 + GDN_HEADS * GDN_D
C0 = AB0 + 2 * GDN_HEADS
KV0 = C0 + NSA_HEADS * NSA_DH
GL0 = KV0 + 6 * NSA_KV * NSA_DH
G0 = GL0 + 3 * NSA_HEADS
N_IN = G0 + 3 * D_MODEL


def _layer_weights(l, w_in, w_pa, w_pb, w_pc, w_o, w_xq, w_xk, w_xv, w_xo, w_ff1, w_ff2):
    w = w_in[l]
    lane_pad = lambda cols: jnp.pad(cols, ((0, 0), (0, LANES - cols.shape[1])))
    groups = [(U_A, w[:, 0:A_COLS]), (U_Q, w[:, C0:KV0]), (U_ROWS, w[:, KV0:KV0 + 4 * LANES]),
              (U_QKV, w[:, B0:Z0]), (U_Z, w[:, Z0:AB0]), (U_G, w[:, G0:N_IN]),
              (U_WIN, w[:, KV0 + 4 * LANES:GL0]), (U_AB, lane_pad(w[:, AB0:C0])), (U_GL, lane_pad(w[:, GL0:G0]))]
    off = 0
    for start, cols in groups:
        assert start == off
        off += cols.shape[1]
    assert off == U_N
    return {
        "in": _bf(jnp.concatenate([cols for _, cols in groups], axis=1)),
        "pa": _bf(w_pa[l]), "pb": _bf(w_pb[l]), "pc": _bf(w_pc[l]), "o": _bf(w_o[l]),
        "xq": _bf(w_xq[l]), "xo": _bf(w_xo[l]),
        "xkv": _bf(jnp.concatenate([w_xk[l], w_xv[l]], axis=1)),
        "ff1": _bf(w_ff1[l]), "ff2": _bf(w_ff2[l]),
    }


def _mixers(x, lw, p, l, conv_state_pad, qkv_state_pad, s0, n_valid, gdn_len, nsa_fn):
    bsz, t_len, d = x.shape
    m = bsz * t_len
    x2 = x.reshape(m, d)
    u2 = _proj_in(x2, p["norm_mix"][l], lw["in"])
    u = u2.reshape(bsz, t_len, U_N)
    rows = u[:, :, U_ROWS:U_ROWS + 4 * LANES]
    win = u[:, :, U_WIN:U_WIN + 2 * LANES]
    qkv_tail = u[:, max(n_valid - 3, 0):n_valid, U_QKV:U_QKV + GDN_QKV]

    ca, conv_new = _conformer(u, conv_state_pad, p["conv_a_w"][l], p["conv_a_b"][l], p["ln_a_g"][l],
                              p["ln_a_b"][l], n_valid if n_valid < t_len else min(t_len, 256))
    w_conv_pad = jnp.pad(p["gdn_conv_w"][l], ((0, 4), (0, 0)))
    gdn_args = (qkv_state_pad, s0, w_conv_pad, p["gdn_a_log"][l], p["gdn_dt_bias"][l], p["gdn_norm_g"][l])
    if gdn_len == t_len:
        ob, s_new = _gated_deltanet(u, U_QKV // LANES, U_Z // LANES, U_AB // LANES, *gdn_args, gdn_len)
    else:
        ug = jnp.concatenate([u[:, :, U_QKV:U_G], u[:, :, U_AB:U_AB + LANES]], axis=-1)
        ug = jnp.pad(ug, ((0, 0), (0, gdn_len - t_len), (0, 0)))
        ob, s_new = _gated_deltanet(ug, 0, (U_Z - U_QKV) // LANES, (U_G - U_QKV) // LANES, *gdn_args, n_valid)
        ob = ob[:, :t_len]
    oc = nsa_fn(u)
    x_new = _mixout(x2, ca.reshape(m, -1), ob.reshape(m, -1), oc.reshape(m, -1), u2,
                    lw["pa"], lw["pb"], lw["pc"], lw["o"])
    return x_new.reshape(bsz, t_len, d), conv_new[:, HALO - (CONV_W - 1):], qkv_tail, s_new, rows, win


def kernel(x_prompt, x_sample, cache_nsa_kv, cache_win_kv, state_conv_a, state_conv_qkv, state_gdn, cache_mem_kv,
           page_table, mem_prompt, rel_bias, norm_mix, w_in, conv_a_w, conv_a_b, ln_a_g, ln_a_b, w_pa, gdn_conv_w,
           gdn_a_log, gdn_dt_bias, gdn_norm_g, w_pb, nsa_cmp_w, w_pc, w_o, norm_x, w_xq, w_xk, w_xv, w_xo,
           norm_mlp, w_ff1, w_ff2, norm_final):
    p = {"norm_mix": norm_mix, "conv_a_w": conv_a_w, "conv_a_b": conv_a_b, "ln_a_g": ln_a_g, "ln_a_b": ln_a_b,
         "gdn_conv_w": gdn_conv_w, "gdn_a_log": gdn_a_log, "gdn_dt_bias": gdn_dt_bias, "gdn_norm_g": gdn_norm_g}
    depth = w_in.shape[0]
    bp, tp, d = x_prompt.shape
    bs, ts, _ = x_sample.shape
    ts_pad = 8
    n_pages = page_table.shape[1]
    wb = cache_win_kv.shape[2]
    xp = x_prompt
    xs = jnp.pad(x_sample, ((0, 0), (0, ts_pad - ts), (0, 0)))
    ptab, stab = _nsa_tables(rel_bias, tp, n_pages, ts_pad)
    cache_t = jnp.transpose(cache_nsa_kv, (0, 1, 3, 4, 5, 2)).reshape(depth, -1, 4 * LANES, PAGE)
    wbuf_t = jnp.transpose(cache_win_kv, (0, 1, 3, 4, 5, 2)).reshape(depth, bs, 2 * LANES, wb)
    outs = {k: [] for k in ("p_rows", "p_win", "p_conv", "p_qkv", "p_gdn", "p_mem",
                            "s_rows", "s_win", "s_conv", "s_qkv", "s_gdn")}
    for l in range(depth):
        lw = _layer_weights(l, w_in, w_pa, w_pb, w_pc, w_o, w_xq, w_xk, w_xv, w_xo, w_ff1, w_ff2)
        nsa_p = lambda u: _nsa_prompt(u, nsa_cmp_w[l], ptab)
        xp, conv_n, qkv_tail, s_n, rows, win = _mixers(
            xp, lw, p, l, jnp.zeros((bp, HALO, CONV_CH), F32), jnp.zeros((bp, 8, GDN_QKV), F32),
            jnp.zeros((bp, GDN_HEADS, GDN_D, GDN_D), F32), tp, tp, nsa_p)
        mem_kv = _matmul(_bf(mem_prompt.reshape(bp * N_MEM, d)), lw["xkv"]).reshape(bp, N_MEM, 2 * d)
        xp = _cross_attn(xp, norm_x[l], mem_kv, lw["xq"], lw["xo"])
        xp = _mlp(xp.reshape(bp * tp, d), norm_mlp[l], lw["ff1"], lw["ff2"]).reshape(bp, tp, d)
        outs["p_rows"].append(rows.reshape(bp, tp, 4, NSA_KV, NSA_DH))
        outs["p_win"].append(win[:, tp - min(WINDOW, tp):].reshape(bp, min(WINDOW, tp), 2, NSA_KV, NSA_DH))
        outs["p_conv"].append(conv_n)
        outs["p_qkv"].append(qkv_tail)
        outs["p_gdn"].append(s_n)
        outs["p_mem"].append(mem_kv.reshape(bp, N_MEM, 2, X_HEADS, X_DH))
        nsa_s = lambda u: _nsa_sample(l, cache_t, page_table, u, wbuf_t, nsa_cmp_w[l], stab, ts)
        conv_pad = jnp.pad(state_conv_a[l], ((0, 0), (HALO - (CONV_W - 1), 0), (0, 0)))
        qkv_pad = jnp.pad(state_conv_qkv[l], ((0, 0), (5, 0), (0, 0)))
        xs, conv_n, qkv_tail, s_n, rows, win = _mixers(
            xs, lw, p, l, conv_pad, qkv_pad, state_gdn[l], ts, GDN_CHUNK, nsa_s)
        mkv = cache_mem_kv[l].reshape(bs, N_MEM, 2 * d)
        xs = _cross_attn(xs, norm_x[l], mkv, lw["xq"], lw["xo"])
        xs = _mlp(xs.reshape(bs * ts_pad, d), norm_mlp[l], lw["ff1"], lw["ff2"]).reshape(bs, ts_pad, d)
        outs["s_rows"].append(rows[:, :ts].reshape(bs, ts, 4, NSA_KV, NSA_DH))
        win_new = win[:, :ts].reshape(bs, ts, 2, NSA_KV, NSA_DH)
        outs["s_win"].append(jnp.concatenate([cache_win_kv[l], win_new], axis=1)[:, ts:])
        outs["s_conv"].append(conv_n)
        outs["s_qkv"].append(qkv_tail)
        outs["s_gdn"].append(s_n)
    y_prompt = _rmsnorm(xp.reshape(bp * tp, d), norm_final, F32).reshape(bp, tp, d)
    y_sample = _rmsnorm(xs.reshape(bs * ts_pad, d), norm_final, F32).reshape(bs, ts_pad, d)[:, :ts]
    st = lambda k: jnp.stack(outs[k], axis=0)
    return (y_prompt, y_sample, st("p_rows"), st("p_win"), st("p_conv"), st("p_qkv"), st("p_gdn"), st("p_mem"),
            st("s_rows"), st("s_win"), st("s_conv"), st("s_qkv"), st("s_gdn"))
```

```python
import functools
import math

import jax
import jax.numpy as jnp
import numpy as np
from jax import lax
from jax.experimental import pallas as pl
from jax.experimental.pallas import tpu as pltpu

F32 = jnp.float32
BF16 = jnp.bfloat16

D_MODEL = 1024
CONV_CH = 512
CONV_W = 31
GDN_HEADS = 4
GDN_D = 128
GDN_CHUNK = 64
GDN_QKV = 3 * GDN_HEADS * GDN_D
NSA_HEADS = 8
NSA_KV = 2
NSA_GQ = 4
NSA_DH = 64
L_CMP = 32
L_SEL = 64
N_SEL = 16
WINDOW = 512
Q_BLOCK = 128
FORCE_BONUS = 1e4
PAGE = 128
N_MEM = 256
X_HEADS = 4
X_DH = 256
D_FF = 4096
N_BUCKETS = 32
EPS = 1e-6
NEG = -1e30

LANES = 128
HALO = 32
VMEM_LIMIT = 48 * 1024 * 1024
GDN_VMEM_LIMIT = 56 * 1024 * 1024


def _bf(x):
    return x.astype(BF16)


def _dot(a, b):
    return jnp.dot(a, b, preferred_element_type=F32)


def _dot_nt(a, b):
    return lax.dot_general(a, b, (((1,), (1,)), ((), ())), preferred_element_type=F32)


def _sigmoid(x):
    return 1.0 / (1.0 + jnp.exp(-x))


def _silu(x):
    return x * _sigmoid(x)


def _params(*sem):
    return pltpu.CompilerParams(dimension_semantics=sem, vmem_limit_bytes=VMEM_LIMIT)


def _mm_kernel(a_ref, w_ref, o_ref):
    o_ref[...] = _dot(a_ref[...], w_ref[...])


def _col_tile(n):
    for tn in (1024, 768, 512, 384, 256, 128):
        if n % tn == 0:
            return tn
    raise ValueError(n)


def _matmul(a, w):
    m, k = a.shape
    n = w.shape[1]
    tm = min(m, 1024)
    tn = _col_tile(n)
    return pl.pallas_call(
        _mm_kernel,
        grid=(m // tm, n // tn),
        in_specs=[pl.BlockSpec((tm, k), lambda i, j: (i, 0)), pl.BlockSpec((k, tn), lambda i, j: (0, j))],
        out_specs=pl.BlockSpec((tm, tn), lambda i, j: (i, j)),
        out_shape=jax.ShapeDtypeStruct((m, n), F32),
        compiler_params=_params("parallel", "parallel"),
        name="matmul",
    )(a, w)


U_A = 0
U_Q = 1024
U_ROWS = 1536
U_QKV = 2048
U_Z = 3584
U_G = 4096
U_WIN = 7168
U_AB = 7424
U_GL = 7552
U_N = 7680
PROJ_TN = 1536


def _proj_kernel(x_ref, g_ref, w_ref, o_ref, h_s):
    @pl.when(pl.program_id(1) == 0)
    def _():
        x = x_ref[...]
        h_s[...] = _bf(x * lax.rsqrt(jnp.mean(x * x, axis=-1, keepdims=True) + EPS) * g_ref[...])

    o_ref[...] = _dot(h_s[...], w_ref[...])


def _proj_in(x, g, w):
    m, d = x.shape
    n = w.shape[1]
    tm = min(m, 1024)
    return pl.pallas_call(
        _proj_kernel,
        grid=(m // tm, n // PROJ_TN),
        in_specs=[pl.BlockSpec((tm, d), lambda i, j: (i, 0)), pl.BlockSpec((1, d), lambda i, j: (0, 0)),
                  pl.BlockSpec((d, PROJ_TN), lambda i, j: (0, j))],
        out_specs=pl.BlockSpec((tm, PROJ_TN), lambda i, j: (i, j)),
        out_shape=jax.ShapeDtypeStruct((m, n), F32),
        scratch_shapes=[pltpu.VMEM((tm, d), BF16)],
        compiler_params=_params("parallel", "arbitrary"),
        name="proj_in",
    )(x, g.reshape(1, d), w)


def _split3(x):
    hi = _bf(x)
    r = x - hi.astype(F32)
    mid = _bf(r)
    return hi, mid, _bf(r - mid.astype(F32))


def _pick_columns(x, onehot):
    hi, mid, lo = _split3(x)
    return (_dot(lo, onehot) + _dot(mid, onehot)) + _dot(hi, onehot)


def _conf_kernel(n_t, tt, tv, u_ref, halo_ref, st_ref, w_ref, b_ref, g_ref, lb_ref, o_ref, nb_ref, xc_ref, zs_ref):
    t = pl.program_id(1)
    u = u_ref[0]
    xc_ref[HALO:HALO + tt, :] = u[:, :CONV_CH] * _sigmoid(u[:, CONV_CH:])
    if n_t > 1:
        uh = halo_ref[0]
        gh = uh[:, :CONV_CH] * _sigmoid(uh[:, CONV_CH:])
        xc_ref[0:HALO, :] = jnp.where(t > 0, gh, st_ref[0])
    else:
        xc_ref[0:HALO, :] = st_ref[0]
    off = HALO - (CONV_W - 1)
    span = tt + HALO - 8
    for r in range(1, 8):
        zs_ref[r - 1] = xc_ref[r:r + span, :]
    acc = None
    for i in range(CONV_W):
        pos = off + i
        r, base = pos % 8, pos - pos % 8
        src = xc_ref[base:base + tt, :] if r == 0 else zs_ref[r - 1, base:base + tt, :]
        term = src * w_ref[i:i + 1, :]
        acc = term if acc is None else acc + term
    y = acc + b_ref[...]
    mu = jnp.mean(y, axis=-1, keepdims=True)
    yc = y - mu
    var = jnp.mean(yc * yc, axis=-1, keepdims=True)
    ln = yc * lax.rsqrt(var + EPS) * g_ref[...] + lb_ref[...]
    o_ref[0] = _silu(ln)

    @pl.when(t == n_t - 1)
    def _():
        nb_ref[0] = xc_ref[tv:tv + HALO, :]


def _conformer(u_a, state_pad, w_dw, b_dw, ln_g, ln_b, n_valid_last):
    bsz, t_len, _ = u_a.shape
    tt = min(t_len, 256)
    n_t = t_len // tt
    hb = tt // HALO if n_t > 1 else 1
    halo_rows = HALO if n_t > 1 else tt
    w_pad = jnp.pad(w_dw, ((0, HALO - CONV_W), (0, 0)))
    row = lambda v: v.reshape(1, CONV_CH)
    kern = functools.partial(_conf_kernel, n_t, tt, n_valid_last)
    return pl.pallas_call(
        kern,
        grid=(bsz, n_t),
        in_specs=[
            pl.BlockSpec((1, tt, 2 * CONV_CH), lambda b, t: (b, t, 0)),
            pl.BlockSpec((1, halo_rows, 2 * CONV_CH), lambda b, t: (b, jnp.maximum(t * hb - 1, 0), 0)),
            pl.BlockSpec((1, HALO, CONV_CH), lambda b, t: (b, 0, 0)),
            pl.BlockSpec((HALO, CONV_CH), lambda b, t: (0, 0)),
            pl.BlockSpec((1, CONV_CH), lambda b, t: (0, 0)),
            pl.BlockSpec((1, CONV_CH), lambda b, t: (0, 0)),
            pl.BlockSpec((1, CONV_CH), lambda b, t: (0, 0)),
        ],
        out_specs=[
            pl.BlockSpec((1, tt, CONV_CH), lambda b, t: (b, t, 0)),
            pl.BlockSpec((1, HALO, CONV_CH), lambda b, t: (b, 0, 0)),
        ],
        out_shape=[
            jax.ShapeDtypeStruct((bsz, t_len, CONV_CH), F32),
            jax.ShapeDtypeStruct((bsz, HALO, CONV_CH), F32),
        ],
        scratch_shapes=[pltpu.VMEM((HALO + tt, CONV_CH), F32), pltpu.VMEM((7, tt + HALO - 8, CONV_CH), F32)],
        compiler_params=_params("parallel", "arbitrary"),
        name="conformer_conv",
    )(u_a, u_a, state_pad, w_pad, row(b_dw), row(ln_g), row(ln_b))


def _tri_inverse(a_list, ii, jj, merge_shifts):
    mm = lambda p, q: _dot(_bf(p), _bf(q))
    a0 = [jnp.where((ii >> 3) == (jj >> 3), a, 0.0) for a in a_list]
    a2 = [mm(p, p) for p in a0]
    a4 = [mm(p, p) for p in a2]
    r = [(q - p) - mm(p, q) for p, q in zip(a0, a2)]
    r = [(p + q) + mm(p, q) for p, q in zip(r, a4)]
    for sh in merge_shifts:
        mask = ((ii >> (sh + 1)) == (jj >> (sh + 1))) & ((ii >> sh) != (jj >> sh))
        off = [jnp.where(mask, a, 0.0) for a in a_list]
        t = [o + mm(o, p) for o, p in zip(off, r)]
        r = [p - (q + mm(p, q)) for p, q in zip(r, t)]
    return r


def _softplus(x):
    return jnp.maximum(x, 0.0) + jnp.log1p(jnp.exp(-jnp.abs(x)))


def _gdn_kernel(t_len, n_valid, hp, alog_ref, dtb_ref, q_ref, k_ref, v_ref, z_ref, ab_ref,
                sq_ref, sk_ref, sv_ref, wq_ref, wk_ref, wv_ref, s0_ref, ng_ref,
                o_ref, sn_ref,
                xp_s, qn_s, kn_s, vn_s, g_s, be_s, vw_s, kcd_s, qg_s, kdt_s, qk_s, ge_s):
    h0 = pl.program_id(1) * hp
    n_chunks = t_len // GDN_CHUNK
    c_len = GDN_CHUNK

    heads = range(hp)
    hcols = lambda hh: slice(hh * GDN_D, (hh + 1) * GDN_D)

    def conv(x_ref, st_ref, w_ref, hh):
        xp_s[0:8, :] = st_ref[0, :, hcols(hh)]
        xp_s[8:8 + t_len, :] = x_ref[0, :, hcols(hh)]
        acc = xp_s[5:5 + t_len, :] * w_ref[0:1, hcols(hh)]
        for i in range(1, 4):
            acc = acc + xp_s[5 + i:5 + i + t_len, :] * w_ref[i:i + 1, hcols(hh)]
        return _silu(acc)

    col = lax.broadcasted_iota(jnp.int32, (LANES, LANES), 0)
    ab = ab_ref[0]
    for hh in heads:
        h = h0 + hh
        qc = conv(q_ref, sq_ref, wq_ref, hh)
        qn_s[hh] = qc * lax.rsqrt(jnp.sum(qc * qc, axis=-1, keepdims=True) + EPS) * (GDN_D ** -0.5)
        kc = conv(k_ref, sk_ref, wk_ref, hh)
        kn_s[hh] = kc * lax.rsqrt(jnp.sum(kc * kc, axis=-1, keepdims=True) + EPS)
        vn_s[hh] = conv(v_ref, sv_ref, wv_ref, hh)
        a_rep = _pick_columns(ab, _bf(jnp.where(col == h, 1.0, 0.0)))
        b_rep = _pick_columns(ab, _bf(jnp.where(col == GDN_HEADS + h, 1.0, 0.0)))
        a_exp = jnp.exp(jnp.full((1, LANES), alog_ref[h], F32))
        g = -a_exp * _softplus(a_rep + dtb_ref[h])
        beta = _sigmoid(b_rep)
        if n_valid < t_len:
            live = lax.broadcasted_iota(jnp.int32, (t_len, LANES), 0) < n_valid
            g = jnp.where(live, g, 0.0)
            beta = jnp.where(live, beta, 0.0)
        g_s[hh] = g
        be_s[hh] = beta

    ii = lax.broadcasted_iota(jnp.int32, (c_len, c_len), 0)
    jj = lax.broadcasted_iota(jnp.int32, (c_len, c_len), 1)
    incl = ii >= jj
    strict = ii > jj
    ltri = _bf(incl.astype(F32))
    unroll = max(u for u in (1, 2, 4, 8) if n_chunks % u == 0 and u * hp <= 16)

    def cumdecay(g_c):
        g_hi, g_mid, g_lo = _split3(g_c)
        return (_dot(ltri, g_lo) + _dot(ltri, g_mid)) + _dot(ltri, g_hi)

    def prep(cu, carry):
        pairs = [(hh, cu * unroll + u) for hh in heads for u in range(unroll)]
        sls = [pl.ds(pl.multiple_of(c * c_len, c_len), c_len) for _, c in pairs]
        each = lambda f, *ls: [f(*a) for a in zip(*ls)]
        q_l = [qn_s[hh, sl, :] for (hh, _), sl in zip(pairs, sls)]
        k_l = [kn_s[hh, sl, :] for (hh, _), sl in zip(pairs, sls)]
        b_l = [be_s[hh, sl, :] for (hh, _), sl in zip(pairs, sls)]
        gc_l = [cumdecay(g_s[hh, sl, :]) for (hh, _), sl in zip(pairs, sls)]
        dec_l = each(lambda gc: jnp.where(
            incl, jnp.exp(jnp.minimum(gc[:, 0:c_len] - gc.T[0:c_len, :], 0.0)), 0.0), gc_l)
        kb_l = each(lambda k, b: k * b, k_l, b_l)
        a_l = each(lambda kb, k, dec: jnp.where(strict, _dot_nt(_bf(kb), _bf(k)) * dec, 0.0), kb_l, k_l, dec_l)
        r_l = _tri_inverse(a_l, ii, jj, () if n_valid <= 8 else (3, 4, 5))
        eg_l = each(jnp.exp, gc_l)
        rhs_l = [jnp.concatenate([vn_s[hh, sl, :] * b, kb * eg], axis=1)
                 for (hh, _), sl, b, kb, eg in zip(pairs, sls, b_l, kb_l, eg_l)]
        sol_l = each(lambda r, rhs: rhs + _dot(_bf(r), _bf(rhs)), r_l, rhs_l)
        qk_l = each(lambda q, k, dec: jnp.where(incl, _dot_nt(_bf(q), _bf(k)) * dec, 0.0), q_l, k_l, dec_l)
        for (hh, c), sl, sol, qk, q, k, gc, eg in zip(pairs, sls, sol_l, qk_l, q_l, k_l, gc_l, eg_l):
            g_end = gc[c_len - 1:c_len, :]
            vw_s[hh, sl, :] = sol[:, :GDN_D]
            kcd_s[hh, sl, :] = _bf(sol[:, GDN_D:])
            qk_s[hh, c] = _bf(qk)
            qg_s[hh, sl, :] = _bf(q * eg)
            kdt_s[hh, c] = _bf((k * jnp.exp(g_end - gc)).T)
            ge_s[hh, c] = jnp.broadcast_to(jnp.exp(g_end), (8, LANES))
        return carry

    lax.fori_loop(0, n_chunks // unroll, prep, 0)

    def step(c, states):
        sl = pl.ds(pl.multiple_of(c * c_len, c_len), c_len)
        sb = [_bf(s) for s in states]
        v_new = [vw_s[hh, sl, :] - _dot(kcd_s[hh, sl, :], sb[hh]) for hh in heads]
        vb = [_bf(v) for v in v_new]
        for hh in heads:
            o_ref[0, sl, hcols(hh)] = _dot(qg_s[hh, sl, :], sb[hh]) + _dot(qk_s[hh, c], vb[hh])
        return tuple(states[hh] * ge_s[hh, c][0:1, :] + _dot(kdt_s[hh, c], vb[hh]) for hh in heads)

    s_fin = lax.fori_loop(0, n_chunks, step, tuple(s0_ref[0, hh] for hh in heads))
    for hh in heads:
        sn_ref[0, hh] = s_fin[hh]
        o = o_ref[0, :, hcols(hh)]
        y = o * lax.rsqrt(jnp.mean(o * o, axis=-1, keepdims=True) + EPS) * ng_ref[...]
        o_ref[0, :, hcols(hh)] = y * _silu(z_ref[0, :, hcols(hh)])


def _gated_deltanet(u, qkv_blk, z_blk, ab_blk, state_pad, s0, w_conv_pad, a_log, dt_bias, norm_g, n_valid):
    bsz, t_len, _ = u.shape
    nh = GDN_HEADS
    hp = nh if t_len <= 4 * GDN_CHUNK else 2
    wide = hp * GDN_D
    assert (qkv_blk * GDN_D) % wide == 0 and (z_blk * GDN_D) % wide == 0
    ublk = lambda blk: pl.BlockSpec((1, t_len, wide), lambda b, j, o=blk * GDN_D // wide: (b, 0, o + j))
    stb = lambda off: pl.BlockSpec((1, 8, wide), lambda b, j, o=off * GDN_D // wide: (b, 0, o + j))
    wb = lambda off: pl.BlockSpec((8, wide), lambda b, j, o=off * GDN_D // wide: (0, o + j))
    smem = pl.BlockSpec(memory_space=pltpu.SMEM)
    n_chunks = t_len // GDN_CHUNK
    seq = lambda dt: pltpu.VMEM((hp, t_len, GDN_D), dt)
    kern = functools.partial(_gdn_kernel, t_len, n_valid, hp)
    return pl.pallas_call(
        kern,
        grid=(bsz, nh // hp),
        in_specs=[smem, smem, ublk(qkv_blk), ublk(qkv_blk + nh), ublk(qkv_blk + 2 * nh), ublk(z_blk),
                  pl.BlockSpec((1, t_len, GDN_D), lambda b, j: (b, 0, ab_blk)),
                  stb(0), stb(nh), stb(2 * nh), wb(0), wb(nh), wb(2 * nh),
                  pl.BlockSpec((1, hp, GDN_D, GDN_D), lambda b, j: (b, j, 0, 0)),
                  pl.BlockSpec((1, GDN_D), lambda b, j: (0, 0))],
        out_specs=[pl.BlockSpec((1, t_len, wide), lambda b, j: (b, 0, j)),
                   pl.BlockSpec((1, hp, GDN_D, GDN_D), lambda b, j: (b, j, 0, 0))],
        out_shape=[jax.ShapeDtypeStruct((bsz, t_len, nh * GDN_D), F32),
                   jax.ShapeDtypeStruct((bsz, nh, GDN_D, GDN_D), F32)],
        scratch_shapes=[pltpu.VMEM((8 + t_len, GDN_D), F32), seq(F32), seq(F32), seq(F32), seq(F32), seq(F32),
                        seq(F32), seq(BF16), seq(BF16),
                        pltpu.VMEM((hp, n_chunks, GDN_D, GDN_CHUNK), BF16),
                        pltpu.VMEM((hp, n_chunks, GDN_CHUNK, GDN_CHUNK), BF16),
                        pltpu.VMEM((hp, n_chunks, 8, LANES), F32)],
        compiler_params=pltpu.CompilerParams(dimension_semantics=("parallel", "parallel"),
                                             vmem_limit_bytes=GDN_VMEM_LIMIT),
        name="gated_deltanet",
    )(a_log, dt_bias, u, u, u, u, u, state_pad, state_pad, state_pad,
      w_conv_pad, w_conv_pad, w_conv_pad, s0, norm_g.reshape(1, GDN_D))


def _heads_to_rows(x, g, nt):
    lane = lax.broadcasted_iota(jnp.int32, (nt, LANES), 1)
    keep = (lane >= NSA_DH * g) & (lane < NSA_DH * (g + 1))
    parts = []
    for r in range(NSA_GQ):
        hh = NSA_GQ * g + r
        blk = x[:, (hh // 2) * LANES:(hh // 2 + 1) * LANES]
        if hh % 2 != g:
            blk = pltpu.roll(blk, NSA_DH, axis=1)
        parts.append(jnp.where(keep, blk, 0.0))
    return jnp.concatenate(parts, axis=0)


def _rows_to_heads(y, g, nt):
    outs = []
    for m in range(2):
        x0 = y[(2 * m) * nt:(2 * m + 1) * nt]
        x1 = y[(2 * m + 1) * nt:(2 * m + 2) * nt]
        if g == 1:
            x0 = pltpu.roll(x0, NSA_DH, axis=1)
        else:
            x1 = pltpu.roll(x1, NSA_DH, axis=1)
        outs.append(x0 + x1)
    return outs


def _masked_softmax_parts(parts, masks, axis):
    sm = [jnp.where(m, s, NEG) for s, m in zip(parts, masks)]
    mx = functools.reduce(jnp.maximum, [jnp.max(s, axis=axis, keepdims=True) for s in sm])
    es = [jnp.where(m, jnp.exp(s - mx), 0.0) for s, m in zip(sm, masks)]
    den = functools.reduce(lambda p, q: p + q, [jnp.sum(e, axis=axis, keepdims=True) for e in es])
    inv = 1.0 / jnp.maximum(den, 1e-30)
    return [e * inv for e in es]


def _bucket_np(rel):
    n = np.maximum(rel, 0)
    nf = np.maximum(n, 1).astype(np.float32)
    large = 16 + (np.log(nf / np.float32(16)) / np.float32(math.log(8.0)) * np.float32(16)).astype(np.int32)
    return np.where(n < 16, n, np.minimum(large, N_BUCKETS - 1)).astype(np.int32)


LOOKUP_TILE = 8192


def _lookup_kernel(idx_ref, tb_ref, o_ref):
    idx = idx_ref[...]
    acc = jnp.zeros(o_ref.shape, F32)
    for k in range(N_BUCKETS):
        acc = jnp.where(idx == k, tb_ref[:, k:k + 1], acc)
    o_ref[...] = acc


def _bias_lookup(rel_bias, idx_list):
    sizes = [int(np.prod(a.shape)) for a in idx_list]
    total = sum(sizes)
    padded = -(-total // LOOKUP_TILE) * LOOKUP_TILE
    flat = np.zeros((1, padded), np.int32)
    flat[0, :total] = np.concatenate([np.asarray(a, np.int32).reshape(-1) for a in idx_list])
    tab = pl.pallas_call(
        _lookup_kernel,
        grid=(padded // LOOKUP_TILE,),
        in_specs=[pl.BlockSpec((1, LOOKUP_TILE), lambda i: (0, i)),
                  pl.BlockSpec((NSA_HEADS, N_BUCKETS), lambda i: (0, 0))],
        out_specs=pl.BlockSpec((NSA_HEADS, LOOKUP_TILE), lambda i: (0, i)),
        out_shape=jax.ShapeDtypeStruct((NSA_HEADS, padded), F32),
        compiler_params=_params("parallel"),
        name="bias_lookup",
    )(jnp.asarray(flat), rel_bias.astype(F32).T)
    outs, off = [], 0
    for a, n in zip(idx_list, sizes):
        outs.append(tab[:, off:off + n].reshape((NSA_HEADS,) + tuple(a.shape)))
        off += n
    return outs


def _head_rows(tab):
    return tab.reshape(NSA_KV, NSA_GQ * tab.shape[1], tab.shape[2])


def _nsa_prompt_kernel(t_len, q_ref, kcmp_ref, vcmp_ref, kslc_ref, vslc_ref, kwin_ref, vwin_ref, gl_ref, rep_ref,
                       wk_ref, wv_ref, bc_ref, bct_ref, bnear_ref, bwin_ref, o_ref, kc_s, vc_s):
    i = pl.program_id(1)
    nsb = t_len // L_SEL
    qb = Q_BLOCK
    rows = NSA_GQ * qb

    @pl.when(i == 0)
    def _():
        n2 = 2 * lax.broadcasted_iota(jnp.int32, (nsb, t_len), 0)
        cb = lax.broadcasted_iota(jnp.int32, (nsb, t_len), 1) >> 5
        kc = _bf(kcmp_ref[0])
        vc = _bf(vcmp_ref[0])
        wk = wk_ref[...]
        wv = wv_ref[...]
        kc_s[0:nsb, :] = _dot(_bf(jnp.where(cb == n2, wk, 0.0)), kc)
        kc_s[nsb:2 * nsb, :] = _dot(_bf(jnp.where(cb == n2 + 1, wk, 0.0)), kc)
        vc_s[0:nsb, :] = _dot(_bf(jnp.where(cb == n2, wv, 0.0)), vc)
        vc_s[nsb:2 * nsb, :] = _dot(_bf(jnp.where(cb == n2 + 1, wv, 0.0)), vc)

    q_all = q_ref[0] * (NSA_DH ** -0.5)
    gl = gl_ref[0]
    gate_all = [_sigmoid(_pick_columns(gl, rep_ref[br])) for br in range(3)]
    t0 = i * qb
    tq = t0 + (lax.broadcasted_iota(jnp.int32, (rows, 1), 0) & (qb - 1))
    tl = t0 + (lax.broadcasted_iota(jnp.int32, (1, rows), 1) & (qb - 1))
    eye_q = _bf((lax.broadcasted_iota(jnp.int32, (qb, qb), 0) == lax.broadcasted_iota(jnp.int32, (qb, qb), 1)).astype(F32))
    far_end = jnp.maximum(t0 - qb, 0)
    n_far = (far_end + 511) >> 9
    kc = _bf(kc_s[...])
    vc = _bf(vc_s[...])

    def key_aug(k0, n_keys, limit):
        kpos = k0 + lax.broadcasted_iota(jnp.int32, (n_keys, LANES), 0)
        lane = lax.broadcasted_iota(jnp.int32, (n_keys, LANES), 1)
        hit = (lane == (kpos >> 6)) | ((lane == nsb) & ((kpos >= limit) | (kpos < 0)))
        return _bf(jnp.where(hit, NEG, 0.0))

    gs = range(NSA_KV)
    qg = [_bf(_heads_to_rows(q_all, g, qb)) for g in gs]

    n_prev = WINDOW // qb
    starts = [pl.multiple_of(jnp.maximum(t0 + (j - n_prev) * qb, 0), qb) for j in range(n_prev + 1)]
    kwin = [_bf(kwin_ref[0, pl.ds(st, qb), :]) for st in starts]
    pens = [jnp.where(i + (j - n_prev) >= 0, 0.0, NEG) for j in range(n_prev)] + [0.0]
    s_w = [jnp.concatenate([_dot_nt(qg[g], kwin[j]) + pens[j] for j in range(n_prev + 1)], axis=1) + bwin_ref[g]
           for g in gs]

    s_c = [_dot_nt(qg[g], kc) + bc_ref[0, g] for g in gs]
    s_t = [_dot_nt(kc, qg[g]) + bct_ref[0, g] for g in gs]
    e_c = [jnp.where(tq >= L_CMP - 1, jnp.exp(s - jnp.max(s, axis=1, keepdims=True)), 0.0) for s in s_c]
    p_c = [e * (1.0 / jnp.maximum(jnp.sum(e, axis=1, keepdims=True), 1e-30)) for e in e_c]
    o_c = [_dot(_bf(p), vc) for p in p_c]

    e_t = [jnp.where(tl >= L_CMP - 1, jnp.exp(s - jnp.max(s, axis=0, keepdims=True)), 0.0) for s in s_t]
    p_t = [e * (1.0 / jnp.maximum(jnp.sum(e, axis=0, keepdims=True), 1e-30)) for e in e_t]
    head_sum = lambda x: x[:, 0:qb] + x[:, qb:2 * qb] + x[:, 2 * qb:3 * qb] + x[:, 3 * qb:4 * qb]
    blk = lax.broadcasted_iota(jnp.int32, (nsb, qb), 0)
    cur = (t0 + lax.broadcasted_iota(jnp.int32, (nsb, qb), 1)) >> 6
    bonus = jnp.where((blk == 0) | (blk == cur) | (blk == cur - 1), FORCE_BONUS, 0.0)
    score = [jnp.where(blk <= cur, (head_sum(p[0:nsb]) + head_sum(p[nsb:2 * nsb])) + bonus, -1.0) for p in p_t]
    rank = [jnp.zeros((nsb, qb), F32) for _ in gs]
    for j in range(nsb):
        for g in gs:
            sj = score[g][j:j + 1, :]
            ahead = (sj > score[g]) | ((sj == score[g]) & (blk > j))
            rank[g] = rank[g] + jnp.where(ahead, 1.0, 0.0)
    pen_rows = jnp.where(lax.broadcasted_iota(jnp.int32, (LANES - nsb, qb), 0) == 0, 1.0, 0.0)
    not_sel_t = [jnp.where(r < float(min(N_SEL, nsb)), 0.0, 1.0) for r in rank]
    q_aug = [_bf(_dot_nt(eye_q, _bf(jnp.concatenate([ns, pen_rows], axis=0)))) for ns in not_sel_t]
    qa = [jnp.concatenate([qg[g], jnp.concatenate([q_aug[g]] * NSA_GQ, axis=0)], axis=1) for g in gs]

    def online(carry, s, pv):
        m_i, l_i, acc = carry
        m_n = jnp.maximum(m_i, jnp.max(s, axis=1, keepdims=True))
        p = jnp.exp(s - m_n)
        alpha = jnp.exp(m_i - m_n)
        return m_n, alpha * l_i + jnp.sum(p, axis=1, keepdims=True), alpha * acc + pv(_bf(p))

    def far_tile(kt, carry):
        k0 = pl.multiple_of(kt * 512, 512)
        ka = jnp.concatenate([_bf(kslc_ref[0, pl.ds(k0, 512), :]), key_aug(k0, 512, far_end)], axis=1)
        vt = _bf(vslc_ref[0, pl.ds(k0, 512), :])
        s = [_dot_nt(qa[g], ka) for g in gs]
        return tuple(online(carry[g], s[g], lambda p: _dot(p, vt)) for g in gs)

    init = (jnp.full((rows, 1), NEG, F32), jnp.zeros((rows, 1), F32), jnp.zeros((rows, LANES), F32))
    far = lax.fori_loop(0, n_far, far_tile, tuple(init for _ in gs))

    p0 = pl.multiple_of(jnp.maximum(t0 - qb, 0), qb)
    d0 = pl.multiple_of(t0, qb)
    ka = jnp.concatenate([
        jnp.concatenate([_bf(kslc_ref[0, pl.ds(p0, qb), :]), _bf(kslc_ref[0, pl.ds(d0, qb), :])], axis=0),
        key_aug(t0 - qb, 2 * qb, t_len)], axis=1)
    vp = _bf(vslc_ref[0, pl.ds(p0, qb), :])
    vd = _bf(vslc_ref[0, pl.ds(d0, qb), :])
    s_near = [_dot_nt(qa[g], ka) + bnear_ref[g] for g in gs]
    fin = [online(far[g], s_near[g], lambda p: _dot(p[:, 0:qb], vp) + _dot(p[:, qb:2 * qb], vd)) for g in gs]
    o_s = [acc * (1.0 / l_n) for _, l_n, acc in fin]

    vwin = [_bf(vwin_ref[0, pl.ds(st, qb), :]) for st in starts]
    e_w = [jnp.exp(s - jnp.max(s, axis=1, keepdims=True)) for s in s_w]
    o_w = []
    for g in gs:
        ew = _bf(e_w[g])
        acc = _dot(ew[:, 0:qb], vwin[0])
        for j in range(1, n_prev + 1):
            acc = acc + _dot(ew[:, j * qb:(j + 1) * qb], vwin[j])
        o_w.append(acc * (1.0 / jnp.sum(e_w[g], axis=1, keepdims=True)))

    for g in gs:
        gates = [_heads_to_rows(gate_all[br], g, qb) for br in range(3)]
        comb = gates[0] * o_c[g] + gates[1] * o_s[g] + gates[2] * o_w[g]
        blocks = _rows_to_heads(comb, g, qb)
        o_ref[0, :, (2 * g) * LANES:(2 * g + 1) * LANES] = blocks[0]
        o_ref[0, :, (2 * g + 1) * LANES:(2 * g + 2) * LANES] = blocks[1]


def _nsa_tables(rel_bias, t_len, n_pages, nt):
    nqb = t_len // Q_BLOCK
    nsb = t_len // L_SEL
    past = n_pages * PAGE
    t = np.arange(Q_BLOCK)
    tq = (np.arange(nqb)[:, None] * Q_BLOCK + t[None, :])[:, :, None]
    n = np.arange(nsb)[None, None, :]
    ts = np.arange(nt)[:, None]
    j = np.arange(2 * n_pages)[None, :]
    c = np.arange(PAGE)[None, :]
    idx = [
        _bucket_np(tq - (n * L_SEL + L_CMP - 1)),
        _bucket_np(tq - (n * L_SEL + L_SEL - 1)),
        _bucket_np(Q_BLOCK + t[:, None] - np.arange(2 * Q_BLOCK)[None, :]),
        _bucket_np(WINDOW + t[:, None] - np.arange(WINDOW + Q_BLOCK)[None, :]),
        _bucket_np(past + ts - (j * L_SEL + L_CMP - 1)),
        _bucket_np(past + ts - (j * L_SEL + L_SEL - 1)),
        _bucket_np(PAGE + ts - c),
        _bucket_np(ts - c),
        _bucket_np(WINDOW + ts - np.arange(WINDOW)[None, :]),
    ]
    ce, co, near, win, sce, sco, slast, snew, swin = _bias_lookup(rel_bias, idx)
    b31 = rel_bias.astype(F32)[N_BUCKETS - 1].reshape(NSA_KV, NSA_GQ, 1, 1)
    shift = lambda tab: (tab.reshape(NSA_KV, NSA_GQ, tab.shape[1], tab.shape[2]) - b31).reshape(
        NSA_KV, NSA_GQ * tab.shape[1], tab.shape[2])
    vis = lambda m: jnp.asarray(np.tile(m, (1,) * (m.ndim - 2) + (NSA_GQ, 1)))
    blocked = lambda tab: jnp.swapaxes(tab, 0, 1).reshape(nqb, NSA_KV, NSA_GQ * Q_BLOCK, nsb)
    bc = jnp.concatenate([blocked(ce), blocked(co)], axis=-1)
    vis_c = np.concatenate([n * L_SEL + L_CMP - 1 <= tq, n * L_SEL + L_SEL - 1 <= tq], axis=-1)
    bc = jnp.where(vis(vis_c)[:, None], bc, NEG)
    c_near = np.arange(2 * Q_BLOCK)[None, :]
    near_m = jnp.where(vis(c_near <= Q_BLOCK + t[:, None])[None], shift(near), NEG)
    c_win = np.arange(WINDOW + Q_BLOCK)[None, :]
    win_m = jnp.where(vis((c_win > t[:, None]) & (c_win <= WINDOW + t[:, None]))[None], _head_rows(win), NEG)
    ptab = (bc, jnp.swapaxes(bc, -1, -2), near_m, win_m)
    rows64 = lambda tab: tab.reshape(NSA_KV * NSA_GQ * nt, tab.shape[-1])
    stab = (rows64(_head_rows(sce)), rows64(_head_rows(sco)), rows64(shift(slast)), rows64(shift(snew)),
            rows64(_head_rows(swin)), rows64(_head_rows(snew)))
    return ptab, stab


def _gate_rep():
    j = np.arange(LANES)[None, :, None]
    c = np.arange(NSA_HEADS * NSA_DH)[None, None, :]
    br = np.arange(3)[:, None, None]
    return jnp.asarray(j == br * NSA_HEADS + c // NSA_DH, BF16)


def _nsa_prompt(u, w_pos, tables):
    bsz, t_len, _ = u.shape
    nqb = t_len // Q_BLOCK
    nsb = t_len // L_SEL
    assert nsb < LANES
    bc, bct, near, wtab = tables
    wk = jnp.tile(w_pos[0], t_len // L_CMP).reshape(1, t_len)
    wv = jnp.tile(w_pos[1], t_len // L_CMP).reshape(1, t_len)
    rep = _gate_rep()
    seq = lambda c: pl.BlockSpec((1, t_len, LANES), lambda b, i, c=c: (b, 0, c))
    full = lambda a: pl.BlockSpec(a.shape, lambda b, i, nd=a.ndim: (0,) * nd)
    per_i = lambda a: pl.BlockSpec((1,) + a.shape[1:], lambda b, i, nd=a.ndim: (i,) + (0,) * (nd - 1))
    kern = functools.partial(_nsa_prompt_kernel, t_len)
    kv0 = U_ROWS // LANES
    w0 = U_WIN // LANES
    return pl.pallas_call(
        kern,
        grid=(bsz, nqb),
        in_specs=[pl.BlockSpec((1, Q_BLOCK, 512), lambda b, i: (b, i, U_Q // 512)),
                  seq(kv0), seq(kv0 + 1), seq(kv0 + 2), seq(kv0 + 3), seq(w0), seq(w0 + 1),
                  pl.BlockSpec((1, Q_BLOCK, LANES), lambda b, i: (b, i, U_GL // LANES)), full(rep),
                  full(wk), full(wv), per_i(bc), per_i(bct), full(near), full(wtab)],
        out_specs=pl.BlockSpec((1, Q_BLOCK, 512), lambda b, i: (b, i, 0)),
        out_shape=jax.ShapeDtypeStruct((bsz, t_len, 512), F32),
        scratch_shapes=[pltpu.VMEM((2 * nsb, LANES), F32)] * 2,
        compiler_params=_params("parallel", "arbitrary"),
        name="nsa_prompt",
    )(u, u, u, u, u, u, u, u, rep, wk, wv, bc, bct, near, wtab)


def _nsa_sample_kernel(layer, n_pages, n_new, pt_ref, cache_ref, q_ref, rows_ref, wnew_ref, wbuf_ref, gl_ref,
                       rep_ref, wpool_ref, bce_ref, bco_ref, blast_ref, bnew_ref, bwin_ref, bwnew_ref,
                       o_ref, cmp_s, slc_s, pool_s, exp_s, pad_s, sem):
    b = pl.program_id(0)
    nb = pl.num_programs(0)
    nt = 8
    past = n_pages * PAGE
    nblk = 2 * n_pages
    rows = NSA_KV * NSA_GQ * nt
    half = 2 * LANES

    def page_copy(seq, p, part, buf, s):
        return pltpu.make_async_copy(
            cache_ref.at[layer, pt_ref[seq, p], pl.ds(part * half, half), :],
            buf.at[:, pl.ds(pl.multiple_of(p * PAGE, PAGE), PAGE)], s)

    def start_gather(seq, part, buf, s):
        def body(p, c):
            page_copy(seq, p, part, buf, s).start()
            return c
        lax.fori_loop(0, n_pages, body, 0)

    def wait_gather(seq, part, buf, s):
        def body(p, c):
            page_copy(seq, p, part, buf, s).wait()
            return c
        lax.fori_loop(0, n_pages, body, 0)

    @pl.when(b == 0)
    def _():
        start_gather(0, 0, cmp_s, sem.at[0])
        start_gather(0, 1, slc_s, sem.at[1])
        cb = lax.broadcasted_iota(jnp.int32, (past, nblk), 0) >> 5
        j2 = 2 * lax.broadcasted_iota(jnp.int32, (past, nblk), 1)
        pool_s[0] = _bf(jnp.where(cb == j2, 1.0, 0.0))
        pool_s[1] = _bf(jnp.where(cb == j2 + 1, 1.0, 0.0))
        ej = lax.broadcasted_iota(jnp.int32, (nblk, past), 0)
        ec = lax.broadcasted_iota(jnp.int32, (nblk, past), 1) >> 6
        exp_s[...] = _bf(jnp.where(ej == ec, 1.0, 0.0))
        pad_s[...] = jnp.zeros(pad_s.shape, F32)

    q_all = q_ref[0] * (NSA_DH ** -0.5)
    qq = _bf(jnp.concatenate([_heads_to_rows(q_all, g, nt) for g in range(NSA_KV)], axis=0))
    tr = lax.broadcasted_iota(jnp.int32, (rows, 1), 0) & (nt - 1)

    wait_gather(b, 0, cmp_s, sem.at[0])
    wp = wpool_ref[...]
    ks = _bf(cmp_s[0:LANES, :] * wp[0:1, :])
    vs = _bf(cmp_s[LANES:half, :] * wp[1:2, :])
    kce = _bf(_dot(ks, pool_s[0]))
    kco = _bf(_dot(ks, pool_s[1]))
    vce = _bf(_dot(vs, pool_s[0]))
    vco = _bf(_dot(vs, pool_s[1]))

    @pl.when(b + 1 < nb)
    def _():
        start_gather(b + 1, 0, cmp_s, sem.at[0])

    se = _dot(qq, kce) + bce_ref[...]
    so = _dot(qq, kco) + bco_ref[...]
    mx = jnp.maximum(jnp.max(se, axis=1, keepdims=True), jnp.max(so, axis=1, keepdims=True))
    ee = jnp.exp(se - mx)
    eo = jnp.exp(so - mx)
    inv = 1.0 / (jnp.sum(ee, axis=1, keepdims=True) + jnp.sum(eo, axis=1, keepdims=True))
    pe = ee * inv
    po = eo * inv
    o_c = _dot_nt(_bf(pe), vce) + _dot_nt(_bf(po), vco)

    def head_sum(pr):
        return jnp.concatenate(
            [pr[g * 4 * nt:g * 4 * nt + nt] + pr[g * 4 * nt + nt:g * 4 * nt + 2 * nt]
             + pr[g * 4 * nt + 2 * nt:g * 4 * nt + 3 * nt] + pr[g * 4 * nt + 3 * nt:g * 4 * nt + 4 * nt]
             for g in range(NSA_KV)], axis=0)

    jcol = lax.broadcasted_iota(jnp.int32, (NSA_KV * nt, nblk), 1)
    forced = (jcol == 0) | (jcol == nblk - 1)
    score = (head_sum(pe) + head_sum(po)) + jnp.where(forced, FORCE_BONUS, 0.0)
    rank = jnp.where(FORCE_BONUS > score, 1.0, 0.0)
    for j in range(nblk):
        sj = score[:, j:j + 1]
        ahead = (sj > score) | ((sj == score) & (jcol > j))
        rank = rank + jnp.where(ahead, 1.0, 0.0)
    sel = jnp.where(rank < float(N_SEL), 1.0, 0.0)
    sel_rows = jnp.concatenate([sel[g * nt:(g + 1) * nt] for g in range(NSA_KV) for _ in range(NSA_GQ)], axis=0)

    new = rows_ref[0]
    wnew = wnew_ref[0]
    pad_s[0, 0:nt, :] = new[:, 2 * LANES:3 * LANES]
    pad_s[1, 0:nt, :] = new[:, 3 * LANES:4 * LANES]
    pad_s[2, 0:nt, :] = wnew[:, 0:LANES]
    pad_s[3, 0:nt, :] = wnew[:, LANES:2 * LANES]
    tc = lax.broadcasted_iota(jnp.int32, (rows, LANES), 1)
    mnew = (tc <= tr) & (tc < n_new)

    wait_gather(b, 1, slc_s, sem.at[1])
    s_all = _dot(qq, _bf(slc_s[0:LANES, :]))
    mk = _dot(_bf(sel_rows), exp_s[...]) > 0.5
    far = past - PAGE
    p_far, p_last, p_new = _masked_softmax_parts(
        [s_all[:, :far], s_all[:, far:] + blast_ref[...], _dot_nt(qq, _bf(pad_s[0])) + bnew_ref[...]],
        [mk[:, :far], mk[:, far:], mnew], 1)
    o_s = _dot_nt(_bf(jnp.concatenate([p_far, p_last], axis=1)), _bf(slc_s[LANES:half, :])) \
        + _dot(_bf(p_new), _bf(pad_s[1]))

    wb = wbuf_ref[0, 0]
    cw = lax.broadcasted_iota(jnp.int32, (rows, WINDOW), 1)
    pw, pn = _masked_softmax_parts(
        [_dot(qq, _bf(wb[0:LANES, :])) + bwin_ref[...], _dot_nt(qq, _bf(pad_s[2])) + bwnew_ref[...]],
        [cw > tr, mnew], 1)
    o_w = _dot_nt(_bf(pw), _bf(wb[LANES:half, :])) + _dot(_bf(pn), _bf(pad_s[3]))

    gl = gl_ref[0]
    gate_all = [_sigmoid(_pick_columns(gl, rep_ref[br])) for br in range(3)]
    gates = [jnp.concatenate([_heads_to_rows(ga, g, nt) for g in range(NSA_KV)], axis=0) for ga in gate_all]
    comb = gates[0] * o_c + gates[1] * o_s + gates[2] * o_w
    for g in range(NSA_KV):
        blocks = _rows_to_heads(comb[g * 4 * nt:(g + 1) * 4 * nt], g, nt)
        o_ref[0, :, (2 * g) * LANES:(2 * g + 1) * LANES] = blocks[0]
        o_ref[0, :, (2 * g + 1) * LANES:(2 * g + 2) * LANES] = blocks[1]

    @pl.when(b + 1 < nb)
    def _():
        start_gather(b + 1, 1, slc_s, sem.at[1])


def _nsa_sample(layer, cache_t, page_table, u, wbuf_t, w_pos, tables, n_new):
    bsz, n_pages = page_table.shape
    nt = u.shape[1]
    past = n_pages * PAGE
    wpool = jnp.tile(w_pos, (1, past // L_CMP))
    rep = _gate_rep()
    full = lambda a: pl.BlockSpec(a.shape, lambda b, pt, nd=a.ndim: (0,) * nd)
    ucols = lambda width, off: pl.BlockSpec((1, nt, width), lambda b, pt: (b, 0, off // width))
    kern = functools.partial(_nsa_sample_kernel, layer, n_pages, n_new)
    grid_spec = pltpu.PrefetchScalarGridSpec(
        num_scalar_prefetch=1,
        grid=(bsz,),
        in_specs=[pl.BlockSpec(memory_space=pl.ANY),
                  ucols(512, U_Q), ucols(512, U_ROWS), ucols(2 * LANES, U_WIN),
                  pl.BlockSpec((1, 1) + wbuf_t.shape[2:], lambda b, pt: (layer, b, 0, 0)),
                  ucols(LANES, U_GL), full(rep), full(wpool)] + [full(t) for t in tables],
        out_specs=pl.BlockSpec((1, nt, 512), lambda b, pt: (b, 0, 0)),
        scratch_shapes=[pltpu.VMEM((2 * LANES, past), F32), pltpu.VMEM((2 * LANES, past), F32),
                        pltpu.VMEM((2, past, 2 * n_pages), BF16), pltpu.VMEM((2 * n_pages, past), BF16),
                        pltpu.VMEM((4, LANES, LANES), F32), pltpu.SemaphoreType.DMA((2,))],
    )
    return pl.pallas_call(
        kern,
        grid_spec=grid_spec,
        out_shape=jax.ShapeDtypeStruct((bsz, nt, 512), F32),
        compiler_params=_params("arbitrary"),
        name="nsa_sample",
    )(page_table, cache_t, u, u, u, wbuf_t, u, rep, wpool, *tables)


def _mixout_kernel(x_ref, ca_ref, ob_ref, oc_ref, ga_ref, gb_ref, gc_ref, wpa_ref, wpb_ref, wpc_ref, wo_ref, o_ref):
    y = _sigmoid(ga_ref[...]) * _dot(_bf(ca_ref[...]), wpa_ref[...])
    y = y + _sigmoid(gb_ref[...]) * _dot(_bf(ob_ref[...]), wpb_ref[...])
    y = y + _sigmoid(gc_ref[...]) * _dot(_bf(oc_ref[...]), wpc_ref[...])
    o_ref[...] = x_ref[...] + _dot(_bf(y), wo_ref[...])


def _mixout(x, ca, ob, oc, u, wpa, wpb, wpc, wo):
    m = x.shape[0]
    tm = min(m, 512)
    rowblk = lambda n: pl.BlockSpec((tm, n), lambda i: (i, 0))
    gate = lambda k: pl.BlockSpec((tm, D_MODEL), lambda i, k=k: (i, U_G // D_MODEL + k))
    full = lambda a: pl.BlockSpec(a.shape, lambda i: (0, 0))
    return pl.pallas_call(
        _mixout_kernel,
        grid=(m // tm,),
        in_specs=[rowblk(D_MODEL), rowblk(512), rowblk(512), rowblk(512), gate(0), gate(1), gate(2),
                  full(wpa), full(wpb), full(wpc), full(wo)],
        out_specs=rowblk(D_MODEL),
        out_shape=jax.ShapeDtypeStruct((m, D_MODEL), F32),
        compiler_params=_params("parallel"),
        name="mixer_out",
    )(x, ca, ob, oc, u, u, u, wpa, wpb, wpc, wo)


def _xattn_kernel(x_ref, g_ref, kv_ref, wq_ref, wo_ref, o_ref):
    x = x_ref[0]
    h = _bf(x * lax.rsqrt(jnp.mean(x * x, axis=-1, keepdims=True) + EPS) * g_ref[...])
    q = _dot(h, wq_ref[...])
    kv = kv_ref[0]
    outs = []
    for hd in range(X_HEADS):
        qh = _bf(q[:, hd * X_DH:(hd + 1) * X_DH])
        kh = _bf(kv[:, hd * X_DH:(hd + 1) * X_DH])
        vh = _bf(kv[:, D_MODEL + hd * X_DH:D_MODEL + (hd + 1) * X_DH])
        s = _dot_nt(qh, kh) * (X_DH ** -0.5)
        e = jnp.exp(s - jnp.max(s, axis=-1, keepdims=True))
        pr = e * (1.0 / jnp.sum(e, axis=-1, keepdims=True))
        outs.append(_dot(_bf(pr), vh))
    o = jnp.concatenate(outs, axis=1)
    o_ref[0] = x + _dot(_bf(o), wo_ref[...])


def _cross_attn(x, g, mem_kv, wq, wo):
    bsz, t_len, d = x.shape
    tt = min(t_len, 512)
    full = lambda a: pl.BlockSpec(a.shape, lambda b, t: (0, 0))
    return pl.pallas_call(
        _xattn_kernel,
        grid=(bsz, t_len // tt),
        in_specs=[pl.BlockSpec((1, tt, d), lambda b, t: (b, t, 0)),
                  pl.BlockSpec((1, d), lambda b, t: (0, 0)),
                  pl.BlockSpec((1, N_MEM, 2 * d), lambda b, t: (b, 0, 0)),
                  full(wq), full(wo)],
        out_specs=pl.BlockSpec((1, tt, d), lambda b, t: (b, t, 0)),
        out_shape=jax.ShapeDtypeStruct((bsz, t_len, d), F32),
        compiler_params=_params("parallel", "parallel"),
        name="cross_attn",
    )(x, g.reshape(1, d), mem_kv, wq, wo)


FF_CHUNK = 1024


def _mlp_kernel(n_k, final_norm, x_ref, g_ref, gf_ref, w1_ref, w2_ref, o_ref, h_s, acc_s):
    k = pl.program_id(1)

    @pl.when(k == 0)
    def _():
        x = x_ref[...]
        h_s[...] = _bf(x * lax.rsqrt(jnp.mean(x * x, axis=-1, keepdims=True) + EPS) * g_ref[...])
        acc_s[...] = x

    a = jnp.maximum(_dot(h_s[...], w1_ref[...]), 0.0)
    acc_s[...] += _dot(_bf(a * a), w2_ref[...])

    @pl.when(k == n_k - 1)
    def _():
        y = acc_s[...]
        if final_norm:
            y = y * lax.rsqrt(jnp.mean(y * y, axis=-1, keepdims=True) + EPS) * gf_ref[...]
        o_ref[...] = y


def _mlp(x, g, w1, w2, final_g=None):
    m, d = x.shape
    tm = min(m, 1024)
    n_k = D_FF // FF_CHUNK
    gf = g if final_g is None else final_g
    return pl.pallas_call(
        functools.partial(_mlp_kernel, n_k, final_g is not None),
        grid=(m // tm, n_k),
        in_specs=[pl.BlockSpec((tm, d), lambda i, k: (i, 0)),
                  pl.BlockSpec((1, d), lambda i, k: (0, 0)),
                  pl.BlockSpec((1, d), lambda i, k: (0, 0)),
                  pl.BlockSpec((d, FF_CHUNK), lambda i, k: (0, k)),
                  pl.BlockSpec((FF_CHUNK, d), lambda i, k: (k, 0))],
        out_specs=pl.BlockSpec((tm, d), lambda i, k: (i, 0)),
        out_shape=jax.ShapeDtypeStruct((m, d), F32),
        scratch_shapes=[pltpu.VMEM((tm, d), BF16), pltpu.VMEM((tm, d), F32)],
        compiler_params=_params("parallel", "arbitrary"),
        name="sq_relu_mlp",
    )(x, g.reshape(1, d), gf.reshape(1, d), w1, w2)


A_COLS = 2 * CONV_CH
B0 = A_COLS
Z0 = B0 + GDN_QKV
AB0 = Z0 + GDN_HEADS * GDN_D
C0 = AB0 + 2 * GDN_HEADS
KV0 = C0 + NSA_HEADS * NSA_DH
GL0 = KV0 + 6 * NSA_KV * NSA_DH
G0 = GL0 + 3 * NSA_HEADS
N_IN = G0 + 3 * D_MODEL


def _layer_weights(l, w_in, w_pa, w_pb, w_pc, w_o, w_xq, w_xk, w_xv, w_xo, w_ff1, w_ff2):
    w = w_in[l]
    lane_pad = lambda cols: jnp.pad(cols, ((0, 0), (0, LANES - cols.shape[1])))
    groups = [(U_A, w[:, 0:A_COLS]), (U_Q, w[:, C0:KV0]), (U_ROWS, w[:, KV0:KV0 + 4 * LANES]),
              (U_QKV, w[:, B0:Z0]), (U_Z, w[:, Z0:AB0]), (U_G, w[:, G0:N_IN]),
              (U_WIN, w[:, KV0 + 4 * LANES:GL0]), (U_AB, lane_pad(w[:, AB0:C0])), (U_GL, lane_pad(w[:, GL0:G0]))]
    off = 0
    for start, cols in groups:
        assert start == off
        off += cols.shape[1]
    assert off == U_N
    return {
        "in": _bf(jnp.concatenate([cols for _, cols in groups], axis=1)),
        "pa": _bf(w_pa[l]), "pb": _bf(w_pb[l]), "pc": _bf(w_pc[l]), "o": _bf(w_o[l]),
        "xq": _bf(w_xq[l]), "xo": _bf(w_xo[l]),
        "xkv": _bf(jnp.concatenate([w_xk[l], w_xv[l]], axis=1)),
        "ff1": _bf(w_ff1[l]), "ff2": _bf(w_ff2[l]),
    }


def _mixers(x, lw, p, l, conv_state_pad, qkv_state_pad, s0, n_valid, gdn_len, nsa_fn):
    bsz, t_len, d = x.shape
    m = bsz * t_len
    x2 = x.reshape(m, d)
    u2 = _proj_in(x2, p["norm_mix"][l], lw["in"])
    u = u2.reshape(bsz, t_len, U_N)
    rows = u[:, :, U_ROWS:U_ROWS + 4 * LANES]
    win = u[:, :, U_WIN:U_WIN + 2 * LANES]
    qkv_tail = u[:, max(n_valid - 3, 0):n_valid, U_QKV:U_QKV + GDN_QKV]

    ca, conv_new = _conformer(u, conv_state_pad, p["conv_a_w"][l], p["conv_a_b"][l], p["ln_a_g"][l],
                              p["ln_a_b"][l], n_valid if n_valid < t_len else min(t_len, 256))
    w_conv_pad = jnp.pad(p["gdn_conv_w"][l], ((0, 4), (0, 0)))
    gdn_args = (qkv_state_pad, s0, w_conv_pad, p["gdn_a_log"][l], p["gdn_dt_bias"][l], p["gdn_norm_g"][l])
    if gdn_len == t_len:
        ob, s_new = _gated_deltanet(u, U_QKV // LANES, U_Z // LANES, U_AB // LANES, *gdn_args, gdn_len)
    else:
        ug = jnp.concatenate([u[:, :, U_QKV:U_G], u[:, :, U_AB:U_AB + LANES]], axis=-1)
        ug = jnp.pad(ug, ((0, 0), (0, gdn_len - t_len), (0, 0)))
        ob, s_new = _gated_deltanet(ug, 0, (U_Z - U_QKV) // LANES, (U_G - U_QKV) // LANES, *gdn_args, n_valid)
        ob = ob[:, :t_len]
    oc = nsa_fn(u)
    x_new = _mixout(x2, ca.reshape(m, -1), ob.reshape(m, -1), oc.reshape(m, -1), u2,
                    lw["pa"], lw["pb"], lw["pc"], lw["o"])
    return x_new.reshape(bsz, t_len, d), conv_new[:, HALO - (CONV_W - 1):], qkv_tail, s_new, rows, win


def kernel(x_prompt, x_sample, cache_nsa_kv, cache_win_kv, state_conv_a, state_conv_qkv, state_gdn, cache_mem_kv,
           page_table, mem_prompt, rel_bias, norm_mix, w_in, conv_a_w, conv_a_b, ln_a_g, ln_a_b, w_pa, gdn_conv_w,
           gdn_a_log, gdn_dt_bias, gdn_norm_g, w_pb, nsa_cmp_w, w_pc, w_o, norm_x, w_xq, w_xk, w_xv, w_xo,
           norm_mlp, w_ff1, w_ff2, norm_final):
    p = {"norm_mix": norm_mix, "conv_a_w": conv_a_w, "conv_a_b": conv_a_b, "ln_a_g": ln_a_g, "ln_a_b": ln_a_b,
         "gdn_conv_w": gdn_conv_w, "gdn_a_log": gdn_a_log, "gdn_dt_bias": gdn_dt_bias, "gdn_norm_g": gdn_norm_g}
    depth = w_in.shape[0]
    bp, tp, d = x_prompt.shape
    bs, ts, _ = x_sample.shape
    ts_pad = 8
    n_pages = page_table.shape[1]
    wb = cache_win_kv.shape[2]
    xp = x_prompt
    xs = jnp.pad(x_sample, ((0, 0), (0, ts_pad - ts), (0, 0)))
    ptab, stab = _nsa_tables(rel_bias, tp, n_pages, ts_pad)
    cache_t = jnp.transpose(cache_nsa_kv, (0, 1, 3, 4, 5, 2)).reshape(depth, -1, 4 * LANES, PAGE)
    wbuf_t = jnp.transpose(cache_win_kv, (0, 1, 3, 4, 5, 2)).reshape(depth, bs, 2 * LANES, wb)
    outs = {k: [] for k in ("p_rows", "p_win", "p_conv", "p_qkv", "p_gdn", "p_mem",
                            "s_rows", "s_win", "s_conv", "s_qkv", "s_gdn")}
    for l in range(depth):
        lw = _layer_weights(l, w_in, w_pa, w_pb, w_pc, w_o, w_xq, w_xk, w_xv, w_xo, w_ff1, w_ff2)
        nsa_p = lambda u: _nsa_prompt(u, nsa_cmp_w[l], ptab)
        xp, conv_n, qkv_tail, s_n, rows, win = _mixers(
            xp, lw, p, l, jnp.zeros((bp, HALO, CONV_CH), F32), jnp.zeros((bp, 8, GDN_QKV), F32),
            jnp.zeros((bp, GDN_HEADS, GDN_D, GDN_D), F32), tp, tp, nsa_p)
        mem_kv = _matmul(_bf(mem_prompt.reshape(bp * N_MEM, d)), lw["xkv"]).reshape(bp, N_MEM, 2 * d)
        xp = _cross_attn(xp, norm_x[l], mem_kv, lw["xq"], lw["xo"])
        final_g = norm_final if l == depth - 1 else None
        xp = _mlp(xp.reshape(bp * tp, d), norm_mlp[l], lw["ff1"], lw["ff2"], final_g).reshape(bp, tp, d)
        outs["p_rows"].append(rows.reshape(bp, tp, 4, NSA_KV, NSA_DH))
        outs["p_win"].append(win[:, tp - min(WINDOW, tp):].reshape(bp, min(WINDOW, tp), 2, NSA_KV, NSA_DH))
        outs["p_conv"].append(conv_n)
        outs["p_qkv"].append(qkv_tail)
        outs["p_gdn"].append(s_n)
        outs["p_mem"].append(mem_kv.reshape(bp, N_MEM, 2, X_HEADS, X_DH))
        nsa_s = lambda u: _nsa_sample(l, cache_t, page_table, u, wbuf_t, nsa_cmp_w[l], stab, ts)
        conv_pad = jnp.pad(state_conv_a[l], ((0, 0), (HALO - (CONV_W - 1), 0), (0, 0)))
        qkv_pad = jnp.pad(state_conv_qkv[l], ((0, 0), (5, 0), (0, 0)))
        xs, conv_n, qkv_tail, s_n, rows, win = _mixers(
            xs, lw, p, l, conv_pad, qkv_pad, state_gdn[l], ts, GDN_CHUNK, nsa_s)
        mkv = cache_mem_kv[l].reshape(bs, N_MEM, 2 * d)
        xs = _cross_attn(xs, norm_x[l], mkv, lw["xq"], lw["xo"])
        xs = _mlp(xs.reshape(bs * ts_pad, d), norm_mlp[l], lw["ff1"], lw["ff2"], final_g).reshape(bs, ts_pad, d)
        outs["s_rows"].append(rows[:, :ts].reshape(bs, ts, 4, NSA_KV, NSA_DH))
        win_new = win[:, :ts].reshape(bs, ts, 2, NSA_KV, NSA_DH)
        outs["s_win"].append(jnp.concatenate([cache_win_kv[l], win_new], axis=1)[:, ts:])
        outs["s_conv"].append(conv_n)
        outs["s_qkv"].append(qkv_tail)
        outs["s_gdn"].append(s_n)
    st = lambda k: jnp.stack(outs[k], axis=0)
    return (xp, xs[:, :ts], st("p_rows"), st("p_win"), st("p_conv"), st("p_qkv"), st("p_gdn"), st("p_mem"),
            st("s_rows"), st("s_win"), st("s_conv"), st("s_qkv"), st("s_gdn"))
```

```python
import functools
import math

import jax
import jax.numpy as jnp
import numpy as np
from jax import lax
from jax.experimental import pallas as pl
from jax.experimental.pallas import tpu as pltpu

F32 = jnp.float32
BF16 = jnp.bfloat16

D_MODEL = 1024
CONV_CH = 512
CONV_W = 31
GDN_HEADS = 4
GDN_D = 128
GDN_CHUNK = 64
GDN_QKV = 3 * GDN_HEADS * GDN_D
NSA_HEADS = 8
NSA_KV = 2
NSA_GQ = 4
NSA_DH = 64
L_CMP = 32
L_SEL = 64
N_SEL = 16
WINDOW = 512
Q_BLOCK = 128
FORCE_BONUS = 1e4
PAGE = 128
N_MEM = 256
X_HEADS = 4
X_DH = 256
D_FF = 4096
N_BUCKETS = 32
EPS = 1e-6
NEG = -1e30

LANES = 128
HALO = 32
VMEM_LIMIT = 48 * 1024 * 1024
BIG_VMEM_LIMIT = 56 * 1024 * 1024


def _bf(x):
    return x.astype(BF16)


def _dot(a, b):
    return jnp.dot(a, b, preferred_element_type=F32)


def _dot_nt(a, b):
    return lax.dot_general(a, b, (((1,), (1,)), ((), ())), preferred_element_type=F32)


def _sigmoid(x):
    return 0.5 * jnp.tanh(0.5 * x) + 0.5


def _silu(x):
    return x * _sigmoid(x)


def _params(*sem):
    return pltpu.CompilerParams(dimension_semantics=sem, vmem_limit_bytes=VMEM_LIMIT)


def _mm_kernel(a_ref, w_ref, o_ref):
    o_ref[...] = _dot(a_ref[...], w_ref[...])


def _col_tile(n):
    for tn in (1024, 768, 512, 384, 256, 128):
        if n % tn == 0:
            return tn
    raise ValueError(n)


def _matmul(a, w):
    m, k = a.shape
    n = w.shape[1]
    tm = min(m, 1024)
    tn = _col_tile(n)
    return pl.pallas_call(
        _mm_kernel,
        grid=(m // tm, n // tn),
        in_specs=[pl.BlockSpec((tm, k), lambda i, j: (i, 0)), pl.BlockSpec((k, tn), lambda i, j: (0, j))],
        out_specs=pl.BlockSpec((tm, tn), lambda i, j: (i, j)),
        out_shape=jax.ShapeDtypeStruct((m, n), F32),
        compiler_params=_params("parallel", "parallel"),
        name="matmul",
    )(a, w)


U_A = 0
U_Q = 1024
U_ROWS = 1536
U_QKV = 2048
U_Z = 3584
U_G = 4096
U_WIN = 7168
U_AB = 7424
U_GL = 7552
U_N = 7680
PROJ_TN = 1536


def _proj_kernel(x_ref, g_ref, w_ref, o_ref, rows_ref, win_ref, h_s):
    j = pl.program_id(1)

    @pl.when(j == 0)
    def _():
        x = x_ref[...]
        h_s[...] = _bf(x * lax.rsqrt(jnp.mean(x * x, axis=-1, keepdims=True) + EPS) * g_ref[...])

    acc = _dot(h_s[...], w_ref[...])
    o_ref[...] = acc

    @pl.when(j == U_ROWS // PROJ_TN)
    def _():
        rows_ref[...] = acc[:, U_ROWS % PROJ_TN:U_ROWS % PROJ_TN + 4 * LANES]

    @pl.when(j == U_WIN // PROJ_TN)
    def _():
        win_ref[...] = acc[:, U_WIN % PROJ_TN:U_WIN % PROJ_TN + 2 * LANES]


def _proj_in(x, g, w):
    m, d = x.shape
    n = w.shape[1]
    tm = min(m, 1024)
    assert U_ROWS % PROJ_TN + 4 * LANES <= PROJ_TN and U_WIN % PROJ_TN + 2 * LANES <= PROJ_TN
    return pl.pallas_call(
        _proj_kernel,
        grid=(m // tm, n // PROJ_TN),
        in_specs=[pl.BlockSpec((tm, d), lambda i, j: (i, 0)), pl.BlockSpec((1, d), lambda i, j: (0, 0)),
                  pl.BlockSpec((d, PROJ_TN), lambda i, j: (0, j))],
        out_specs=[pl.BlockSpec((tm, PROJ_TN), lambda i, j: (i, j)),
                   pl.BlockSpec((tm, 4 * LANES), lambda i, j: (i, 0)),
                   pl.BlockSpec((tm, 2 * LANES), lambda i, j: (i, 0))],
        out_shape=[jax.ShapeDtypeStruct((m, n), F32), jax.ShapeDtypeStruct((m, 4 * LANES), F32),
                   jax.ShapeDtypeStruct((m, 2 * LANES), F32)],
        scratch_shapes=[pltpu.VMEM((tm, d), BF16)],
        compiler_params=_params("parallel", "arbitrary"),
        name="proj_in",
    )(x, g.reshape(1, d), w)


def _split3(x):
    hi = _bf(x)
    r = x - hi.astype(F32)
    mid = _bf(r)
    return hi, mid, _bf(r - mid.astype(F32))


def _pick_columns(x, onehot):
    hi, mid, lo = _split3(x)
    return (_dot(lo, onehot) + _dot(mid, onehot)) + _dot(hi, onehot)


def _conf_kernel(n_t, tt, tv, u_ref, halo_ref, st_ref, w_ref, b_ref, g_ref, lb_ref, o_ref, nb_ref, xc_ref, zs_ref):
    t = pl.program_id(1)
    u = u_ref[0]
    xc_ref[HALO:HALO + tt, :] = u[:, :CONV_CH] * _sigmoid(u[:, CONV_CH:])
    if n_t > 1:
        uh = halo_ref[0]
        gh = uh[:, :CONV_CH] * _sigmoid(uh[:, CONV_CH:])
        xc_ref[0:HALO, :] = jnp.where(t > 0, gh, st_ref[0])
    else:
        xc_ref[0:HALO, :] = st_ref[0]
    off = HALO - (CONV_W - 1)
    span = tt + HALO - 8
    for r in range(1, 8):
        zs_ref[r - 1] = xc_ref[r:r + span, :]
    acc = None
    for i in range(CONV_W):
        pos = off + i
        r, base = pos % 8, pos - pos % 8
        src = xc_ref[base:base + tt, :] if r == 0 else zs_ref[r - 1, base:base + tt, :]
        term = src * w_ref[i:i + 1, :]
        acc = term if acc is None else acc + term
    y = acc + b_ref[...]
    mu = jnp.mean(y, axis=-1, keepdims=True)
    yc = y - mu
    var = jnp.mean(yc * yc, axis=-1, keepdims=True)
    ln = yc * lax.rsqrt(var + EPS) * g_ref[...] + lb_ref[...]
    o_ref[0] = _silu(ln)

    @pl.when(t == n_t - 1)
    def _():
        nb_ref[0] = xc_ref[tv:tv + HALO, :]


def _conformer(u_a, state_pad, w_dw, b_dw, ln_g, ln_b, n_valid_last):
    bsz, t_len, _ = u_a.shape
    tt = min(t_len, 256)
    n_t = t_len // tt
    hb = tt // HALO if n_t > 1 else 1
    halo_rows = HALO if n_t > 1 else tt
    w_pad = jnp.pad(w_dw, ((0, HALO - CONV_W), (0, 0)))
    row = lambda v: v.reshape(1, CONV_CH)
    kern = functools.partial(_conf_kernel, n_t, tt, n_valid_last)
    return pl.pallas_call(
        kern,
        grid=(bsz, n_t),
        in_specs=[
            pl.BlockSpec((1, tt, 2 * CONV_CH), lambda b, t: (b, t, 0)),
            pl.BlockSpec((1, halo_rows, 2 * CONV_CH), lambda b, t: (b, jnp.maximum(t * hb - 1, 0), 0)),
            pl.BlockSpec((1, HALO, CONV_CH), lambda b, t: (b, 0, 0)),
            pl.BlockSpec((HALO, CONV_CH), lambda b, t: (0, 0)),
            pl.BlockSpec((1, CONV_CH), lambda b, t: (0, 0)),
            pl.BlockSpec((1, CONV_CH), lambda b, t: (0, 0)),
            pl.BlockSpec((1, CONV_CH), lambda b, t: (0, 0)),
        ],
        out_specs=[
            pl.BlockSpec((1, tt, CONV_CH), lambda b, t: (b, t, 0)),
            pl.BlockSpec((1, HALO, CONV_CH), lambda b, t: (b, 0, 0)),
        ],
        out_shape=[
            jax.ShapeDtypeStruct((bsz, t_len, CONV_CH), F32),
            jax.ShapeDtypeStruct((bsz, HALO, CONV_CH), F32),
        ],
        scratch_shapes=[pltpu.VMEM((HALO + tt, CONV_CH), F32), pltpu.VMEM((7, tt + HALO - 8, CONV_CH), F32)],
        compiler_params=_params("parallel", "arbitrary"),
        name="conformer_conv",
    )(u_a, u_a, state_pad, w_pad, row(b_dw), row(ln_g), row(ln_b))


def _tri_inverse(a_list, ii, jj, merge_shifts):
    mm = lambda p, q: _dot(_bf(p), _bf(q))
    a0 = [jnp.where((ii >> 3) == (jj >> 3), a, 0.0) for a in a_list]
    a2 = [mm(p, p) for p in a0]
    a4 = [mm(p, p) for p in a2]
    r = [(q - p) - mm(p, q) for p, q in zip(a0, a2)]
    r = [(p + q) + mm(p, q) for p, q in zip(r, a4)]
    for sh in merge_shifts:
        mask = ((ii >> (sh + 1)) == (jj >> (sh + 1))) & ((ii >> sh) != (jj >> sh))
        off = [jnp.where(mask, a, 0.0) for a in a_list]
        t = [o + mm(o, p) for o, p in zip(off, r)]
        r = [p - (q + mm(p, q)) for p, q in zip(r, t)]
    return r


def _softplus(x):
    return jnp.maximum(x, 0.0) + jnp.log1p(jnp.exp(-jnp.abs(x)))


def _gdn_kernel(t_len, n_valid, hp, alog_ref, dtb_ref, q_ref, k_ref, v_ref, z_ref, ab_ref,
                sq_ref, sk_ref, sv_ref, wq_ref, wk_ref, wv_ref, s0_ref, ng_ref,
                o_ref, sn_ref,
                xp_s, qn_s, kn_s, vn_s, g_s, be_s, vw_s, kcd_s, qg_s, kdt_s, qk_s, ge_s):
    h0 = pl.program_id(1) * hp
    n_chunks = t_len // GDN_CHUNK
    c_len = GDN_CHUNK

    heads = range(hp)
    hcols = lambda hh: slice(hh * GDN_D, (hh + 1) * GDN_D)

    def conv(x_ref, st_ref, w_ref, hh):
        xp_s[0:8, :] = st_ref[0, :, hcols(hh)]
        xp_s[8:8 + t_len, :] = x_ref[0, :, hcols(hh)]
        acc = xp_s[5:5 + t_len, :] * w_ref[0:1, hcols(hh)]
        for i in range(1, 4):
            acc = acc + xp_s[5 + i:5 + i + t_len, :] * w_ref[i:i + 1, hcols(hh)]
        return _silu(acc)

    col = lax.broadcasted_iota(jnp.int32, (LANES, LANES), 0)
    ab = ab_ref[0]
    for hh in heads:
        h = h0 + hh
        qc = conv(q_ref, sq_ref, wq_ref, hh)
        qn_s[hh] = qc * lax.rsqrt(jnp.sum(qc * qc, axis=-1, keepdims=True) + EPS) * (GDN_D ** -0.5)
        kc = conv(k_ref, sk_ref, wk_ref, hh)
        kn_s[hh] = kc * lax.rsqrt(jnp.sum(kc * kc, axis=-1, keepdims=True) + EPS)
        vn_s[hh] = conv(v_ref, sv_ref, wv_ref, hh)
        a_rep = _pick_columns(ab, _bf(jnp.where(col == h, 1.0, 0.0)))
        b_rep = _pick_columns(ab, _bf(jnp.where(col == GDN_HEADS + h, 1.0, 0.0)))
        a_exp = jnp.exp(jnp.full((1, LANES), alog_ref[h], F32))
        g = -a_exp * _softplus(a_rep + dtb_ref[h])
        beta = _sigmoid(b_rep)
        if n_valid < t_len:
            live = lax.broadcasted_iota(jnp.int32, (t_len, LANES), 0) < n_valid
            g = jnp.where(live, g, 0.0)
            beta = jnp.where(live, beta, 0.0)
        g_s[hh] = g
        be_s[hh] = beta

    ii = lax.broadcasted_iota(jnp.int32, (c_len, c_len), 0)
    jj = lax.broadcasted_iota(jnp.int32, (c_len, c_len), 1)
    incl = ii >= jj
    strict = ii > jj
    ltri = _bf(incl.astype(F32))
    unroll = max(u for u in (1, 2, 4, 8) if n_chunks % u == 0 and u * hp <= 16)

    def cumdecay(g_c):
        g_hi, g_mid, g_lo = _split3(g_c)
        return (_dot(ltri, g_lo) + _dot(ltri, g_mid)) + _dot(ltri, g_hi)

    def prep(cu, carry):
        pairs = [(hh, cu * unroll + u) for hh in heads for u in range(unroll)]
        sls = [pl.ds(pl.multiple_of(c * c_len, c_len), c_len) for _, c in pairs]
        each = lambda f, *ls: [f(*a) for a in zip(*ls)]
        q_l = [qn_s[hh, sl, :] for (hh, _), sl in zip(pairs, sls)]
        k_l = [kn_s[hh, sl, :] for (hh, _), sl in zip(pairs, sls)]
        b_l = [be_s[hh, sl, :] for (hh, _), sl in zip(pairs, sls)]
        gc_l = [cumdecay(g_s[hh, sl, :]) for (hh, _), sl in zip(pairs, sls)]
        dec_l = each(lambda gc: jnp.where(
            incl, jnp.exp(jnp.minimum(gc[:, 0:c_len] - gc.T[0:c_len, :], 0.0)), 0.0), gc_l)
        kb_l = each(lambda k, b: k * b, k_l, b_l)
        a_l = each(lambda kb, k, dec: jnp.where(strict, _dot_nt(_bf(kb), _bf(k)) * dec, 0.0), kb_l, k_l, dec_l)
        r_l = _tri_inverse(a_l, ii, jj, () if n_valid <= 8 else (3, 4, 5))
        eg_l = each(jnp.exp, gc_l)
        rhs_l = [jnp.concatenate([vn_s[hh, sl, :] * b, kb * eg], axis=1)
                 for (hh, _), sl, b, kb, eg in zip(pairs, sls, b_l, kb_l, eg_l)]
        sol_l = each(lambda r, rhs: rhs + _dot(_bf(r), _bf(rhs)), r_l, rhs_l)
        qk_l = each(lambda q, k, dec: jnp.where(incl, _dot_nt(_bf(q), _bf(k)) * dec, 0.0), q_l, k_l, dec_l)
        for (hh, c), sl, sol, qk, q, k, gc, eg in zip(pairs, sls, sol_l, qk_l, q_l, k_l, gc_l, eg_l):
            g_end = gc[c_len - 1:c_len, :]
            vw_s[hh, sl, :] = sol[:, :GDN_D]
            kcd_s[hh, sl, :] = _bf(sol[:, GDN_D:])
            qk_s[hh, c] = _bf(qk)
            qg_s[hh, sl, :] = _bf(q * eg)
            kdt_s[hh, c] = _bf((k * jnp.exp(g_end - gc)).T)
            ge_s[hh, c] = jnp.broadcast_to(jnp.exp(g_end), (8, LANES))
        return carry

    lax.fori_loop(0, n_chunks // unroll, prep, 0)

    def step(c, states):
        sl = pl.ds(pl.multiple_of(c * c_len, c_len), c_len)
        sb = [_bf(s) for s in states]
        v_new = [vw_s[hh, sl, :] - _dot(kcd_s[hh, sl, :], sb[hh]) for hh in heads]
        vb = [_bf(v) for v in v_new]
        for hh in heads:
            o_ref[0, sl, hcols(hh)] = _dot(qg_s[hh, sl, :], sb[hh]) + _dot(qk_s[hh, c], vb[hh])
        return tuple(states[hh] * ge_s[hh, c][0:1, :] + _dot(kdt_s[hh, c], vb[hh]) for hh in heads)

    s_fin = lax.fori_loop(0, n_chunks, step, tuple(s0_ref[0, hh] for hh in heads))
    for hh in heads:
        sn_ref[0, hh] = s_fin[hh]
        o = o_ref[0, :, hcols(hh)]
        y = o * lax.rsqrt(jnp.mean(o * o, axis=-1, keepdims=True) + EPS) * ng_ref[...]
        o_ref[0, :, hcols(hh)] = y * _silu(z_ref[0, :, hcols(hh)])


def _gated_deltanet(u, qkv_blk, z_blk, ab_blk, state_pad, s0, w_conv_pad, a_log, dt_bias, norm_g, n_valid):
    bsz, t_len, _ = u.shape
    nh = GDN_HEADS
    hp = nh if t_len <= 4 * GDN_CHUNK else 2
    wide = hp * GDN_D
    assert (qkv_blk * GDN_D) % wide == 0 and (z_blk * GDN_D) % wide == 0
    ublk = lambda blk: pl.BlockSpec((1, t_len, wide), lambda b, j, o=blk * GDN_D // wide: (b, 0, o + j))
    stb = lambda off: pl.BlockSpec((1, 8, wide), lambda b, j, o=off * GDN_D // wide: (b, 0, o + j))
    wb = lambda off: pl.BlockSpec((8, wide), lambda b, j, o=off * GDN_D // wide: (0, o + j))
    smem = pl.BlockSpec(memory_space=pltpu.SMEM)
    n_chunks = t_len // GDN_CHUNK
    seq = lambda dt: pltpu.VMEM((hp, t_len, GDN_D), dt)
    kern = functools.partial(_gdn_kernel, t_len, n_valid, hp)
    return pl.pallas_call(
        kern,
        grid=(bsz, nh // hp),
        in_specs=[smem, smem, ublk(qkv_blk), ublk(qkv_blk + nh), ublk(qkv_blk + 2 * nh), ublk(z_blk),
                  pl.BlockSpec((1, t_len, GDN_D), lambda b, j: (b, 0, ab_blk)),
                  stb(0), stb(nh), stb(2 * nh), wb(0), wb(nh), wb(2 * nh),
                  pl.BlockSpec((1, hp, GDN_D, GDN_D), lambda b, j: (b, j, 0, 0)),
                  pl.BlockSpec((1, GDN_D), lambda b, j: (0, 0))],
        out_specs=[pl.BlockSpec((1, t_len, wide), lambda b, j: (b, 0, j)),
                   pl.BlockSpec((1, hp, GDN_D, GDN_D), lambda b, j: (b, j, 0, 0))],
        out_shape=[jax.ShapeDtypeStruct((bsz, t_len, nh * GDN_D), F32),
                   jax.ShapeDtypeStruct((bsz, nh, GDN_D, GDN_D), F32)],
        scratch_shapes=[pltpu.VMEM((8 + t_len, GDN_D), F32), seq(F32), seq(F32), seq(F32), seq(F32), seq(F32),
                        seq(F32), seq(BF16), seq(BF16),
                        pltpu.VMEM((hp, n_chunks, GDN_D, GDN_CHUNK), BF16),
                        pltpu.VMEM((hp, n_chunks, GDN_CHUNK, GDN_CHUNK), BF16),
                        pltpu.VMEM((hp, n_chunks, 8, LANES), F32)],
        compiler_params=pltpu.CompilerParams(dimension_semantics=("parallel", "parallel"),
                                             vmem_limit_bytes=BIG_VMEM_LIMIT),
        name="gated_deltanet",
    )(a_log, dt_bias, u, u, u, u, u, state_pad, state_pad, state_pad,
      w_conv_pad, w_conv_pad, w_conv_pad, s0, norm_g.reshape(1, GDN_D))


def _heads_to_rows(x, g, nt):
    lane = lax.broadcasted_iota(jnp.int32, (nt, LANES), 1)
    keep = (lane >= NSA_DH * g) & (lane < NSA_DH * (g + 1))
    parts = []
    for r in range(NSA_GQ):
        hh = NSA_GQ * g + r
        blk = x[:, (hh // 2) * LANES:(hh // 2 + 1) * LANES]
        if hh % 2 != g:
            blk = pltpu.roll(blk, NSA_DH, axis=1)
        parts.append(jnp.where(keep, blk, 0.0))
    return jnp.concatenate(parts, axis=0)


def _rows_to_heads(y, g, nt):
    outs = []
    for m in range(2):
        x0 = y[(2 * m) * nt:(2 * m + 1) * nt]
        x1 = y[(2 * m + 1) * nt:(2 * m + 2) * nt]
        if g == 1:
            x0 = pltpu.roll(x0, NSA_DH, axis=1)
        else:
            x1 = pltpu.roll(x1, NSA_DH, axis=1)
        outs.append(x0 + x1)
    return outs


def _masked_softmax_parts(parts, masks, axis):
    sm = [jnp.where(m, s, NEG) for s, m in zip(parts, masks)]
    mx = functools.reduce(jnp.maximum, [jnp.max(s, axis=axis, keepdims=True) for s in sm])
    es = [jnp.where(m, jnp.exp(s - mx), 0.0) for s, m in zip(sm, masks)]
    den = functools.reduce(lambda p, q: p + q, [jnp.sum(e, axis=axis, keepdims=True) for e in es])
    inv = 1.0 / jnp.maximum(den, 1e-30)
    return [e * inv for e in es]


def _bucket_np(rel):
    n = np.maximum(rel, 0)
    nf = np.maximum(n, 1).astype(np.float32)
    large = 16 + (np.log(nf / np.float32(16)) / np.float32(math.log(8.0)) * np.float32(16)).astype(np.int32)
    return np.where(n < 16, n, np.minimum(large, N_BUCKETS - 1)).astype(np.int32)


LOOKUP_TILE = 8192


def _lookup_kernel(idx_ref, tb_ref, o_ref):
    idx = idx_ref[...]
    acc = jnp.zeros(o_ref.shape, F32)
    for k in range(N_BUCKETS):
        acc = jnp.where(idx == k, tb_ref[:, k:k + 1], acc)
    o_ref[...] = acc


def _bias_lookup(rel_bias, idx_list):
    sizes = [int(np.prod(a.shape)) for a in idx_list]
    total = sum(sizes)
    padded = -(-total // LOOKUP_TILE) * LOOKUP_TILE
    flat = np.zeros((1, padded), np.int32)
    flat[0, :total] = np.concatenate([np.asarray(a, np.int32).reshape(-1) for a in idx_list])
    tab = pl.pallas_call(
        _lookup_kernel,
        grid=(padded // LOOKUP_TILE,),
        in_specs=[pl.BlockSpec((1, LOOKUP_TILE), lambda i: (0, i)),
                  pl.BlockSpec((NSA_HEADS, N_BUCKETS), lambda i: (0, 0))],
        out_specs=pl.BlockSpec((NSA_HEADS, LOOKUP_TILE), lambda i: (0, i)),
        out_shape=jax.ShapeDtypeStruct((NSA_HEADS, padded), F32),
        compiler_params=_params("parallel"),
        name="bias_lookup",
    )(jnp.asarray(flat), rel_bias.astype(F32).T)
    outs, off = [], 0
    for a, n in zip(idx_list, sizes):
        outs.append(tab[:, off:off + n].reshape((NSA_HEADS,) + tuple(a.shape)))
        off += n
    return outs


def _head_rows(tab):
    return tab.reshape(NSA_KV, NSA_GQ * tab.shape[1], tab.shape[2])


def _nsa_prompt_kernel(t_len, q_ref, kcmp_ref, vcmp_ref, kslc_ref, vslc_ref, kwin_ref, vwin_ref, gl_ref, rep_ref,
                       wk_ref, wv_ref, bc_ref, bct_ref, bnear_ref, bwin_ref, o_ref, kc_s, vc_s):
    i = pl.program_id(1)
    nsb = t_len // L_SEL
    qb = Q_BLOCK
    rows = NSA_GQ * qb

    @pl.when(i == 0)
    def _():
        n2 = 2 * lax.broadcasted_iota(jnp.int32, (nsb, t_len), 0)
        cb = lax.broadcasted_iota(jnp.int32, (nsb, t_len), 1) >> 5
        kc = _bf(kcmp_ref[0])
        vc = _bf(vcmp_ref[0])
        wk = wk_ref[...]
        wv = wv_ref[...]
        kc_s[0:nsb, :] = _dot(_bf(jnp.where(cb == n2, wk, 0.0)), kc)
        kc_s[nsb:2 * nsb, :] = _dot(_bf(jnp.where(cb == n2 + 1, wk, 0.0)), kc)
        vc_s[0:nsb, :] = _dot(_bf(jnp.where(cb == n2, wv, 0.0)), vc)
        vc_s[nsb:2 * nsb, :] = _dot(_bf(jnp.where(cb == n2 + 1, wv, 0.0)), vc)

    q_all = q_ref[0] * (NSA_DH ** -0.5)
    gl = gl_ref[0]
    gate_all = [_sigmoid(_pick_columns(gl, rep_ref[br])) for br in range(3)]
    t0 = i * qb
    tq = t0 + (lax.broadcasted_iota(jnp.int32, (rows, 1), 0) & (qb - 1))
    tl = t0 + (lax.broadcasted_iota(jnp.int32, (1, rows), 1) & (qb - 1))
    eye_q = _bf((lax.broadcasted_iota(jnp.int32, (qb, qb), 0) == lax.broadcasted_iota(jnp.int32, (qb, qb), 1)).astype(F32))
    far_end = jnp.maximum(t0 - qb, 0)
    n_far = (far_end + 511) >> 9
    kc = _bf(kc_s[...])
    vc = _bf(vc_s[...])

    def key_aug(k0, n_keys, limit):
        kpos = k0 + lax.broadcasted_iota(jnp.int32, (n_keys, LANES), 0)
        lane = lax.broadcasted_iota(jnp.int32, (n_keys, LANES), 1)
        hit = (lane == (kpos >> 6)) | ((lane == nsb) & ((kpos >= limit) | (kpos < 0)))
        return _bf(jnp.where(hit, NEG, 0.0))

    gs = range(NSA_KV)
    qg = [_bf(_heads_to_rows(q_all, g, qb)) for g in gs]

    n_prev = WINDOW // qb
    starts = [pl.multiple_of(jnp.maximum(t0 + (j - n_prev) * qb, 0), qb) for j in range(n_prev + 1)]
    kwin = [_bf(kwin_ref[0, pl.ds(st, qb), :]) for st in starts]
    pens = [jnp.where(i + (j - n_prev) >= 0, 0.0, NEG) for j in range(n_prev)] + [0.0]
    s_w = [jnp.concatenate([_dot_nt(qg[g], kwin[j]) + pens[j] for j in range(n_prev + 1)], axis=1) + bwin_ref[g]
           for g in gs]

    s_c = [_dot_nt(qg[g], kc) + bc_ref[0, g] for g in gs]
    s_t = [_dot_nt(kc, qg[g]) + bct_ref[0, g] for g in gs]
    e_c = [jnp.where(tq >= L_CMP - 1, jnp.exp(s - jnp.max(s, axis=1, keepdims=True)), 0.0) for s in s_c]
    p_c = [e * (1.0 / jnp.maximum(jnp.sum(e, axis=1, keepdims=True), 1e-30)) for e in e_c]
    o_c = [_dot(_bf(p), vc) for p in p_c]

    e_t = [jnp.where(tl >= L_CMP - 1, jnp.exp(s - jnp.max(s, axis=0, keepdims=True)), 0.0) for s in s_t]
    p_t = [e * (1.0 / jnp.maximum(jnp.sum(e, axis=0, keepdims=True), 1e-30)) for e in e_t]
    head_sum = lambda x: x[:, 0:qb] + x[:, qb:2 * qb] + x[:, 2 * qb:3 * qb] + x[:, 3 * qb:4 * qb]
    blk = lax.broadcasted_iota(jnp.int32, (nsb, qb), 0)
    cur = (t0 + lax.broadcasted_iota(jnp.int32, (nsb, qb), 1)) >> 6
    bonus = jnp.where((blk == 0) | (blk == cur) | (blk == cur - 1), FORCE_BONUS, 0.0)
    score = [jnp.where(blk <= cur, (head_sum(p[0:nsb]) + head_sum(p[nsb:2 * nsb])) + bonus, -1.0) for p in p_t]
    rank = [jnp.zeros((nsb, qb), F32) for _ in gs]
    for j in range(nsb):
        for g in gs:
            sj = score[g][j:j + 1, :]
            ahead = (sj > score[g]) | ((sj == score[g]) & (blk > j))
            rank[g] = rank[g] + jnp.where(ahead, 1.0, 0.0)
    pen_rows = jnp.where(lax.broadcasted_iota(jnp.int32, (LANES - nsb, qb), 0) == 0, 1.0, 0.0)
    not_sel_t = [jnp.where(r < float(min(N_SEL, nsb)), 0.0, 1.0) for r in rank]
    q_aug = [_bf(_dot_nt(eye_q, _bf(jnp.concatenate([ns, pen_rows], axis=0)))) for ns in not_sel_t]
    qa = [jnp.concatenate([qg[g], jnp.concatenate([q_aug[g]] * NSA_GQ, axis=0)], axis=1) for g in gs]

    def online(carry, s, pv):
        m_i, l_i, acc = carry
        m_n = jnp.maximum(m_i, jnp.max(s, axis=1, keepdims=True))
        p = jnp.exp(s - m_n)
        alpha = jnp.exp(m_i - m_n)
        return m_n, alpha * l_i + jnp.sum(p, axis=1, keepdims=True), alpha * acc + pv(_bf(p))

    def far_tile(kt, carry):
        k0 = pl.multiple_of(kt * 512, 512)
        ka = jnp.concatenate([_bf(kslc_ref[0, pl.ds(k0, 512), :]), key_aug(k0, 512, far_end)], axis=1)
        vt = _bf(vslc_ref[0, pl.ds(k0, 512), :])
        s = [_dot_nt(qa[g], ka) for g in gs]
        return tuple(online(carry[g], s[g], lambda p: _dot(p, vt)) for g in gs)

    init = (jnp.full((rows, 1), NEG, F32), jnp.zeros((rows, 1), F32), jnp.zeros((rows, LANES), F32))
    far = lax.fori_loop(0, n_far, far_tile, tuple(init for _ in gs))

    p0 = pl.multiple_of(jnp.maximum(t0 - qb, 0), qb)
    d0 = pl.multiple_of(t0, qb)
    ka = jnp.concatenate([
        jnp.concatenate([_bf(kslc_ref[0, pl.ds(p0, qb), :]), _bf(kslc_ref[0, pl.ds(d0, qb), :])], axis=0),
        key_aug(t0 - qb, 2 * qb, t_len)], axis=1)
    vp = _bf(vslc_ref[0, pl.ds(p0, qb), :])
    vd = _bf(vslc_ref[0, pl.ds(d0, qb), :])
    s_near = [_dot_nt(qa[g], ka) + bnear_ref[g] for g in gs]
    fin = [online(far[g], s_near[g], lambda p: _dot(p[:, 0:qb], vp) + _dot(p[:, qb:2 * qb], vd)) for g in gs]
    o_s = [acc * (1.0 / l_n) for _, l_n, acc in fin]

    vwin = [_bf(vwin_ref[0, pl.ds(st, qb), :]) for st in starts]
    e_w = [jnp.exp(s - jnp.max(s, axis=1, keepdims=True)) for s in s_w]
    o_w = []
    for g in gs:
        ew = _bf(e_w[g])
        acc = _dot(ew[:, 0:qb], vwin[0])
        for j in range(1, n_prev + 1):
            acc = acc + _dot(ew[:, j * qb:(j + 1) * qb], vwin[j])
        o_w.append(acc * (1.0 / jnp.sum(e_w[g], axis=1, keepdims=True)))

    for g in gs:
        gates = [_heads_to_rows(gate_all[br], g, qb) for br in range(3)]
        comb = gates[0] * o_c[g] + gates[1] * o_s[g] + gates[2] * o_w[g]
        blocks = _rows_to_heads(comb, g, qb)
        o_ref[0, :, (2 * g) * LANES:(2 * g + 1) * LANES] = blocks[0]
        o_ref[0, :, (2 * g + 1) * LANES:(2 * g + 2) * LANES] = blocks[1]


def _nsa_tables(rel_bias, t_len, n_pages, nt):
    nqb = t_len // Q_BLOCK
    nsb = t_len // L_SEL
    past = n_pages * PAGE
    t = np.arange(Q_BLOCK)
    tq = (np.arange(nqb)[:, None] * Q_BLOCK + t[None, :])[:, :, None]
    n = np.arange(nsb)[None, None, :]
    ts = np.arange(nt)[:, None]
    j = np.arange(2 * n_pages)[None, :]
    c = np.arange(PAGE)[None, :]
    idx = [
        _bucket_np(tq - (n * L_SEL + L_CMP - 1)),
        _bucket_np(tq - (n * L_SEL + L_SEL - 1)),
        _bucket_np(Q_BLOCK + t[:, None] - np.arange(2 * Q_BLOCK)[None, :]),
        _bucket_np(WINDOW + t[:, None] - np.arange(WINDOW + Q_BLOCK)[None, :]),
        _bucket_np(past + ts - (j * L_SEL + L_CMP - 1)),
        _bucket_np(past + ts - (j * L_SEL + L_SEL - 1)),
        _bucket_np(PAGE + ts - c),
        _bucket_np(ts - c),
        _bucket_np(WINDOW + ts - np.arange(WINDOW)[None, :]),
    ]
    ce, co, near, win, sce, sco, slast, snew, swin = _bias_lookup(rel_bias, idx)
    b31 = rel_bias.astype(F32)[N_BUCKETS - 1].reshape(NSA_KV, NSA_GQ, 1, 1)
    shift = lambda tab: (tab.reshape(NSA_KV, NSA_GQ, tab.shape[1], tab.shape[2]) - b31).reshape(
        NSA_KV, NSA_GQ * tab.shape[1], tab.shape[2])
    vis = lambda m: jnp.asarray(np.tile(m, (1,) * (m.ndim - 2) + (NSA_GQ, 1)))
    blocked = lambda tab: jnp.swapaxes(tab, 0, 1).reshape(nqb, NSA_KV, NSA_GQ * Q_BLOCK, nsb)
    bc = jnp.concatenate([blocked(ce), blocked(co)], axis=-1)
    vis_c = np.concatenate([n * L_SEL + L_CMP - 1 <= tq, n * L_SEL + L_SEL - 1 <= tq], axis=-1)
    bc = jnp.where(vis(vis_c)[:, None], bc, NEG)
    c_near = np.arange(2 * Q_BLOCK)[None, :]
    near_m = jnp.where(vis(c_near <= Q_BLOCK + t[:, None])[None], shift(near), NEG)
    c_win = np.arange(WINDOW + Q_BLOCK)[None, :]
    win_m = jnp.where(vis((c_win > t[:, None]) & (c_win <= WINDOW + t[:, None]))[None], _head_rows(win), NEG)
    ptab = (bc, jnp.swapaxes(bc, -1, -2), near_m, win_m)
    rows64 = lambda tab: tab.reshape(NSA_KV * NSA_GQ * nt, tab.shape[-1])
    stab = (rows64(_head_rows(sce)), rows64(_head_rows(sco)), rows64(shift(slast)), rows64(shift(snew)),
            rows64(_head_rows(swin)), rows64(_head_rows(snew)))
    return ptab, stab


def _gate_rep():
    j = np.arange(LANES)[None, :, None]
    c = np.arange(NSA_HEADS * NSA_DH)[None, None, :]
    br = np.arange(3)[:, None, None]
    return jnp.asarray(j == br * NSA_HEADS + c // NSA_DH, BF16)


def _nsa_prompt(u, w_pos, tables):
    bsz, t_len, _ = u.shape
    nqb = t_len // Q_BLOCK
    nsb = t_len // L_SEL
    assert nsb < LANES
    bc, bct, near, wtab = tables
    wk = jnp.tile(w_pos[0], t_len // L_CMP).reshape(1, t_len)
    wv = jnp.tile(w_pos[1], t_len // L_CMP).reshape(1, t_len)
    rep = _gate_rep()
    seq = lambda c: pl.BlockSpec((1, t_len, LANES), lambda b, i, c=c: (b, 0, c))
    full = lambda a: pl.BlockSpec(a.shape, lambda b, i, nd=a.ndim: (0,) * nd)
    per_i = lambda a: pl.BlockSpec((1,) + a.shape[1:], lambda b, i, nd=a.ndim: (i,) + (0,) * (nd - 1))
    kern = functools.partial(_nsa_prompt_kernel, t_len)
    kv0 = U_ROWS // LANES
    w0 = U_WIN // LANES
    return pl.pallas_call(
        kern,
        grid=(bsz, nqb),
        in_specs=[pl.BlockSpec((1, Q_BLOCK, 512), lambda b, i: (b, i, U_Q // 512)),
                  seq(kv0), seq(kv0 + 1), seq(kv0 + 2), seq(kv0 + 3), seq(w0), seq(w0 + 1),
                  pl.BlockSpec((1, Q_BLOCK, LANES), lambda b, i: (b, i, U_GL // LANES)), full(rep),
                  full(wk), full(wv), per_i(bc), per_i(bct), full(near), full(wtab)],
        out_specs=pl.BlockSpec((1, Q_BLOCK, 512), lambda b, i: (b, i, 0)),
        out_shape=jax.ShapeDtypeStruct((bsz, t_len, 512), F32),
        scratch_shapes=[pltpu.VMEM((2 * nsb, LANES), F32)] * 2,
        compiler_params=_params("parallel", "arbitrary"),
        name="nsa_prompt",
    )(u, u, u, u, u, u, u, u, rep, wk, wv, bc, bct, near, wtab)


def _nsa_sample_kernel(layer, n_pages, n_new, pt_ref, cache_ref, q_ref, rows_ref, wnew_ref, wbuf_ref, gl_ref,
                       rep_ref, wpool_ref, bce_ref, bco_ref, blast_ref, bnew_ref, bwin_ref, bwnew_ref,
                       o_ref, cmp_s, slc_s, pool_s, exp_s, pad_s, sem):
    b = pl.program_id(0)
    nb = pl.num_programs(0)
    nt = 8
    past = n_pages * PAGE
    nblk = 2 * n_pages
    rows = NSA_KV * NSA_GQ * nt
    half = 2 * LANES

    def page_copy(seq, p, part, buf, s):
        return pltpu.make_async_copy(
            cache_ref.at[layer, pt_ref[seq, p], pl.ds(part * half, half), :],
            buf.at[:, pl.ds(pl.multiple_of(p * PAGE, PAGE), PAGE)], s)

    def start_gather(seq, part, buf, s):
        def body(p, c):
            page_copy(seq, p, part, buf, s).start()
            return c
        lax.fori_loop(0, n_pages, body, 0)

    def wait_gather(seq, part, buf, s):
        def body(p, c):
            page_copy(seq, p, part, buf, s).wait()
            return c
        lax.fori_loop(0, n_pages, body, 0)

    @pl.when(b == 0)
    def _():
        start_gather(0, 0, cmp_s, sem.at[0])
        start_gather(0, 1, slc_s.at[0], sem.at[1])
        cb = lax.broadcasted_iota(jnp.int32, (past, nblk), 0) >> 5
        j2 = 2 * lax.broadcasted_iota(jnp.int32, (past, nblk), 1)
        pool_s[0] = _bf(jnp.where(cb == j2, 1.0, 0.0))
        pool_s[1] = _bf(jnp.where(cb == j2 + 1, 1.0, 0.0))
        ej = lax.broadcasted_iota(jnp.int32, (nblk, past), 0)
        ec = lax.broadcasted_iota(jnp.int32, (nblk, past), 1) >> 6
        exp_s[...] = _bf(jnp.where(ej == ec, 1.0, 0.0))
        pad_s[...] = jnp.zeros(pad_s.shape, F32)

    slot = b % 2
    slc = slc_s.at[slot]

    @pl.when(b + 1 < nb)
    def _():
        start_gather(b + 1, 1, slc_s.at[1 - slot], sem.at[2 - slot])

    wait_gather(b, 0, cmp_s, sem.at[0])
    wait_gather(b, 1, slc, sem.at[1 + slot])

    q_all = q_ref[0] * (NSA_DH ** -0.5)
    qq = _bf(jnp.concatenate([_heads_to_rows(q_all, g, nt) for g in range(NSA_KV)], axis=0))
    tr = lax.broadcasted_iota(jnp.int32, (rows, 1), 0) & (nt - 1)

    wp = wpool_ref[...]
    ks = _bf(cmp_s[0:LANES, :] * wp[0:1, :])
    vs = _bf(cmp_s[LANES:half, :] * wp[1:2, :])
    kce = _bf(_dot(ks, pool_s[0]))
    kco = _bf(_dot(ks, pool_s[1]))
    vce = _bf(_dot(vs, pool_s[0]))
    vco = _bf(_dot(vs, pool_s[1]))

    @pl.when(b + 1 < nb)
    def _():
        start_gather(b + 1, 0, cmp_s, sem.at[0])

    new = rows_ref[0]
    wnew = wnew_ref[0]
    pad_s[0, 0:nt, :] = new[:, 2 * LANES:3 * LANES]
    pad_s[1, 0:nt, :] = new[:, 3 * LANES:4 * LANES]
    pad_s[2, 0:nt, :] = wnew[:, 0:LANES]
    pad_s[3, 0:nt, :] = wnew[:, LANES:2 * LANES]

    wb = wbuf_ref[0, 0]
    s_all = _dot(qq, _bf(slc[0:LANES, :]))
    s_new = _dot_nt(qq, _bf(pad_s[0])) + bnew_ref[...]
    s_win = _dot(qq, _bf(wb[0:LANES, :])) + bwin_ref[...]
    s_wnew = _dot_nt(qq, _bf(pad_s[2])) + bwnew_ref[...]

    se = _dot(qq, kce) + bce_ref[...]
    so = _dot(qq, kco) + bco_ref[...]
    mx = jnp.maximum(jnp.max(se, axis=1, keepdims=True), jnp.max(so, axis=1, keepdims=True))
    ee = jnp.exp(se - mx)
    eo = jnp.exp(so - mx)
    inv = 1.0 / (jnp.sum(ee, axis=1, keepdims=True) + jnp.sum(eo, axis=1, keepdims=True))
    pe = ee * inv
    po = eo * inv
    o_c = _dot_nt(_bf(pe), vce) + _dot_nt(_bf(po), vco)

    def head_sum(pr):
        return jnp.concatenate(
            [pr[g * 4 * nt:g * 4 * nt + nt] + pr[g * 4 * nt + nt:g * 4 * nt + 2 * nt]
             + pr[g * 4 * nt + 2 * nt:g * 4 * nt + 3 * nt] + pr[g * 4 * nt + 3 * nt:g * 4 * nt + 4 * nt]
             for g in range(NSA_KV)], axis=0)

    jcol = lax.broadcasted_iota(jnp.int32, (NSA_KV * nt, nblk), 1)
    forced = (jcol == 0) | (jcol == nblk - 1)
    score = (head_sum(pe) + head_sum(po)) + jnp.where(forced, FORCE_BONUS, 0.0)
    rank = jnp.where(FORCE_BONUS > score, 1.0, 0.0)
    for j in range(nblk):
        sj = score[:, j:j + 1]
        ahead = (sj > score) | ((sj == score) & (jcol > j))
        rank = rank + jnp.where(ahead, 1.0, 0.0)
    sel = jnp.where(rank < float(N_SEL), 1.0, 0.0)
    sel_rows = jnp.concatenate([sel[g * nt:(g + 1) * nt] for g in range(NSA_KV) for _ in range(NSA_GQ)], axis=0)

    tc = lax.broadcasted_iota(jnp.int32, (rows, LANES), 1)
    mnew = (tc <= tr) & (tc < n_new)

    mk = _dot(_bf(sel_rows), exp_s[...]) > 0.5
    far = past - PAGE
    p_far, p_last, p_new = _masked_softmax_parts(
        [s_all[:, :far], s_all[:, far:] + blast_ref[...], s_new], [mk[:, :far], mk[:, far:], mnew], 1)
    o_s = _dot_nt(_bf(jnp.concatenate([p_far, p_last], axis=1)), _bf(slc[LANES:half, :])) \
        + _dot(_bf(p_new), _bf(pad_s[1]))

    cw = lax.broadcasted_iota(jnp.int32, (rows, WINDOW), 1)
    pw, pn = _masked_softmax_parts([s_win, s_wnew], [cw > tr, mnew], 1)
    o_w = _dot_nt(_bf(pw), _bf(wb[LANES:half, :])) + _dot(_bf(pn), _bf(pad_s[3]))

    gl = gl_ref[0]
    gate_all = [_sigmoid(_pick_columns(gl, rep_ref[br])) for br in range(3)]
    gates = [jnp.concatenate([_heads_to_rows(ga, g, nt) for g in range(NSA_KV)], axis=0) for ga in gate_all]
    comb = gates[0] * o_c + gates[1] * o_s + gates[2] * o_w
    for g in range(NSA_KV):
        blocks = _rows_to_heads(comb[g * 4 * nt:(g + 1) * 4 * nt], g, nt)
        o_ref[0, :, (2 * g) * LANES:(2 * g + 1) * LANES] = blocks[0]
        o_ref[0, :, (2 * g + 1) * LANES:(2 * g + 2) * LANES] = blocks[1]


def _nsa_sample(layer, cache_t, page_table, u, wbuf_t, w_pos, tables, n_new):
    bsz, n_pages = page_table.shape
    nt = u.shape[1]
    past = n_pages * PAGE
    wpool = jnp.tile(w_pos, (1, past // L_CMP))
    rep = _gate_rep()
    full = lambda a: pl.BlockSpec(a.shape, lambda b, pt, nd=a.ndim: (0,) * nd)
    ucols = lambda width, off: pl.BlockSpec((1, nt, width), lambda b, pt: (b, 0, off // width))
    kern = functools.partial(_nsa_sample_kernel, layer, n_pages, n_new)
    grid_spec = pltpu.PrefetchScalarGridSpec(
        num_scalar_prefetch=1,
        grid=(bsz,),
        in_specs=[pl.BlockSpec(memory_space=pl.ANY),
                  ucols(512, U_Q), ucols(512, U_ROWS), ucols(2 * LANES, U_WIN),
                  pl.BlockSpec((1, 1) + wbuf_t.shape[2:], lambda b, pt: (layer, b, 0, 0)),
                  ucols(LANES, U_GL), full(rep), full(wpool)] + [full(t) for t in tables],
        out_specs=pl.BlockSpec((1, nt, 512), lambda b, pt: (b, 0, 0)),
        scratch_shapes=[pltpu.VMEM((2 * LANES, past), F32), pltpu.VMEM((2, 2 * LANES, past), F32),
                        pltpu.VMEM((2, past, 2 * n_pages), BF16), pltpu.VMEM((2 * n_pages, past), BF16),
                        pltpu.VMEM((4, LANES, LANES), F32), pltpu.SemaphoreType.DMA((3,))],
    )
    return pl.pallas_call(
        kern,
        grid_spec=grid_spec,
        out_shape=jax.ShapeDtypeStruct((bsz, nt, 512), F32),
        compiler_params=pltpu.CompilerParams(dimension_semantics=("arbitrary",), vmem_limit_bytes=BIG_VMEM_LIMIT),
        name="nsa_sample",
    )(page_table, cache_t, u, u, u, wbuf_t, u, rep, wpool, *tables)


def _mixout_kernel(x_ref, ca_ref, ob_ref, oc_ref, ga_ref, gb_ref, gc_ref, wpa_ref, wpb_ref, wpc_ref, wo_ref, o_ref):
    y = _sigmoid(ga_ref[...]) * _dot(_bf(ca_ref[...]), wpa_ref[...])
    y = y + _sigmoid(gb_ref[...]) * _dot(_bf(ob_ref[...]), wpb_ref[...])
    y = y + _sigmoid(gc_ref[...]) * _dot(_bf(oc_ref[...]), wpc_ref[...])
    o_ref[...] = x_ref[...] + _dot(_bf(y), wo_ref[...])


def _mixout(x, ca, ob, oc, u, wpa, wpb, wpc, wo):
    m = x.shape[0]
    tm = min(m, 512)
    rowblk = lambda n: pl.BlockSpec((tm, n), lambda i: (i, 0))
    gate = lambda k: pl.BlockSpec((tm, D_MODEL), lambda i, k=k: (i, U_G // D_MODEL + k))
    full = lambda a: pl.BlockSpec(a.shape, lambda i: (0, 0))
    return pl.pallas_call(
        _mixout_kernel,
        grid=(m // tm,),
        in_specs=[rowblk(D_MODEL), rowblk(512), rowblk(512), rowblk(512), gate(0), gate(1), gate(2),
                  full(wpa), full(wpb), full(wpc), full(wo)],
        out_specs=rowblk(D_MODEL),
        out_shape=jax.ShapeDtypeStruct((m, D_MODEL), F32),
        compiler_params=_params("parallel"),
        name="mixer_out",
    )(x, ca, ob, oc, u, u, u, wpa, wpb, wpc, wo)


def _xattn_kernel(tiled_kv, x_ref, g_ref, kv_ref, wq_ref, wo_ref, o_ref):
    x = x_ref[0]
    h = _bf(x * lax.rsqrt(jnp.mean(x * x, axis=-1, keepdims=True) + EPS) * g_ref[...])
    q = _dot(h, wq_ref[...])

    def mem_head(which, hd):
        if not tiled_kv:
            c0 = which * D_MODEL + hd * X_DH
            return kv_ref[0, :, c0:c0 + X_DH]
        return jnp.concatenate(
            [kv_ref[0, 0, pl.ds(which * 8 + half * X_HEADS + hd, N_MEM, stride=16), :] for half in range(2)], axis=1)

    outs = []
    for hd in range(X_HEADS):
        qh = _bf(q[:, hd * X_DH:(hd + 1) * X_DH])
        kh = _bf(mem_head(0, hd))
        vh = _bf(mem_head(1, hd))
        s = _dot_nt(qh, kh) * (X_DH ** -0.5)
        e = jnp.exp(s - jnp.max(s, axis=-1, keepdims=True))
        pr = e * (1.0 / jnp.sum(e, axis=-1, keepdims=True))
        outs.append(_dot(_bf(pr), vh))
    o = jnp.concatenate(outs, axis=1)
    o_ref[0] = x + _dot(_bf(o), wo_ref[...])


def _cross_attn(x, g, mem_kv, wq, wo, layer=None):
    bsz, t_len, d = x.shape
    tt = min(t_len, 512)
    full = lambda a: pl.BlockSpec(a.shape, lambda b, t: (0, 0))
    if layer is None:
        kv_spec = pl.BlockSpec((1, N_MEM, 2 * d), lambda b, t: (b, 0, 0))
    else:
        kv_spec = pl.BlockSpec((1, 1) + mem_kv.shape[2:], lambda b, t: (layer, b, 0, 0))
    return pl.pallas_call(
        functools.partial(_xattn_kernel, layer is not None),
        grid=(bsz, t_len // tt),
        in_specs=[pl.BlockSpec((1, tt, d), lambda b, t: (b, t, 0)),
                  pl.BlockSpec((1, d), lambda b, t: (0, 0)),
                  kv_spec, full(wq), full(wo)],
        out_specs=pl.BlockSpec((1, tt, d), lambda b, t: (b, t, 0)),
        out_shape=jax.ShapeDtypeStruct((bsz, t_len, d), F32),
        compiler_params=_params("parallel", "parallel"),
        name="cross_attn",
    )(x, g.reshape(1, d), mem_kv, wq, wo)


FF_CHUNK = 1024


def _mlp_kernel(n_k, final_norm, x_ref, g_ref, gf_ref, w1_ref, w2_ref, o_ref, h_s, acc_s):
    k = pl.program_id(1)

    @pl.when(k == 0)
    def _():
        x = x_ref[...]
        h_s[...] = _bf(x * lax.rsqrt(jnp.mean(x * x, axis=-1, keepdims=True) + EPS) * g_ref[...])
        acc_s[...] = x

    a = jnp.maximum(_dot(h_s[...], w1_ref[...]), 0.0)
    acc_s[...] += _dot(_bf(a * a), w2_ref[...])

    @pl.when(k == n_k - 1)
    def _():
        y = acc_s[...]
        if final_norm:
            y = y * lax.rsqrt(jnp.mean(y * y, axis=-1, keepdims=True) + EPS) * gf_ref[...]
        o_ref[...] = y


def _mlp(x, g, w1, w2, final_g=None):
    m, d = x.shape
    tm = min(m, 1024)
    n_k = D_FF // FF_CHUNK
    gf = g if final_g is None else final_g
    return pl.pallas_call(
        functools.partial(_mlp_kernel, n_k, final_g is not None),
        grid=(m // tm, n_k),
        in_specs=[pl.BlockSpec((tm, d), lambda i, k: (i, 0)),
                  pl.BlockSpec((1, d), lambda i, k: (0, 0)),
                  pl.BlockSpec((1, d), lambda i, k: (0, 0)),
                  pl.BlockSpec((d, FF_CHUNK), lambda i, k: (0, k)),
                  pl.BlockSpec((FF_CHUNK, d), lambda i, k: (k, 0))],
        out_specs=pl.BlockSpec((tm, d), lambda i, k: (i, 0)),
        out_shape=jax.ShapeDtypeStruct((m, d), F32),
        scratch_shapes=[pltpu.VMEM((tm, d), BF16), pltpu.VMEM((tm, d), F32)],
        compiler_params=_params("parallel", "arbitrary"),
        name="sq_relu_mlp",
    )(x, g.reshape(1, d), gf.reshape(1, d), w1, w2)


A_COLS = 2 * CONV_CH
B0 = A_COLS
Z0 = B0 + GDN_QKV
AB0 = Z0 + GDN_HEADS * GDN_D
C0 = AB0 + 2 * GDN_HEADS
KV0 = C0 + NSA_HEADS * NSA_DH
GL0 = KV0 + 6 * NSA_KV * NSA_DH
G0 = GL0 + 3 * NSA_HEADS
N_IN = G0 + 3 * D_MODEL


def _layer_weights(l, w_in, w_pa, w_pb, w_pc, w_o, w_xq, w_xk, w_xv, w_xo, w_ff1, w_ff2):
    w = w_in[l]
    lane_pad = lambda cols: jnp.pad(cols, ((0, 0), (0, LANES - cols.shape[1])))
    groups = [(U_A, w[:, 0:A_COLS]), (U_Q, w[:, C0:KV0]), (U_ROWS, w[:, KV0:KV0 + 4 * LANES]),
              (U_QKV, w[:, B0:Z0]), (U_Z, w[:, Z0:AB0]), (U_G, w[:, G0:N_IN]),
              (U_WIN, w[:, KV0 + 4 * LANES:GL0]), (U_AB, lane_pad(w[:, AB0:C0])), (U_GL, lane_pad(w[:, GL0:G0]))]
    off = 0
    for start, cols in groups:
        assert start == off
        off += cols.shape[1]
    assert off == U_N
    return {
        "in": _bf(jnp.concatenate([cols for _, cols in groups], axis=1)),
        "pa": _bf(w_pa[l]), "pb": _bf(w_pb[l]), "pc": _bf(w_pc[l]), "o": _bf(w_o[l]),
        "xq": _bf(w_xq[l]), "xo": _bf(w_xo[l]),
        "xkv": _bf(jnp.concatenate([w_xk[l], w_xv[l]], axis=1)),
        "ff1": _bf(w_ff1[l]), "ff2": _bf(w_ff2[l]),
    }


def _mixers(x, lw, p, l, conv_state_pad, qkv_state_pad, s0, n_valid, gdn_len, nsa_fn):
    bsz, t_len, d = x.shape
    m = bsz * t_len
    x2 = x.reshape(m, d)
    u2, rows, win = _proj_in(x2, p["norm_mix"][l], lw["in"])
    u = u2.reshape(bsz, t_len, U_N)
    rows = rows.reshape(bsz, t_len, 4 * LANES)
    win = win.reshape(bsz, t_len, 2 * LANES)
    qkv_tail = u[:, max(n_valid - 3, 0):n_valid, U_QKV:U_QKV + GDN_QKV]

    ca, conv_new = _conformer(u, conv_state_pad, p["conv_a_w"][l], p["conv_a_b"][l], p["ln_a_g"][l],
                              p["ln_a_b"][l], n_valid if n_valid < t_len else min(t_len, 256))
    w_conv_pad = jnp.pad(p["gdn_conv_w"][l], ((0, 4), (0, 0)))
    gdn_args = (qkv_state_pad, s0, w_conv_pad, p["gdn_a_log"][l], p["gdn_dt_bias"][l], p["gdn_norm_g"][l])
    if gdn_len == t_len:
        ob, s_new = _gated_deltanet(u, U_QKV // LANES, U_Z // LANES, U_AB // LANES, *gdn_args, gdn_len)
    else:
        ug = jnp.concatenate([u[:, :, U_QKV:U_G], u[:, :, U_AB:U_AB + LANES]], axis=-1)
        ug = jnp.pad(ug, ((0, 0), (0, gdn_len - t_len), (0, 0)))
        ob, s_new = _gated_deltanet(ug, 0, (U_Z - U_QKV) // LANES, (U_G - U_QKV) // LANES, *gdn_args, n_valid)
        ob = ob[:, :t_len]
    oc = nsa_fn(u)
    x_new = _mixout(x2, ca.reshape(m, -1), ob.reshape(m, -1), oc.reshape(m, -1), u2,
                    lw["pa"], lw["pb"], lw["pc"], lw["o"])
    return x_new.reshape(bsz, t_len, d), conv_new[:, HALO - (CONV_W - 1):], qkv_tail, s_new, rows, win


def kernel(x_prompt, x_sample, cache_nsa_kv, cache_win_kv, state_conv_a, state_conv_qkv, state_gdn, cache_mem_kv,
           page_table, mem_prompt, rel_bias, norm_mix, w_in, conv_a_w, conv_a_b, ln_a_g, ln_a_b, w_pa, gdn_conv_w,
           gdn_a_log, gdn_dt_bias, gdn_norm_g, w_pb, nsa_cmp_w, w_pc, w_o, norm_x, w_xq, w_xk, w_xv, w_xo,
           norm_mlp, w_ff1, w_ff2, norm_final):
    p = {"norm_mix": norm_mix, "conv_a_w": conv_a_w, "conv_a_b": conv_a_b, "ln_a_g": ln_a_g, "ln_a_b": ln_a_b,
         "gdn_conv_w": gdn_conv_w, "gdn_a_log": gdn_a_log, "gdn_dt_bias": gdn_dt_bias, "gdn_norm_g": gdn_norm_g}
    depth = w_in.shape[0]
    bp, tp, d = x_prompt.shape
    bs, ts, _ = x_sample.shape
    ts_pad = 8
    n_pages = page_table.shape[1]
    wb = cache_win_kv.shape[2]
    xp = x_prompt
    xs = jnp.pad(x_sample, ((0, 0), (0, ts_pad - ts), (0, 0)))
    ptab, stab = _nsa_tables(rel_bias, tp, n_pages, ts_pad)
    cache_t = jnp.transpose(cache_nsa_kv, (0, 1, 3, 4, 5, 2)).reshape(depth, -1, 4 * LANES, PAGE)
    wbuf_t = jnp.transpose(cache_win_kv, (0, 1, 3, 4, 5, 2)).reshape(depth, bs, 2 * LANES, wb)
    mem_t = cache_mem_kv.reshape(depth, bs, N_MEM, 2, X_HEADS, 2, LANES)
    mem_t = jnp.transpose(mem_t, (0, 1, 2, 3, 5, 4, 6)).reshape(depth, bs, N_MEM * 2 * 2 * X_HEADS, LANES)
    outs = {k: [] for k in ("p_rows", "p_win", "p_conv", "p_qkv", "p_gdn", "p_mem",
                            "s_rows", "s_win", "s_conv", "s_qkv", "s_gdn")}
    for l in range(depth):
        lw = _layer_weights(l, w_in, w_pa, w_pb, w_pc, w_o, w_xq, w_xk, w_xv, w_xo, w_ff1, w_ff2)
        nsa_p = lambda u: _nsa_prompt(u, nsa_cmp_w[l], ptab)
        xp, conv_n, qkv_tail, s_n, rows, win = _mixers(
            xp, lw, p, l, jnp.zeros((bp, HALO, CONV_CH), F32), jnp.zeros((bp, 8, GDN_QKV), F32),
            jnp.zeros((bp, GDN_HEADS, GDN_D, GDN_D), F32), tp, tp, nsa_p)
        mem_kv = _matmul(_bf(mem_prompt.reshape(bp * N_MEM, d)), lw["xkv"]).reshape(bp, N_MEM, 2 * d)
        xp = _cross_attn(xp, norm_x[l], mem_kv, lw["xq"], lw["xo"])
        final_g = norm_final if l == depth - 1 else None
        xp = _mlp(xp.reshape(bp * tp, d), norm_mlp[l], lw["ff1"], lw["ff2"], final_g).reshape(bp, tp, d)
        outs["p_rows"].append(rows.reshape(bp, tp, 4, NSA_KV, NSA_DH))
        outs["p_win"].append(win[:, tp - min(WINDOW, tp):].reshape(bp, min(WINDOW, tp), 2, NSA_KV, NSA_DH))
        outs["p_conv"].append(conv_n)
        outs["p_qkv"].append(qkv_tail)
        outs["p_gdn"].append(s_n)
        outs["p_mem"].append(mem_kv.reshape(bp, N_MEM, 2, X_HEADS, X_DH))
        nsa_s = lambda u: _nsa_sample(l, cache_t, page_table, u, wbuf_t, nsa_cmp_w[l], stab, ts)
        conv_pad = jnp.pad(state_conv_a[l], ((0, 0), (HALO - (CONV_W - 1), 0), (0, 0)))
        qkv_pad = jnp.pad(state_conv_qkv[l], ((0, 0), (5, 0), (0, 0)))
        xs, conv_n, qkv_tail, s_n, rows, win = _mixers(
            xs, lw, p, l, conv_pad, qkv_pad, state_gdn[l], ts, GDN_CHUNK, nsa_s)
        xs = _cross_attn(xs, norm_x[l], mem_t, lw["xq"], lw["xo"], layer=l)
        xs = _mlp(xs.reshape(bs * ts_pad, d), norm_mlp[l], lw["ff1"], lw["ff2"], final_g).reshape(bs, ts_pad, d)
        outs["s_rows"].append(rows[:, :ts].reshape(bs, ts, 4, NSA_KV, NSA_DH))
        win_new = win[:, :ts].reshape(bs, ts, 2, NSA_KV, NSA_DH)
        outs["s_win"].append(jnp.concatenate([cache_win_kv[l], win_new], axis=1)[:, ts:])
        outs["s_conv"].append(conv_n)
        outs["s_qkv"].append(qkv_tail)
        outs["s_gdn"].append(s_n)
    st = lambda k: jnp.stack(outs[k], axis=0)
    return (xp, xs[:, :ts], st("p_rows"), st("p_win"), st("p_conv"), st("p_qkv"), st("p_gdn"), st("p_mem"),
            st("s_rows"), st("s_win"), st("s_conv"), st("s_qkv"), st("s_gdn"))
```

```python
import functools
import math

import jax
import jax.numpy as jnp
import numpy as np
from jax import lax
from jax.experimental import pallas as pl
from jax.experimental.pallas import tpu as pltpu

F32 = jnp.float32
BF16 = jnp.bfloat16

D_MODEL = 1024
CONV_CH = 512
CONV_W = 31
GDN_HEADS = 4
GDN_D = 128
GDN_CHUNK = 64
GDN_QKV = 3 * GDN_HEADS * GDN_D
NSA_HEADS = 8
NSA_KV = 2
NSA_GQ = 4
NSA_DH = 64
L_CMP = 32
L_SEL = 64
N_SEL = 16
WINDOW = 512
Q_BLOCK = 128
FORCE_BONUS = 1e4
PAGE = 128
N_MEM = 256
X_HEADS = 4
X_DH = 256
D_FF = 4096
N_BUCKETS = 32
EPS = 1e-6
NEG = -1e30

LANES = 128
HALO = 32
VMEM_LIMIT = 48 * 1024 * 1024
BIG_VMEM_LIMIT = 56 * 1024 * 1024


def _bf(x):
    return x.astype(BF16)


def _dot(a, b):
    return jnp.dot(a, b, preferred_element_type=F32)


def _dot_nt(a, b):
    return lax.dot_general(a, b, (((1,), (1,)), ((), ())), preferred_element_type=F32)


def _sigmoid(x):
    return 0.5 * jnp.tanh(0.5 * x) + 0.5


def _silu(x):
    return x * _sigmoid(x)


def _params(*sem):
    return pltpu.CompilerParams(dimension_semantics=sem, vmem_limit_bytes=VMEM_LIMIT)


def _mm_kernel(a_ref, w_ref, o_ref):
    o_ref[...] = _dot(a_ref[...], w_ref[...])


def _col_tile(n):
    for tn in (1024, 768, 512, 384, 256, 128):
        if n % tn == 0:
            return tn
    raise ValueError(n)


def _matmul(a, w):
    m, k = a.shape
    n = w.shape[1]
    tm = min(m, 1024)
    tn = _col_tile(n)
    return pl.pallas_call(
        _mm_kernel,
        grid=(m // tm, n // tn),
        in_specs=[pl.BlockSpec((tm, k), lambda i, j: (i, 0)), pl.BlockSpec((k, tn), lambda i, j: (0, j))],
        out_specs=pl.BlockSpec((tm, tn), lambda i, j: (i, j)),
        out_shape=jax.ShapeDtypeStruct((m, n), F32),
        compiler_params=_params("parallel", "parallel"),
        name="matmul",
    )(a, w)


U_A = 0
U_Q = 1024
U_ROWS = 1536
U_QKV = 2048
U_Z = 3584
U_WIN = 4096
U_AB = 4352
U_GL = 4480
U_N = 4608
PROJ_TN = 1536


def _proj_kernel(x_ref, g_ref, w_ref, o_ref, rows_ref, win_ref, h_s):
    j = pl.program_id(1)

    @pl.when(j == 0)
    def _():
        x = x_ref[...]
        h_s[...] = _bf(x * lax.rsqrt(jnp.mean(x * x, axis=-1, keepdims=True) + EPS) * g_ref[...])

    acc = _dot(h_s[...], w_ref[...])
    o_ref[...] = acc

    @pl.when(j == U_ROWS // PROJ_TN)
    def _():
        rows_ref[...] = acc[:, U_ROWS % PROJ_TN:U_ROWS % PROJ_TN + 4 * LANES]

    @pl.when(j == U_WIN // PROJ_TN)
    def _():
        win_ref[...] = acc[:, U_WIN % PROJ_TN:U_WIN % PROJ_TN + 2 * LANES]


def _proj_in(x, g, w):
    m, d = x.shape
    n = w.shape[1]
    tm = min(m, 1024)
    assert U_ROWS % PROJ_TN + 4 * LANES <= PROJ_TN and U_WIN % PROJ_TN + 2 * LANES <= PROJ_TN
    return pl.pallas_call(
        _proj_kernel,
        grid=(m // tm, n // PROJ_TN),
        in_specs=[pl.BlockSpec((tm, d), lambda i, j: (i, 0)), pl.BlockSpec((1, d), lambda i, j: (0, 0)),
                  pl.BlockSpec((d, PROJ_TN), lambda i, j: (0, j))],
        out_specs=[pl.BlockSpec((tm, PROJ_TN), lambda i, j: (i, j)),
                   pl.BlockSpec((tm, 4 * LANES), lambda i, j: (i, 0)),
                   pl.BlockSpec((tm, 2 * LANES), lambda i, j: (i, 0))],
        out_shape=[jax.ShapeDtypeStruct((m, n), F32), jax.ShapeDtypeStruct((m, 4 * LANES), F32),
                   jax.ShapeDtypeStruct((m, 2 * LANES), F32)],
        scratch_shapes=[pltpu.VMEM((tm, d), BF16)],
        compiler_params=_params("parallel", "arbitrary"),
        name="proj_in",
    )(x, g.reshape(1, d), w)


def _split3(x):
    hi = _bf(x)
    r = x - hi.astype(F32)
    mid = _bf(r)
    return hi, mid, _bf(r - mid.astype(F32))


def _pick_columns(x, onehot):
    hi, mid, lo = _split3(x)
    return (_dot(lo, onehot) + _dot(mid, onehot)) + _dot(hi, onehot)


def _conf_kernel(n_t, tt, tv, u_ref, halo_ref, st_ref, w_ref, b_ref, g_ref, lb_ref, o_ref, nb_ref, xc_ref, zs_ref):
    t = pl.program_id(1)
    u = u_ref[0]
    xc_ref[HALO:HALO + tt, :] = u[:, :CONV_CH] * _sigmoid(u[:, CONV_CH:])
    if n_t > 1:
        uh = halo_ref[0]
        gh = uh[:, :CONV_CH] * _sigmoid(uh[:, CONV_CH:])
        xc_ref[0:HALO, :] = jnp.where(t > 0, gh, st_ref[0])
    else:
        xc_ref[0:HALO, :] = st_ref[0]
    off = HALO - (CONV_W - 1)
    span = tt + HALO - 8
    for r in range(1, 8):
        zs_ref[r - 1] = xc_ref[r:r + span, :]
    acc = None
    for i in range(CONV_W):
        pos = off + i
        r, base = pos % 8, pos - pos % 8
        src = xc_ref[base:base + tt, :] if r == 0 else zs_ref[r - 1, base:base + tt, :]
        term = src * w_ref[i:i + 1, :]
        acc = term if acc is None else acc + term
    y = acc + b_ref[...]
    mu = jnp.mean(y, axis=-1, keepdims=True)
    yc = y - mu
    var = jnp.mean(yc * yc, axis=-1, keepdims=True)
    ln = yc * lax.rsqrt(var + EPS) * g_ref[...] + lb_ref[...]
    o_ref[0] = _silu(ln)

    @pl.when(t == n_t - 1)
    def _():
        nb_ref[0] = xc_ref[tv:tv + HALO, :]


def _conformer(u_a, state_pad, w_dw, b_dw, ln_g, ln_b, n_valid_last):
    bsz, t_len, _ = u_a.shape
    tt = min(t_len, 256)
    n_t = t_len // tt
    hb = tt // HALO if n_t > 1 else 1
    halo_rows = HALO if n_t > 1 else tt
    w_pad = jnp.pad(w_dw, ((0, HALO - CONV_W), (0, 0)))
    row = lambda v: v.reshape(1, CONV_CH)
    kern = functools.partial(_conf_kernel, n_t, tt, n_valid_last)
    return pl.pallas_call(
        kern,
        grid=(bsz, n_t),
        in_specs=[
            pl.BlockSpec((1, tt, 2 * CONV_CH), lambda b, t: (b, t, 0)),
            pl.BlockSpec((1, halo_rows, 2 * CONV_CH), lambda b, t: (b, jnp.maximum(t * hb - 1, 0), 0)),
            pl.BlockSpec((1, HALO, CONV_CH), lambda b, t: (b, 0, 0)),
            pl.BlockSpec((HALO, CONV_CH), lambda b, t: (0, 0)),
            pl.BlockSpec((1, CONV_CH), lambda b, t: (0, 0)),
            pl.BlockSpec((1, CONV_CH), lambda b, t: (0, 0)),
            pl.BlockSpec((1, CONV_CH), lambda b, t: (0, 0)),
        ],
        out_specs=[
            pl.BlockSpec((1, tt, CONV_CH), lambda b, t: (b, t, 0)),
            pl.BlockSpec((1, HALO, CONV_CH), lambda b, t: (b, 0, 0)),
        ],
        out_shape=[
            jax.ShapeDtypeStruct((bsz, t_len, CONV_CH), F32),
            jax.ShapeDtypeStruct((bsz, HALO, CONV_CH), F32),
        ],
        scratch_shapes=[pltpu.VMEM((HALO + tt, CONV_CH), F32), pltpu.VMEM((7, tt + HALO - 8, CONV_CH), F32)],
        compiler_params=_params("parallel", "arbitrary"),
        name="conformer_conv",
    )(u_a, u_a, state_pad, w_pad, row(b_dw), row(ln_g), row(ln_b))


def _tri_inverse(a_list, ii, jj, merge_shifts):
    mm = lambda p, q: _dot(_bf(p), _bf(q))
    a0 = [jnp.where((ii >> 3) == (jj >> 3), a, 0.0) for a in a_list]
    a2 = [mm(p, p) for p in a0]
    a4 = [mm(p, p) for p in a2]
    r = [(q - p) - mm(p, q) for p, q in zip(a0, a2)]
    r = [(p + q) + mm(p, q) for p, q in zip(r, a4)]
    for sh in merge_shifts:
        mask = ((ii >> (sh + 1)) == (jj >> (sh + 1))) & ((ii >> sh) != (jj >> sh))
        off = [jnp.where(mask, a, 0.0) for a in a_list]
        t = [o + mm(o, p) for o, p in zip(off, r)]
        r = [p - (q + mm(p, q)) for p, q in zip(r, t)]
    return r


def _softplus(x):
    return jnp.maximum(x, 0.0) + jnp.log1p(jnp.exp(-jnp.abs(x)))


def _gdn_kernel(t_len, n_valid, hp, alog_ref, dtb_ref, q_ref, k_ref, v_ref, z_ref, ab_ref,
                sq_ref, sk_ref, sv_ref, wq_ref, wk_ref, wv_ref, s0_ref, ng_ref,
                o_ref, sn_ref,
                xp_s, qn_s, kn_s, vn_s, g_s, be_s, vw_s, kcd_s, qg_s, kdt_s, qk_s, ge_s):
    h0 = pl.program_id(1) * hp
    n_chunks = t_len // GDN_CHUNK
    c_len = GDN_CHUNK

    heads = range(hp)
    hcols = lambda hh: slice(hh * GDN_D, (hh + 1) * GDN_D)

    def conv(x_ref, st_ref, w_ref, hh):
        xp_s[0:8, :] = st_ref[0, :, hcols(hh)]
        xp_s[8:8 + t_len, :] = x_ref[0, :, hcols(hh)]
        acc = xp_s[5:5 + t_len, :] * w_ref[0:1, hcols(hh)]
        for i in range(1, 4):
            acc = acc + xp_s[5 + i:5 + i + t_len, :] * w_ref[i:i + 1, hcols(hh)]
        return _silu(acc)

    col = lax.broadcasted_iota(jnp.int32, (LANES, LANES), 0)
    ab = ab_ref[0]
    for hh in heads:
        h = h0 + hh
        qc = conv(q_ref, sq_ref, wq_ref, hh)
        qn_s[hh] = qc * lax.rsqrt(jnp.sum(qc * qc, axis=-1, keepdims=True) + EPS) * (GDN_D ** -0.5)
        kc = conv(k_ref, sk_ref, wk_ref, hh)
        kn_s[hh] = kc * lax.rsqrt(jnp.sum(kc * kc, axis=-1, keepdims=True) + EPS)
        vn_s[hh] = conv(v_ref, sv_ref, wv_ref, hh)
        a_rep = _pick_columns(ab, _bf(jnp.where(col == h, 1.0, 0.0)))
        b_rep = _pick_columns(ab, _bf(jnp.where(col == GDN_HEADS + h, 1.0, 0.0)))
        a_exp = jnp.exp(jnp.full((1, LANES), alog_ref[h], F32))
        g = -a_exp * _softplus(a_rep + dtb_ref[h])
        beta = _sigmoid(b_rep)
        if n_valid < t_len:
            live = lax.broadcasted_iota(jnp.int32, (t_len, LANES), 0) < n_valid
            g = jnp.where(live, g, 0.0)
            beta = jnp.where(live, beta, 0.0)
        g_s[hh] = g
        be_s[hh] = beta

    ii = lax.broadcasted_iota(jnp.int32, (c_len, c_len), 0)
    jj = lax.broadcasted_iota(jnp.int32, (c_len, c_len), 1)
    incl = ii >= jj
    strict = ii > jj
    ltri = _bf(incl.astype(F32))
    unroll = max(u for u in (1, 2, 4, 8) if n_chunks % u == 0 and u * hp <= 16)

    def cumdecay(g_c):
        g_hi, g_mid, g_lo = _split3(g_c)
        return (_dot(ltri, g_lo) + _dot(ltri, g_mid)) + _dot(ltri, g_hi)

    def prep(cu, carry):
        pairs = [(hh, cu * unroll + u) for hh in heads for u in range(unroll)]
        sls = [pl.ds(pl.multiple_of(c * c_len, c_len), c_len) for _, c in pairs]
        each = lambda f, *ls: [f(*a) for a in zip(*ls)]
        q_l = [qn_s[hh, sl, :] for (hh, _), sl in zip(pairs, sls)]
        k_l = [kn_s[hh, sl, :] for (hh, _), sl in zip(pairs, sls)]
        b_l = [be_s[hh, sl, :] for (hh, _), sl in zip(pairs, sls)]
        gc_l = [cumdecay(g_s[hh, sl, :]) for (hh, _), sl in zip(pairs, sls)]
        dec_l = each(lambda gc: jnp.where(
            incl, jnp.exp(jnp.minimum(gc[:, 0:c_len] - gc.T[0:c_len, :], 0.0)), 0.0), gc_l)
        kb_l = each(lambda k, b: k * b, k_l, b_l)
        a_l = each(lambda kb, k, dec: jnp.where(strict, _dot_nt(_bf(kb), _bf(k)) * dec, 0.0), kb_l, k_l, dec_l)
        r_l = _tri_inverse(a_l, ii, jj, () if n_valid <= 8 else (3, 4, 5))
        eg_l = each(jnp.exp, gc_l)
        rhs_l = [jnp.concatenate([vn_s[hh, sl, :] * b, kb * eg], axis=1)
                 for (hh, _), sl, b, kb, eg in zip(pairs, sls, b_l, kb_l, eg_l)]
        sol_l = each(lambda r, rhs: rhs + _dot(_bf(r), _bf(rhs)), r_l, rhs_l)
        qk_l = each(lambda q, k, dec: jnp.where(incl, _dot_nt(_bf(q), _bf(k)) * dec, 0.0), q_l, k_l, dec_l)
        for (hh, c), sl, sol, qk, q, k, gc, eg in zip(pairs, sls, sol_l, qk_l, q_l, k_l, gc_l, eg_l):
            g_end = gc[c_len - 1:c_len, :]
            vw_s[hh, sl, :] = sol[:, :GDN_D]
            kcd_s[hh, sl, :] = _bf(sol[:, GDN_D:])
            qk_s[hh, c] = _bf(qk)
            qg_s[hh, sl, :] = _bf(q * eg)
            kdt_s[hh, c] = _bf((k * jnp.exp(g_end - gc)).T)
            ge_s[hh, c] = jnp.broadcast_to(jnp.exp(g_end), (8, LANES))
        return carry

    lax.fori_loop(0, n_chunks // unroll, prep, 0)

    def step(c, states):
        sl = pl.ds(pl.multiple_of(c * c_len, c_len), c_len)
        sb = [_bf(s) for s in states]
        v_new = [vw_s[hh, sl, :] - _dot(kcd_s[hh, sl, :], sb[hh]) for hh in heads]
        vb = [_bf(v) for v in v_new]
        for hh in heads:
            o_ref[0, sl, hcols(hh)] = _dot(qg_s[hh, sl, :], sb[hh]) + _dot(qk_s[hh, c], vb[hh])
        return tuple(states[hh] * ge_s[hh, c][0:1, :] + _dot(kdt_s[hh, c], vb[hh]) for hh in heads)

    s_fin = lax.fori_loop(0, n_chunks, step, tuple(s0_ref[0, hh] for hh in heads))
    for hh in heads:
        sn_ref[0, hh] = s_fin[hh]
        o = o_ref[0, :, hcols(hh)]
        y = o * lax.rsqrt(jnp.mean(o * o, axis=-1, keepdims=True) + EPS) * ng_ref[...]
        o_ref[0, :, hcols(hh)] = y * _silu(z_ref[0, :, hcols(hh)])


def _gated_deltanet(u, qkv_blk, z_blk, ab_blk, state_pad, s0, w_conv_pad, a_log, dt_bias, norm_g, n_valid):
    bsz, t_len, _ = u.shape
    nh = GDN_HEADS
    hp = nh if t_len <= 4 * GDN_CHUNK else 2
    wide = hp * GDN_D
    assert (qkv_blk * GDN_D) % wide == 0 and (z_blk * GDN_D) % wide == 0
    ublk = lambda blk: pl.BlockSpec((1, t_len, wide), lambda b, j, o=blk * GDN_D // wide: (b, 0, o + j))
    stb = lambda off: pl.BlockSpec((1, 8, wide), lambda b, j, o=off * GDN_D // wide: (b, 0, o + j))
    wb = lambda off: pl.BlockSpec((8, wide), lambda b, j, o=off * GDN_D // wide: (0, o + j))
    smem = pl.BlockSpec(memory_space=pltpu.SMEM)
    n_chunks = t_len // GDN_CHUNK
    seq = lambda dt: pltpu.VMEM((hp, t_len, GDN_D), dt)
    kern = functools.partial(_gdn_kernel, t_len, n_valid, hp)
    return pl.pallas_call(
        kern,
        grid=(bsz, nh // hp),
        in_specs=[smem, smem, ublk(qkv_blk), ublk(qkv_blk + nh), ublk(qkv_blk + 2 * nh), ublk(z_blk),
                  pl.BlockSpec((1, t_len, GDN_D), lambda b, j: (b, 0, ab_blk)),
                  stb(0), stb(nh), stb(2 * nh), wb(0), wb(nh), wb(2 * nh),
                  pl.BlockSpec((1, hp, GDN_D, GDN_D), lambda b, j: (b, j, 0, 0)),
                  pl.BlockSpec((1, GDN_D), lambda b, j: (0, 0))],
        out_specs=[pl.BlockSpec((1, t_len, wide), lambda b, j: (b, 0, j)),
                   pl.BlockSpec((1, hp, GDN_D, GDN_D), lambda b, j: (b, j, 0, 0))],
        out_shape=[jax.ShapeDtypeStruct((bsz, t_len, nh * GDN_D), F32),
                   jax.ShapeDtypeStruct((bsz, nh, GDN_D, GDN_D), F32)],
        scratch_shapes=[pltpu.VMEM((8 + t_len, GDN_D), F32), seq(F32), seq(F32), seq(F32), seq(F32), seq(F32),
                        seq(F32), seq(BF16), seq(BF16),
                        pltpu.VMEM((hp, n_chunks, GDN_D, GDN_CHUNK), BF16),
                        pltpu.VMEM((hp, n_chunks, GDN_CHUNK, GDN_CHUNK), BF16),
                        pltpu.VMEM((hp, n_chunks, 8, LANES), F32)],
        compiler_params=pltpu.CompilerParams(dimension_semantics=("parallel", "parallel"),
                                             vmem_limit_bytes=BIG_VMEM_LIMIT),
        name="gated_deltanet",
    )(a_log, dt_bias, u, u, u, u, u, state_pad, state_pad, state_pad,
      w_conv_pad, w_conv_pad, w_conv_pad, s0, norm_g.reshape(1, GDN_D))


def _heads_to_rows(x, g, nt):
    lane = lax.broadcasted_iota(jnp.int32, (nt, LANES), 1)
    keep = (lane >= NSA_DH * g) & (lane < NSA_DH * (g + 1))
    parts = []
    for r in range(NSA_GQ):
        hh = NSA_GQ * g + r
        blk = x[:, (hh // 2) * LANES:(hh // 2 + 1) * LANES]
        if hh % 2 != g:
            blk = pltpu.roll(blk, NSA_DH, axis=1)
        parts.append(jnp.where(keep, blk, 0.0))
    return jnp.concatenate(parts, axis=0)


def _rows_to_heads(y, g, nt):
    outs = []
    for m in range(2):
        x0 = y[(2 * m) * nt:(2 * m + 1) * nt]
        x1 = y[(2 * m + 1) * nt:(2 * m + 2) * nt]
        if g == 1:
            x0 = pltpu.roll(x0, NSA_DH, axis=1)
        else:
            x1 = pltpu.roll(x1, NSA_DH, axis=1)
        outs.append(x0 + x1)
    return outs


def _masked_softmax_parts(parts, masks, axis):
    sm = [jnp.where(m, s, NEG) for s, m in zip(parts, masks)]
    mx = functools.reduce(jnp.maximum, [jnp.max(s, axis=axis, keepdims=True) for s in sm])
    es = [jnp.where(m, jnp.exp(s - mx), 0.0) for s, m in zip(sm, masks)]
    den = functools.reduce(lambda p, q: p + q, [jnp.sum(e, axis=axis, keepdims=True) for e in es])
    inv = 1.0 / jnp.maximum(den, 1e-30)
    return [e * inv for e in es]


def _bucket_np(rel):
    n = np.maximum(rel, 0)
    nf = np.maximum(n, 1).astype(np.float32)
    large = 16 + (np.log(nf / np.float32(16)) / np.float32(math.log(8.0)) * np.float32(16)).astype(np.int32)
    return np.where(n < 16, n, np.minimum(large, N_BUCKETS - 1)).astype(np.int32)


LOOKUP_TILE = 8192


def _lookup_kernel(idx_ref, tb_ref, o_ref):
    idx = idx_ref[...]
    acc = jnp.zeros(o_ref.shape, F32)
    for k in range(N_BUCKETS):
        acc = jnp.where(idx == k, tb_ref[:, k:k + 1], acc)
    o_ref[...] = acc


def _bias_lookup(rel_bias, idx_list):
    sizes = [int(np.prod(a.shape)) for a in idx_list]
    total = sum(sizes)
    padded = -(-total // LOOKUP_TILE) * LOOKUP_TILE
    flat = np.zeros((1, padded), np.int32)
    flat[0, :total] = np.concatenate([np.asarray(a, np.int32).reshape(-1) for a in idx_list])
    tab = pl.pallas_call(
        _lookup_kernel,
        grid=(padded // LOOKUP_TILE,),
        in_specs=[pl.BlockSpec((1, LOOKUP_TILE), lambda i: (0, i)),
                  pl.BlockSpec((NSA_HEADS, N_BUCKETS), lambda i: (0, 0))],
        out_specs=pl.BlockSpec((NSA_HEADS, LOOKUP_TILE), lambda i: (0, i)),
        out_shape=jax.ShapeDtypeStruct((NSA_HEADS, padded), F32),
        compiler_params=_params("parallel"),
        name="bias_lookup",
    )(jnp.asarray(flat), rel_bias.astype(F32).T)
    outs, off = [], 0
    for a, n in zip(idx_list, sizes):
        outs.append(tab[:, off:off + n].reshape((NSA_HEADS,) + tuple(a.shape)))
        off += n
    return outs


def _head_rows(tab):
    return tab.reshape(NSA_KV, NSA_GQ * tab.shape[1], tab.shape[2])


def _nsa_prompt_kernel(t_len, q_ref, kcmp_ref, vcmp_ref, kslc_ref, vslc_ref, kwin_ref, vwin_ref, gl_ref, rep_ref,
                       wk_ref, wv_ref, bc_ref, bct_ref, bnear_ref, bwin_ref, o_ref, kc_s, vc_s):
    i = pl.program_id(1)
    nsb = t_len // L_SEL
    qb = Q_BLOCK
    rows = NSA_GQ * qb

    @pl.when(i == 0)
    def _():
        n2 = 2 * lax.broadcasted_iota(jnp.int32, (nsb, t_len), 0)
        cb = lax.broadcasted_iota(jnp.int32, (nsb, t_len), 1) >> 5
        kc = _bf(kcmp_ref[0])
        vc = _bf(vcmp_ref[0])
        wk = wk_ref[...]
        wv = wv_ref[...]
        kc_s[0:nsb, :] = _dot(_bf(jnp.where(cb == n2, wk, 0.0)), kc)
        kc_s[nsb:2 * nsb, :] = _dot(_bf(jnp.where(cb == n2 + 1, wk, 0.0)), kc)
        vc_s[0:nsb, :] = _dot(_bf(jnp.where(cb == n2, wv, 0.0)), vc)
        vc_s[nsb:2 * nsb, :] = _dot(_bf(jnp.where(cb == n2 + 1, wv, 0.0)), vc)

    q_all = q_ref[0] * (NSA_DH ** -0.5)
    gl = gl_ref[0]
    gate_all = [_sigmoid(_pick_columns(gl, rep_ref[br])) for br in range(3)]
    t0 = i * qb
    tq = t0 + (lax.broadcasted_iota(jnp.int32, (rows, 1), 0) & (qb - 1))
    tl = t0 + (lax.broadcasted_iota(jnp.int32, (1, rows), 1) & (qb - 1))
    eye_q = _bf((lax.broadcasted_iota(jnp.int32, (qb, qb), 0) == lax.broadcasted_iota(jnp.int32, (qb, qb), 1)).astype(F32))
    far_end = jnp.maximum(t0 - qb, 0)
    n_far = (far_end + 511) >> 9
    kc = _bf(kc_s[...])
    vc = _bf(vc_s[...])

    def key_aug(k0, n_keys, limit):
        kpos = k0 + lax.broadcasted_iota(jnp.int32, (n_keys, LANES), 0)
        lane = lax.broadcasted_iota(jnp.int32, (n_keys, LANES), 1)
        hit = (lane == (kpos >> 6)) | ((lane == nsb) & ((kpos >= limit) | (kpos < 0)))
        return _bf(jnp.where(hit, NEG, 0.0))

    gs = range(NSA_KV)
    qg = [_bf(_heads_to_rows(q_all, g, qb)) for g in gs]

    n_prev = WINDOW // qb
    starts = [pl.multiple_of(jnp.maximum(t0 + (j - n_prev) * qb, 0), qb) for j in range(n_prev + 1)]
    kwin = [_bf(kwin_ref[0, pl.ds(st, qb), :]) for st in starts]
    pens = [jnp.where(i + (j - n_prev) >= 0, 0.0, NEG) for j in range(n_prev)] + [0.0]
    s_w = [jnp.concatenate([_dot_nt(qg[g], kwin[j]) + pens[j] for j in range(n_prev + 1)], axis=1) + bwin_ref[g]
           for g in gs]

    s_c = [_dot_nt(qg[g], kc) + bc_ref[0, g] for g in gs]
    s_t = [_dot_nt(kc, qg[g]) + bct_ref[0, g] for g in gs]
    e_c = [jnp.where(tq >= L_CMP - 1, jnp.exp(s - jnp.max(s, axis=1, keepdims=True)), 0.0) for s in s_c]
    p_c = [e * (1.0 / jnp.maximum(jnp.sum(e, axis=1, keepdims=True), 1e-30)) for e in e_c]
    o_c = [_dot(_bf(p), vc) for p in p_c]

    e_t = [jnp.where(tl >= L_CMP - 1, jnp.exp(s - jnp.max(s, axis=0, keepdims=True)), 0.0) for s in s_t]
    p_t = [e * (1.0 / jnp.maximum(jnp.sum(e, axis=0, keepdims=True), 1e-30)) for e in e_t]
    head_sum = lambda x: x[:, 0:qb] + x[:, qb:2 * qb] + x[:, 2 * qb:3 * qb] + x[:, 3 * qb:4 * qb]
    blk = lax.broadcasted_iota(jnp.int32, (nsb, qb), 0)
    cur = (t0 + lax.broadcasted_iota(jnp.int32, (nsb, qb), 1)) >> 6
    bonus = jnp.where((blk == 0) | (blk == cur) | (blk == cur - 1), FORCE_BONUS, 0.0)
    score = [jnp.where(blk <= cur, (head_sum(p[0:nsb]) + head_sum(p[nsb:2 * nsb])) + bonus, -1.0) for p in p_t]
    rank = [jnp.zeros((nsb, qb), F32) for _ in gs]
    for j in range(nsb):
        for g in gs:
            sj = score[g][j:j + 1, :]
            ahead = (sj > score[g]) | ((sj == score[g]) & (blk > j))
            rank[g] = rank[g] + jnp.where(ahead, 1.0, 0.0)
    pen_rows = jnp.where(lax.broadcasted_iota(jnp.int32, (LANES - nsb, qb), 0) == 0, 1.0, 0.0)
    not_sel_t = [jnp.where(r < float(min(N_SEL, nsb)), 0.0, 1.0) for r in rank]
    q_aug = [_bf(_dot_nt(eye_q, _bf(jnp.concatenate([ns, pen_rows], axis=0)))) for ns in not_sel_t]
    qa = [jnp.concatenate([qg[g], jnp.concatenate([q_aug[g]] * NSA_GQ, axis=0)], axis=1) for g in gs]

    def online(carry, s, pv):
        m_i, l_i, acc = carry
        m_n = jnp.maximum(m_i, jnp.max(s, axis=1, keepdims=True))
        p = jnp.exp(s - m_n)
        alpha = jnp.exp(m_i - m_n)
        return m_n, alpha * l_i + jnp.sum(p, axis=1, keepdims=True), alpha * acc + pv(_bf(p))

    def far_tile(kt, carry):
        k0 = pl.multiple_of(kt * 512, 512)
        ka = jnp.concatenate([_bf(kslc_ref[0, pl.ds(k0, 512), :]), key_aug(k0, 512, far_end)], axis=1)
        vt = _bf(vslc_ref[0, pl.ds(k0, 512), :])
        s = [_dot_nt(qa[g], ka) for g in gs]
        return tuple(online(carry[g], s[g], lambda p: _dot(p, vt)) for g in gs)

    init = (jnp.full((rows, 1), NEG, F32), jnp.zeros((rows, 1), F32), jnp.zeros((rows, LANES), F32))
    far = lax.fori_loop(0, n_far, far_tile, tuple(init for _ in gs))

    p0 = pl.multiple_of(jnp.maximum(t0 - qb, 0), qb)
    d0 = pl.multiple_of(t0, qb)
    ka = jnp.concatenate([
        jnp.concatenate([_bf(kslc_ref[0, pl.ds(p0, qb), :]), _bf(kslc_ref[0, pl.ds(d0, qb), :])], axis=0),
        key_aug(t0 - qb, 2 * qb, t_len)], axis=1)
    vp = _bf(vslc_ref[0, pl.ds(p0, qb), :])
    vd = _bf(vslc_ref[0, pl.ds(d0, qb), :])
    s_near = [_dot_nt(qa[g], ka) + bnear_ref[g] for g in gs]
    fin = [online(far[g], s_near[g], lambda p: _dot(p[:, 0:qb], vp) + _dot(p[:, qb:2 * qb], vd)) for g in gs]
    o_s = [acc * (1.0 / l_n) for _, l_n, acc in fin]

    vwin = [_bf(vwin_ref[0, pl.ds(st, qb), :]) for st in starts]
    e_w = [jnp.exp(s - jnp.max(s, axis=1, keepdims=True)) for s in s_w]
    o_w = []
    for g in gs:
        ew = _bf(e_w[g])
        acc = _dot(ew[:, 0:qb], vwin[0])
        for j in range(1, n_prev + 1):
            acc = acc + _dot(ew[:, j * qb:(j + 1) * qb], vwin[j])
        o_w.append(acc * (1.0 / jnp.sum(e_w[g], axis=1, keepdims=True)))

    for g in gs:
        gates = [_heads_to_rows(gate_all[br], g, qb) for br in range(3)]
        comb = gates[0] * o_c[g] + gates[1] * o_s[g] + gates[2] * o_w[g]
        blocks = _rows_to_heads(comb, g, qb)
        o_ref[0, :, (2 * g) * LANES:(2 * g + 1) * LANES] = blocks[0]
        o_ref[0, :, (2 * g + 1) * LANES:(2 * g + 2) * LANES] = blocks[1]


def _nsa_tables(rel_bias, t_len, n_pages, nt):
    nqb = t_len // Q_BLOCK
    nsb = t_len // L_SEL
    past = n_pages * PAGE
    t = np.arange(Q_BLOCK)
    tq = (np.arange(nqb)[:, None] * Q_BLOCK + t[None, :])[:, :, None]
    n = np.arange(nsb)[None, None, :]
    ts = np.arange(nt)[:, None]
    j = np.arange(2 * n_pages)[None, :]
    c = np.arange(PAGE)[None, :]
    idx = [
        _bucket_np(tq - (n * L_SEL + L_CMP - 1)),
        _bucket_np(tq - (n * L_SEL + L_SEL - 1)),
        _bucket_np(Q_BLOCK + t[:, None] - np.arange(2 * Q_BLOCK)[None, :]),
        _bucket_np(WINDOW + t[:, None] - np.arange(WINDOW + Q_BLOCK)[None, :]),
        _bucket_np(past + ts - (j * L_SEL + L_CMP - 1)),
        _bucket_np(past + ts - (j * L_SEL + L_SEL - 1)),
        _bucket_np(PAGE + ts - c),
        _bucket_np(ts - c),
        _bucket_np(WINDOW + ts - np.arange(WINDOW)[None, :]),
    ]
    ce, co, near, win, sce, sco, slast, snew, swin = _bias_lookup(rel_bias, idx)
    b31 = rel_bias.astype(F32)[N_BUCKETS - 1].reshape(NSA_KV, NSA_GQ, 1, 1)
    shift = lambda tab: (tab.reshape(NSA_KV, NSA_GQ, tab.shape[1], tab.shape[2]) - b31).reshape(
        NSA_KV, NSA_GQ * tab.shape[1], tab.shape[2])
    vis = lambda m: jnp.asarray(np.tile(m, (1,) * (m.ndim - 2) + (NSA_GQ, 1)))
    blocked = lambda tab: jnp.swapaxes(tab, 0, 1).reshape(nqb, NSA_KV, NSA_GQ * Q_BLOCK, nsb)
    bc = jnp.concatenate([blocked(ce), blocked(co)], axis=-1)
    vis_c = np.concatenate([n * L_SEL + L_CMP - 1 <= tq, n * L_SEL + L_SEL - 1 <= tq], axis=-1)
    bc = jnp.where(vis(vis_c)[:, None], bc, NEG)
    c_near = np.arange(2 * Q_BLOCK)[None, :]
    near_m = jnp.where(vis(c_near <= Q_BLOCK + t[:, None])[None], shift(near), NEG)
    c_win = np.arange(WINDOW + Q_BLOCK)[None, :]
    win_m = jnp.where(vis((c_win > t[:, None]) & (c_win <= WINDOW + t[:, None]))[None], _head_rows(win), NEG)
    ptab = (bc, jnp.swapaxes(bc, -1, -2), near_m, win_m)
    rows64 = lambda tab: tab.reshape(NSA_KV * NSA_GQ * nt, tab.shape[-1])
    stab = (rows64(_head_rows(sce)), rows64(_head_rows(sco)), rows64(shift(slast)), rows64(shift(snew)),
            rows64(_head_rows(swin)), rows64(_head_rows(snew)))
    return ptab, stab


def _gate_rep():
    j = np.arange(LANES)[None, :, None]
    c = np.arange(NSA_HEADS * NSA_DH)[None, None, :]
    br = np.arange(3)[:, None, None]
    return jnp.asarray(j == br * NSA_HEADS + c // NSA_DH, BF16)


def _nsa_prompt(u, w_pos, tables):
    bsz, t_len, _ = u.shape
    nqb = t_len // Q_BLOCK
    nsb = t_len // L_SEL
    assert nsb < LANES
    bc, bct, near, wtab = tables
    wk = jnp.tile(w_pos[0], t_len // L_CMP).reshape(1, t_len)
    wv = jnp.tile(w_pos[1], t_len // L_CMP).reshape(1, t_len)
    rep = _gate_rep()
    seq = lambda c: pl.BlockSpec((1, t_len, LANES), lambda b, i, c=c: (b, 0, c))
    full = lambda a: pl.BlockSpec(a.shape, lambda b, i, nd=a.ndim: (0,) * nd)
    per_i = lambda a: pl.BlockSpec((1,) + a.shape[1:], lambda b, i, nd=a.ndim: (i,) + (0,) * (nd - 1))
    kern = functools.partial(_nsa_prompt_kernel, t_len)
    kv0 = U_ROWS // LANES
    w0 = U_WIN // LANES
    return pl.pallas_call(
        kern,
        grid=(bsz, nqb),
        in_specs=[pl.BlockSpec((1, Q_BLOCK, 512), lambda b, i: (b, i, U_Q // 512)),
                  seq(kv0), seq(kv0 + 1), seq(kv0 + 2), seq(kv0 + 3), seq(w0), seq(w0 + 1),
                  pl.BlockSpec((1, Q_BLOCK, LANES), lambda b, i: (b, i, U_GL // LANES)), full(rep),
                  full(wk), full(wv), per_i(bc), per_i(bct), full(near), full(wtab)],
        out_specs=pl.BlockSpec((1, Q_BLOCK, 512), lambda b, i: (b, i, 0)),
        out_shape=jax.ShapeDtypeStruct((bsz, t_len, 512), F32),
        scratch_shapes=[pltpu.VMEM((2 * nsb, LANES), F32)] * 2,
        compiler_params=_params("parallel", "arbitrary"),
        name="nsa_prompt",
    )(u, u, u, u, u, u, u, u, rep, wk, wv, bc, bct, near, wtab)


def _nsa_sample_kernel(layer, n_pages, n_new, pt_ref, cache_ref, q_ref, rows_ref, wnew_ref, wbuf_ref, gl_ref,
                       rep_ref, wpool_ref, bce_ref, bco_ref, blast_ref, bnew_ref, bwin_ref, bwnew_ref,
                       o_ref, cmp_s, slc_s, pool_s, exp_s, pad_s, sem):
    b = pl.program_id(0)
    nb = pl.num_programs(0)
    nt = 8
    past = n_pages * PAGE
    nblk = 2 * n_pages
    rows = NSA_KV * NSA_GQ * nt
    half = 2 * LANES

    def page_copy(seq, p, part, buf, s):
        return pltpu.make_async_copy(
            cache_ref.at[layer, pt_ref[seq, p], pl.ds(part * half, half), :],
            buf.at[:, pl.ds(pl.multiple_of(p * PAGE, PAGE), PAGE)], s)

    def start_gather(seq, part, buf, s):
        def body(p, c):
            page_copy(seq, p, part, buf, s).start()
            return c
        lax.fori_loop(0, n_pages, body, 0)

    def wait_gather(seq, part, buf, s):
        def body(p, c):
            page_copy(seq, p, part, buf, s).wait()
            return c
        lax.fori_loop(0, n_pages, body, 0)

    @pl.when(b == 0)
    def _():
        start_gather(0, 0, cmp_s, sem.at[0])
        start_gather(0, 1, slc_s.at[0], sem.at[1])
        cb = lax.broadcasted_iota(jnp.int32, (past, nblk), 0) >> 5
        j2 = 2 * lax.broadcasted_iota(jnp.int32, (past, nblk), 1)
        pool_s[0] = _bf(jnp.where(cb == j2, 1.0, 0.0))
        pool_s[1] = _bf(jnp.where(cb == j2 + 1, 1.0, 0.0))
        ej = lax.broadcasted_iota(jnp.int32, (nblk, past), 0)
        ec = lax.broadcasted_iota(jnp.int32, (nblk, past), 1) >> 6
        exp_s[...] = _bf(jnp.where(ej == ec, 1.0, 0.0))
        pad_s[...] = jnp.zeros(pad_s.shape, F32)

    slot = b % 2
    slc = slc_s.at[slot]

    @pl.when(b + 1 < nb)
    def _():
        start_gather(b + 1, 1, slc_s.at[1 - slot], sem.at[2 - slot])

    wait_gather(b, 0, cmp_s, sem.at[0])
    wait_gather(b, 1, slc, sem.at[1 + slot])

    q_all = q_ref[0] * (NSA_DH ** -0.5)
    qq = _bf(jnp.concatenate([_heads_to_rows(q_all, g, nt) for g in range(NSA_KV)], axis=0))
    tr = lax.broadcasted_iota(jnp.int32, (rows, 1), 0) & (nt - 1)

    wp = wpool_ref[...]
    ks = _bf(cmp_s[0:LANES, :] * wp[0:1, :])
    vs = _bf(cmp_s[LANES:half, :] * wp[1:2, :])
    kce = _bf(_dot(ks, pool_s[0]))
    kco = _bf(_dot(ks, pool_s[1]))
    vce = _bf(_dot(vs, pool_s[0]))
    vco = _bf(_dot(vs, pool_s[1]))

    @pl.when(b + 1 < nb)
    def _():
        start_gather(b + 1, 0, cmp_s, sem.at[0])

    new = rows_ref[0]
    wnew = wnew_ref[0]
    pad_s[0, 0:nt, :] = new[:, 2 * LANES:3 * LANES]
    pad_s[1, 0:nt, :] = new[:, 3 * LANES:4 * LANES]
    pad_s[2, 0:nt, :] = wnew[:, 0:LANES]
    pad_s[3, 0:nt, :] = wnew[:, LANES:2 * LANES]

    wb = wbuf_ref[0, 0]
    s_all = _dot(qq, _bf(slc[0:LANES, :]))
    s_new = _dot_nt(qq, _bf(pad_s[0])) + bnew_ref[...]
    s_win = _dot(qq, _bf(wb[0:LANES, :])) + bwin_ref[...]
    s_wnew = _dot_nt(qq, _bf(pad_s[2])) + bwnew_ref[...]

    se = _dot(qq, kce) + bce_ref[...]
    so = _dot(qq, kco) + bco_ref[...]
    mx = jnp.maximum(jnp.max(se, axis=1, keepdims=True), jnp.max(so, axis=1, keepdims=True))
    ee = jnp.exp(se - mx)
    eo = jnp.exp(so - mx)
    inv = 1.0 / (jnp.sum(ee, axis=1, keepdims=True) + jnp.sum(eo, axis=1, keepdims=True))
    pe = ee * inv
    po = eo * inv
    o_c = _dot_nt(_bf(pe), vce) + _dot_nt(_bf(po), vco)

    def head_sum(pr):
        return jnp.concatenate(
            [pr[g * 4 * nt:g * 4 * nt + nt] + pr[g * 4 * nt + nt:g * 4 * nt + 2 * nt]
             + pr[g * 4 * nt + 2 * nt:g * 4 * nt + 3 * nt] + pr[g * 4 * nt + 3 * nt:g * 4 * nt + 4 * nt]
             for g in range(NSA_KV)], axis=0)

    jcol = lax.broadcasted_iota(jnp.int32, (NSA_KV * nt, nblk), 1)
    forced = (jcol == 0) | (jcol == nblk - 1)
    score = (head_sum(pe) + head_sum(po)) + jnp.where(forced, FORCE_BONUS, 0.0)
    rank = jnp.where(FORCE_BONUS > score, 1.0, 0.0)
    for j in range(nblk):
        sj = score[:, j:j + 1]
        ahead = (sj > score) | ((sj == score) & (jcol > j))
        rank = rank + jnp.where(ahead, 1.0, 0.0)
    sel = jnp.where(rank < float(N_SEL), 1.0, 0.0)
    sel_rows = jnp.concatenate([sel[g * nt:(g + 1) * nt] for g in range(NSA_KV) for _ in range(NSA_GQ)], axis=0)

    tc = lax.broadcasted_iota(jnp.int32, (rows, LANES), 1)
    mnew = (tc <= tr) & (tc < n_new)

    mk = _dot(_bf(sel_rows), exp_s[...]) > 0.5
    far = past - PAGE
    p_far, p_last, p_new = _masked_softmax_parts(
        [s_all[:, :far], s_all[:, far:] + blast_ref[...], s_new], [mk[:, :far], mk[:, far:], mnew], 1)
    o_s = _dot_nt(_bf(jnp.concatenate([p_far, p_last], axis=1)), _bf(slc[LANES:half, :])) \
        + _dot(_bf(p_new), _bf(pad_s[1]))

    cw = lax.broadcasted_iota(jnp.int32, (rows, WINDOW), 1)
    pw, pn = _masked_softmax_parts([s_win, s_wnew], [cw > tr, mnew], 1)
    o_w = _dot_nt(_bf(pw), _bf(wb[LANES:half, :])) + _dot(_bf(pn), _bf(pad_s[3]))

    gl = gl_ref[0]
    gate_all = [_sigmoid(_pick_columns(gl, rep_ref[br])) for br in range(3)]
    gates = [jnp.concatenate([_heads_to_rows(ga, g, nt) for g in range(NSA_KV)], axis=0) for ga in gate_all]
    comb = gates[0] * o_c + gates[1] * o_s + gates[2] * o_w
    for g in range(NSA_KV):
        blocks = _rows_to_heads(comb[g * 4 * nt:(g + 1) * 4 * nt], g, nt)
        o_ref[0, :, (2 * g) * LANES:(2 * g + 1) * LANES] = blocks[0]
        o_ref[0, :, (2 * g + 1) * LANES:(2 * g + 2) * LANES] = blocks[1]


def _nsa_sample(layer, cache_t, page_table, u, wbuf_t, w_pos, tables, n_new):
    bsz, n_pages = page_table.shape
    nt = u.shape[1]
    past = n_pages * PAGE
    wpool = jnp.tile(w_pos, (1, past // L_CMP))
    rep = _gate_rep()
    full = lambda a: pl.BlockSpec(a.shape, lambda b, pt, nd=a.ndim: (0,) * nd)
    ucols = lambda width, off: pl.BlockSpec((1, nt, width), lambda b, pt: (b, 0, off // width))
    kern = functools.partial(_nsa_sample_kernel, layer, n_pages, n_new)
    grid_spec = pltpu.PrefetchScalarGridSpec(
        num_scalar_prefetch=1,
        grid=(bsz,),
        in_specs=[pl.BlockSpec(memory_space=pl.ANY),
                  ucols(512, U_Q), ucols(512, U_ROWS), ucols(2 * LANES, U_WIN),
                  pl.BlockSpec((1, 1) + wbuf_t.shape[2:], lambda b, pt: (layer, b, 0, 0)),
                  ucols(LANES, U_GL), full(rep), full(wpool)] + [full(t) for t in tables],
        out_specs=pl.BlockSpec((1, nt, 512), lambda b, pt: (b, 0, 0)),
        scratch_shapes=[pltpu.VMEM((2 * LANES, past), F32), pltpu.VMEM((2, 2 * LANES, past), F32),
                        pltpu.VMEM((2, past, 2 * n_pages), BF16), pltpu.VMEM((2 * n_pages, past), BF16),
                        pltpu.VMEM((4, LANES, LANES), F32), pltpu.SemaphoreType.DMA((3,))],
    )
    return pl.pallas_call(
        kern,
        grid_spec=grid_spec,
        out_shape=jax.ShapeDtypeStruct((bsz, nt, 512), F32),
        compiler_params=pltpu.CompilerParams(dimension_semantics=("arbitrary",), vmem_limit_bytes=BIG_VMEM_LIMIT),
        name="nsa_sample",
    )(page_table, cache_t, u, u, u, wbuf_t, u, rep, wpool, *tables)


def _mixout_kernel(x_ref, g_ref, ca_ref, ob_ref, oc_ref, wg_ref, wpa_ref, wpb_ref, wpc_ref, wo_ref, o_ref):
    x = x_ref[...]
    h = _bf(x * lax.rsqrt(jnp.mean(x * x, axis=-1, keepdims=True) + EPS) * g_ref[...])
    gate = lambda k: _sigmoid(_dot(h, wg_ref[:, k * D_MODEL:(k + 1) * D_MODEL]))
    y = gate(0) * _dot(_bf(ca_ref[...]), wpa_ref[...])
    y = y + gate(1) * _dot(_bf(ob_ref[...]), wpb_ref[...])
    y = y + gate(2) * _dot(_bf(oc_ref[...]), wpc_ref[...])
    o_ref[...] = x + _dot(_bf(y), wo_ref[...])


def _mixout(x, g, ca, ob, oc, wg, wpa, wpb, wpc, wo):
    m = x.shape[0]
    tm = min(m, 512)
    rowblk = lambda n: pl.BlockSpec((tm, n), lambda i: (i, 0))
    full = lambda a: pl.BlockSpec(a.shape, lambda i: (0, 0))
    return pl.pallas_call(
        _mixout_kernel,
        grid=(m // tm,),
        in_specs=[rowblk(D_MODEL), pl.BlockSpec((1, D_MODEL), lambda i: (0, 0)),
                  rowblk(512), rowblk(512), rowblk(512), full(wg), full(wpa), full(wpb), full(wpc), full(wo)],
        out_specs=rowblk(D_MODEL),
        out_shape=jax.ShapeDtypeStruct((m, D_MODEL), F32),
        compiler_params=_params("parallel"),
        name="mixer_out",
    )(x, g.reshape(1, D_MODEL), ca, ob, oc, wg, wpa, wpb, wpc, wo)


def _xattn_kernel(tiled_kv, x_ref, g_ref, kv_ref, wq_ref, wo_ref, o_ref):
    nb, tt, d = x_ref.shape
    x = x_ref[...].reshape(nb * tt, d)
    h = _bf(x * lax.rsqrt(jnp.mean(x * x, axis=-1, keepdims=True) + EPS) * g_ref[...])
    q = _dot(h, wq_ref[...])

    def mem_head(sq, which, hd):
        if not tiled_kv:
            c0 = which * D_MODEL + hd * X_DH
            return kv_ref[sq, :, c0:c0 + X_DH]
        return jnp.concatenate(
            [kv_ref[0, sq, pl.ds(which * 8 + half * X_HEADS + hd, N_MEM, stride=16), :] for half in range(2)], axis=1)

    seqs = []
    for sq in range(nb):
        outs = []
        for hd in range(X_HEADS):
            qh = _bf(q[sq * tt:(sq + 1) * tt, hd * X_DH:(hd + 1) * X_DH])
            kh = _bf(mem_head(sq, 0, hd))
            vh = _bf(mem_head(sq, 1, hd))
            s = _dot_nt(qh, kh) * (X_DH ** -0.5)
            e = jnp.exp(s - jnp.max(s, axis=-1, keepdims=True))
            pr = e * (1.0 / jnp.sum(e, axis=-1, keepdims=True))
            outs.append(_dot(_bf(pr), vh))
        seqs.append(jnp.concatenate(outs, axis=1))
    o = seqs[0] if nb == 1 else jnp.concatenate(seqs, axis=0)
    o_ref[...] = (x + _dot(_bf(o), wo_ref[...])).reshape(nb, tt, d)


def _cross_attn(x, g, mem_kv, wq, wo, layer=None):
    bsz, t_len, d = x.shape
    tt = min(t_len, 512)
    nb = 4 if (t_len <= 8 and bsz % 4 == 0) else 1
    full = lambda a: pl.BlockSpec(a.shape, lambda b, t: (0, 0))
    if layer is None:
        kv_spec = pl.BlockSpec((nb, N_MEM, 2 * d), lambda b, t: (b, 0, 0))
    else:
        kv_spec = pl.BlockSpec((1, nb) + mem_kv.shape[2:], lambda b, t: (layer, b, 0, 0))
    return pl.pallas_call(
        functools.partial(_xattn_kernel, layer is not None),
        grid=(bsz // nb, t_len // tt),
        in_specs=[pl.BlockSpec((nb, tt, d), lambda b, t: (b, t, 0)),
                  pl.BlockSpec((1, d), lambda b, t: (0, 0)),
                  kv_spec, full(wq), full(wo)],
        out_specs=pl.BlockSpec((nb, tt, d), lambda b, t: (b, t, 0)),
        out_shape=jax.ShapeDtypeStruct((bsz, t_len, d), F32),
        compiler_params=_params("parallel", "parallel"),
        name="cross_attn",
    )(x, g.reshape(1, d), mem_kv, wq, wo)


FF_CHUNK = 1024


def _mlp_kernel(n_k, final_norm, x_ref, g_ref, gf_ref, w1_ref, w2_ref, o_ref, h_s, acc_s):
    k = pl.program_id(1)

    @pl.when(k == 0)
    def _():
        x = x_ref[...]
        h_s[...] = _bf(x * lax.rsqrt(jnp.mean(x * x, axis=-1, keepdims=True) + EPS) * g_ref[...])
        acc_s[...] = x

    a = jnp.maximum(_dot(h_s[...], w1_ref[...]), 0.0)
    acc_s[...] += _dot(_bf(a * a), w2_ref[...])

    @pl.when(k == n_k - 1)
    def _():
        y = acc_s[...]
        if final_norm:
            y = y * lax.rsqrt(jnp.mean(y * y, axis=-1, keepdims=True) + EPS) * gf_ref[...]
        o_ref[...] = y


def _mlp(x, g, w1, w2, final_g=None):
    m, d = x.shape
    tm = min(m, 1024)
    n_k = D_FF // FF_CHUNK
    gf = g if final_g is None else final_g
    return pl.pallas_call(
        functools.partial(_mlp_kernel, n_k, final_g is not None),
        grid=(m // tm, n_k),
        in_specs=[pl.BlockSpec((tm, d), lambda i, k: (i, 0)),
                  pl.BlockSpec((1, d), lambda i, k: (0, 0)),
                  pl.BlockSpec((1, d), lambda i, k: (0, 0)),
                  pl.BlockSpec((d, FF_CHUNK), lambda i, k: (0, k)),
                  pl.BlockSpec((FF_CHUNK, d), lambda i, k: (k, 0))],
        out_specs=pl.BlockSpec((tm, d), lambda i, k: (i, 0)),
        out_shape=jax.ShapeDtypeStruct((m, d), F32),
        scratch_shapes=[pltpu.VMEM((tm, d), BF16), pltpu.VMEM((tm, d), F32)],
        compiler_params=_params("parallel", "arbitrary"),
        name="sq_relu_mlp",
    )(x, g.reshape(1, d), gf.reshape(1, d), w1, w2)


A_COLS = 2 * CONV_CH
B0 = A_COLS
Z0 = B0 + GDN_QKV
AB0 = Z0 + GDN_HEADS * GDN_D
C0 = AB0 + 2 * GDN_HEADS
KV0 = C0 + NSA_HEADS * NSA_DH
GL0 = KV0 + 6 * NSA_KV * NSA_DH
G0 = GL0 + 3 * NSA_HEADS
N_IN = G0 + 3 * D_MODEL


def _layer_weights(l, w_in, w_pa, w_pb, w_pc, w_o, w_xq, w_xk, w_xv, w_xo, w_ff1, w_ff2):
    w = w_in[l]
    lane_pad = lambda cols: jnp.pad(cols, ((0, 0), (0, LANES - cols.shape[1])))
    groups = [(U_A, w[:, 0:A_COLS]), (U_Q, w[:, C0:KV0]), (U_ROWS, w[:, KV0:KV0 + 4 * LANES]),
              (U_QKV, w[:, B0:Z0]), (U_Z, w[:, Z0:AB0]),
              (U_WIN, w[:, KV0 + 4 * LANES:GL0]), (U_AB, lane_pad(w[:, AB0:C0])), (U_GL, lane_pad(w[:, GL0:G0]))]
    off = 0
    for start, cols in groups:
        assert start == off
        off += cols.shape[1]
    assert off == U_N
    return {
        "in": _bf(jnp.concatenate([cols for _, cols in groups], axis=1)),
        "g": _bf(w[:, G0:N_IN]),
        "pa": _bf(w_pa[l]), "pb": _bf(w_pb[l]), "pc": _bf(w_pc[l]), "o": _bf(w_o[l]),
        "xq": _bf(w_xq[l]), "xo": _bf(w_xo[l]),
        "xkv": _bf(jnp.concatenate([w_xk[l], w_xv[l]], axis=1)),
        "ff1": _bf(w_ff1[l]), "ff2": _bf(w_ff2[l]),
    }


def _mixers(x, lw, p, l, conv_state_pad, qkv_state_pad, s0, n_valid, gdn_len, nsa_fn):
    bsz, t_len, d = x.shape
    m = bsz * t_len
    x2 = x.reshape(m, d)
    u2, rows, win = _proj_in(x2, p["norm_mix"][l], lw["in"])
    u = u2.reshape(bsz, t_len, U_N)
    rows = rows.reshape(bsz, t_len, 4 * LANES)
    win = win.reshape(bsz, t_len, 2 * LANES)
    qkv_tail = u[:, max(n_valid - 3, 0):n_valid, U_QKV:U_QKV + GDN_QKV]

    ca, conv_new = _conformer(u, conv_state_pad, p["conv_a_w"][l], p["conv_a_b"][l], p["ln_a_g"][l],
                              p["ln_a_b"][l], n_valid if n_valid < t_len else min(t_len, 256))
    w_conv_pad = jnp.pad(p["gdn_conv_w"][l], ((0, 4), (0, 0)))
    gdn_args = (qkv_state_pad, s0, w_conv_pad, p["gdn_a_log"][l], p["gdn_dt_bias"][l], p["gdn_norm_g"][l])
    if gdn_len == t_len:
        ob, s_new = _gated_deltanet(u, U_QKV // LANES, U_Z // LANES, U_AB // LANES, *gdn_args, gdn_len)
    else:
        ug = jnp.concatenate([u[:, :, U_QKV:U_WIN], u[:, :, U_AB:U_AB + LANES]], axis=-1)
        ug = jnp.pad(ug, ((0, 0), (0, gdn_len - t_len), (0, 0)))
        ob, s_new = _gated_deltanet(ug, 0, (U_Z - U_QKV) // LANES, (U_WIN - U_QKV) // LANES, *gdn_args, n_valid)
        ob = ob[:, :t_len]
    oc = nsa_fn(u)
    x_new = _mixout(x2, p["norm_mix"][l], ca.reshape(m, -1), ob.reshape(m, -1), oc.reshape(m, -1),
                    lw["g"], lw["pa"], lw["pb"], lw["pc"], lw["o"])
    return x_new.reshape(bsz, t_len, d), conv_new[:, HALO - (CONV_W - 1):], qkv_tail, s_new, rows, win


def kernel(x_prompt, x_sample, cache_nsa_kv, cache_win_kv, state_conv_a, state_conv_qkv, state_gdn, cache_mem_kv,
           page_table, mem_prompt, rel_bias, norm_mix, w_in, conv_a_w, conv_a_b, ln_a_g, ln_a_b, w_pa, gdn_conv_w,
           gdn_a_log, gdn_dt_bias, gdn_norm_g, w_pb, nsa_cmp_w, w_pc, w_o, norm_x, w_xq, w_xk, w_xv, w_xo,
           norm_mlp, w_ff1, w_ff2, norm_final):
    p = {"norm_mix": norm_mix, "conv_a_w": conv_a_w, "conv_a_b": conv_a_b, "ln_a_g": ln_a_g, "ln_a_b": ln_a_b,
         "gdn_conv_w": gdn_conv_w, "gdn_a_log": gdn_a_log, "gdn_dt_bias": gdn_dt_bias, "gdn_norm_g": gdn_norm_g}
    depth = w_in.shape[0]
    bp, tp, d = x_prompt.shape
    bs, ts, _ = x_sample.shape
    ts_pad = 8
    n_pages = page_table.shape[1]
    wb = cache_win_kv.shape[2]
    xp = x_prompt
    xs = jnp.pad(x_sample, ((0, 0), (0, ts_pad - ts), (0, 0)))
    ptab, stab = _nsa_tables(rel_bias, tp, n_pages, ts_pad)
    cache_t = jnp.transpose(cache_nsa_kv, (0, 1, 3, 4, 5, 2)).reshape(depth, -1, 4 * LANES, PAGE)
    wbuf_t = jnp.transpose(cache_win_kv, (0, 1, 3, 4, 5, 2)).reshape(depth, bs, 2 * LANES, wb)
    mem_t = cache_mem_kv.reshape(depth, bs, N_MEM, 2, X_HEADS, 2, LANES)
    mem_t = jnp.transpose(mem_t, (0, 1, 2, 3, 5, 4, 6)).reshape(depth, bs, N_MEM * 2 * 2 * X_HEADS, LANES)
    outs = {k: [] for k in ("p_rows", "p_win", "p_conv", "p_qkv", "p_gdn", "p_mem",
                            "s_rows", "s_win", "s_conv", "s_qkv", "s_gdn")}
    for l in range(depth):
        lw = _layer_weights(l, w_in, w_pa, w_pb, w_pc, w_o, w_xq, w_xk, w_xv, w_xo, w_ff1, w_ff2)
        nsa_p = lambda u: _nsa_prompt(u, nsa_cmp_w[l], ptab)
        xp, conv_n, qkv_tail, s_n, rows, win = _mixers(
            xp, lw, p, l, jnp.zeros((bp, HALO, CONV_CH), F32), jnp.zeros((bp, 8, GDN_QKV), F32),
            jnp.zeros((bp, GDN_HEADS, GDN_D, GDN_D), F32), tp, tp, nsa_p)
        mem_kv = _matmul(_bf(mem_prompt.reshape(bp * N_MEM, d)), lw["xkv"]).reshape(bp, N_MEM, 2 * d)
        xp = _cross_attn(xp, norm_x[l], mem_kv, lw["xq"], lw["xo"])
        final_g = norm_final if l == depth - 1 else None
        xp = _mlp(xp.reshape(bp * tp, d), norm_mlp[l], lw["ff1"], lw["ff2"], final_g).reshape(bp, tp, d)
        outs["p_rows"].append(rows.reshape(bp, tp, 4, NSA_KV, NSA_DH))
        outs["p_win"].append(win[:, tp - min(WINDOW, tp):].reshape(bp, min(WINDOW, tp), 2, NSA_KV, NSA_DH))
        outs["p_conv"].append(conv_n)
        outs["p_qkv"].append(qkv_tail)
        outs["p_gdn"].append(s_n)
        outs["p_mem"].append(mem_kv.reshape(bp, N_MEM, 2, X_HEADS, X_DH))
        nsa_s = lambda u: _nsa_sample(l, cache_t, page_table, u, wbuf_t, nsa_cmp_w[l], stab, ts)
        conv_pad = jnp.pad(state_conv_a[l], ((0, 0), (HALO - (CONV_W - 1), 0), (0, 0)))
        qkv_pad = jnp.pad(state_conv_qkv[l], ((0, 0), (5, 0), (0, 0)))
        xs, conv_n, qkv_tail, s_n, rows, win = _mixers(
            xs, lw, p, l, conv_pad, qkv_pad, state_gdn[l], ts, GDN_CHUNK, nsa_s)
        xs = _cross_attn(xs, norm_x[l], mem_t, lw["xq"], lw["xo"], layer=l)
        xs = _mlp(xs.reshape(bs * ts_pad, d), norm_mlp[l], lw["ff1"], lw["ff2"], final_g).reshape(bs, ts_pad, d)
        outs["s_rows"].append(rows[:, :ts].reshape(bs, ts, 4, NSA_KV, NSA_DH))
        win_new = win[:, :ts].reshape(bs, ts, 2, NSA_KV, NSA_DH)
        outs["s_win"].append(jnp.concatenate([cache_win_kv[l], win_new], axis=1)[:, ts:])
        outs["s_conv"].append(conv_n)
        outs["s_qkv"].append(qkv_tail)
        outs["s_gdn"].append(s_n)
    st = lambda k: jnp.stack(outs[k], axis=0)
    return (xp, xs[:, :ts], st("p_rows"), st("p_win"), st("p_conv"), st("p_qkv"), st("p_gdn"), st("p_mem"),
            st("s_rows"), st("s_win"), st("s_conv"), st("s_qkv"), st("s_gdn"))
```

```python
import functools
import math

import jax
import jax.numpy as jnp
import numpy as np
from jax import lax
from jax.experimental import pallas as pl
from jax.experimental.pallas import tpu as pltpu

F32 = jnp.float32
BF16 = jnp.bfloat16

D_MODEL = 1024
CONV_CH = 512
CONV_W = 31
GDN_HEADS = 4
GDN_D = 128
GDN_CHUNK = 64
GDN_QKV = 3 * GDN_HEADS * GDN_D
NSA_HEADS = 8
NSA_KV = 2
NSA_GQ = 4
NSA_DH = 64
L_CMP = 32
L_SEL = 64
N_SEL = 16
WINDOW = 512
Q_BLOCK = 128
FORCE_BONUS = 1e4
PAGE = 128
N_MEM = 256
X_HEADS = 4
X_DH = 256
D_FF = 4096
N_BUCKETS = 32
EPS = 1e-6
NEG = -1e30

LANES = 128
HALO = 32
VMEM_LIMIT = 48 * 1024 * 1024
BIG_VMEM_LIMIT = 56 * 1024 * 1024


def _bf(x):
    return x.astype(BF16)


def _dot(a, b):
    return jnp.dot(a, b, preferred_element_type=F32)


def _dot_nt(a, b):
    return lax.dot_general(a, b, (((1,), (1,)), ((), ())), preferred_element_type=F32)


def _sigmoid(x):
    return 0.5 * jnp.tanh(0.5 * x) + 0.5


def _silu(x):
    return x * _sigmoid(x)


def _params(*sem):
    return pltpu.CompilerParams(dimension_semantics=sem, vmem_limit_bytes=VMEM_LIMIT)


MEM_ROWS = 2 * 2 * X_HEADS


def _memkv_kernel(a_ref, w_ref, o_ref):
    tm = a_ref.shape[0]
    acc = _dot(a_ref[...], w_ref[...])
    for which in range(2):
        for hd in range(X_HEADS):
            for half in range(2):
                c0 = which * D_MODEL + hd * X_DH + half * LANES
                o_ref[pl.ds(which * 8 + half * X_HEADS + hd, tm, stride=MEM_ROWS), :] = acc[:, c0:c0 + LANES]


def _mem_kv(a, w):
    m, k = a.shape
    tm = min(m, 512)
    return pl.pallas_call(
        _memkv_kernel,
        grid=(m // tm,),
        in_specs=[pl.BlockSpec((tm, k), lambda i: (i, 0)), pl.BlockSpec(w.shape, lambda i: (0, 0))],
        out_specs=pl.BlockSpec((tm * MEM_ROWS, LANES), lambda i: (i, 0)),
        out_shape=jax.ShapeDtypeStruct((m * MEM_ROWS, LANES), F32),
        compiler_params=_params("parallel"),
        name="mem_kv",
    )(a, w)


U_A = 0
U_Q = 1024
U_ROWS = 1536
U_QKV = 2048
U_Z = 3584
U_WIN = 4096
U_AB = 4352
U_GL = 4480
U_N = 4608
PROJ_TN = 1536


def _proj_kernel(x_ref, g_ref, w_ref, o_ref, rows_ref, win_ref, h_s):
    j = pl.program_id(1)

    @pl.when(j == 0)
    def _():
        x = x_ref[...]
        h_s[...] = _bf(x * lax.rsqrt(jnp.mean(x * x, axis=-1, keepdims=True) + EPS) * g_ref[...])

    acc = _dot(h_s[...], w_ref[...])
    o_ref[...] = acc

    @pl.when(j == U_ROWS // PROJ_TN)
    def _():
        rows_ref[...] = acc[:, U_ROWS % PROJ_TN:U_ROWS % PROJ_TN + 4 * LANES]

    @pl.when(j == U_WIN // PROJ_TN)
    def _():
        win_ref[...] = acc[:, U_WIN % PROJ_TN:U_WIN % PROJ_TN + 2 * LANES]


def _proj_in(x, g, w):
    m, d = x.shape
    n = w.shape[1]
    tm = min(m, 1024)
    assert U_ROWS % PROJ_TN + 4 * LANES <= PROJ_TN and U_WIN % PROJ_TN + 2 * LANES <= PROJ_TN
    return pl.pallas_call(
        _proj_kernel,
        grid=(m // tm, n // PROJ_TN),
        in_specs=[pl.BlockSpec((tm, d), lambda i, j: (i, 0)), pl.BlockSpec((1, d), lambda i, j: (0, 0)),
                  pl.BlockSpec((d, PROJ_TN), lambda i, j: (0, j))],
        out_specs=[pl.BlockSpec((tm, PROJ_TN), lambda i, j: (i, j)),
                   pl.BlockSpec((tm, 4 * LANES), lambda i, j: (i, 0)),
                   pl.BlockSpec((tm, 2 * LANES), lambda i, j: (i, 0))],
        out_shape=[jax.ShapeDtypeStruct((m, n), F32), jax.ShapeDtypeStruct((m, 4 * LANES), F32),
                   jax.ShapeDtypeStruct((m, 2 * LANES), F32)],
        scratch_shapes=[pltpu.VMEM((tm, d), BF16)],
        compiler_params=_params("parallel", "arbitrary"),
        name="proj_in",
    )(x, g.reshape(1, d), w)


def _split3(x):
    hi = _bf(x)
    r = x - hi.astype(F32)
    mid = _bf(r)
    return hi, mid, _bf(r - mid.astype(F32))


def _pick_columns(x, onehot):
    hi, mid, lo = _split3(x)
    return (_dot(lo, onehot) + _dot(mid, onehot)) + _dot(hi, onehot)


def _conf_kernel(n_t, tt, tv, u_ref, halo_ref, st_ref, w_ref, b_ref, g_ref, lb_ref, o_ref, nb_ref, xc_ref, zs_ref):
    t = pl.program_id(1)
    u = u_ref[0]
    xc_ref[HALO:HALO + tt, :] = u[:, :CONV_CH] * _sigmoid(u[:, CONV_CH:])
    if n_t > 1:
        uh = halo_ref[0]
        gh = uh[:, :CONV_CH] * _sigmoid(uh[:, CONV_CH:])
        xc_ref[0:HALO, :] = jnp.where(t > 0, gh, st_ref[0])
    else:
        xc_ref[0:HALO, :] = st_ref[0]
    off = HALO - (CONV_W - 1)
    span = tt + HALO - 8
    for r in range(1, 8):
        zs_ref[r - 1] = xc_ref[r:r + span, :]
    acc = None
    for i in range(CONV_W):
        pos = off + i
        r, base = pos % 8, pos - pos % 8
        src = xc_ref[base:base + tt, :] if r == 0 else zs_ref[r - 1, base:base + tt, :]
        term = src * w_ref[i:i + 1, :]
        acc = term if acc is None else acc + term
    y = acc + b_ref[...]
    mu = jnp.mean(y, axis=-1, keepdims=True)
    yc = y - mu
    var = jnp.mean(yc * yc, axis=-1, keepdims=True)
    ln = yc * lax.rsqrt(var + EPS) * g_ref[...] + lb_ref[...]
    o_ref[0] = _silu(ln)

    @pl.when(t == n_t - 1)
    def _():
        nb_ref[0] = xc_ref[tv:tv + HALO, :]


def _conformer(u_a, state_pad, w_dw, b_dw, ln_g, ln_b, n_valid_last):
    bsz, t_len, _ = u_a.shape
    tt = min(t_len, 256)
    n_t = t_len // tt
    hb = tt // HALO if n_t > 1 else 1
    halo_rows = HALO if n_t > 1 else tt
    w_pad = jnp.pad(w_dw, ((0, HALO - CONV_W), (0, 0)))
    row = lambda v: v.reshape(1, CONV_CH)
    kern = functools.partial(_conf_kernel, n_t, tt, n_valid_last)
    return pl.pallas_call(
        kern,
        grid=(bsz, n_t),
        in_specs=[
            pl.BlockSpec((1, tt, 2 * CONV_CH), lambda b, t: (b, t, 0)),
            pl.BlockSpec((1, halo_rows, 2 * CONV_CH), lambda b, t: (b, jnp.maximum(t * hb - 1, 0), 0)),
            pl.BlockSpec((1, HALO, CONV_CH), lambda b, t: (b, 0, 0)),
            pl.BlockSpec((HALO, CONV_CH), lambda b, t: (0, 0)),
            pl.BlockSpec((1, CONV_CH), lambda b, t: (0, 0)),
            pl.BlockSpec((1, CONV_CH), lambda b, t: (0, 0)),
            pl.BlockSpec((1, CONV_CH), lambda b, t: (0, 0)),
        ],
        out_specs=[
            pl.BlockSpec((1, tt, CONV_CH), lambda b, t: (b, t, 0)),
            pl.BlockSpec((1, HALO, CONV_CH), lambda b, t: (b, 0, 0)),
        ],
        out_shape=[
            jax.ShapeDtypeStruct((bsz, t_len, CONV_CH), F32),
            jax.ShapeDtypeStruct((bsz, HALO, CONV_CH), F32),
        ],
        scratch_shapes=[pltpu.VMEM((HALO + tt, CONV_CH), F32), pltpu.VMEM((7, tt + HALO - 8, CONV_CH), F32)],
        compiler_params=_params("parallel", "arbitrary"),
        name="conformer_conv",
    )(u_a, u_a, state_pad, w_pad, row(b_dw), row(ln_g), row(ln_b))


def _tri_inverse(a_list, ii, jj, merge_shifts):
    mm = lambda p, q: _dot(_bf(p), _bf(q))
    a0 = [jnp.where((ii >> 3) == (jj >> 3), a, 0.0) for a in a_list]
    a2 = [mm(p, p) for p in a0]
    a4 = [mm(p, p) for p in a2]
    r = [(q - p) - mm(p, q) for p, q in zip(a0, a2)]
    r = [(p + q) + mm(p, q) for p, q in zip(r, a4)]
    for sh in merge_shifts:
        mask = ((ii >> (sh + 1)) == (jj >> (sh + 1))) & ((ii >> sh) != (jj >> sh))
        off = [jnp.where(mask, a, 0.0) for a in a_list]
        t = [o + mm(o, p) for o, p in zip(off, r)]
        r = [p - (q + mm(p, q)) for p, q in zip(r, t)]
    return r


def _softplus(x):
    return jnp.maximum(x, 0.0) + jnp.log1p(jnp.exp(-jnp.abs(x)))


def _gdn_kernel(t_len, n_valid, hp, alog_ref, dtb_ref, q_ref, k_ref, v_ref, z_ref, ab_ref,
                sq_ref, sk_ref, sv_ref, wq_ref, wk_ref, wv_ref, s0_ref, ng_ref,
                o_ref, sn_ref,
                xp_s, qn_s, kn_s, vn_s, g_s, be_s, vw_s, kcd_s, qg_s, kdt_s, qk_s, ge_s):
    h0 = pl.program_id(1) * hp
    n_chunks = t_len // GDN_CHUNK
    c_len = GDN_CHUNK

    heads = range(hp)
    hcols = lambda hh: slice(hh * GDN_D, (hh + 1) * GDN_D)

    def conv(x_ref, st_ref, w_ref, hh):
        xp_s[0:8, :] = st_ref[0, :, hcols(hh)]
        xp_s[8:8 + t_len, :] = x_ref[0, :, hcols(hh)]
        acc = xp_s[5:5 + t_len, :] * w_ref[0:1, hcols(hh)]
        for i in range(1, 4):
            acc = acc + xp_s[5 + i:5 + i + t_len, :] * w_ref[i:i + 1, hcols(hh)]
        return _silu(acc)

    col = lax.broadcasted_iota(jnp.int32, (LANES, LANES), 0)
    ab = ab_ref[0]
    for hh in heads:
        h = h0 + hh
        qc = conv(q_ref, sq_ref, wq_ref, hh)
        qn_s[hh] = qc * lax.rsqrt(jnp.sum(qc * qc, axis=-1, keepdims=True) + EPS) * (GDN_D ** -0.5)
        kc = conv(k_ref, sk_ref, wk_ref, hh)
        kn_s[hh] = kc * lax.rsqrt(jnp.sum(kc * kc, axis=-1, keepdims=True) + EPS)
        vn_s[hh] = conv(v_ref, sv_ref, wv_ref, hh)
        a_rep = _pick_columns(ab, _bf(jnp.where(col == h, 1.0, 0.0)))
        b_rep = _pick_columns(ab, _bf(jnp.where(col == GDN_HEADS + h, 1.0, 0.0)))
        a_exp = jnp.exp(jnp.full((1, LANES), alog_ref[h], F32))
        g = -a_exp * _softplus(a_rep + dtb_ref[h])
        beta = _sigmoid(b_rep)
        if n_valid < t_len:
            live = lax.broadcasted_iota(jnp.int32, (t_len, LANES), 0) < n_valid
            g = jnp.where(live, g, 0.0)
            beta = jnp.where(live, beta, 0.0)
        g_s[hh] = g
        be_s[hh] = beta

    ii = lax.broadcasted_iota(jnp.int32, (c_len, c_len), 0)
    jj = lax.broadcasted_iota(jnp.int32, (c_len, c_len), 1)
    incl = ii >= jj
    strict = ii > jj
    ltri = _bf(incl.astype(F32))
    unroll = max(u for u in (1, 2, 4, 8) if n_chunks % u == 0 and u * hp <= 16)

    def cumdecay(g_c):
        g_hi, g_mid, g_lo = _split3(g_c)
        return (_dot(ltri, g_lo) + _dot(ltri, g_mid)) + _dot(ltri, g_hi)

    def prep(cu, carry):
        pairs = [(hh, cu * unroll + u) for hh in heads for u in range(unroll)]
        sls = [pl.ds(pl.multiple_of(c * c_len, c_len), c_len) for _, c in pairs]
        each = lambda f, *ls: [f(*a) for a in zip(*ls)]
        q_l = [qn_s[hh, sl, :] for (hh, _), sl in zip(pairs, sls)]
        k_l = [kn_s[hh, sl, :] for (hh, _), sl in zip(pairs, sls)]
        b_l = [be_s[hh, sl, :] for (hh, _), sl in zip(pairs, sls)]
        gc_l = [cumdecay(g_s[hh, sl, :]) for (hh, _), sl in zip(pairs, sls)]
        dec_l = each(lambda gc: jnp.where(
            incl, jnp.exp(jnp.minimum(gc[:, 0:c_len] - gc.T[0:c_len, :], 0.0)), 0.0), gc_l)
        kb_l = each(lambda k, b: k * b, k_l, b_l)
        a_l = each(lambda kb, k, dec: jnp.where(strict, _dot_nt(_bf(kb), _bf(k)) * dec, 0.0), kb_l, k_l, dec_l)
        r_l = _tri_inverse(a_l, ii, jj, () if n_valid <= 8 else (3, 4, 5))
        eg_l = each(jnp.exp, gc_l)
        rhs_l = [jnp.concatenate([vn_s[hh, sl, :] * b, kb * eg], axis=1)
                 for (hh, _), sl, b, kb, eg in zip(pairs, sls, b_l, kb_l, eg_l)]
        sol_l = each(lambda r, rhs: rhs + _dot(_bf(r), _bf(rhs)), r_l, rhs_l)
        qk_l = each(lambda q, k, dec: jnp.where(incl, _dot_nt(_bf(q), _bf(k)) * dec, 0.0), q_l, k_l, dec_l)
        for (hh, c), sl, sol, qk, q, k, gc, eg in zip(pairs, sls, sol_l, qk_l, q_l, k_l, gc_l, eg_l):
            g_end = gc[c_len - 1:c_len, :]
            vw_s[hh, sl, :] = sol[:, :GDN_D]
            kcd_s[hh, sl, :] = _bf(sol[:, GDN_D:])
            qk_s[hh, c] = _bf(qk)
            qg_s[hh, sl, :] = _bf(q * eg)
            kdt_s[hh, c] = _bf((k * jnp.exp(g_end - gc)).T)
            ge_s[hh, c] = jnp.broadcast_to(jnp.exp(g_end), (8, LANES))
        return carry

    lax.fori_loop(0, n_chunks // unroll, prep, 0)

    def step(c, states):
        sl = pl.ds(pl.multiple_of(c * c_len, c_len), c_len)
        sb = [_bf(s) for s in states]
        v_new = [vw_s[hh, sl, :] - _dot(kcd_s[hh, sl, :], sb[hh]) for hh in heads]
        vb = [_bf(v) for v in v_new]
        for hh in heads:
            o_ref[0, sl, hcols(hh)] = _dot(qg_s[hh, sl, :], sb[hh]) + _dot(qk_s[hh, c], vb[hh])
        return tuple(states[hh] * ge_s[hh, c][0:1, :] + _dot(kdt_s[hh, c], vb[hh]) for hh in heads)

    s_fin = lax.fori_loop(0, n_chunks, step, tuple(s0_ref[0, hh] for hh in heads))
    for hh in heads:
        sn_ref[0, hh] = s_fin[hh]
        o = o_ref[0, :, hcols(hh)]
        y = o * lax.rsqrt(jnp.mean(o * o, axis=-1, keepdims=True) + EPS) * ng_ref[...]
        o_ref[0, :, hcols(hh)] = y * _silu(z_ref[0, :, hcols(hh)])


def _gated_deltanet(u, qkv_blk, z_blk, ab_blk, state_pad, s0, w_conv_pad, a_log, dt_bias, norm_g, n_valid):
    bsz, t_len, _ = u.shape
    nh = GDN_HEADS
    hp = nh if t_len <= 4 * GDN_CHUNK else 2
    wide = hp * GDN_D
    assert (qkv_blk * GDN_D) % wide == 0 and (z_blk * GDN_D) % wide == 0
    ublk = lambda blk: pl.BlockSpec((1, t_len, wide), lambda b, j, o=blk * GDN_D // wide: (b, 0, o + j))
    stb = lambda off: pl.BlockSpec((1, 8, wide), lambda b, j, o=off * GDN_D // wide: (b, 0, o + j))
    wb = lambda off: pl.BlockSpec((8, wide), lambda b, j, o=off * GDN_D // wide: (0, o + j))
    smem = pl.BlockSpec(memory_space=pltpu.SMEM)
    n_chunks = t_len // GDN_CHUNK
    seq = lambda dt: pltpu.VMEM((hp, t_len, GDN_D), dt)
    kern = functools.partial(_gdn_kernel, t_len, n_valid, hp)
    return pl.pallas_call(
        kern,
        grid=(bsz, nh // hp),
        in_specs=[smem, smem, ublk(qkv_blk), ublk(qkv_blk + nh), ublk(qkv_blk + 2 * nh), ublk(z_blk),
                  pl.BlockSpec((1, t_len, GDN_D), lambda b, j: (b, 0, ab_blk)),
                  stb(0), stb(nh), stb(2 * nh), wb(0), wb(nh), wb(2 * nh),
                  pl.BlockSpec((1, hp, GDN_D, GDN_D), lambda b, j: (b, j, 0, 0)),
                  pl.BlockSpec((1, GDN_D), lambda b, j: (0, 0))],
        out_specs=[pl.BlockSpec((1, t_len, wide), lambda b, j: (b, 0, j)),
                   pl.BlockSpec((1, hp, GDN_D, GDN_D), lambda b, j: (b, j, 0, 0))],
        out_shape=[jax.ShapeDtypeStruct((bsz, t_len, nh * GDN_D), F32),
                   jax.ShapeDtypeStruct((bsz, nh, GDN_D, GDN_D), F32)],
        scratch_shapes=[pltpu.VMEM((8 + t_len, GDN_D), F32), seq(F32), seq(F32), seq(F32), seq(F32), seq(F32),
                        seq(F32), seq(BF16), seq(BF16),
                        pltpu.VMEM((hp, n_chunks, GDN_D, GDN_CHUNK), BF16),
                        pltpu.VMEM((hp, n_chunks, GDN_CHUNK, GDN_CHUNK), BF16),
                        pltpu.VMEM((hp, n_chunks, 8, LANES), F32)],
        compiler_params=pltpu.CompilerParams(dimension_semantics=("parallel", "parallel"),
                                             vmem_limit_bytes=BIG_VMEM_LIMIT),
        name="gated_deltanet",
    )(a_log, dt_bias, u, u, u, u, u, state_pad, state_pad, state_pad,
      w_conv_pad, w_conv_pad, w_conv_pad, s0, norm_g.reshape(1, GDN_D))


def _heads_to_rows(x, g, nt):
    lane = lax.broadcasted_iota(jnp.int32, (nt, LANES), 1)
    keep = (lane >= NSA_DH * g) & (lane < NSA_DH * (g + 1))
    parts = []
    for r in range(NSA_GQ):
        hh = NSA_GQ * g + r
        blk = x[:, (hh // 2) * LANES:(hh // 2 + 1) * LANES]
        if hh % 2 != g:
            blk = pltpu.roll(blk, NSA_DH, axis=1)
        parts.append(jnp.where(keep, blk, 0.0))
    return jnp.concatenate(parts, axis=0)


def _rows_to_heads(y, g, nt):
    outs = []
    for m in range(2):
        x0 = y[(2 * m) * nt:(2 * m + 1) * nt]
        x1 = y[(2 * m + 1) * nt:(2 * m + 2) * nt]
        if g == 1:
            x0 = pltpu.roll(x0, NSA_DH, axis=1)
        else:
            x1 = pltpu.roll(x1, NSA_DH, axis=1)
        outs.append(x0 + x1)
    return outs


def _masked_softmax_parts(parts, masks, axis):
    sm = [jnp.where(m, s, NEG) for s, m in zip(parts, masks)]
    mx = functools.reduce(jnp.maximum, [jnp.max(s, axis=axis, keepdims=True) for s in sm])
    es = [jnp.where(m, jnp.exp(s - mx), 0.0) for s, m in zip(sm, masks)]
    den = functools.reduce(lambda p, q: p + q, [jnp.sum(e, axis=axis, keepdims=True) for e in es])
    inv = 1.0 / jnp.maximum(den, 1e-30)
    return [e * inv for e in es]


def _bucket_np(rel):
    n = np.maximum(rel, 0)
    nf = np.maximum(n, 1).astype(np.float32)
    large = 16 + (np.log(nf / np.float32(16)) / np.float32(math.log(8.0)) * np.float32(16)).astype(np.int32)
    return np.where(n < 16, n, np.minimum(large, N_BUCKETS - 1)).astype(np.int32)


LOOKUP_TILE = 8192


def _lookup_kernel(idx_ref, tb_ref, o_ref):
    idx = idx_ref[...]
    acc = jnp.zeros(o_ref.shape, F32)
    for k in range(N_BUCKETS):
        acc = jnp.where(idx == k, tb_ref[:, k:k + 1], acc)
    o_ref[...] = acc


def _bias_lookup(rel_bias, idx_list):
    sizes = [int(np.prod(a.shape)) for a in idx_list]
    total = sum(sizes)
    padded = -(-total // LOOKUP_TILE) * LOOKUP_TILE
    flat = np.zeros((1, padded), np.int32)
    flat[0, :total] = np.concatenate([np.asarray(a, np.int32).reshape(-1) for a in idx_list])
    tab = pl.pallas_call(
        _lookup_kernel,
        grid=(padded // LOOKUP_TILE,),
        in_specs=[pl.BlockSpec((1, LOOKUP_TILE), lambda i: (0, i)),
                  pl.BlockSpec((NSA_HEADS, N_BUCKETS), lambda i: (0, 0))],
        out_specs=pl.BlockSpec((NSA_HEADS, LOOKUP_TILE), lambda i: (0, i)),
        out_shape=jax.ShapeDtypeStruct((NSA_HEADS, padded), F32),
        compiler_params=_params("parallel"),
        name="bias_lookup",
    )(jnp.asarray(flat), rel_bias.astype(F32).T)
    outs, off = [], 0
    for a, n in zip(idx_list, sizes):
        outs.append(tab[:, off:off + n].reshape((NSA_HEADS,) + tuple(a.shape)))
        off += n
    return outs


def _head_rows(tab):
    return tab.reshape(NSA_KV, NSA_GQ * tab.shape[1], tab.shape[2])


FAR_TILE = 512


def _nsa_prompt_kernel(t_len, q_ref, kcmp_ref, vcmp_ref, kslc_ref, vslc_ref, kwin_ref, vwin_ref, gl_ref, rep_ref,
                       wk_ref, wv_ref, bc_ref, bct_ref, bnear_ref, bwin_ref, o_ref, kc_s, vc_s):
    i = pl.program_id(1)
    nsb = t_len // L_SEL
    qb = Q_BLOCK
    rows = NSA_GQ * qb

    @pl.when(i == 0)
    def _():
        n2 = 2 * lax.broadcasted_iota(jnp.int32, (nsb, t_len), 0)
        cb = lax.broadcasted_iota(jnp.int32, (nsb, t_len), 1) >> 5
        kc = _bf(kcmp_ref[0])
        vc = _bf(vcmp_ref[0])
        wk = wk_ref[...]
        wv = wv_ref[...]
        kc_s[0:nsb, :] = _dot(_bf(jnp.where(cb == n2, wk, 0.0)), kc)
        kc_s[nsb:2 * nsb, :] = _dot(_bf(jnp.where(cb == n2 + 1, wk, 0.0)), kc)
        vc_s[0:nsb, :] = _dot(_bf(jnp.where(cb == n2, wv, 0.0)), vc)
        vc_s[nsb:2 * nsb, :] = _dot(_bf(jnp.where(cb == n2 + 1, wv, 0.0)), vc)

    q_all = q_ref[0] * (NSA_DH ** -0.5)
    gl = gl_ref[0]
    gate_all = [_sigmoid(_pick_columns(gl, rep_ref[br])) for br in range(3)]
    t0 = i * qb
    tq = t0 + (lax.broadcasted_iota(jnp.int32, (rows, 1), 0) & (qb - 1))
    tl = t0 + (lax.broadcasted_iota(jnp.int32, (1, rows), 1) & (qb - 1))
    eye_q = _bf((lax.broadcasted_iota(jnp.int32, (qb, qb), 0) == lax.broadcasted_iota(jnp.int32, (qb, qb), 1)).astype(F32))
    far_end = jnp.maximum(t0 - qb, 0)
    n_far = (far_end + FAR_TILE - 1) // FAR_TILE
    kc = _bf(kc_s[...])
    vc = _bf(vc_s[...])

    def key_aug(k0, n_keys, limit):
        kpos = k0 + lax.broadcasted_iota(jnp.int32, (n_keys, LANES), 0)
        lane = lax.broadcasted_iota(jnp.int32, (n_keys, LANES), 1)
        hit = (lane == (kpos >> 6)) | ((lane == nsb) & ((kpos >= limit) | (kpos < 0)))
        return _bf(jnp.where(hit, NEG, 0.0))

    gs = range(NSA_KV)
    qg = [_bf(_heads_to_rows(q_all, g, qb)) for g in gs]

    n_prev = WINDOW // qb
    starts = [pl.multiple_of(jnp.maximum(t0 + (j - n_prev) * qb, 0), qb) for j in range(n_prev + 1)]
    kwin = [_bf(kwin_ref[0, pl.ds(st, qb), :]) for st in starts]
    pens = [jnp.where(i + (j - n_prev) >= 0, 0.0, NEG) for j in range(n_prev)] + [0.0]
    s_w = [jnp.concatenate([_dot_nt(qg[g], kwin[j]) + pens[j] for j in range(n_prev + 1)], axis=1) + bwin_ref[g]
           for g in gs]

    s_c = [_dot_nt(qg[g], kc) + bc_ref[0, g] for g in gs]
    s_t = [_dot_nt(kc, qg[g]) + bct_ref[0, g] for g in gs]
    e_c = [jnp.where(tq >= L_CMP - 1, jnp.exp(s - jnp.max(s, axis=1, keepdims=True)), 0.0) for s in s_c]
    p_c = [e * (1.0 / jnp.maximum(jnp.sum(e, axis=1, keepdims=True), 1e-30)) for e in e_c]
    o_c = [_dot(_bf(p), vc) for p in p_c]

    e_t = [jnp.where(tl >= L_CMP - 1, jnp.exp(s - jnp.max(s, axis=0, keepdims=True)), 0.0) for s in s_t]
    p_t = [e * (1.0 / jnp.maximum(jnp.sum(e, axis=0, keepdims=True), 1e-30)) for e in e_t]
    head_sum = lambda x: x[:, 0:qb] + x[:, qb:2 * qb] + x[:, 2 * qb:3 * qb] + x[:, 3 * qb:4 * qb]
    blk = lax.broadcasted_iota(jnp.int32, (nsb, qb), 0)
    cur = (t0 + lax.broadcasted_iota(jnp.int32, (nsb, qb), 1)) >> 6
    bonus = jnp.where((blk == 0) | (blk == cur) | (blk == cur - 1), FORCE_BONUS, 0.0)
    score = [jnp.where(blk <= cur, (head_sum(p[0:nsb]) + head_sum(p[nsb:2 * nsb])) + bonus, -1.0) for p in p_t]
    rank = [jnp.zeros((nsb, qb), F32) for _ in gs]
    for j in range(nsb):
        for g in gs:
            sj = score[g][j:j + 1, :]
            ahead = (sj > score[g]) | ((sj == score[g]) & (blk > j))
            rank[g] = rank[g] + jnp.where(ahead, 1.0, 0.0)
    pen_rows = jnp.where(lax.broadcasted_iota(jnp.int32, (LANES - nsb, qb), 0) == 0, 1.0, 0.0)
    not_sel_t = [jnp.where(r < float(min(N_SEL, nsb)), 0.0, 1.0) for r in rank]
    q_aug = [_bf(_dot_nt(eye_q, _bf(jnp.concatenate([ns, pen_rows], axis=0)))) for ns in not_sel_t]
    qa = [jnp.concatenate([qg[g], jnp.concatenate([q_aug[g]] * NSA_GQ, axis=0)], axis=1) for g in gs]

    def online(carry, s, pv):
        m_i, l_i, acc = carry
        m_n = jnp.maximum(m_i, jnp.max(s, axis=1, keepdims=True))
        p = jnp.exp(s - m_n)
        alpha = jnp.exp(m_i - m_n)
        return m_n, alpha * l_i + jnp.sum(p, axis=1, keepdims=True), alpha * acc + pv(_bf(p))

    def far_tile(kt, carry):
        k0 = pl.multiple_of(kt * FAR_TILE, FAR_TILE)
        ka = jnp.concatenate([_bf(kslc_ref[0, pl.ds(k0, FAR_TILE), :]), key_aug(k0, FAR_TILE, far_end)], axis=1)
        vt = _bf(vslc_ref[0, pl.ds(k0, FAR_TILE), :])
        s = [_dot_nt(qa[g], ka) for g in gs]
        return tuple(online(carry[g], s[g], lambda p: _dot(p, vt)) for g in gs)

    init = (jnp.full((rows, 1), NEG, F32), jnp.zeros((rows, 1), F32), jnp.zeros((rows, LANES), F32))
    far = lax.fori_loop(0, n_far, far_tile, tuple(init for _ in gs))

    p0 = pl.multiple_of(jnp.maximum(t0 - qb, 0), qb)
    d0 = pl.multiple_of(t0, qb)
    ka = jnp.concatenate([
        jnp.concatenate([_bf(kslc_ref[0, pl.ds(p0, qb), :]), _bf(kslc_ref[0, pl.ds(d0, qb), :])], axis=0),
        key_aug(t0 - qb, 2 * qb, t_len)], axis=1)
    vp = _bf(vslc_ref[0, pl.ds(p0, qb), :])
    vd = _bf(vslc_ref[0, pl.ds(d0, qb), :])
    s_near = [_dot_nt(qa[g], ka) + bnear_ref[g] for g in gs]
    fin = [online(far[g], s_near[g], lambda p: _dot(p[:, 0:qb], vp) + _dot(p[:, qb:2 * qb], vd)) for g in gs]
    o_s = [acc * (1.0 / l_n) for _, l_n, acc in fin]

    vwin = [_bf(vwin_ref[0, pl.ds(st, qb), :]) for st in starts]
    e_w = [jnp.exp(s - jnp.max(s, axis=1, keepdims=True)) for s in s_w]
    o_w = []
    for g in gs:
        ew = _bf(e_w[g])
        acc = _dot(ew[:, 0:qb], vwin[0])
        for j in range(1, n_prev + 1):
            acc = acc + _dot(ew[:, j * qb:(j + 1) * qb], vwin[j])
        o_w.append(acc * (1.0 / jnp.sum(e_w[g], axis=1, keepdims=True)))

    for g in gs:
        gates = [_heads_to_rows(gate_all[br], g, qb) for br in range(3)]
        comb = gates[0] * o_c[g] + gates[1] * o_s[g] + gates[2] * o_w[g]
        blocks = _rows_to_heads(comb, g, qb)
        o_ref[0, :, (2 * g) * LANES:(2 * g + 1) * LANES] = blocks[0]
        o_ref[0, :, (2 * g + 1) * LANES:(2 * g + 2) * LANES] = blocks[1]


def _nsa_tables(rel_bias, t_len, n_pages, nt):
    nqb = t_len // Q_BLOCK
    nsb = t_len // L_SEL
    past = n_pages * PAGE
    t = np.arange(Q_BLOCK)
    tq = (np.arange(nqb)[:, None] * Q_BLOCK + t[None, :])[:, :, None]
    n = np.arange(nsb)[None, None, :]
    ts = np.arange(nt)[:, None]
    j = np.arange(2 * n_pages)[None, :]
    c = np.arange(PAGE)[None, :]
    idx = [
        _bucket_np(tq - (n * L_SEL + L_CMP - 1)),
        _bucket_np(tq - (n * L_SEL + L_SEL - 1)),
        _bucket_np(Q_BLOCK + t[:, None] - np.arange(2 * Q_BLOCK)[None, :]),
        _bucket_np(WINDOW + t[:, None] - np.arange(WINDOW + Q_BLOCK)[None, :]),
        _bucket_np(past + ts - (j * L_SEL + L_CMP - 1)),
        _bucket_np(past + ts - (j * L_SEL + L_SEL - 1)),
        _bucket_np(PAGE + ts - c),
        _bucket_np(ts - c),
        _bucket_np(WINDOW + ts - np.arange(WINDOW)[None, :]),
    ]
    ce, co, near, win, sce, sco, slast, snew, swin = _bias_lookup(rel_bias, idx)
    b31 = rel_bias.astype(F32)[N_BUCKETS - 1].reshape(NSA_KV, NSA_GQ, 1, 1)
    shift = lambda tab: (tab.reshape(NSA_KV, NSA_GQ, tab.shape[1], tab.shape[2]) - b31).reshape(
        NSA_KV, NSA_GQ * tab.shape[1], tab.shape[2])
    vis = lambda m: jnp.asarray(np.tile(m, (1,) * (m.ndim - 2) + (NSA_GQ, 1)))
    blocked = lambda tab: jnp.swapaxes(tab, 0, 1).reshape(nqb, NSA_KV, NSA_GQ * Q_BLOCK, nsb)
    bc = jnp.concatenate([blocked(ce), blocked(co)], axis=-1)
    vis_c = np.concatenate([n * L_SEL + L_CMP - 1 <= tq, n * L_SEL + L_SEL - 1 <= tq], axis=-1)
    bc = jnp.where(vis(vis_c)[:, None], bc, NEG)
    c_near = np.arange(2 * Q_BLOCK)[None, :]
    near_m = jnp.where(vis(c_near <= Q_BLOCK + t[:, None])[None], shift(near), NEG)
    c_win = np.arange(WINDOW + Q_BLOCK)[None, :]
    win_m = jnp.where(vis((c_win > t[:, None]) & (c_win <= WINDOW + t[:, None]))[None], _head_rows(win), NEG)
    ptab = (bc, jnp.swapaxes(bc, -1, -2), near_m, win_m)
    rows64 = lambda tab: tab.reshape(NSA_KV * NSA_GQ * nt, tab.shape[-1])
    stab = (rows64(_head_rows(sce)), rows64(_head_rows(sco)), rows64(shift(slast)), rows64(shift(snew)),
            rows64(_head_rows(swin)), rows64(_head_rows(snew)))
    return ptab, stab


def _gate_rep():
    j = np.arange(LANES)[None, :, None]
    c = np.arange(NSA_HEADS * NSA_DH)[None, None, :]
    br = np.arange(3)[:, None, None]
    return jnp.asarray(j == br * NSA_HEADS + c // NSA_DH, BF16)


def _nsa_prompt(u, w_pos, tables):
    bsz, t_len, _ = u.shape
    nqb = t_len // Q_BLOCK
    nsb = t_len // L_SEL
    assert nsb < LANES
    bc, bct, near, wtab = tables
    wk = jnp.tile(w_pos[0], t_len // L_CMP).reshape(1, t_len)
    wv = jnp.tile(w_pos[1], t_len // L_CMP).reshape(1, t_len)
    rep = _gate_rep()
    seq = lambda c: pl.BlockSpec((1, t_len, LANES), lambda b, i, c=c: (b, 0, c))
    full = lambda a: pl.BlockSpec(a.shape, lambda b, i, nd=a.ndim: (0,) * nd)
    per_i = lambda a: pl.BlockSpec((1,) + a.shape[1:], lambda b, i, nd=a.ndim: (i,) + (0,) * (nd - 1))
    kern = functools.partial(_nsa_prompt_kernel, t_len)
    kv0 = U_ROWS // LANES
    w0 = U_WIN // LANES
    return pl.pallas_call(
        kern,
        grid=(bsz, nqb),
        in_specs=[pl.BlockSpec((1, Q_BLOCK, 512), lambda b, i: (b, i, U_Q // 512)),
                  seq(kv0), seq(kv0 + 1), seq(kv0 + 2), seq(kv0 + 3), seq(w0), seq(w0 + 1),
                  pl.BlockSpec((1, Q_BLOCK, LANES), lambda b, i: (b, i, U_GL // LANES)), full(rep),
                  full(wk), full(wv), per_i(bc), per_i(bct), full(near), full(wtab)],
        out_specs=pl.BlockSpec((1, Q_BLOCK, 512), lambda b, i: (b, i, 0)),
        out_shape=jax.ShapeDtypeStruct((bsz, t_len, 512), F32),
        scratch_shapes=[pltpu.VMEM((2 * nsb, LANES), F32)] * 2,
        compiler_params=_params("parallel", "arbitrary"),
        name="nsa_prompt",
    )(u, u, u, u, u, u, u, u, rep, wk, wv, bc, bct, near, wtab)


def _nsa_sample_kernel(layer, n_pages, n_new, pt_ref, cache_ref, q_ref, rows_ref, wnew_ref, wbuf_ref, gl_ref,
                       rep_ref, wpool_ref, bce_ref, bco_ref, blast_ref, bnew_ref, bwin_ref, bwnew_ref,
                       o_ref, cmp_s, slc_s, pool_s, exp_s, pad_s, sem):
    b = pl.program_id(0)
    nb = pl.num_programs(0)
    nt = 8
    past = n_pages * PAGE
    nblk = 2 * n_pages
    rows = NSA_KV * NSA_GQ * nt
    half = 2 * LANES

    def page_copy(seq, p, part, buf, s):
        return pltpu.make_async_copy(
            cache_ref.at[layer, pt_ref[seq, p], pl.ds(part * half, half), :],
            buf.at[:, pl.ds(pl.multiple_of(p * PAGE, PAGE), PAGE)], s)

    def start_gather(seq, part, buf, s):
        def body(p, c):
            page_copy(seq, p, part, buf, s).start()
            return c
        lax.fori_loop(0, n_pages, body, 0)

    def wait_gather(seq, part, buf, s):
        def body(p, c):
            page_copy(seq, p, part, buf, s).wait()
            return c
        lax.fori_loop(0, n_pages, body, 0)

    @pl.when(b == 0)
    def _():
        start_gather(0, 0, cmp_s, sem.at[0])
        start_gather(0, 1, slc_s.at[0], sem.at[1])
        cb = lax.broadcasted_iota(jnp.int32, (past, nblk), 0) >> 5
        j2 = 2 * lax.broadcasted_iota(jnp.int32, (past, nblk), 1)
        pool_s[0] = _bf(jnp.where(cb == j2, 1.0, 0.0))
        pool_s[1] = _bf(jnp.where(cb == j2 + 1, 1.0, 0.0))
        ej = lax.broadcasted_iota(jnp.int32, (nblk, past), 0)
        ec = lax.broadcasted_iota(jnp.int32, (nblk, past), 1) >> 6
        exp_s[...] = _bf(jnp.where(ej == ec, 1.0, 0.0))
        pad_s[...] = jnp.zeros(pad_s.shape, F32)

    slot = b % 2
    slc = slc_s.at[slot]

    @pl.when(b + 1 < nb)
    def _():
        start_gather(b + 1, 1, slc_s.at[1 - slot], sem.at[2 - slot])

    wait_gather(b, 0, cmp_s, sem.at[0])
    wait_gather(b, 1, slc, sem.at[1 + slot])

    q_all = q_ref[0] * (NSA_DH ** -0.5)
    qq = _bf(jnp.concatenate([_heads_to_rows(q_all, g, nt) for g in range(NSA_KV)], axis=0))
    tr = lax.broadcasted_iota(jnp.int32, (rows, 1), 0) & (nt - 1)

    wp = wpool_ref[...]
    ks = _bf(cmp_s[0:LANES, :] * wp[0:1, :])
    vs = _bf(cmp_s[LANES:half, :] * wp[1:2, :])
    kce = _bf(_dot(ks, pool_s[0]))
    kco = _bf(_dot(ks, pool_s[1]))
    vce = _bf(_dot(vs, pool_s[0]))
    vco = _bf(_dot(vs, pool_s[1]))

    @pl.when(b + 1 < nb)
    def _():
        start_gather(b + 1, 0, cmp_s, sem.at[0])

    new = rows_ref[0]
    wnew = wnew_ref[0]
    pad_s[0, 0:nt, :] = new[:, 2 * LANES:3 * LANES]
    pad_s[1, 0:nt, :] = new[:, 3 * LANES:4 * LANES]
    pad_s[2, 0:nt, :] = wnew[:, 0:LANES]
    pad_s[3, 0:nt, :] = wnew[:, LANES:2 * LANES]

    wb = wbuf_ref[0, 0]
    s_all = _dot(qq, _bf(slc[0:LANES, :]))
    s_new = _dot_nt(qq, _bf(pad_s[0])) + bnew_ref[...]
    s_win = _dot(qq, _bf(wb[0:LANES, :])) + bwin_ref[...]
    s_wnew = _dot_nt(qq, _bf(pad_s[2])) + bwnew_ref[...]

    se = _dot(qq, kce) + bce_ref[...]
    so = _dot(qq, kco) + bco_ref[...]
    mx = jnp.maximum(jnp.max(se, axis=1, keepdims=True), jnp.max(so, axis=1, keepdims=True))
    ee = jnp.exp(se - mx)
    eo = jnp.exp(so - mx)
    inv = 1.0 / (jnp.sum(ee, axis=1, keepdims=True) + jnp.sum(eo, axis=1, keepdims=True))
    pe = ee * inv
    po = eo * inv
    o_c = _dot_nt(_bf(pe), vce) + _dot_nt(_bf(po), vco)

    def head_sum(pr):
        return jnp.concatenate(
            [pr[g * 4 * nt:g * 4 * nt + nt] + pr[g * 4 * nt + nt:g * 4 * nt + 2 * nt]
             + pr[g * 4 * nt + 2 * nt:g * 4 * nt + 3 * nt] + pr[g * 4 * nt + 3 * nt:g * 4 * nt + 4 * nt]
             for g in range(NSA_KV)], axis=0)

    jcol = lax.broadcasted_iota(jnp.int32, (NSA_KV * nt, nblk), 1)
    forced = (jcol == 0) | (jcol == nblk - 1)
    score = (head_sum(pe) + head_sum(po)) + jnp.where(forced, FORCE_BONUS, 0.0)
    rank = jnp.where(FORCE_BONUS > score, 1.0, 0.0)
    for j in range(nblk):
        sj = score[:, j:j + 1]
        ahead = (sj > score) | ((sj == score) & (jcol > j))
        rank = rank + jnp.where(ahead, 1.0, 0.0)
    sel = jnp.where(rank < float(N_SEL), 1.0, 0.0)
    sel_rows = jnp.concatenate([sel[g * nt:(g + 1) * nt] for g in range(NSA_KV) for _ in range(NSA_GQ)], axis=0)

    tc = lax.broadcasted_iota(jnp.int32, (rows, LANES), 1)
    mnew = (tc <= tr) & (tc < n_new)

    mk = _dot(_bf(sel_rows), exp_s[...]) > 0.5
    far = past - PAGE
    p_far, p_last, p_new = _masked_softmax_parts(
        [s_all[:, :far], s_all[:, far:] + blast_ref[...], s_new], [mk[:, :far], mk[:, far:], mnew], 1)
    o_s = _dot_nt(_bf(jnp.concatenate([p_far, p_last], axis=1)), _bf(slc[LANES:half, :])) \
        + _dot(_bf(p_new), _bf(pad_s[1]))

    cw = lax.broadcasted_iota(jnp.int32, (rows, WINDOW), 1)
    pw, pn = _masked_softmax_parts([s_win, s_wnew], [cw > tr, mnew], 1)
    o_w = _dot_nt(_bf(pw), _bf(wb[LANES:half, :])) + _dot(_bf(pn), _bf(pad_s[3]))

    gl = gl_ref[0]
    gate_all = [_sigmoid(_pick_columns(gl, rep_ref[br])) for br in range(3)]
    gates = [jnp.concatenate([_heads_to_rows(ga, g, nt) for g in range(NSA_KV)], axis=0) for ga in gate_all]
    comb = gates[0] * o_c + gates[1] * o_s + gates[2] * o_w
    for g in range(NSA_KV):
        blocks = _rows_to_heads(comb[g * 4 * nt:(g + 1) * 4 * nt], g, nt)
        o_ref[0, :, (2 * g) * LANES:(2 * g + 1) * LANES] = blocks[0]
        o_ref[0, :, (2 * g + 1) * LANES:(2 * g + 2) * LANES] = blocks[1]


def _nsa_sample(layer, cache_t, page_table, u, wbuf_t, w_pos, tables, n_new):
    bsz, n_pages = page_table.shape
    nt = u.shape[1]
    past = n_pages * PAGE
    wpool = jnp.tile(w_pos, (1, past // L_CMP))
    rep = _gate_rep()
    full = lambda a: pl.BlockSpec(a.shape, lambda b, pt, nd=a.ndim: (0,) * nd)
    ucols = lambda width, off: pl.BlockSpec((1, nt, width), lambda b, pt: (b, 0, off // width))
    kern = functools.partial(_nsa_sample_kernel, layer, n_pages, n_new)
    grid_spec = pltpu.PrefetchScalarGridSpec(
        num_scalar_prefetch=1,
        grid=(bsz,),
        in_specs=[pl.BlockSpec(memory_space=pl.ANY),
                  ucols(512, U_Q), ucols(512, U_ROWS), ucols(2 * LANES, U_WIN),
                  pl.BlockSpec((1, 1) + wbuf_t.shape[2:], lambda b, pt: (layer, b, 0, 0)),
                  ucols(LANES, U_GL), full(rep), full(wpool)] + [full(t) for t in tables],
        out_specs=pl.BlockSpec((1, nt, 512), lambda b, pt: (b, 0, 0)),
        scratch_shapes=[pltpu.VMEM((2 * LANES, past), F32), pltpu.VMEM((2, 2 * LANES, past), F32),
                        pltpu.VMEM((2, past, 2 * n_pages), BF16), pltpu.VMEM((2 * n_pages, past), BF16),
                        pltpu.VMEM((4, LANES, LANES), F32), pltpu.SemaphoreType.DMA((3,))],
    )
    return pl.pallas_call(
        kern,
        grid_spec=grid_spec,
        out_shape=jax.ShapeDtypeStruct((bsz, nt, 512), F32),
        compiler_params=pltpu.CompilerParams(dimension_semantics=("arbitrary",), vmem_limit_bytes=BIG_VMEM_LIMIT),
        name="nsa_sample",
    )(page_table, cache_t, u, u, u, wbuf_t, u, rep, wpool, *tables)


def _mixout_kernel(x_ref, g_ref, ca_ref, ob_ref, oc_ref, wg_ref, wpa_ref, wpb_ref, wpc_ref, wo_ref, o_ref):
    x = x_ref[...]
    h = _bf(x * lax.rsqrt(jnp.mean(x * x, axis=-1, keepdims=True) + EPS) * g_ref[...])
    gate = lambda k: _sigmoid(_dot(h, wg_ref[:, k * D_MODEL:(k + 1) * D_MODEL]))
    y = gate(0) * _dot(_bf(ca_ref[...]), wpa_ref[...])
    y = y + gate(1) * _dot(_bf(ob_ref[...]), wpb_ref[...])
    y = y + gate(2) * _dot(_bf(oc_ref[...]), wpc_ref[...])
    o_ref[...] = x + _dot(_bf(y), wo_ref[...])


def _mixout(x, g, ca, ob, oc, wg, wpa, wpb, wpc, wo):
    m = x.shape[0]
    tm = min(m, 512)
    rowblk = lambda n: pl.BlockSpec((tm, n), lambda i: (i, 0))
    full = lambda a: pl.BlockSpec(a.shape, lambda i: (0, 0))
    return pl.pallas_call(
        _mixout_kernel,
        grid=(m // tm,),
        in_specs=[rowblk(D_MODEL), pl.BlockSpec((1, D_MODEL), lambda i: (0, 0)),
                  rowblk(512), rowblk(512), rowblk(512), full(wg), full(wpa), full(wpb), full(wpc), full(wo)],
        out_specs=rowblk(D_MODEL),
        out_shape=jax.ShapeDtypeStruct((m, D_MODEL), F32),
        compiler_params=_params("parallel"),
        name="mixer_out",
    )(x, g.reshape(1, D_MODEL), ca, ob, oc, wg, wpa, wpb, wpc, wo)


def _xattn_kernel(x_ref, g_ref, kv_ref, wq_ref, wo_ref, o_ref):
    nb, tt, d = x_ref.shape
    x = x_ref[...].reshape(nb * tt, d)
    h = _bf(x * lax.rsqrt(jnp.mean(x * x, axis=-1, keepdims=True) + EPS) * g_ref[...])
    q = _dot(h, wq_ref[...])

    def mem_head(sq, which, hd):
        return jnp.concatenate(
            [kv_ref[0, sq, pl.ds(which * 8 + half * X_HEADS + hd, N_MEM, stride=MEM_ROWS), :]
             for half in range(2)], axis=1)

    seqs = []
    for sq in range(nb):
        outs = []
        for hd in range(X_HEADS):
            qh = _bf(q[sq * tt:(sq + 1) * tt, hd * X_DH:(hd + 1) * X_DH])
            kh = _bf(mem_head(sq, 0, hd))
            vh = _bf(mem_head(sq, 1, hd))
            s = _dot_nt(qh, kh) * (X_DH ** -0.5)
            e = jnp.exp(s - jnp.max(s, axis=-1, keepdims=True))
            pr = e * (1.0 / jnp.sum(e, axis=-1, keepdims=True))
            outs.append(_dot(_bf(pr), vh))
        seqs.append(jnp.concatenate(outs, axis=1))
    o = seqs[0] if nb == 1 else jnp.concatenate(seqs, axis=0)
    o_ref[...] = (x + _dot(_bf(o), wo_ref[...])).reshape(nb, tt, d)


def _cross_attn(x, g, mem_kv, layer, wq, wo):
    bsz, t_len, d = x.shape
    tt = min(t_len, 512)
    nb = 4 if (t_len <= 8 and bsz % 4 == 0) else 1
    full = lambda a: pl.BlockSpec(a.shape, lambda b, t: (0, 0))
    return pl.pallas_call(
        _xattn_kernel,
        grid=(bsz // nb, t_len // tt),
        in_specs=[pl.BlockSpec((nb, tt, d), lambda b, t: (b, t, 0)),
                  pl.BlockSpec((1, d), lambda b, t: (0, 0)),
                  pl.BlockSpec((1, nb) + mem_kv.shape[2:], lambda b, t: (layer, b, 0, 0)),
                  full(wq), full(wo)],
        out_specs=pl.BlockSpec((nb, tt, d), lambda b, t: (b, t, 0)),
        out_shape=jax.ShapeDtypeStruct((bsz, t_len, d), F32),
        compiler_params=_params("parallel", "parallel"),
        name="cross_attn",
    )(x, g.reshape(1, d), mem_kv, wq, wo)


FF_CHUNK = 1024


def _mlp_kernel(n_k, final_norm, x_ref, g_ref, gf_ref, w1_ref, w2_ref, o_ref, h_s, acc_s):
    k = pl.program_id(1)

    @pl.when(k == 0)
    def _():
        x = x_ref[...]
        h_s[...] = _bf(x * lax.rsqrt(jnp.mean(x * x, axis=-1, keepdims=True) + EPS) * g_ref[...])
        acc_s[...] = x

    a = jnp.maximum(_dot(h_s[...], w1_ref[...]), 0.0)
    acc_s[...] += _dot(_bf(a * a), w2_ref[...])

    @pl.when(k == n_k - 1)
    def _():
        y = acc_s[...]
        if final_norm:
            y = y * lax.rsqrt(jnp.mean(y * y, axis=-1, keepdims=True) + EPS) * gf_ref[...]
        o_ref[...] = y


def _mlp(x, g, w1, w2, final_g=None):
    m, d = x.shape
    tm = min(m, 1024)
    n_k = D_FF // FF_CHUNK
    gf = g if final_g is None else final_g
    return pl.pallas_call(
        functools.partial(_mlp_kernel, n_k, final_g is not None),
        grid=(m // tm, n_k),
        in_specs=[pl.BlockSpec((tm, d), lambda i, k: (i, 0)),
                  pl.BlockSpec((1, d), lambda i, k: (0, 0)),
                  pl.BlockSpec((1, d), lambda i, k: (0, 0)),
                  pl.BlockSpec((d, FF_CHUNK), lambda i, k: (0, k)),
                  pl.BlockSpec((FF_CHUNK, d), lambda i, k: (k, 0))],
        out_specs=pl.BlockSpec((tm, d), lambda i, k: (i, 0)),
        out_shape=jax.ShapeDtypeStruct((m, d), F32),
        scratch_shapes=[pltpu.VMEM((tm, d), BF16), pltpu.VMEM((tm, d), F32)],
        compiler_params=_params("parallel", "arbitrary"),
        name="sq_relu_mlp",
    )(x, g.reshape(1, d), gf.reshape(1, d), w1, w2)


A_COLS = 2 * CONV_CH
B0 = A_COLS
Z0 = B0 + GDN_QKV
AB0 = Z0 + GDN_HEADS * GDN_D
C0 = AB0 + 2 * GDN_HEADS
KV0 = C0 + NSA_HEADS * NSA_DH
GL0 = KV0 + 6 * NSA_KV * NSA_DH
G0 = GL0 + 3 * NSA_HEADS
N_IN = G0 + 3 * D_MODEL


def _layer_weights(l, w_in, w_pa, w_pb, w_pc, w_o, w_xq, w_xk, w_xv, w_xo, w_ff1, w_ff2):
    w = w_in[l]
    lane_pad = lambda cols: jnp.pad(cols, ((0, 0), (0, LANES - cols.shape[1])))
    groups = [(U_A, w[:, 0:A_COLS]), (U_Q, w[:, C0:KV0]), (U_ROWS, w[:, KV0:KV0 + 4 * LANES]),
              (U_QKV, w[:, B0:Z0]), (U_Z, w[:, Z0:AB0]),
              (U_WIN, w[:, KV0 + 4 * LANES:GL0]), (U_AB, lane_pad(w[:, AB0:C0])), (U_GL, lane_pad(w[:, GL0:G0]))]
    off = 0
    for start, cols in groups:
        assert start == off
        off += cols.shape[1]
    assert off == U_N
    return {
        "in": _bf(jnp.concatenate([cols for _, cols in groups], axis=1)),
        "g": _bf(w[:, G0:N_IN]),
        "pa": _bf(w_pa[l]), "pb": _bf(w_pb[l]), "pc": _bf(w_pc[l]), "o": _bf(w_o[l]),
        "xq": _bf(w_xq[l]), "xo": _bf(w_xo[l]),
        "xkv": _bf(jnp.concatenate([w_xk[l], w_xv[l]], axis=1)),
        "ff1": _bf(w_ff1[l]), "ff2": _bf(w_ff2[l]),
    }


def _mixers(x, lw, p, l, conv_state_pad, qkv_state_pad, s0, n_valid, gdn_len, nsa_fn):
    bsz, t_len, d = x.shape
    m = bsz * t_len
    x2 = x.reshape(m, d)
    u2, rows, win = _proj_in(x2, p["norm_mix"][l], lw["in"])
    u = u2.reshape(bsz, t_len, U_N)
    rows = rows.reshape(bsz, t_len, 4 * LANES)
    win = win.reshape(bsz, t_len, 2 * LANES)
    qkv_tail = u[:, max(n_valid - 3, 0):n_valid, U_QKV:U_QKV + GDN_QKV]

    ca, conv_new = _conformer(u, conv_state_pad, p["conv_a_w"][l], p["conv_a_b"][l], p["ln_a_g"][l],
                              p["ln_a_b"][l], n_valid if n_valid < t_len else min(t_len, 256))
    w_conv_pad = jnp.pad(p["gdn_conv_w"][l], ((0, 4), (0, 0)))
    gdn_args = (qkv_state_pad, s0, w_conv_pad, p["gdn_a_log"][l], p["gdn_dt_bias"][l], p["gdn_norm_g"][l])
    if gdn_len == t_len:
        ob, s_new = _gated_deltanet(u, U_QKV // LANES, U_Z // LANES, U_AB // LANES, *gdn_args, gdn_len)
    else:
        ug = jnp.concatenate([u[:, :, U_QKV:U_WIN], u[:, :, U_AB:U_AB + LANES]], axis=-1)
        ug = jnp.pad(ug, ((0, 0), (0, gdn_len - t_len), (0, 0)))
        ob, s_new = _gated_deltanet(ug, 0, (U_Z - U_QKV) // LANES, (U_WIN - U_QKV) // LANES, *gdn_args, n_valid)
        ob = ob[:, :t_len]
    oc = nsa_fn(u)
    x_new = _mixout(x2, p["norm_mix"][l], ca.reshape(m, -1), ob.reshape(m, -1), oc.reshape(m, -1),
                    lw["g"], lw["pa"], lw["pb"], lw["pc"], lw["o"])
    return x_new.reshape(bsz, t_len, d), conv_new[:, HALO - (CONV_W - 1):], qkv_tail, s_new, rows, win


def kernel(x_prompt, x_sample, cache_nsa_kv, cache_win_kv, state_conv_a, state_conv_qkv, state_gdn, cache_mem_kv,
           page_table, mem_prompt, rel_bias, norm_mix, w_in, conv_a_w, conv_a_b, ln_a_g, ln_a_b, w_pa, gdn_conv_w,
           gdn_a_log, gdn_dt_bias, gdn_norm_g, w_pb, nsa_cmp_w, w_pc, w_o, norm_x, w_xq, w_xk, w_xv, w_xo,
           norm_mlp, w_ff1, w_ff2, norm_final):
    p = {"norm_mix": norm_mix, "conv_a_w": conv_a_w, "conv_a_b": conv_a_b, "ln_a_g": ln_a_g, "ln_a_b": ln_a_b,
         "gdn_conv_w": gdn_conv_w, "gdn_a_log": gdn_a_log, "gdn_dt_bias": gdn_dt_bias, "gdn_norm_g": gdn_norm_g}
    depth = w_in.shape[0]
    bp, tp, d = x_prompt.shape
    bs, ts, _ = x_sample.shape
    ts_pad = 8
    n_pages = page_table.shape[1]
    wb = cache_win_kv.shape[2]
    xp = x_prompt
    xs = jnp.pad(x_sample, ((0, 0), (0, ts_pad - ts), (0, 0)))
    ptab, stab = _nsa_tables(rel_bias, tp, n_pages, ts_pad)
    cache_t = jnp.transpose(cache_nsa_kv, (0, 1, 3, 4, 5, 2)).reshape(depth, -1, 4 * LANES, PAGE)
    wbuf_t = jnp.transpose(cache_win_kv, (0, 1, 3, 4, 5, 2)).reshape(depth, bs, 2 * LANES, wb)
    mem_t = cache_mem_kv.reshape(depth, bs, N_MEM, 2, X_HEADS, 2, LANES)
    mem_t = jnp.transpose(mem_t, (0, 1, 2, 3, 5, 4, 6)).reshape(depth, bs, N_MEM * MEM_ROWS, LANES)
    outs = {k: [] for k in ("p_rows", "p_win", "p_conv", "p_qkv", "p_gdn", "p_mem",
                            "s_rows", "s_win", "s_conv", "s_qkv", "s_gdn")}
    for l in range(depth):
        lw = _layer_weights(l, w_in, w_pa, w_pb, w_pc, w_o, w_xq, w_xk, w_xv, w_xo, w_ff1, w_ff2)
        nsa_p = lambda u: _nsa_prompt(u, nsa_cmp_w[l], ptab)
        xp, conv_n, qkv_tail, s_n, rows, win = _mixers(
            xp, lw, p, l, jnp.zeros((bp, HALO, CONV_CH), F32), jnp.zeros((bp, 8, GDN_QKV), F32),
            jnp.zeros((bp, GDN_HEADS, GDN_D, GDN_D), F32), tp, tp, nsa_p)
        mem_kv = _mem_kv(_bf(mem_prompt.reshape(bp * N_MEM, d)), lw["xkv"]).reshape(1, bp, N_MEM * MEM_ROWS, LANES)
        xp = _cross_attn(xp, norm_x[l], mem_kv, 0, lw["xq"], lw["xo"])
        final_g = norm_final if l == depth - 1 else None
        xp = _mlp(xp.reshape(bp * tp, d), norm_mlp[l], lw["ff1"], lw["ff2"], final_g).reshape(bp, tp, d)
        outs["p_rows"].append(rows.reshape(bp, tp, 4, NSA_KV, NSA_DH))
        outs["p_win"].append(win[:, tp - min(WINDOW, tp):].reshape(bp, min(WINDOW, tp), 2, NSA_KV, NSA_DH))
        outs["p_conv"].append(conv_n)
        outs["p_qkv"].append(qkv_tail)
        outs["p_gdn"].append(s_n)
        mem_out = mem_kv.reshape(bp, N_MEM, 2, 2, X_HEADS, LANES)
        outs["p_mem"].append(jnp.swapaxes(mem_out, 3, 4).reshape(bp, N_MEM, 2, X_HEADS, X_DH))
        nsa_s = lambda u: _nsa_sample(l, cache_t, page_table, u, wbuf_t, nsa_cmp_w[l], stab, ts)
        conv_pad = jnp.pad(state_conv_a[l], ((0, 0), (HALO - (CONV_W - 1), 0), (0, 0)))
        qkv_pad = jnp.pad(state_conv_qkv[l], ((0, 0), (5, 0), (0, 0)))
        xs, conv_n, qkv_tail, s_n, rows, win = _mixers(
            xs, lw, p, l, conv_pad, qkv_pad, state_gdn[l], ts, GDN_CHUNK, nsa_s)
        xs = _cross_attn(xs, norm_x[l], mem_t, l, lw["xq"], lw["xo"])
        xs = _mlp(xs.reshape(bs * ts_pad, d), norm_mlp[l], lw["ff1"], lw["ff2"], final_g).reshape(bs, ts_pad, d)
        outs["s_rows"].append(rows[:, :ts].reshape(bs, ts, 4, NSA_KV, NSA_DH))
        win_new = win[:, :ts].reshape(bs, ts, 2, NSA_KV, NSA_DH)
        outs["s_win"].append(jnp.concatenate([cache_win_kv[l], win_new], axis=1)[:, ts:])
        outs["s_conv"].append(conv_n)
        outs["s_qkv"].append(qkv_tail)
        outs["s_gdn"].append(s_n)
    st = lambda k: jnp.stack(outs[k], axis=0)
    return (xp, xs[:, :ts], st("p_rows"), st("p_win"), st("p_conv"), st("p_qkv"), st("p_gdn"), st("p_mem"),
            st("s_rows"), st("s_win"), st("s_conv"), st("s_qkv"), st("s_gdn"))
```

```python
import functools
import math

import jax
import jax.numpy as jnp
import numpy as np
from jax import lax
from jax.experimental import pallas as pl
from jax.experimental.pallas import tpu as pltpu

F32 = jnp.float32
BF16 = jnp.bfloat16

D_MODEL = 1024
CONV_CH = 512
CONV_W = 31
GDN_HEADS = 4
GDN_D = 128
GDN_CHUNK = 64
GDN_QKV = 3 * GDN_HEADS * GDN_D
NSA_HEADS = 8
NSA_KV = 2
NSA_GQ = 4
NSA_DH = 64
L_CMP = 32
L_SEL = 64
N_SEL = 16
WINDOW = 512
Q_BLOCK = 128
FORCE_BONUS = 1e4
PAGE = 128
N_MEM = 256
X_HEADS = 4
X_DH = 256
D_FF = 4096
N_BUCKETS = 32
EPS = 1e-6
NEG = -1e30

LANES = 128
HALO = 32
VMEM_LIMIT = 48 * 1024 * 1024
BIG_VMEM_LIMIT = 56 * 1024 * 1024


def _bf(x):
    return x.astype(BF16)


def _dot(a, b):
    return jnp.dot(a, b, preferred_element_type=F32)


def _dot_nt(a, b):
    return lax.dot_general(a, b, (((1,), (1,)), ((), ())), preferred_element_type=F32)


def _sigmoid(x):
    return 0.5 * jnp.tanh(0.5 * x) + 0.5


def _silu(x):
    return x * _sigmoid(x)


def _params(*sem):
    return pltpu.CompilerParams(dimension_semantics=sem, vmem_limit_bytes=VMEM_LIMIT)


MEM_ROWS = 2 * 2 * X_HEADS


def _memkv_kernel(a_ref, w_ref, o_ref):
    tm = a_ref.shape[0]
    acc = _dot(a_ref[...], w_ref[...])
    for which in range(2):
        for hd in range(X_HEADS):
            for half in range(2):
                c0 = which * D_MODEL + hd * X_DH + half * LANES
                o_ref[pl.ds(which * 8 + half * X_HEADS + hd, tm, stride=MEM_ROWS), :] = acc[:, c0:c0 + LANES]


def _mem_kv(a, w):
    m, k = a.shape
    tm = min(m, 512)
    return pl.pallas_call(
        _memkv_kernel,
        grid=(m // tm,),
        in_specs=[pl.BlockSpec((tm, k), lambda i: (i, 0)), pl.BlockSpec(w.shape, lambda i: (0, 0))],
        out_specs=pl.BlockSpec((tm * MEM_ROWS, LANES), lambda i: (i, 0)),
        out_shape=jax.ShapeDtypeStruct((m * MEM_ROWS, LANES), F32),
        compiler_params=_params("parallel"),
        name="mem_kv",
    )(a, w)


U_A = 0
U_Q = 1024
U_ROWS = 1536
U_QKV = 2048
U_Z = 3584
U_WIN = 4096
U_AB = 4352
U_GL = 4480
U_N = 4608
PROJ_TN = 1536


def _proj_kernel(x_ref, g_ref, w_ref, o_ref, rows_ref, win_ref, h_s):
    j = pl.program_id(1)

    @pl.when(j == 0)
    def _():
        x = x_ref[...]
        h_s[...] = _bf(x * lax.rsqrt(jnp.mean(x * x, axis=-1, keepdims=True) + EPS) * g_ref[...])

    acc = _dot(h_s[...], w_ref[...])
    o_ref[...] = acc

    @pl.when(j == U_ROWS // PROJ_TN)
    def _():
        rows_ref[...] = acc[:, U_ROWS % PROJ_TN:U_ROWS % PROJ_TN + 4 * LANES]

    @pl.when(j == U_WIN // PROJ_TN)
    def _():
        win_ref[...] = acc[:, U_WIN % PROJ_TN:U_WIN % PROJ_TN + 2 * LANES]


def _proj_in(x, g, w):
    m, d = x.shape
    n = w.shape[1]
    tm = min(m, 1024)
    assert U_ROWS % PROJ_TN + 4 * LANES <= PROJ_TN and U_WIN % PROJ_TN + 2 * LANES <= PROJ_TN
    return pl.pallas_call(
        _proj_kernel,
        grid=(m // tm, n // PROJ_TN),
        in_specs=[pl.BlockSpec((tm, d), lambda i, j: (i, 0)), pl.BlockSpec((1, d), lambda i, j: (0, 0)),
                  pl.BlockSpec((d, PROJ_TN), lambda i, j: (0, j))],
        out_specs=[pl.BlockSpec((tm, PROJ_TN), lambda i, j: (i, j)),
                   pl.BlockSpec((tm, 4 * LANES), lambda i, j: (i, 0)),
                   pl.BlockSpec((tm, 2 * LANES), lambda i, j: (i, 0))],
        out_shape=[jax.ShapeDtypeStruct((m, n), F32), jax.ShapeDtypeStruct((m, 4 * LANES), F32),
                   jax.ShapeDtypeStruct((m, 2 * LANES), F32)],
        scratch_shapes=[pltpu.VMEM((tm, d), BF16)],
        compiler_params=_params("parallel", "arbitrary"),
        name="proj_in",
    )(x, g.reshape(1, d), w)


def _split3(x):
    hi = _bf(x)
    r = x - hi.astype(F32)
    mid = _bf(r)
    return hi, mid, _bf(r - mid.astype(F32))


def _pick_columns(x, onehot):
    hi, mid, lo = _split3(x)
    return (_dot(lo, onehot) + _dot(mid, onehot)) + _dot(hi, onehot)


def _conf_kernel(n_t, tt, tv, u_ref, halo_ref, st_ref, w_ref, b_ref, g_ref, lb_ref, o_ref, nb_ref, xc_ref, zs_ref):
    t = pl.program_id(1)
    u = u_ref[0]
    xc_ref[HALO:HALO + tt, :] = u[:, :CONV_CH] * _sigmoid(u[:, CONV_CH:])
    if n_t > 1:
        uh = halo_ref[0]
        gh = uh[:, :CONV_CH] * _sigmoid(uh[:, CONV_CH:])
        xc_ref[0:HALO, :] = jnp.where(t > 0, gh, st_ref[0])
    else:
        xc_ref[0:HALO, :] = st_ref[0]
    off = HALO - (CONV_W - 1)
    span = tt + HALO - 8
    for r in range(1, 8):
        zs_ref[r - 1] = xc_ref[r:r + span, :]
    acc = None
    for i in range(CONV_W):
        pos = off + i
        r, base = pos % 8, pos - pos % 8
        src = xc_ref[base:base + tt, :] if r == 0 else zs_ref[r - 1, base:base + tt, :]
        term = src * w_ref[i:i + 1, :]
        acc = term if acc is None else acc + term
    y = acc + b_ref[...]
    mu = jnp.mean(y, axis=-1, keepdims=True)
    yc = y - mu
    var = jnp.mean(yc * yc, axis=-1, keepdims=True)
    ln = yc * lax.rsqrt(var + EPS) * g_ref[...] + lb_ref[...]
    o_ref[0] = _silu(ln)

    @pl.when(t == n_t - 1)
    def _():
        nb_ref[0] = xc_ref[tv:tv + HALO, :]


def _conformer(u_a, state_pad, w_dw, b_dw, ln_g, ln_b, n_valid_last):
    bsz, t_len, _ = u_a.shape
    tt = min(t_len, 256)
    n_t = t_len // tt
    hb = tt // HALO if n_t > 1 else 1
    halo_rows = HALO if n_t > 1 else tt
    w_pad = jnp.pad(w_dw, ((0, HALO - CONV_W), (0, 0)))
    row = lambda v: v.reshape(1, CONV_CH)
    kern = functools.partial(_conf_kernel, n_t, tt, n_valid_last)
    return pl.pallas_call(
        kern,
        grid=(bsz, n_t),
        in_specs=[
            pl.BlockSpec((1, tt, 2 * CONV_CH), lambda b, t: (b, t, 0)),
            pl.BlockSpec((1, halo_rows, 2 * CONV_CH), lambda b, t: (b, jnp.maximum(t * hb - 1, 0), 0)),
            pl.BlockSpec((1, HALO, CONV_CH), lambda b, t: (b, 0, 0)),
            pl.BlockSpec((HALO, CONV_CH), lambda b, t: (0, 0)),
            pl.BlockSpec((1, CONV_CH), lambda b, t: (0, 0)),
            pl.BlockSpec((1, CONV_CH), lambda b, t: (0, 0)),
            pl.BlockSpec((1, CONV_CH), lambda b, t: (0, 0)),
        ],
        out_specs=[
            pl.BlockSpec((1, tt, CONV_CH), lambda b, t: (b, t, 0)),
            pl.BlockSpec((1, HALO, CONV_CH), lambda b, t: (b, 0, 0)),
        ],
        out_shape=[
            jax.ShapeDtypeStruct((bsz, t_len, CONV_CH), F32),
            jax.ShapeDtypeStruct((bsz, HALO, CONV_CH), F32),
        ],
        scratch_shapes=[pltpu.VMEM((HALO + tt, CONV_CH), F32), pltpu.VMEM((7, tt + HALO - 8, CONV_CH), F32)],
        compiler_params=_params("parallel", "arbitrary"),
        name="conformer_conv",
    )(u_a, u_a, state_pad, w_pad, row(b_dw), row(ln_g), row(ln_b))


def _tri_inverse(a_list, ii, jj, merge_shifts):
    mm = lambda p, q: _dot(_bf(p), _bf(q))
    a0 = [jnp.where((ii >> 3) == (jj >> 3), a, 0.0) for a in a_list]
    a2 = [mm(p, p) for p in a0]
    a4 = [mm(p, p) for p in a2]
    r = [(q - p) - mm(p, q) for p, q in zip(a0, a2)]
    r = [(p + q) + mm(p, q) for p, q in zip(r, a4)]
    for sh in merge_shifts:
        mask = ((ii >> (sh + 1)) == (jj >> (sh + 1))) & ((ii >> sh) != (jj >> sh))
        off = [jnp.where(mask, a, 0.0) for a in a_list]
        t = [o + mm(o, p) for o, p in zip(off, r)]
        r = [p - (q + mm(p, q)) for p, q in zip(r, t)]
    return r


def _softplus(x):
    return jnp.maximum(x, 0.0) + jnp.log1p(jnp.exp(-jnp.abs(x)))


def _gdn_kernel(t_len, n_valid, hp, alog_ref, dtb_ref, q_ref, k_ref, v_ref, z_ref, ab_ref,
                sq_ref, sk_ref, sv_ref, wq_ref, wk_ref, wv_ref, s0_ref, ng_ref,
                o_ref, sn_ref,
                xp_s, qn_s, kn_s, vn_s, g_s, be_s, vw_s, kcd_s, qg_s, kdt_s, qk_s, ge_s):
    h0 = pl.program_id(1) * hp
    n_chunks = t_len // GDN_CHUNK
    c_len = GDN_CHUNK

    heads = range(hp)
    hcols = lambda hh: slice(hh * GDN_D, (hh + 1) * GDN_D)

    def conv(x_ref, st_ref, w_ref, hh):
        xp_s[0:8, :] = st_ref[0, :, hcols(hh)]
        xp_s[8:8 + t_len, :] = x_ref[0, :, hcols(hh)]
        acc = xp_s[5:5 + t_len, :] * w_ref[0:1, hcols(hh)]
        for i in range(1, 4):
            acc = acc + xp_s[5 + i:5 + i + t_len, :] * w_ref[i:i + 1, hcols(hh)]
        return _silu(acc)

    col = lax.broadcasted_iota(jnp.int32, (LANES, LANES), 0)
    ab = ab_ref[0]
    for hh in heads:
        h = h0 + hh
        qc = conv(q_ref, sq_ref, wq_ref, hh)
        qn_s[hh] = qc * lax.rsqrt(jnp.sum(qc * qc, axis=-1, keepdims=True) + EPS) * (GDN_D ** -0.5)
        kc = conv(k_ref, sk_ref, wk_ref, hh)
        kn_s[hh] = kc * lax.rsqrt(jnp.sum(kc * kc, axis=-1, keepdims=True) + EPS)
        vn_s[hh] = conv(v_ref, sv_ref, wv_ref, hh)
        a_rep = _pick_columns(ab, _bf(jnp.where(col == h, 1.0, 0.0)))
        b_rep = _pick_columns(ab, _bf(jnp.where(col == GDN_HEADS + h, 1.0, 0.0)))
        a_exp = jnp.exp(jnp.full((1, LANES), alog_ref[h], F32))
        g = -a_exp * _softplus(a_rep + dtb_ref[h])
        beta = _sigmoid(b_rep)
        if n_valid < t_len:
            live = lax.broadcasted_iota(jnp.int32, (t_len, LANES), 0) < n_valid
            g = jnp.where(live, g, 0.0)
            beta = jnp.where(live, beta, 0.0)
        g_s[hh] = g
        be_s[hh] = beta

    ii = lax.broadcasted_iota(jnp.int32, (c_len, c_len), 0)
    jj = lax.broadcasted_iota(jnp.int32, (c_len, c_len), 1)
    incl = ii >= jj
    strict = ii > jj
    ltri = _bf(incl.astype(F32))
    unroll = max(u for u in (1, 2, 4, 8) if n_chunks % u == 0 and u * hp <= 16)

    def cumdecay(g_c):
        g_hi, g_mid, g_lo = _split3(g_c)
        return (_dot(ltri, g_lo) + _dot(ltri, g_mid)) + _dot(ltri, g_hi)

    def prep(cu, carry):
        pairs = [(hh, cu * unroll + u) for hh in heads for u in range(unroll)]
        sls = [pl.ds(pl.multiple_of(c * c_len, c_len), c_len) for _, c in pairs]
        each = lambda f, *ls: [f(*a) for a in zip(*ls)]
        q_l = [qn_s[hh, sl, :] for (hh, _), sl in zip(pairs, sls)]
        k_l = [kn_s[hh, sl, :] for (hh, _), sl in zip(pairs, sls)]
        b_l = [be_s[hh, sl, :] for (hh, _), sl in zip(pairs, sls)]
        gc_l = [cumdecay(g_s[hh, sl, :]) for (hh, _), sl in zip(pairs, sls)]
        dec_l = each(lambda gc: jnp.where(
            incl, jnp.exp(jnp.minimum(gc[:, 0:c_len] - gc.T[0:c_len, :], 0.0)), 0.0), gc_l)
        kb_l = each(lambda k, b: k * b, k_l, b_l)
        a_l = each(lambda kb, k, dec: jnp.where(strict, _dot_nt(_bf(kb), _bf(k)) * dec, 0.0), kb_l, k_l, dec_l)
        r_l = _tri_inverse(a_l, ii, jj, () if n_valid <= 8 else (3, 4, 5))
        eg_l = each(jnp.exp, gc_l)
        rhs_l = [jnp.concatenate([vn_s[hh, sl, :] * b, kb * eg], axis=1)
                 for (hh, _), sl, b, kb, eg in zip(pairs, sls, b_l, kb_l, eg_l)]
        sol_l = each(lambda r, rhs: rhs + _dot(_bf(r), _bf(rhs)), r_l, rhs_l)
        qk_l = each(lambda q, k, dec: jnp.where(incl, _dot_nt(_bf(q), _bf(k)) * dec, 0.0), q_l, k_l, dec_l)
        for (hh, c), sl, sol, qk, q, k, gc, eg in zip(pairs, sls, sol_l, qk_l, q_l, k_l, gc_l, eg_l):
            g_end = gc[c_len - 1:c_len, :]
            vw_s[hh, sl, :] = sol[:, :GDN_D]
            kcd_s[hh, sl, :] = _bf(sol[:, GDN_D:])
            qk_s[hh, c] = _bf(qk)
            qg_s[hh, sl, :] = _bf(q * eg)
            kdt_s[hh, c] = _bf((k * jnp.exp(g_end - gc)).T)
            ge_s[hh, c] = jnp.broadcast_to(jnp.exp(g_end), (8, LANES))
        return carry

    lax.fori_loop(0, n_chunks // unroll, prep, 0)

    def step(c, states):
        sl = pl.ds(pl.multiple_of(c * c_len, c_len), c_len)
        sb = [_bf(s) for s in states]
        v_new = [vw_s[hh, sl, :] - _dot(kcd_s[hh, sl, :], sb[hh]) for hh in heads]
        vb = [_bf(v) for v in v_new]
        for hh in heads:
            o_ref[0, sl, hcols(hh)] = _dot(qg_s[hh, sl, :], sb[hh]) + _dot(qk_s[hh, c], vb[hh])
        return tuple(states[hh] * ge_s[hh, c][0:1, :] + _dot(kdt_s[hh, c], vb[hh]) for hh in heads)

    s_fin = lax.fori_loop(0, n_chunks, step, tuple(s0_ref[0, hh] for hh in heads))
    for hh in heads:
        sn_ref[0, hh] = s_fin[hh]
        o = o_ref[0, :, hcols(hh)]
        y = o * lax.rsqrt(jnp.mean(o * o, axis=-1, keepdims=True) + EPS) * ng_ref[...]
        o_ref[0, :, hcols(hh)] = y * _silu(z_ref[0, :, hcols(hh)])


def _gated_deltanet(u, qkv_blk, z_blk, ab_blk, state_pad, s0, w_conv_pad, a_log, dt_bias, norm_g, n_valid):
    bsz, t_len, _ = u.shape
    nh = GDN_HEADS
    hp = nh if t_len <= 4 * GDN_CHUNK else 2
    wide = hp * GDN_D
    assert (qkv_blk * GDN_D) % wide == 0 and (z_blk * GDN_D) % wide == 0
    ublk = lambda blk: pl.BlockSpec((1, t_len, wide), lambda b, j, o=blk * GDN_D // wide: (b, 0, o + j))
    stb = lambda off: pl.BlockSpec((1, 8, wide), lambda b, j, o=off * GDN_D // wide: (b, 0, o + j))
    wb = lambda off: pl.BlockSpec((8, wide), lambda b, j, o=off * GDN_D // wide: (0, o + j))
    smem = pl.BlockSpec(memory_space=pltpu.SMEM)
    n_chunks = t_len // GDN_CHUNK
    seq = lambda dt: pltpu.VMEM((hp, t_len, GDN_D), dt)
    kern = functools.partial(_gdn_kernel, t_len, n_valid, hp)
    return pl.pallas_call(
        kern,
        grid=(bsz, nh // hp),
        in_specs=[smem, smem, ublk(qkv_blk), ublk(qkv_blk + nh), ublk(qkv_blk + 2 * nh), ublk(z_blk),
                  pl.BlockSpec((1, t_len, GDN_D), lambda b, j: (b, 0, ab_blk)),
                  stb(0), stb(nh), stb(2 * nh), wb(0), wb(nh), wb(2 * nh),
                  pl.BlockSpec((1, hp, GDN_D, GDN_D), lambda b, j: (b, j, 0, 0)),
                  pl.BlockSpec((1, GDN_D), lambda b, j: (0, 0))],
        out_specs=[pl.BlockSpec((1, t_len, wide), lambda b, j: (b, 0, j)),
                   pl.BlockSpec((1, hp, GDN_D, GDN_D), lambda b, j: (b, j, 0, 0))],
        out_shape=[jax.ShapeDtypeStruct((bsz, t_len, nh * GDN_D), F32),
                   jax.ShapeDtypeStruct((bsz, nh, GDN_D, GDN_D), F32)],
        scratch_shapes=[pltpu.VMEM((8 + t_len, GDN_D), F32), seq(F32), seq(F32), seq(F32), seq(F32), seq(F32),
                        seq(F32), seq(BF16), seq(BF16),
                        pltpu.VMEM((hp, n_chunks, GDN_D, GDN_CHUNK), BF16),
                        pltpu.VMEM((hp, n_chunks, GDN_CHUNK, GDN_CHUNK), BF16),
                        pltpu.VMEM((hp, n_chunks, 8, LANES), F32)],
        compiler_params=pltpu.CompilerParams(dimension_semantics=("parallel", "parallel"),
                                             vmem_limit_bytes=BIG_VMEM_LIMIT),
        name="gated_deltanet",
    )(a_log, dt_bias, u, u, u, u, u, state_pad, state_pad, state_pad,
      w_conv_pad, w_conv_pad, w_conv_pad, s0, norm_g.reshape(1, GDN_D))


def _heads_to_rows(x, g, nt):
    lane = lax.broadcasted_iota(jnp.int32, (nt, LANES), 1)
    keep = (lane >= NSA_DH * g) & (lane < NSA_DH * (g + 1))
    parts = []
    for r in range(NSA_GQ):
        hh = NSA_GQ * g + r
        blk = x[:, (hh // 2) * LANES:(hh // 2 + 1) * LANES]
        if hh % 2 != g:
            blk = pltpu.roll(blk, NSA_DH, axis=1)
        parts.append(jnp.where(keep, blk, 0.0))
    return jnp.concatenate(parts, axis=0)


def _rows_to_heads(y, g, nt):
    outs = []
    for m in range(2):
        x0 = y[(2 * m) * nt:(2 * m + 1) * nt]
        x1 = y[(2 * m + 1) * nt:(2 * m + 2) * nt]
        if g == 1:
            x0 = pltpu.roll(x0, NSA_DH, axis=1)
        else:
            x1 = pltpu.roll(x1, NSA_DH, axis=1)
        outs.append(x0 + x1)
    return outs


def _masked_softmax_parts(parts, masks, axis):
    sm = [jnp.where(m, s, NEG) for s, m in zip(parts, masks)]
    mx = functools.reduce(jnp.maximum, [jnp.max(s, axis=axis, keepdims=True) for s in sm])
    es = [jnp.where(m, jnp.exp(s - mx), 0.0) for s, m in zip(sm, masks)]
    den = functools.reduce(lambda p, q: p + q, [jnp.sum(e, axis=axis, keepdims=True) for e in es])
    inv = 1.0 / jnp.maximum(den, 1e-30)
    return [e * inv for e in es]


def _bucket_np(rel):
    n = np.maximum(rel, 0)
    nf = np.maximum(n, 1).astype(np.float32)
    large = 16 + (np.log(nf / np.float32(16)) / np.float32(math.log(8.0)) * np.float32(16)).astype(np.int32)
    return np.where(n < 16, n, np.minimum(large, N_BUCKETS - 1)).astype(np.int32)


LOOKUP_TILE = 8192


def _lookup_kernel(idx_ref, tb_ref, o_ref):
    idx = idx_ref[...]
    acc = jnp.zeros(o_ref.shape, F32)
    for k in range(N_BUCKETS):
        acc = jnp.where(idx == k, tb_ref[:, k:k + 1], acc)
    o_ref[...] = acc


def _bias_lookup(rel_bias, idx_list):
    sizes = [int(np.prod(a.shape)) for a in idx_list]
    total = sum(sizes)
    padded = -(-total // LOOKUP_TILE) * LOOKUP_TILE
    flat = np.zeros((1, padded), np.int32)
    flat[0, :total] = np.concatenate([np.asarray(a, np.int32).reshape(-1) for a in idx_list])
    tab = pl.pallas_call(
        _lookup_kernel,
        grid=(padded // LOOKUP_TILE,),
        in_specs=[pl.BlockSpec((1, LOOKUP_TILE), lambda i: (0, i)),
                  pl.BlockSpec((NSA_HEADS, N_BUCKETS), lambda i: (0, 0))],
        out_specs=pl.BlockSpec((NSA_HEADS, LOOKUP_TILE), lambda i: (0, i)),
        out_shape=jax.ShapeDtypeStruct((NSA_HEADS, padded), F32),
        compiler_params=_params("parallel"),
        name="bias_lookup",
    )(jnp.asarray(flat), rel_bias.astype(F32).T)
    outs, off = [], 0
    for a, n in zip(idx_list, sizes):
        outs.append(tab[:, off:off + n].reshape((NSA_HEADS,) + tuple(a.shape)))
        off += n
    return outs


def _head_rows(tab):
    return tab.reshape(NSA_KV, NSA_GQ * tab.shape[1], tab.shape[2])


FAR_TILE = 512


def _nsa_prompt_kernel(t_len, q_ref, kcmp_ref, vcmp_ref, kslc_ref, vslc_ref, kwin_ref, vwin_ref, gl_ref, rep_ref,
                       wk_ref, wv_ref, bc_ref, bct_ref, bnear_ref, bwin_ref, o_ref, kc_s, vc_s):
    i = pl.program_id(1)
    nsb = t_len // L_SEL
    qb = Q_BLOCK
    rows = NSA_GQ * qb

    @pl.when(i == 0)
    def _():
        n2 = 2 * lax.broadcasted_iota(jnp.int32, (nsb, t_len), 0)
        cb = lax.broadcasted_iota(jnp.int32, (nsb, t_len), 1) >> 5
        kc = _bf(kcmp_ref[0])
        vc = _bf(vcmp_ref[0])
        wk = wk_ref[...]
        wv = wv_ref[...]
        kc_s[0:nsb, :] = _dot(_bf(jnp.where(cb == n2, wk, 0.0)), kc)
        kc_s[nsb:2 * nsb, :] = _dot(_bf(jnp.where(cb == n2 + 1, wk, 0.0)), kc)
        vc_s[0:nsb, :] = _dot(_bf(jnp.where(cb == n2, wv, 0.0)), vc)
        vc_s[nsb:2 * nsb, :] = _dot(_bf(jnp.where(cb == n2 + 1, wv, 0.0)), vc)

    q_all = q_ref[0] * (NSA_DH ** -0.5)
    gl = gl_ref[0]
    gate_all = [_sigmoid(_pick_columns(gl, rep_ref[br])) for br in range(3)]
    t0 = i * qb
    tq = t0 + (lax.broadcasted_iota(jnp.int32, (rows, 1), 0) & (qb - 1))
    tl = t0 + (lax.broadcasted_iota(jnp.int32, (1, rows), 1) & (qb - 1))
    eye_q = _bf((lax.broadcasted_iota(jnp.int32, (qb, qb), 0) == lax.broadcasted_iota(jnp.int32, (qb, qb), 1)).astype(F32))
    far_end = jnp.maximum(t0 - qb, 0)
    n_far = (far_end + FAR_TILE - 1) // FAR_TILE
    kc = _bf(kc_s[...])
    vc = _bf(vc_s[...])

    def key_aug(k0, n_keys, limit):
        kpos = k0 + lax.broadcasted_iota(jnp.int32, (n_keys, LANES), 0)
        lane = lax.broadcasted_iota(jnp.int32, (n_keys, LANES), 1)
        hit = (lane == (kpos >> 6)) | ((lane == nsb) & ((kpos >= limit) | (kpos < 0)))
        return _bf(jnp.where(hit, 2.0 * NEG, 0.0))

    gs = range(NSA_KV)
    qg = [_bf(_heads_to_rows(q_all, g, qb)) for g in gs]

    n_prev = WINDOW // qb
    starts = [pl.multiple_of(jnp.maximum(t0 + (j - n_prev) * qb, 0), qb) for j in range(n_prev + 1)]
    kwin = [_bf(kwin_ref[0, pl.ds(st, qb), :]) for st in starts]
    pens = [jnp.where(i + (j - n_prev) >= 0, 0.0, NEG) for j in range(n_prev)] + [0.0]
    s_w = [jnp.concatenate([_dot_nt(qg[g], kwin[j]) + pens[j] for j in range(n_prev + 1)], axis=1) + bwin_ref[g]
           for g in gs]

    s_c = [_dot_nt(qg[g], kc) + bc_ref[0, g] for g in gs]
    s_t = [_dot_nt(kc, qg[g]) + bct_ref[0, g] for g in gs]
    e_c = [jnp.where(tq >= L_CMP - 1, jnp.exp(s - jnp.max(s, axis=1, keepdims=True)), 0.0) for s in s_c]
    p_c = [e * (1.0 / jnp.maximum(jnp.sum(e, axis=1, keepdims=True), 1e-30)) for e in e_c]
    o_c = [_dot(_bf(p), vc) for p in p_c]

    e_t = [jnp.where(tl >= L_CMP - 1, jnp.exp(s - jnp.max(s, axis=0, keepdims=True)), 0.0) for s in s_t]
    p_t = [e * (1.0 / jnp.maximum(jnp.sum(e, axis=0, keepdims=True), 1e-30)) for e in e_t]
    head_sum = lambda x: x[:, 0:qb] + x[:, qb:2 * qb] + x[:, 2 * qb:3 * qb] + x[:, 3 * qb:4 * qb]
    blk = lax.broadcasted_iota(jnp.int32, (nsb, qb), 0)
    cur = (t0 + lax.broadcasted_iota(jnp.int32, (nsb, qb), 1)) >> 6
    bonus = jnp.where((blk == 0) | (blk == cur) | (blk == cur - 1), FORCE_BONUS, 0.0)
    score = [jnp.where(blk <= cur, (head_sum(p[0:nsb]) + head_sum(p[nsb:2 * nsb])) + bonus, -1.0) for p in p_t]
    rank = [jnp.zeros((nsb, qb), F32) for _ in gs]
    for j in range(nsb):
        for g in gs:
            sj = score[g][j:j + 1, :]
            ahead = (sj > score[g]) | ((sj == score[g]) & (blk > j))
            rank[g] = rank[g] + jnp.where(ahead, 1.0, 0.0)
    pen_rows = jnp.where(lax.broadcasted_iota(jnp.int32, (LANES - nsb, qb), 0) == 0, 1.0, 0.0)
    not_sel_t = [jnp.where(r < float(min(N_SEL, nsb)), 0.0, 1.0) for r in rank]
    q_aug = [_bf(_dot_nt(eye_q, _bf(jnp.concatenate([ns, pen_rows], axis=0)))) for ns in not_sel_t]
    qa = [jnp.concatenate([qg[g], jnp.concatenate([q_aug[g]] * NSA_GQ, axis=0)], axis=1) for g in gs]

    def online(carry, s, pv):
        m_i, l_i, acc = carry
        m_n = jnp.maximum(m_i, jnp.max(s, axis=1, keepdims=True))
        p = jnp.exp(s - m_n)
        alpha = jnp.exp(m_i - m_n)
        return m_n, alpha * l_i + jnp.sum(p, axis=1, keepdims=True), alpha * acc + pv(_bf(p))

    def far_tile(kt, carry):
        k0 = pl.multiple_of(kt * FAR_TILE, FAR_TILE)
        ka = jnp.concatenate([_bf(kslc_ref[0, pl.ds(k0, FAR_TILE), :]), key_aug(k0, FAR_TILE, far_end)], axis=1)
        vt = _bf(vslc_ref[0, pl.ds(k0, FAR_TILE), :])
        s = [_dot_nt(qa[g], ka) for g in gs]
        return tuple(online(carry[g], s[g], lambda p: _dot(p, vt)) for g in gs)

    init = (jnp.full((rows, 1), NEG, F32), jnp.zeros((rows, 1), F32), jnp.zeros((rows, LANES), F32))
    far = lax.fori_loop(0, n_far, far_tile, tuple(init for _ in gs))

    p0 = pl.multiple_of(jnp.maximum(t0 - qb, 0), qb)
    d0 = pl.multiple_of(t0, qb)
    ka = jnp.concatenate([
        jnp.concatenate([_bf(kslc_ref[0, pl.ds(p0, qb), :]), _bf(kslc_ref[0, pl.ds(d0, qb), :])], axis=0),
        key_aug(t0 - qb, 2 * qb, t_len)], axis=1)
    vp = _bf(vslc_ref[0, pl.ds(p0, qb), :])
    vd = _bf(vslc_ref[0, pl.ds(d0, qb), :])
    s_near = [_dot_nt(qa[g], ka) + bnear_ref[g] for g in gs]
    fin = [online(far[g], s_near[g], lambda p: _dot(p[:, 0:qb], vp) + _dot(p[:, qb:2 * qb], vd)) for g in gs]
    o_s = [acc * (1.0 / l_n) for _, l_n, acc in fin]

    vwin = [_bf(vwin_ref[0, pl.ds(st, qb), :]) for st in starts]
    e_w = [jnp.exp(s - jnp.max(s, axis=1, keepdims=True)) for s in s_w]
    o_w = []
    for g in gs:
        ew = _bf(e_w[g])
        acc = _dot(ew[:, 0:qb], vwin[0])
        for j in range(1, n_prev + 1):
            acc = acc + _dot(ew[:, j * qb:(j + 1) * qb], vwin[j])
        o_w.append(acc * (1.0 / jnp.sum(e_w[g], axis=1, keepdims=True)))

    for g in gs:
        gates = [_heads_to_rows(gate_all[br], g, qb) for br in range(3)]
        comb = gates[0] * o_c[g] + gates[1] * o_s[g] + gates[2] * o_w[g]
        blocks = _rows_to_heads(comb, g, qb)
        o_ref[0, :, (2 * g) * LANES:(2 * g + 1) * LANES] = blocks[0]
        o_ref[0, :, (2 * g + 1) * LANES:(2 * g + 2) * LANES] = blocks[1]


def _nsa_tables(rel_bias, t_len, n_pages, nt):
    nqb = t_len // Q_BLOCK
    nsb = t_len // L_SEL
    past = n_pages * PAGE
    t = np.arange(Q_BLOCK)
    tq = (np.arange(nqb)[:, None] * Q_BLOCK + t[None, :])[:, :, None]
    n = np.arange(nsb)[None, None, :]
    ts = np.arange(nt)[:, None]
    j = np.arange(2 * n_pages)[None, :]
    c = np.arange(PAGE)[None, :]
    idx = [
        _bucket_np(tq - (n * L_SEL + L_CMP - 1)),
        _bucket_np(tq - (n * L_SEL + L_SEL - 1)),
        _bucket_np(Q_BLOCK + t[:, None] - np.arange(2 * Q_BLOCK)[None, :]),
        _bucket_np(WINDOW + t[:, None] - np.arange(WINDOW + Q_BLOCK)[None, :]),
        _bucket_np(past + ts - (j * L_SEL + L_CMP - 1)),
        _bucket_np(past + ts - (j * L_SEL + L_SEL - 1)),
        _bucket_np(PAGE + ts - c),
        _bucket_np(ts - c),
        _bucket_np(WINDOW + ts - np.arange(WINDOW)[None, :]),
    ]
    ce, co, near, win, sce, sco, slast, snew, swin = _bias_lookup(rel_bias, idx)
    b31 = rel_bias.astype(F32)[N_BUCKETS - 1].reshape(NSA_KV, NSA_GQ, 1, 1)
    shift = lambda tab: (tab.reshape(NSA_KV, NSA_GQ, tab.shape[1], tab.shape[2]) - b31).reshape(
        NSA_KV, NSA_GQ * tab.shape[1], tab.shape[2])
    vis = lambda m: jnp.asarray(np.tile(m, (1,) * (m.ndim - 2) + (NSA_GQ, 1)))
    blocked = lambda tab: jnp.swapaxes(tab, 0, 1).reshape(nqb, NSA_KV, NSA_GQ * Q_BLOCK, nsb)
    bc = jnp.concatenate([blocked(ce), blocked(co)], axis=-1)
    vis_c = np.concatenate([n * L_SEL + L_CMP - 1 <= tq, n * L_SEL + L_SEL - 1 <= tq], axis=-1)
    bc = jnp.where(vis(vis_c)[:, None], bc, NEG)
    c_near = np.arange(2 * Q_BLOCK)[None, :]
    near_m = jnp.where(vis(c_near <= Q_BLOCK + t[:, None])[None], shift(near), NEG)
    c_win = np.arange(WINDOW + Q_BLOCK)[None, :]
    win_m = jnp.where(vis((c_win > t[:, None]) & (c_win <= WINDOW + t[:, None]))[None], _head_rows(win), NEG)
    ptab = (bc, jnp.swapaxes(bc, -1, -2), near_m, win_m)
    rows64 = lambda tab: tab.reshape(NSA_KV * NSA_GQ * nt, tab.shape[-1])
    stab = (rows64(_head_rows(sce)), rows64(_head_rows(sco)), rows64(shift(slast)), rows64(shift(snew)),
            rows64(_head_rows(swin)), rows64(_head_rows(snew)))
    return ptab, stab


def _gate_rep():
    j = np.arange(LANES)[None, :, None]
    c = np.arange(NSA_HEADS * NSA_DH)[None, None, :]
    br = np.arange(3)[:, None, None]
    return jnp.asarray(j == br * NSA_HEADS + c // NSA_DH, BF16)


def _nsa_prompt(u, w_pos, tables):
    bsz, t_len, _ = u.shape
    nqb = t_len // Q_BLOCK
    nsb = t_len // L_SEL
    assert nsb < LANES
    bc, bct, near, wtab = tables
    wk = jnp.tile(w_pos[0], t_len // L_CMP).reshape(1, t_len)
    wv = jnp.tile(w_pos[1], t_len // L_CMP).reshape(1, t_len)
    rep = _gate_rep()
    seq = lambda c: pl.BlockSpec((1, t_len, LANES), lambda b, i, c=c: (b, 0, c))
    full = lambda a: pl.BlockSpec(a.shape, lambda b, i, nd=a.ndim: (0,) * nd)
    per_i = lambda a: pl.BlockSpec((1,) + a.shape[1:], lambda b, i, nd=a.ndim: (i,) + (0,) * (nd - 1))
    kern = functools.partial(_nsa_prompt_kernel, t_len)
    kv0 = U_ROWS // LANES
    w0 = U_WIN // LANES
    return pl.pallas_call(
        kern,
        grid=(bsz, nqb),
        in_specs=[pl.BlockSpec((1, Q_BLOCK, 512), lambda b, i: (b, i, U_Q // 512)),
                  seq(kv0), seq(kv0 + 1), seq(kv0 + 2), seq(kv0 + 3), seq(w0), seq(w0 + 1),
                  pl.BlockSpec((1, Q_BLOCK, LANES), lambda b, i: (b, i, U_GL // LANES)), full(rep),
                  full(wk), full(wv), per_i(bc), per_i(bct), full(near), full(wtab)],
        out_specs=pl.BlockSpec((1, Q_BLOCK, 512), lambda b, i: (b, i, 0)),
        out_shape=jax.ShapeDtypeStruct((bsz, t_len, 512), F32),
        scratch_shapes=[pltpu.VMEM((2 * nsb, LANES), F32)] * 2,
        compiler_params=_params("parallel", "arbitrary"),
        name="nsa_prompt",
    )(u, u, u, u, u, u, u, u, rep, wk, wv, bc, bct, near, wtab)


def _nsa_sample_kernel(layer, n_pages, n_new, pt_ref, cache_ref, q_ref, rows_ref, wnew_ref, wbuf_ref, gl_ref,
                       rep_ref, wpool_ref, bce_ref, bco_ref, blast_ref, bnew_ref, bwin_ref, bwnew_ref,
                       o_ref, cmp_s, slc_s, pool_s, exp_s, pad_s, sem):
    b = pl.program_id(0)
    nb = pl.num_programs(0)
    nt = 8
    past = n_pages * PAGE
    nblk = 2 * n_pages
    rows = NSA_KV * NSA_GQ * nt
    half = 2 * LANES

    def page_copy(seq, p, part, buf, s):
        return pltpu.make_async_copy(
            cache_ref.at[layer, pt_ref[seq, p], pl.ds(part * half, half), :],
            buf.at[:, pl.ds(pl.multiple_of(p * PAGE, PAGE), PAGE)], s)

    def start_gather(seq, part, buf, s):
        def body(p, c):
            page_copy(seq, p, part, buf, s).start()
            return c
        lax.fori_loop(0, n_pages, body, 0)

    def wait_gather(seq, part, buf, s):
        def body(p, c):
            page_copy(seq, p, part, buf, s).wait()
            return c
        lax.fori_loop(0, n_pages, body, 0)

    @pl.when(b == 0)
    def _():
        start_gather(0, 0, cmp_s, sem.at[0])
        start_gather(0, 1, slc_s.at[0], sem.at[1])
        cb = lax.broadcasted_iota(jnp.int32, (past, nblk), 0) >> 5
        j2 = 2 * lax.broadcasted_iota(jnp.int32, (past, nblk), 1)
        pool_s[0] = _bf(jnp.where(cb == j2, 1.0, 0.0))
        pool_s[1] = _bf(jnp.where(cb == j2 + 1, 1.0, 0.0))
        ej = lax.broadcasted_iota(jnp.int32, (nblk, past), 0)
        ec = lax.broadcasted_iota(jnp.int32, (nblk, past), 1) >> 6
        exp_s[...] = _bf(jnp.where(ej == ec, 1.0, 0.0))
        pad_s[...] = jnp.zeros(pad_s.shape, F32)

    slot = b % 2
    slc = slc_s.at[slot]

    @pl.when(b + 1 < nb)
    def _():
        start_gather(b + 1, 1, slc_s.at[1 - slot], sem.at[2 - slot])

    wait_gather(b, 0, cmp_s, sem.at[0])
    wait_gather(b, 1, slc, sem.at[1 + slot])

    q_all = q_ref[0] * (NSA_DH ** -0.5)
    qq = _bf(jnp.concatenate([_heads_to_rows(q_all, g, nt) for g in range(NSA_KV)], axis=0))
    tr = lax.broadcasted_iota(jnp.int32, (rows, 1), 0) & (nt - 1)

    wp = wpool_ref[...]
    ks = _bf(cmp_s[0:LANES, :] * wp[0:1, :])
    vs = _bf(cmp_s[LANES:half, :] * wp[1:2, :])
    kce = _bf(_dot(ks, pool_s[0]))
    kco = _bf(_dot(ks, pool_s[1]))
    vce = _bf(_dot(vs, pool_s[0]))
    vco = _bf(_dot(vs, pool_s[1]))

    @pl.when(b + 1 < nb)
    def _():
        start_gather(b + 1, 0, cmp_s, sem.at[0])

    new = rows_ref[0]
    wnew = wnew_ref[0]
    pad_s[0, 0:nt, :] = new[:, 2 * LANES:3 * LANES]
    pad_s[1, 0:nt, :] = new[:, 3 * LANES:4 * LANES]
    pad_s[2, 0:nt, :] = wnew[:, 0:LANES]
    pad_s[3, 0:nt, :] = wnew[:, LANES:2 * LANES]

    wb = wbuf_ref[0, 0]
    s_all = _dot(qq, _bf(slc[0:LANES, :]))
    s_new = _dot_nt(qq, _bf(pad_s[0])) + bnew_ref[...]
    s_win = _dot(qq, _bf(wb[0:LANES, :])) + bwin_ref[...]
    s_wnew = _dot_nt(qq, _bf(pad_s[2])) + bwnew_ref[...]

    se = _dot(qq, kce) + bce_ref[...]
    so = _dot(qq, kco) + bco_ref[...]
    mx = jnp.maximum(jnp.max(se, axis=1, keepdims=True), jnp.max(so, axis=1, keepdims=True))
    ee = jnp.exp(se - mx)
    eo = jnp.exp(so - mx)
    inv = 1.0 / (jnp.sum(ee, axis=1, keepdims=True) + jnp.sum(eo, axis=1, keepdims=True))
    pe = ee * inv
    po = eo * inv
    o_c = _dot_nt(_bf(pe), vce) + _dot_nt(_bf(po), vco)

    def head_sum(pr):
        return jnp.concatenate(
            [pr[g * 4 * nt:g * 4 * nt + nt] + pr[g * 4 * nt + nt:g * 4 * nt + 2 * nt]
             + pr[g * 4 * nt + 2 * nt:g * 4 * nt + 3 * nt] + pr[g * 4 * nt + 3 * nt:g * 4 * nt + 4 * nt]
             for g in range(NSA_KV)], axis=0)

    jcol = lax.broadcasted_iota(jnp.int32, (NSA_KV * nt, nblk), 1)
    forced = (jcol == 0) | (jcol == nblk - 1)
    score = (head_sum(pe) + head_sum(po)) + jnp.where(forced, FORCE_BONUS, 0.0)
    rank = jnp.where(FORCE_BONUS > score, 1.0, 0.0)
    for j in range(nblk):
        sj = score[:, j:j + 1]
        ahead = (sj > score) | ((sj == score) & (jcol > j))
        rank = rank + jnp.where(ahead, 1.0, 0.0)
    sel = jnp.where(rank < float(N_SEL), 1.0, 0.0)
    sel_rows = jnp.concatenate([sel[g * nt:(g + 1) * nt] for g in range(NSA_KV) for _ in range(NSA_GQ)], axis=0)

    tc = lax.broadcasted_iota(jnp.int32, (rows, LANES), 1)
    mnew = (tc <= tr) & (tc < n_new)

    mk = _dot(_bf(sel_rows), exp_s[...]) > 0.5
    far = past - PAGE
    p_far, p_last, p_new = _masked_softmax_parts(
        [s_all[:, :far], s_all[:, far:] + blast_ref[...], s_new], [mk[:, :far], mk[:, far:], mnew], 1)
    o_s = _dot_nt(_bf(jnp.concatenate([p_far, p_last], axis=1)), _bf(slc[LANES:half, :])) \
        + _dot(_bf(p_new), _bf(pad_s[1]))

    cw = lax.broadcasted_iota(jnp.int32, (rows, WINDOW), 1)
    pw, pn = _masked_softmax_parts([s_win, s_wnew], [cw > tr, mnew], 1)
    o_w = _dot_nt(_bf(pw), _bf(wb[LANES:half, :])) + _dot(_bf(pn), _bf(pad_s[3]))

    gl = gl_ref[0]
    gate_all = [_sigmoid(_pick_columns(gl, rep_ref[br])) for br in range(3)]
    gates = [jnp.concatenate([_heads_to_rows(ga, g, nt) for g in range(NSA_KV)], axis=0) for ga in gate_all]
    comb = gates[0] * o_c + gates[1] * o_s + gates[2] * o_w
    for g in range(NSA_KV):
        blocks = _rows_to_heads(comb[g * 4 * nt:(g + 1) * 4 * nt], g, nt)
        o_ref[0, :, (2 * g) * LANES:(2 * g + 1) * LANES] = blocks[0]
        o_ref[0, :, (2 * g + 1) * LANES:(2 * g + 2) * LANES] = blocks[1]


def _nsa_sample(layer, cache_t, page_table, u, wbuf_t, w_pos, tables, n_new):
    bsz, n_pages = page_table.shape
    nt = u.shape[1]
    past = n_pages * PAGE
    wpool = jnp.tile(w_pos, (1, past // L_CMP))
    rep = _gate_rep()
    full = lambda a: pl.BlockSpec(a.shape, lambda b, pt, nd=a.ndim: (0,) * nd)
    ucols = lambda width, off: pl.BlockSpec((1, nt, width), lambda b, pt: (b, 0, off // width))
    kern = functools.partial(_nsa_sample_kernel, layer, n_pages, n_new)
    grid_spec = pltpu.PrefetchScalarGridSpec(
        num_scalar_prefetch=1,
        grid=(bsz,),
        in_specs=[pl.BlockSpec(memory_space=pl.ANY),
                  ucols(512, U_Q), ucols(512, U_ROWS), ucols(2 * LANES, U_WIN),
                  pl.BlockSpec((1, 1) + wbuf_t.shape[2:], lambda b, pt: (layer, b, 0, 0)),
                  ucols(LANES, U_GL), full(rep), full(wpool)] + [full(t) for t in tables],
        out_specs=pl.BlockSpec((1, nt, 512), lambda b, pt: (b, 0, 0)),
        scratch_shapes=[pltpu.VMEM((2 * LANES, past), F32), pltpu.VMEM((2, 2 * LANES, past), F32),
                        pltpu.VMEM((2, past, 2 * n_pages), BF16), pltpu.VMEM((2 * n_pages, past), BF16),
                        pltpu.VMEM((4, LANES, LANES), F32), pltpu.SemaphoreType.DMA((3,))],
    )
    return pl.pallas_call(
        kern,
        grid_spec=grid_spec,
        out_shape=jax.ShapeDtypeStruct((bsz, nt, 512), F32),
        compiler_params=pltpu.CompilerParams(dimension_semantics=("arbitrary",), vmem_limit_bytes=BIG_VMEM_LIMIT),
        name="nsa_sample",
    )(page_table, cache_t, u, u, u, wbuf_t, u, rep, wpool, *tables)


def _mixout_kernel(x_ref, g_ref, ca_ref, ob_ref, oc_ref, wg_ref, wpa_ref, wpb_ref, wpc_ref, wo_ref, o_ref):
    x = x_ref[...]
    h = _bf(x * lax.rsqrt(jnp.mean(x * x, axis=-1, keepdims=True) + EPS) * g_ref[...])
    gate = lambda k: _sigmoid(_dot(h, wg_ref[:, k * D_MODEL:(k + 1) * D_MODEL]))
    y = gate(0) * _dot(_bf(ca_ref[...]), wpa_ref[...])
    y = y + gate(1) * _dot(_bf(ob_ref[...]), wpb_ref[...])
    y = y + gate(2) * _dot(_bf(oc_ref[...]), wpc_ref[...])
    o_ref[...] = x + _dot(_bf(y), wo_ref[...])


def _mixout(x, g, ca, ob, oc, wg, wpa, wpb, wpc, wo):
    m = x.shape[0]
    tm = min(m, 512)
    rowblk = lambda n: pl.BlockSpec((tm, n), lambda i: (i, 0))
    full = lambda a: pl.BlockSpec(a.shape, lambda i: (0, 0))
    return pl.pallas_call(
        _mixout_kernel,
        grid=(m // tm,),
        in_specs=[rowblk(D_MODEL), pl.BlockSpec((1, D_MODEL), lambda i: (0, 0)),
                  rowblk(512), rowblk(512), rowblk(512), full(wg), full(wpa), full(wpb), full(wpc), full(wo)],
        out_specs=rowblk(D_MODEL),
        out_shape=jax.ShapeDtypeStruct((m, D_MODEL), F32),
        compiler_params=_params("parallel"),
        name="mixer_out",
    )(x, g.reshape(1, D_MODEL), ca, ob, oc, wg, wpa, wpb, wpc, wo)


def _xattn_kernel(x_ref, g_ref, kv_ref, wq_ref, wo_ref, o_ref):
    nb, tt, d = x_ref.shape
    x = x_ref[...].reshape(nb * tt, d)
    h = _bf(x * lax.rsqrt(jnp.mean(x * x, axis=-1, keepdims=True) + EPS) * g_ref[...])
    q = _dot(h, wq_ref[...])

    def mem_head(sq, which, hd):
        return jnp.concatenate(
            [kv_ref[0, sq, pl.ds(which * 8 + half * X_HEADS + hd, N_MEM, stride=MEM_ROWS), :]
             for half in range(2)], axis=1)

    seqs = []
    for sq in range(nb):
        outs = []
        for hd in range(X_HEADS):
            qh = _bf(q[sq * tt:(sq + 1) * tt, hd * X_DH:(hd + 1) * X_DH])
            kh = _bf(mem_head(sq, 0, hd))
            vh = _bf(mem_head(sq, 1, hd))
            s = _dot_nt(qh, kh) * (X_DH ** -0.5)
            e = jnp.exp(s - jnp.max(s, axis=-1, keepdims=True))
            pr = e * (1.0 / jnp.sum(e, axis=-1, keepdims=True))
            outs.append(_dot(_bf(pr), vh))
        seqs.append(jnp.concatenate(outs, axis=1))
    o = seqs[0] if nb == 1 else jnp.concatenate(seqs, axis=0)
    o_ref[...] = (x + _dot(_bf(o), wo_ref[...])).reshape(nb, tt, d)


def _cross_attn(x, g, mem_kv, layer, wq, wo):
    bsz, t_len, d = x.shape
    tt = min(t_len, 512)
    nb = 4 if (t_len <= 8 and bsz % 4 == 0) else 1
    full = lambda a: pl.BlockSpec(a.shape, lambda b, t: (0, 0))
    return pl.pallas_call(
        _xattn_kernel,
        grid=(bsz // nb, t_len // tt),
        in_specs=[pl.BlockSpec((nb, tt, d), lambda b, t: (b, t, 0)),
                  pl.BlockSpec((1, d), lambda b, t: (0, 0)),
                  pl.BlockSpec((1, nb) + mem_kv.shape[2:], lambda b, t: (layer, b, 0, 0)),
                  full(wq), full(wo)],
        out_specs=pl.BlockSpec((nb, tt, d), lambda b, t: (b, t, 0)),
        out_shape=jax.ShapeDtypeStruct((bsz, t_len, d), F32),
        compiler_params=_params("parallel", "parallel"),
        name="cross_attn",
    )(x, g.reshape(1, d), mem_kv, wq, wo)


FF_CHUNK = 1024


def _mlp_kernel(n_k, final_norm, x_ref, g_ref, gf_ref, w1_ref, w2_ref, o_ref, h_s, acc_s):
    k = pl.program_id(1)

    @pl.when(k == 0)
    def _():
        x = x_ref[...]
        h_s[...] = _bf(x * lax.rsqrt(jnp.mean(x * x, axis=-1, keepdims=True) + EPS) * g_ref[...])
        acc_s[...] = x

    a = jnp.maximum(_dot(h_s[...], w1_ref[...]), 0.0)
    acc_s[...] += _dot(_bf(a * a), w2_ref[...])

    @pl.when(k == n_k - 1)
    def _():
        y = acc_s[...]
        if final_norm:
            y = y * lax.rsqrt(jnp.mean(y * y, axis=-1, keepdims=True) + EPS) * gf_ref[...]
        o_ref[...] = y


def _mlp(x, g, w1, w2, final_g=None):
    m, d = x.shape
    tm = min(m, 1024)
    n_k = D_FF // FF_CHUNK
    gf = g if final_g is None else final_g
    return pl.pallas_call(
        functools.partial(_mlp_kernel, n_k, final_g is not None),
        grid=(m // tm, n_k),
        in_specs=[pl.BlockSpec((tm, d), lambda i, k: (i, 0)),
                  pl.BlockSpec((1, d), lambda i, k: (0, 0)),
                  pl.BlockSpec((1, d), lambda i, k: (0, 0)),
                  pl.BlockSpec((d, FF_CHUNK), lambda i, k: (0, k)),
                  pl.BlockSpec((FF_CHUNK, d), lambda i, k: (k, 0))],
        out_specs=pl.BlockSpec((tm, d), lambda i, k: (i, 0)),
        out_shape=jax.ShapeDtypeStruct((m, d), F32),
        scratch_shapes=[pltpu.VMEM((tm, d), BF16), pltpu.VMEM((tm, d), F32)],
        compiler_params=_params("parallel", "arbitrary"),
        name="sq_relu_mlp",
    )(x, g.reshape(1, d), gf.reshape(1, d), w1, w2)


A_COLS = 2 * CONV_CH
B0 = A_COLS
Z0 = B0 + GDN_QKV
AB0 = Z0 + GDN_HEADS * GDN_D
C0 = AB0 + 2 * GDN_HEADS
KV0 = C0 + NSA_HEADS * NSA_DH
GL0 = KV0 + 6 * NSA_KV * NSA_DH
G0 = GL0 + 3 * NSA_HEADS
N_IN = G0 + 3 * D_MODEL


def _layer_weights(l, w_in, w_pa, w_pb, w_pc, w_o, w_xq, w_xk, w_xv, w_xo, w_ff1, w_ff2):
    w = w_in[l]
    lane_pad = lambda cols: jnp.pad(cols, ((0, 0), (0, LANES - cols.shape[1])))
    groups = [(U_A, w[:, 0:A_COLS]), (U_Q, w[:, C0:KV0]), (U_ROWS, w[:, KV0:KV0 + 4 * LANES]),
              (U_QKV, w[:, B0:Z0]), (U_Z, w[:, Z0:AB0]),
              (U_WIN, w[:, KV0 + 4 * LANES:GL0]), (U_AB, lane_pad(w[:, AB0:C0])), (U_GL, lane_pad(w[:, GL0:G0]))]
    off = 0
    for start, cols in groups:
        assert start == off
        off += cols.shape[1]
    assert off == U_N
    return {
        "in": _bf(jnp.concatenate([cols for _, cols in groups], axis=1)),
        "g": _bf(w[:, G0:N_IN]),
        "pa": _bf(w_pa[l]), "pb": _bf(w_pb[l]), "pc": _bf(w_pc[l]), "o": _bf(w_o[l]),
        "xq": _bf(w_xq[l]), "xo": _bf(w_xo[l]),
        "xkv": _bf(jnp.concatenate([w_xk[l], w_xv[l]], axis=1)),
        "ff1": _bf(w_ff1[l]), "ff2": _bf(w_ff2[l]),
    }


def _mixers(x, lw, p, l, conv_state_pad, qkv_state_pad, s0, n_valid, gdn_len, nsa_fn):
    bsz, t_len, d = x.shape
    m = bsz * t_len
    x2 = x.reshape(m, d)
    u2, rows, win = _proj_in(x2, p["norm_mix"][l], lw["in"])
    u = u2.reshape(bsz, t_len, U_N)
    rows = rows.reshape(bsz, t_len, 4 * LANES)
    win = win.reshape(bsz, t_len, 2 * LANES)
    qkv_tail = u[:, max(n_valid - 3, 0):n_valid, U_QKV:U_QKV + GDN_QKV]

    ca, conv_new = _conformer(u, conv_state_pad, p["conv_a_w"][l], p["conv_a_b"][l], p["ln_a_g"][l],
                              p["ln_a_b"][l], n_valid if n_valid < t_len else min(t_len, 256))
    w_conv_pad = jnp.pad(p["gdn_conv_w"][l], ((0, 4), (0, 0)))
    gdn_args = (qkv_state_pad, s0, w_conv_pad, p["gdn_a_log"][l], p["gdn_dt_bias"][l], p["gdn_norm_g"][l])
    if gdn_len == t_len:
        ob, s_new = _gated_deltanet(u, U_QKV // LANES, U_Z // LANES, U_AB // LANES, *gdn_args, gdn_len)
    else:
        ug = jnp.concatenate([u[:, :, U_QKV:U_WIN], u[:, :, U_AB:U_AB + LANES]], axis=-1)
        ug = jnp.pad(ug, ((0, 0), (0, gdn_len - t_len), (0, 0)))
        ob, s_new = _gated_deltanet(ug, 0, (U_Z - U_QKV) // LANES, (U_WIN - U_QKV) // LANES, *gdn_args, n_valid)
        ob = ob[:, :t_len]
    oc = nsa_fn(u)
    x_new = _mixout(x2, p["norm_mix"][l], ca.reshape(m, -1), ob.reshape(m, -1), oc.reshape(m, -1),
                    lw["g"], lw["pa"], lw["pb"], lw["pc"], lw["o"])
    return x_new.reshape(bsz, t_len, d), conv_new[:, HALO - (CONV_W - 1):], qkv_tail, s_new, rows, win


def kernel(x_prompt, x_sample, cache_nsa_kv, cache_win_kv, state_conv_a, state_conv_qkv, state_gdn, cache_mem_kv,
           page_table, mem_prompt, rel_bias, norm_mix, w_in, conv_a_w, conv_a_b, ln_a_g, ln_a_b, w_pa, gdn_conv_w,
           gdn_a_log, gdn_dt_bias, gdn_norm_g, w_pb, nsa_cmp_w, w_pc, w_o, norm_x, w_xq, w_xk, w_xv, w_xo,
           norm_mlp, w_ff1, w_ff2, norm_final):
    p = {"norm_mix": norm_mix, "conv_a_w": conv_a_w, "conv_a_b": conv_a_b, "ln_a_g": ln_a_g, "ln_a_b": ln_a_b,
         "gdn_conv_w": gdn_conv_w, "gdn_a_log": gdn_a_log, "gdn_dt_bias": gdn_dt_bias, "gdn_norm_g": gdn_norm_g}
    depth = w_in.shape[0]
    bp, tp, d = x_prompt.shape
    bs, ts, _ = x_sample.shape
    ts_pad = 8
    n_pages = page_table.shape[1]
    wb = cache_win_kv.shape[2]
    xp = x_prompt
    xs = jnp.pad(x_sample, ((0, 0), (0, ts_pad - ts), (0, 0)))
    ptab, stab = _nsa_tables(rel_bias, tp, n_pages, ts_pad)
    cache_t = jnp.transpose(cache_nsa_kv, (0, 1, 3, 4, 5, 2)).reshape(depth, -1, 4 * LANES, PAGE)
    wbuf_t = jnp.transpose(cache_win_kv, (0, 1, 3, 4, 5, 2)).reshape(depth, bs, 2 * LANES, wb)
    mem_t = cache_mem_kv.reshape(depth, bs, N_MEM, 2, X_HEADS, 2, LANES)
    mem_t = jnp.transpose(mem_t, (0, 1, 2, 3, 5, 4, 6)).reshape(depth, bs, N_MEM * MEM_ROWS, LANES)
    outs = {k: [] for k in ("p_rows", "p_win", "p_conv", "p_qkv", "p_gdn", "p_mem",
                            "s_rows", "s_win", "s_conv", "s_qkv", "s_gdn")}
    for l in range(depth):
        lw = _layer_weights(l, w_in, w_pa, w_pb, w_pc, w_o, w_xq, w_xk, w_xv, w_xo, w_ff1, w_ff2)
        nsa_p = lambda u: _nsa_prompt(u, nsa_cmp_w[l], ptab)
        xp, conv_n, qkv_tail, s_n, rows, win = _mixers(
            xp, lw, p, l, jnp.zeros((bp, HALO, CONV_CH), F32), jnp.zeros((bp, 8, GDN_QKV), F32),
            jnp.zeros((bp, GDN_HEADS, GDN_D, GDN_D), F32), tp, tp, nsa_p)
        mem_kv = _mem_kv(_bf(mem_prompt.reshape(bp * N_MEM, d)), lw["xkv"]).reshape(1, bp, N_MEM * MEM_ROWS, LANES)
        xp = _cross_attn(xp, norm_x[l], mem_kv, 0, lw["xq"], lw["xo"])
        final_g = norm_final if l == depth - 1 else None
        xp = _mlp(xp.reshape(bp * tp, d), norm_mlp[l], lw["ff1"], lw["ff2"], final_g).reshape(bp, tp, d)
        outs["p_rows"].append(rows.reshape(bp, tp, 4, NSA_KV, NSA_DH))
        outs["p_win"].append(win[:, tp - min(WINDOW, tp):].reshape(bp, min(WINDOW, tp), 2, NSA_KV, NSA_DH))
        outs["p_conv"].append(conv_n)
        outs["p_qkv"].append(qkv_tail)
        outs["p_gdn"].append(s_n)
        mem_out = mem_kv.reshape(bp, N_MEM, 2, 2, X_HEADS, LANES)
        outs["p_mem"].append(jnp.swapaxes(mem_out, 3, 4).reshape(bp, N_MEM, 2, X_HEADS, X_DH))
        nsa_s = lambda u: _nsa_sample(l, cache_t, page_table, u, wbuf_t, nsa_cmp_w[l], stab, ts)
        conv_pad = jnp.pad(state_conv_a[l], ((0, 0), (HALO - (CONV_W - 1), 0), (0, 0)))
        qkv_pad = jnp.pad(state_conv_qkv[l], ((0, 0), (5, 0), (0, 0)))
        xs, conv_n, qkv_tail, s_n, rows, win = _mixers(
            xs, lw, p, l, conv_pad, qkv_pad, state_gdn[l], ts, GDN_CHUNK, nsa_s)
        xs = _cross_attn(xs, norm_x[l], mem_t, l, lw["xq"], lw["xo"])
        xs = _mlp(xs.reshape(bs * ts_pad, d), norm_mlp[l], lw["ff1"], lw["ff2"], final_g).reshape(bs, ts_pad, d)
        outs["s_rows"].append(rows[:, :ts].reshape(bs, ts, 4, NSA_KV, NSA_DH))
        win_new = win[:, :ts].reshape(bs, ts, 2, NSA_KV, NSA_DH)
        outs["s_win"].append(jnp.concatenate([cache_win_kv[l], win_new], axis=1)[:, ts:])
        outs["s_conv"].append(conv_n)
        outs["s_qkv"].append(qkv_tail)
        outs["s_gdn"].append(s_n)
    st = lambda k: jnp.stack(outs[k], axis=0)
    return (xp, xs[:, :ts], st("p_rows"), st("p_win"), st("p_conv"), st("p_qkv"), st("p_gdn"), st("p_mem"),
            st("s_rows"), st("s_win"), st("s_conv"), st("s_qkv"), st("s_gdn"))
```

```python
import functools
import math

import jax
import jax.numpy as jnp
import numpy as np
from jax import lax
from jax.experimental import pallas as pl
from jax.experimental.pallas import tpu as pltpu

F32 = jnp.float32
BF16 = jnp.bfloat16

D_MODEL = 1024
CONV_CH = 512
CONV_W = 31
GDN_HEADS = 4
GDN_D = 128
GDN_CHUNK = 64
GDN_QKV = 3 * GDN_HEADS * GDN_D
NSA_HEADS = 8
NSA_KV = 2
NSA_GQ = 4
NSA_DH = 64
L_CMP = 32
L_SEL = 64
N_SEL = 16
WINDOW = 512
Q_BLOCK = 128
FORCE_BONUS = 1e4
PAGE = 128
N_MEM = 256
X_HEADS = 4
X_DH = 256
D_FF = 4096
N_BUCKETS = 32
EPS = 1e-6
NEG = -1e30

LANES = 128
HALO = 32
CONV_TILE = 512
VMEM_LIMIT = 48 * 1024 * 1024
BIG_VMEM_LIMIT = 56 * 1024 * 1024


def _bf(x):
    return x.astype(BF16)


def _dot(a, b):
    return jnp.dot(a, b, preferred_element_type=F32)


def _dot_nt(a, b):
    return lax.dot_general(a, b, (((1,), (1,)), ((), ())), preferred_element_type=F32)


def _sigmoid(x):
    return 0.5 * jnp.tanh(0.5 * x) + 0.5


def _silu(x):
    return x * _sigmoid(x)


def _params(*sem):
    return pltpu.CompilerParams(dimension_semantics=sem, vmem_limit_bytes=VMEM_LIMIT)


MEM_ROWS = 2 * 2 * X_HEADS


def _memkv_kernel(a_ref, w_ref, o_ref):
    tm = a_ref.shape[0]
    acc = _dot(a_ref[...], w_ref[...])
    for which in range(2):
        for hd in range(X_HEADS):
            for half in range(2):
                c0 = which * D_MODEL + hd * X_DH + half * LANES
                o_ref[pl.ds(which * 8 + half * X_HEADS + hd, tm, stride=MEM_ROWS), :] = acc[:, c0:c0 + LANES]


def _mem_kv(a, w):
    m, k = a.shape
    tm = min(m, 512)
    return pl.pallas_call(
        _memkv_kernel,
        grid=(m // tm,),
        in_specs=[pl.BlockSpec((tm, k), lambda i: (i, 0)), pl.BlockSpec(w.shape, lambda i: (0, 0))],
        out_specs=pl.BlockSpec((tm * MEM_ROWS, LANES), lambda i: (i, 0)),
        out_shape=jax.ShapeDtypeStruct((m * MEM_ROWS, LANES), F32),
        compiler_params=_params("parallel"),
        name="mem_kv",
    )(a, w)


U_A = 0
U_Q = 1024
U_ROWS = 1536
U_QKV = 2048
U_Z = 3584
U_WIN = 4096
U_AB = 4352
U_GL = 4480
U_N = 4608
PROJ_TN = 1536


def _proj_kernel(x_ref, g_ref, w_ref, o_ref, rows_ref, win_ref, h_s):
    j = pl.program_id(1)

    @pl.when(j == 0)
    def _():
        x = x_ref[...]
        h_s[...] = _bf(x * lax.rsqrt(jnp.mean(x * x, axis=-1, keepdims=True) + EPS) * g_ref[...])

    acc = _dot(h_s[...], w_ref[...])
    o_ref[...] = acc

    @pl.when(j == U_ROWS // PROJ_TN)
    def _():
        rows_ref[...] = acc[:, U_ROWS % PROJ_TN:U_ROWS % PROJ_TN + 4 * LANES]

    @pl.when(j == U_WIN // PROJ_TN)
    def _():
        win_ref[...] = acc[:, U_WIN % PROJ_TN:U_WIN % PROJ_TN + 2 * LANES]


def _proj_in(x, g, w):
    m, d = x.shape
    n = w.shape[1]
    tm = min(m, 1024)
    assert U_ROWS % PROJ_TN + 4 * LANES <= PROJ_TN and U_WIN % PROJ_TN + 2 * LANES <= PROJ_TN
    return pl.pallas_call(
        _proj_kernel,
        grid=(m // tm, n // PROJ_TN),
        in_specs=[pl.BlockSpec((tm, d), lambda i, j: (i, 0)), pl.BlockSpec((1, d), lambda i, j: (0, 0)),
                  pl.BlockSpec((d, PROJ_TN), lambda i, j: (0, j))],
        out_specs=[pl.BlockSpec((tm, PROJ_TN), lambda i, j: (i, j)),
                   pl.BlockSpec((tm, 4 * LANES), lambda i, j: (i, 0)),
                   pl.BlockSpec((tm, 2 * LANES), lambda i, j: (i, 0))],
        out_shape=[jax.ShapeDtypeStruct((m, n), F32), jax.ShapeDtypeStruct((m, 4 * LANES), F32),
                   jax.ShapeDtypeStruct((m, 2 * LANES), F32)],
        scratch_shapes=[pltpu.VMEM((tm, d), BF16)],
        compiler_params=_params("parallel", "arbitrary"),
        name="proj_in",
    )(x, g.reshape(1, d), w)


def _split3(x):
    hi = _bf(x)
    r = x - hi.astype(F32)
    mid = _bf(r)
    return hi, mid, _bf(r - mid.astype(F32))


def _pick_columns(x, onehot):
    hi, mid, lo = _split3(x)
    return (_dot(lo, onehot) + _dot(mid, onehot)) + _dot(hi, onehot)


def _conf_kernel(n_t, tt, tv, u_ref, halo_ref, st_ref, w_ref, b_ref, g_ref, lb_ref, o_ref, nb_ref, xc_ref, zs_ref):
    t = pl.program_id(1)
    u = u_ref[0]
    xc_ref[HALO:HALO + tt, :] = u[:, :CONV_CH] * _sigmoid(u[:, CONV_CH:])
    if n_t > 1:
        uh = halo_ref[0]
        gh = uh[:, :CONV_CH] * _sigmoid(uh[:, CONV_CH:])
        xc_ref[0:HALO, :] = jnp.where(t > 0, gh, st_ref[0])
    else:
        xc_ref[0:HALO, :] = st_ref[0]
    off = HALO - (CONV_W - 1)
    span = tt + HALO - 8
    for r in range(1, 8):
        zs_ref[r - 1] = xc_ref[r:r + span, :]
    acc = None
    for i in range(CONV_W):
        pos = off + i
        r, base = pos % 8, pos - pos % 8
        src = xc_ref[base:base + tt, :] if r == 0 else zs_ref[r - 1, base:base + tt, :]
        term = src * w_ref[i:i + 1, :]
        acc = term if acc is None else acc + term
    y = acc + b_ref[...]
    mu = jnp.mean(y, axis=-1, keepdims=True)
    yc = y - mu
    var = jnp.mean(yc * yc, axis=-1, keepdims=True)
    ln = yc * lax.rsqrt(var + EPS) * g_ref[...] + lb_ref[...]
    o_ref[0] = _silu(ln)

    @pl.when(t == n_t - 1)
    def _():
        nb_ref[0] = xc_ref[tv:tv + HALO, :]


def _conformer(u_a, state_pad, w_dw, b_dw, ln_g, ln_b, n_valid_last):
    bsz, t_len, _ = u_a.shape
    tt = min(t_len, CONV_TILE)
    n_t = t_len // tt
    hb = tt // HALO if n_t > 1 else 1
    halo_rows = HALO if n_t > 1 else tt
    w_pad = jnp.pad(w_dw, ((0, HALO - CONV_W), (0, 0)))
    row = lambda v: v.reshape(1, CONV_CH)
    kern = functools.partial(_conf_kernel, n_t, tt, n_valid_last)
    return pl.pallas_call(
        kern,
        grid=(bsz, n_t),
        in_specs=[
            pl.BlockSpec((1, tt, 2 * CONV_CH), lambda b, t: (b, t, 0)),
            pl.BlockSpec((1, halo_rows, 2 * CONV_CH), lambda b, t: (b, jnp.maximum(t * hb - 1, 0), 0)),
            pl.BlockSpec((1, HALO, CONV_CH), lambda b, t: (b, 0, 0)),
            pl.BlockSpec((HALO, CONV_CH), lambda b, t: (0, 0)),
            pl.BlockSpec((1, CONV_CH), lambda b, t: (0, 0)),
            pl.BlockSpec((1, CONV_CH), lambda b, t: (0, 0)),
            pl.BlockSpec((1, CONV_CH), lambda b, t: (0, 0)),
        ],
        out_specs=[
            pl.BlockSpec((1, tt, CONV_CH), lambda b, t: (b, t, 0)),
            pl.BlockSpec((1, HALO, CONV_CH), lambda b, t: (b, 0, 0)),
        ],
        out_shape=[
            jax.ShapeDtypeStruct((bsz, t_len, CONV_CH), F32),
            jax.ShapeDtypeStruct((bsz, HALO, CONV_CH), F32),
        ],
        scratch_shapes=[pltpu.VMEM((HALO + tt, CONV_CH), F32), pltpu.VMEM((7, tt + HALO - 8, CONV_CH), F32)],
        compiler_params=_params("parallel", "arbitrary"),
        name="conformer_conv",
    )(u_a, u_a, state_pad, w_pad, row(b_dw), row(ln_g), row(ln_b))


def _tri_inverse(a_list, ii, jj, merge_shifts):
    mm = lambda p, q: _dot(_bf(p), _bf(q))
    a0 = [jnp.where((ii >> 3) == (jj >> 3), a, 0.0) for a in a_list]
    a2 = [mm(p, p) for p in a0]
    a4 = [mm(p, p) for p in a2]
    r = [(q - p) - mm(p, q) for p, q in zip(a0, a2)]
    r = [(p + q) + mm(p, q) for p, q in zip(r, a4)]
    for sh in merge_shifts:
        mask = ((ii >> (sh + 1)) == (jj >> (sh + 1))) & ((ii >> sh) != (jj >> sh))
        off = [jnp.where(mask, a, 0.0) for a in a_list]
        t = [o + mm(o, p) for o, p in zip(off, r)]
        r = [p - (q + mm(p, q)) for p, q in zip(r, t)]
    return r


def _softplus(x):
    return jnp.maximum(x, 0.0) + jnp.log1p(jnp.exp(-jnp.abs(x)))


def _gdn_kernel(t_len, n_valid, hp, alog_ref, dtb_ref, q_ref, k_ref, v_ref, z_ref, ab_ref,
                sq_ref, sk_ref, sv_ref, wq_ref, wk_ref, wv_ref, s0_ref, ng_ref,
                o_ref, sn_ref,
                xp_s, qn_s, kn_s, vn_s, g_s, be_s, vw_s, kcd_s, qg_s, kdt_s, qk_s, ge_s):
    h0 = pl.program_id(1) * hp
    n_chunks = t_len // GDN_CHUNK
    c_len = GDN_CHUNK

    heads = range(hp)
    hcols = lambda hh: slice(hh * GDN_D, (hh + 1) * GDN_D)

    def conv(x_ref, st_ref, w_ref, hh):
        xp_s[0:8, :] = st_ref[0, :, hcols(hh)]
        xp_s[8:8 + t_len, :] = x_ref[0, :, hcols(hh)]
        acc = xp_s[5:5 + t_len, :] * w_ref[0:1, hcols(hh)]
        for i in range(1, 4):
            acc = acc + xp_s[5 + i:5 + i + t_len, :] * w_ref[i:i + 1, hcols(hh)]
        return _silu(acc)

    col = lax.broadcasted_iota(jnp.int32, (LANES, LANES), 0)
    ab = ab_ref[0]
    for hh in heads:
        h = h0 + hh
        qc = conv(q_ref, sq_ref, wq_ref, hh)
        qn_s[hh] = qc * lax.rsqrt(jnp.sum(qc * qc, axis=-1, keepdims=True) + EPS) * (GDN_D ** -0.5)
        kc = conv(k_ref, sk_ref, wk_ref, hh)
        kn_s[hh] = kc * lax.rsqrt(jnp.sum(kc * kc, axis=-1, keepdims=True) + EPS)
        vn_s[hh] = conv(v_ref, sv_ref, wv_ref, hh)
        a_rep = _pick_columns(ab, _bf(jnp.where(col == h, 1.0, 0.0)))
        b_rep = _pick_columns(ab, _bf(jnp.where(col == GDN_HEADS + h, 1.0, 0.0)))
        a_exp = jnp.exp(jnp.full((1, LANES), alog_ref[h], F32))
        g = -a_exp * _softplus(a_rep + dtb_ref[h])
        beta = _sigmoid(b_rep)
        if n_valid < t_len:
            live = lax.broadcasted_iota(jnp.int32, (t_len, LANES), 0) < n_valid
            g = jnp.where(live, g, 0.0)
            beta = jnp.where(live, beta, 0.0)
        g_s[hh] = g
        be_s[hh] = beta

    ii = lax.broadcasted_iota(jnp.int32, (c_len, c_len), 0)
    jj = lax.broadcasted_iota(jnp.int32, (c_len, c_len), 1)
    incl = ii >= jj
    strict = ii > jj
    ltri = _bf(incl.astype(F32))
    unroll = max(u for u in (1, 2, 4, 8) if n_chunks % u == 0 and u * hp <= 16)

    def cumdecay(g_c):
        g_hi, g_mid, g_lo = _split3(g_c)
        return (_dot(ltri, g_lo) + _dot(ltri, g_mid)) + _dot(ltri, g_hi)

    def prep(cu, carry):
        pairs = [(hh, cu * unroll + u) for hh in heads for u in range(unroll)]
        sls = [pl.ds(pl.multiple_of(c * c_len, c_len), c_len) for _, c in pairs]
        each = lambda f, *ls: [f(*a) for a in zip(*ls)]
        q_l = [qn_s[hh, sl, :] for (hh, _), sl in zip(pairs, sls)]
        k_l = [kn_s[hh, sl, :] for (hh, _), sl in zip(pairs, sls)]
        b_l = [be_s[hh, sl, :] for (hh, _), sl in zip(pairs, sls)]
        gc_l = [cumdecay(g_s[hh, sl, :]) for (hh, _), sl in zip(pairs, sls)]
        dec_l = each(lambda gc: jnp.where(
            incl, jnp.exp(jnp.minimum(gc[:, 0:c_len] - gc.T[0:c_len, :], 0.0)), 0.0), gc_l)
        kb_l = each(lambda k, b: k * b, k_l, b_l)
        a_l = each(lambda kb, k, dec: jnp.where(strict, _dot_nt(_bf(kb), _bf(k)) * dec, 0.0), kb_l, k_l, dec_l)
        r_l = _tri_inverse(a_l, ii, jj, () if n_valid <= 8 else (3, 4, 5))
        eg_l = each(jnp.exp, gc_l)
        rhs_l = [jnp.concatenate([vn_s[hh, sl, :] * b, kb * eg], axis=1)
                 for (hh, _), sl, b, kb, eg in zip(pairs, sls, b_l, kb_l, eg_l)]
        sol_l = each(lambda r, rhs: rhs + _dot(_bf(r), _bf(rhs)), r_l, rhs_l)
        qk_l = each(lambda q, k, dec: jnp.where(incl, _dot_nt(_bf(q), _bf(k)) * dec, 0.0), q_l, k_l, dec_l)
        for (hh, c), sl, sol, qk, q, k, gc, eg in zip(pairs, sls, sol_l, qk_l, q_l, k_l, gc_l, eg_l):
            g_end = gc[c_len - 1:c_len, :]
            vw_s[hh, sl, :] = sol[:, :GDN_D]
            kcd_s[hh, sl, :] = _bf(sol[:, GDN_D:])
            qk_s[hh, c] = _bf(qk)
            qg_s[hh, sl, :] = _bf(q * eg)
            kdt_s[hh, c] = _bf((k * jnp.exp(g_end - gc)).T)
            ge_s[hh, c] = jnp.broadcast_to(jnp.exp(g_end), (8, LANES))
        return carry

    lax.fori_loop(0, n_chunks // unroll, prep, 0)

    def step(c, states):
        sl = pl.ds(pl.multiple_of(c * c_len, c_len), c_len)
        sb = [_bf(s) for s in states]
        v_new = [vw_s[hh, sl, :] - _dot(kcd_s[hh, sl, :], sb[hh]) for hh in heads]
        vb = [_bf(v) for v in v_new]
        for hh in heads:
            o_ref[0, sl, hcols(hh)] = _dot(qg_s[hh, sl, :], sb[hh]) + _dot(qk_s[hh, c], vb[hh])
        return tuple(states[hh] * ge_s[hh, c][0:1, :] + _dot(kdt_s[hh, c], vb[hh]) for hh in heads)

    s_fin = lax.fori_loop(0, n_chunks, step, tuple(s0_ref[0, hh] for hh in heads), unroll=2 - n_chunks % 2)
    for hh in heads:
        sn_ref[0, hh] = s_fin[hh]
        o = o_ref[0, :, hcols(hh)]
        y = o * lax.rsqrt(jnp.mean(o * o, axis=-1, keepdims=True) + EPS) * ng_ref[...]
        o_ref[0, :, hcols(hh)] = y * _silu(z_ref[0, :, hcols(hh)])


def _gated_deltanet(u, qkv_blk, z_blk, ab_blk, state_pad, s0, w_conv_pad, a_log, dt_bias, norm_g, n_valid):
    bsz, t_len, _ = u.shape
    nh = GDN_HEADS
    hp = nh if t_len <= 4 * GDN_CHUNK else 2
    wide = hp * GDN_D
    assert (qkv_blk * GDN_D) % wide == 0 and (z_blk * GDN_D) % wide == 0
    ublk = lambda blk: pl.BlockSpec((1, t_len, wide), lambda b, j, o=blk * GDN_D // wide: (b, 0, o + j))
    stb = lambda off: pl.BlockSpec((1, 8, wide), lambda b, j, o=off * GDN_D // wide: (b, 0, o + j))
    wb = lambda off: pl.BlockSpec((8, wide), lambda b, j, o=off * GDN_D // wide: (0, o + j))
    smem = pl.BlockSpec(memory_space=pltpu.SMEM)
    n_chunks = t_len // GDN_CHUNK
    seq = lambda dt: pltpu.VMEM((hp, t_len, GDN_D), dt)
    kern = functools.partial(_gdn_kernel, t_len, n_valid, hp)
    return pl.pallas_call(
        kern,
        grid=(bsz, nh // hp),
        in_specs=[smem, smem, ublk(qkv_blk), ublk(qkv_blk + nh), ublk(qkv_blk + 2 * nh), ublk(z_blk),
                  pl.BlockSpec((1, t_len, GDN_D), lambda b, j: (b, 0, ab_blk)),
                  stb(0), stb(nh), stb(2 * nh), wb(0), wb(nh), wb(2 * nh),
                  pl.BlockSpec((1, hp, GDN_D, GDN_D), lambda b, j: (b, j, 0, 0)),
                  pl.BlockSpec((1, GDN_D), lambda b, j: (0, 0))],
        out_specs=[pl.BlockSpec((1, t_len, wide), lambda b, j: (b, 0, j)),
                   pl.BlockSpec((1, hp, GDN_D, GDN_D), lambda b, j: (b, j, 0, 0))],
        out_shape=[jax.ShapeDtypeStruct((bsz, t_len, nh * GDN_D), F32),
                   jax.ShapeDtypeStruct((bsz, nh, GDN_D, GDN_D), F32)],
        scratch_shapes=[pltpu.VMEM((8 + t_len, GDN_D), F32), seq(F32), seq(F32), seq(F32), seq(F32), seq(F32),
                        seq(F32), seq(BF16), seq(BF16),
                        pltpu.VMEM((hp, n_chunks, GDN_D, GDN_CHUNK), BF16),
                        pltpu.VMEM((hp, n_chunks, GDN_CHUNK, GDN_CHUNK), BF16),
                        pltpu.VMEM((hp, n_chunks, 8, LANES), F32)],
        compiler_params=pltpu.CompilerParams(dimension_semantics=("parallel", "parallel"),
                                             vmem_limit_bytes=BIG_VMEM_LIMIT),
        name="gated_deltanet",
    )(a_log, dt_bias, u, u, u, u, u, state_pad, state_pad, state_pad,
      w_conv_pad, w_conv_pad, w_conv_pad, s0, norm_g.reshape(1, GDN_D))


def _heads_to_rows(x, g, nt):
    lane = lax.broadcasted_iota(jnp.int32, (nt, LANES), 1)
    keep = (lane >= NSA_DH * g) & (lane < NSA_DH * (g + 1))
    parts = []
    for r in range(NSA_GQ):
        hh = NSA_GQ * g + r
        blk = x[:, (hh // 2) * LANES:(hh // 2 + 1) * LANES]
        if hh % 2 != g:
            blk = pltpu.roll(blk, NSA_DH, axis=1)
        parts.append(jnp.where(keep, blk, 0.0))
    return jnp.concatenate(parts, axis=0)


def _rows_to_heads(y, g, nt):
    outs = []
    for m in range(2):
        x0 = y[(2 * m) * nt:(2 * m + 1) * nt]
        x1 = y[(2 * m + 1) * nt:(2 * m + 2) * nt]
        if g == 1:
            x0 = pltpu.roll(x0, NSA_DH, axis=1)
        else:
            x1 = pltpu.roll(x1, NSA_DH, axis=1)
        outs.append(x0 + x1)
    return outs


def _masked_softmax_parts(parts, masks, axis):
    sm = [jnp.where(m, s, NEG) for s, m in zip(parts, masks)]
    mx = functools.reduce(jnp.maximum, [jnp.max(s, axis=axis, keepdims=True) for s in sm])
    es = [jnp.where(m, jnp.exp(s - mx), 0.0) for s, m in zip(sm, masks)]
    den = functools.reduce(lambda p, q: p + q, [jnp.sum(e, axis=axis, keepdims=True) for e in es])
    inv = 1.0 / jnp.maximum(den, 1e-30)
    return [e * inv for e in es]


def _bucket_np(rel):
    n = np.maximum(rel, 0)
    nf = np.maximum(n, 1).astype(np.float32)
    large = 16 + (np.log(nf / np.float32(16)) / np.float32(math.log(8.0)) * np.float32(16)).astype(np.int32)
    return np.where(n < 16, n, np.minimum(large, N_BUCKETS - 1)).astype(np.int32)


LOOKUP_TILE = 8192


def _lookup_kernel(idx_ref, tb_ref, o_ref):
    idx = idx_ref[...]
    acc = jnp.zeros(o_ref.shape, F32)
    for k in range(N_BUCKETS):
        acc = jnp.where(idx == k, tb_ref[:, k:k + 1], acc)
    o_ref[...] = acc


def _bias_lookup(rel_bias, idx_list):
    sizes = [int(np.prod(a.shape)) for a in idx_list]
    total = sum(sizes)
    padded = -(-total // LOOKUP_TILE) * LOOKUP_TILE
    flat = np.zeros((1, padded), np.int32)
    flat[0, :total] = np.concatenate([np.asarray(a, np.int32).reshape(-1) for a in idx_list])
    tab = pl.pallas_call(
        _lookup_kernel,
        grid=(padded // LOOKUP_TILE,),
        in_specs=[pl.BlockSpec((1, LOOKUP_TILE), lambda i: (0, i)),
                  pl.BlockSpec((NSA_HEADS, N_BUCKETS), lambda i: (0, 0))],
        out_specs=pl.BlockSpec((NSA_HEADS, LOOKUP_TILE), lambda i: (0, i)),
        out_shape=jax.ShapeDtypeStruct((NSA_HEADS, padded), F32),
        compiler_params=_params("parallel"),
        name="bias_lookup",
    )(jnp.asarray(flat), rel_bias.astype(F32).T)
    outs, off = [], 0
    for a, n in zip(idx_list, sizes):
        outs.append(tab[:, off:off + n].reshape((NSA_HEADS,) + tuple(a.shape)))
        off += n
    return outs


def _head_rows(tab):
    return tab.reshape(NSA_KV, NSA_GQ * tab.shape[1], tab.shape[2])


FAR_TILE = 512


def _nsa_prompt_kernel(t_len, q_ref, kcmp_ref, vcmp_ref, kslc_ref, vslc_ref, kwin_ref, vwin_ref, gl_ref, rep_ref,
                       wk_ref, wv_ref, bc_ref, bct_ref, bnear_ref, bwin_ref, o_ref, kc_s, vc_s):
    i = pl.program_id(1)
    nsb = t_len // L_SEL
    qb = Q_BLOCK
    rows = NSA_GQ * qb

    @pl.when(i == 0)
    def _():
        n2 = 2 * lax.broadcasted_iota(jnp.int32, (nsb, t_len), 0)
        cb = lax.broadcasted_iota(jnp.int32, (nsb, t_len), 1) >> 5
        kc = _bf(kcmp_ref[0])
        vc = _bf(vcmp_ref[0])
        wk = wk_ref[...]
        wv = wv_ref[...]
        kc_s[0:nsb, :] = _dot(_bf(jnp.where(cb == n2, wk, 0.0)), kc)
        kc_s[nsb:2 * nsb, :] = _dot(_bf(jnp.where(cb == n2 + 1, wk, 0.0)), kc)
        vc_s[0:nsb, :] = _dot(_bf(jnp.where(cb == n2, wv, 0.0)), vc)
        vc_s[nsb:2 * nsb, :] = _dot(_bf(jnp.where(cb == n2 + 1, wv, 0.0)), vc)

    q_all = q_ref[0] * (NSA_DH ** -0.5)
    gl = gl_ref[0]
    gate_all = [_sigmoid(_pick_columns(gl, rep_ref[br])) for br in range(3)]
    t0 = i * qb
    tq = t0 + (lax.broadcasted_iota(jnp.int32, (rows, 1), 0) & (qb - 1))
    tl = t0 + (lax.broadcasted_iota(jnp.int32, (1, rows), 1) & (qb - 1))
    eye_q = _bf((lax.broadcasted_iota(jnp.int32, (qb, qb), 0) == lax.broadcasted_iota(jnp.int32, (qb, qb), 1)).astype(F32))
    far_end = jnp.maximum(t0 - qb, 0)
    n_far = (far_end + FAR_TILE - 1) // FAR_TILE
    kc = _bf(kc_s[...])
    vc = _bf(vc_s[...])

    def key_aug(k0, n_keys, limit):
        kpos = k0 + lax.broadcasted_iota(jnp.int32, (n_keys, LANES), 0)
        lane = lax.broadcasted_iota(jnp.int32, (n_keys, LANES), 1)
        hit = (lane == (kpos >> 6)) | ((lane == nsb) & ((kpos >= limit) | (kpos < 0)))
        return _bf(jnp.where(hit, 2.0 * NEG, 0.0))

    gs = range(NSA_KV)
    qg = [_bf(_heads_to_rows(q_all, g, qb)) for g in gs]

    n_prev = WINDOW // qb
    starts = [pl.multiple_of(jnp.maximum(t0 + (j - n_prev) * qb, 0), qb) for j in range(n_prev + 1)]
    kwin = [_bf(kwin_ref[0, pl.ds(st, qb), :]) for st in starts]
    pens = [jnp.where(i + (j - n_prev) >= 0, 0.0, NEG) for j in range(n_prev)] + [0.0]
    s_w = [jnp.concatenate([_dot_nt(qg[g], kwin[j]) + pens[j] for j in range(n_prev + 1)], axis=1) + bwin_ref[g]
           for g in gs]

    s_c = [_dot_nt(qg[g], kc) + bc_ref[0, g] for g in gs]
    s_t = [_dot_nt(kc, qg[g]) + bct_ref[0, g] for g in gs]
    e_c = [jnp.where(tq >= L_CMP - 1, jnp.exp(s - jnp.max(s, axis=1, keepdims=True)), 0.0) for s in s_c]
    p_c = [e * (1.0 / jnp.maximum(jnp.sum(e, axis=1, keepdims=True), 1e-30)) for e in e_c]
    o_c = [_dot(_bf(p), vc) for p in p_c]

    e_t = [jnp.where(tl >= L_CMP - 1, jnp.exp(s - jnp.max(s, axis=0, keepdims=True)), 0.0) for s in s_t]
    p_t = [e * (1.0 / jnp.maximum(jnp.sum(e, axis=0, keepdims=True), 1e-30)) for e in e_t]
    head_sum = lambda x: x[:, 0:qb] + x[:, qb:2 * qb] + x[:, 2 * qb:3 * qb] + x[:, 3 * qb:4 * qb]
    blk = lax.broadcasted_iota(jnp.int32, (nsb, qb), 0)
    cur = (t0 + lax.broadcasted_iota(jnp.int32, (nsb, qb), 1)) >> 6
    bonus = jnp.where((blk == 0) | (blk == cur) | (blk == cur - 1), FORCE_BONUS, 0.0)
    score = [jnp.where(blk <= cur, (head_sum(p[0:nsb]) + head_sum(p[nsb:2 * nsb])) + bonus, -1.0) for p in p_t]
    rank = [jnp.zeros((nsb, qb), F32) for _ in gs]
    for j in range(nsb):
        for g in gs:
            sj = score[g][j:j + 1, :]
            ahead = (sj > score[g]) | ((sj == score[g]) & (blk > j))
            rank[g] = rank[g] + jnp.where(ahead, 1.0, 0.0)
    pen_rows = jnp.where(lax.broadcasted_iota(jnp.int32, (LANES - nsb, qb), 0) == 0, 1.0, 0.0)
    not_sel_t = [jnp.where(r < float(min(N_SEL, nsb)), 0.0, 1.0) for r in rank]
    q_aug = [_bf(_dot_nt(eye_q, _bf(jnp.concatenate([ns, pen_rows], axis=0)))) for ns in not_sel_t]
    qa = [jnp.concatenate([qg[g], jnp.concatenate([q_aug[g]] * NSA_GQ, axis=0)], axis=1) for g in gs]

    def online(carry, s, pv):
        m_i, l_i, acc = carry
        m_n = jnp.maximum(m_i, jnp.max(s, axis=1, keepdims=True))
        p = jnp.exp(s - m_n)
        alpha = jnp.exp(m_i - m_n)
        return m_n, alpha * l_i + jnp.sum(p, axis=1, keepdims=True), alpha * acc + pv(_bf(p))

    def far_tile(kt, carry):
        k0 = pl.multiple_of(kt * FAR_TILE, FAR_TILE)
        ka = jnp.concatenate([_bf(kslc_ref[0, pl.ds(k0, FAR_TILE), :]), key_aug(k0, FAR_TILE, far_end)], axis=1)
        vt = _bf(vslc_ref[0, pl.ds(k0, FAR_TILE), :])
        s = [_dot_nt(qa[g], ka) for g in gs]
        return tuple(online(carry[g], s[g], lambda p: _dot(p, vt)) for g in gs)

    init = (jnp.full((rows, 1), NEG, F32), jnp.zeros((rows, 1), F32), jnp.zeros((rows, LANES), F32))
    far = lax.fori_loop(0, n_far, far_tile, tuple(init for _ in gs))

    p0 = pl.multiple_of(jnp.maximum(t0 - qb, 0), qb)
    d0 = pl.multiple_of(t0, qb)
    ka = jnp.concatenate([
        jnp.concatenate([_bf(kslc_ref[0, pl.ds(p0, qb), :]), _bf(kslc_ref[0, pl.ds(d0, qb), :])], axis=0),
        key_aug(t0 - qb, 2 * qb, t_len)], axis=1)
    vp = _bf(vslc_ref[0, pl.ds(p0, qb), :])
    vd = _bf(vslc_ref[0, pl.ds(d0, qb), :])
    s_near = [_dot_nt(qa[g], ka) + bnear_ref[g] for g in gs]
    fin = [online(far[g], s_near[g], lambda p: _dot(p[:, 0:qb], vp) + _dot(p[:, qb:2 * qb], vd)) for g in gs]
    o_s = [acc * (1.0 / l_n) for _, l_n, acc in fin]

    vwin = [_bf(vwin_ref[0, pl.ds(st, qb), :]) for st in starts]
    e_w = [jnp.exp(s - jnp.max(s, axis=1, keepdims=True)) for s in s_w]
    o_w = []
    for g in gs:
        ew = _bf(e_w[g])
        acc = _dot(ew[:, 0:qb], vwin[0])
        for j in range(1, n_prev + 1):
            acc = acc + _dot(ew[:, j * qb:(j + 1) * qb], vwin[j])
        o_w.append(acc * (1.0 / jnp.sum(e_w[g], axis=1, keepdims=True)))

    for g in gs:
        gates = [_heads_to_rows(gate_all[br], g, qb) for br in range(3)]
        comb = gates[0] * o_c[g] + gates[1] * o_s[g] + gates[2] * o_w[g]
        blocks = _rows_to_heads(comb, g, qb)
        o_ref[0, :, (2 * g) * LANES:(2 * g + 1) * LANES] = blocks[0]
        o_ref[0, :, (2 * g + 1) * LANES:(2 * g + 2) * LANES] = blocks[1]


def _nsa_tables(rel_bias, t_len, n_pages, nt):
    nqb = t_len // Q_BLOCK
    nsb = t_len // L_SEL
    past = n_pages * PAGE
    t = np.arange(Q_BLOCK)
    tq = (np.arange(nqb)[:, None] * Q_BLOCK + t[None, :])[:, :, None]
    n = np.arange(nsb)[None, None, :]
    ts = np.arange(nt)[:, None]
    j = np.arange(2 * n_pages)[None, :]
    c = np.arange(PAGE)[None, :]
    idx = [
        _bucket_np(tq - (n * L_SEL + L_CMP - 1)),
        _bucket_np(tq - (n * L_SEL + L_SEL - 1)),
        _bucket_np(Q_BLOCK + t[:, None] - np.arange(2 * Q_BLOCK)[None, :]),
        _bucket_np(WINDOW + t[:, None] - np.arange(WINDOW + Q_BLOCK)[None, :]),
        _bucket_np(past + ts - (j * L_SEL + L_CMP - 1)),
        _bucket_np(past + ts - (j * L_SEL + L_SEL - 1)),
        _bucket_np(PAGE + ts - c),
        _bucket_np(ts - c),
        _bucket_np(WINDOW + ts - np.arange(WINDOW)[None, :]),
    ]
    ce, co, near, win, sce, sco, slast, snew, swin = _bias_lookup(rel_bias, idx)
    b31 = rel_bias.astype(F32)[N_BUCKETS - 1].reshape(NSA_KV, NSA_GQ, 1, 1)
    shift = lambda tab: (tab.reshape(NSA_KV, NSA_GQ, tab.shape[1], tab.shape[2]) - b31).reshape(
        NSA_KV, NSA_GQ * tab.shape[1], tab.shape[2])
    vis = lambda m: jnp.asarray(np.tile(m, (1,) * (m.ndim - 2) + (NSA_GQ, 1)))
    blocked = lambda tab: jnp.swapaxes(tab, 0, 1).reshape(nqb, NSA_KV, NSA_GQ * Q_BLOCK, nsb)
    bc = jnp.concatenate([blocked(ce), blocked(co)], axis=-1)
    vis_c = np.concatenate([n * L_SEL + L_CMP - 1 <= tq, n * L_SEL + L_SEL - 1 <= tq], axis=-1)
    bc = jnp.where(vis(vis_c)[:, None], bc, NEG)
    c_near = np.arange(2 * Q_BLOCK)[None, :]
    near_m = jnp.where(vis(c_near <= Q_BLOCK + t[:, None])[None], shift(near), NEG)
    c_win = np.arange(WINDOW + Q_BLOCK)[None, :]
    win_m = jnp.where(vis((c_win > t[:, None]) & (c_win <= WINDOW + t[:, None]))[None], _head_rows(win), NEG)
    ptab = (bc, jnp.swapaxes(bc, -1, -2), near_m, win_m)
    rows64 = lambda tab: tab.reshape(NSA_KV * NSA_GQ * nt, tab.shape[-1])
    stab = (rows64(_head_rows(sce)), rows64(_head_rows(sco)), rows64(shift(slast)), rows64(shift(snew)),
            rows64(_head_rows(swin)), rows64(_head_rows(snew)))
    return ptab, stab


def _gate_rep():
    j = np.arange(LANES)[None, :, None]
    c = np.arange(NSA_HEADS * NSA_DH)[None, None, :]
    br = np.arange(3)[:, None, None]
    return jnp.asarray(j == br * NSA_HEADS + c // NSA_DH, BF16)


def _nsa_prompt(u, w_pos, tables):
    bsz, t_len, _ = u.shape
    nqb = t_len // Q_BLOCK
    nsb = t_len // L_SEL
    assert nsb < LANES
    bc, bct, near, wtab = tables
    wk = jnp.tile(w_pos[0], t_len // L_CMP).reshape(1, t_len)
    wv = jnp.tile(w_pos[1], t_len // L_CMP).reshape(1, t_len)
    rep = _gate_rep()
    seq = lambda c: pl.BlockSpec((1, t_len, LANES), lambda b, i, c=c: (b, 0, c))
    full = lambda a: pl.BlockSpec(a.shape, lambda b, i, nd=a.ndim: (0,) * nd)
    per_i = lambda a: pl.BlockSpec((1,) + a.shape[1:], lambda b, i, nd=a.ndim: (i,) + (0,) * (nd - 1))
    kern = functools.partial(_nsa_prompt_kernel, t_len)
    kv0 = U_ROWS // LANES
    w0 = U_WIN // LANES
    return pl.pallas_call(
        kern,
        grid=(bsz, nqb),
        in_specs=[pl.BlockSpec((1, Q_BLOCK, 512), lambda b, i: (b, i, U_Q // 512)),
                  seq(kv0), seq(kv0 + 1), seq(kv0 + 2), seq(kv0 + 3), seq(w0), seq(w0 + 1),
                  pl.BlockSpec((1, Q_BLOCK, LANES), lambda b, i: (b, i, U_GL // LANES)), full(rep),
                  full(wk), full(wv), per_i(bc), per_i(bct), full(near), full(wtab)],
        out_specs=pl.BlockSpec((1, Q_BLOCK, 512), lambda b, i: (b, i, 0)),
        out_shape=jax.ShapeDtypeStruct((bsz, t_len, 512), F32),
        scratch_shapes=[pltpu.VMEM((2 * nsb, LANES), F32)] * 2,
        compiler_params=_params("parallel", "arbitrary"),
        name="nsa_prompt",
    )(u, u, u, u, u, u, u, u, rep, wk, wv, bc, bct, near, wtab)


def _nsa_sample_kernel(layer, n_pages, n_new, pt_ref, cache_ref, q_ref, rows_ref, wnew_ref, wbuf_ref, gl_ref,
                       rep_ref, wpool_ref, bce_ref, bco_ref, blast_ref, bnew_ref, bwin_ref, bwnew_ref,
                       o_ref, cmp_s, slc_s, pool_s, exp_s, pad_s, sem):
    b = pl.program_id(0)
    nb = pl.num_programs(0)
    nt = 8
    past = n_pages * PAGE
    nblk = 2 * n_pages
    rows = NSA_KV * NSA_GQ * nt
    half = 2 * LANES

    def page_copy(seq, p, part, buf, s):
        return pltpu.make_async_copy(
            cache_ref.at[layer, pt_ref[seq, p], pl.ds(part * half, half), :],
            buf.at[:, pl.ds(pl.multiple_of(p * PAGE, PAGE), PAGE)], s)

    def start_gather(seq, part, buf, s):
        def body(p, c):
            page_copy(seq, p, part, buf, s).start()
            return c
        lax.fori_loop(0, n_pages, body, 0)

    def wait_gather(seq, part, buf, s):
        def body(p, c):
            page_copy(seq, p, part, buf, s).wait()
            return c
        lax.fori_loop(0, n_pages, body, 0)

    @pl.when(b == 0)
    def _():
        start_gather(0, 0, cmp_s, sem.at[0])
        start_gather(0, 1, slc_s.at[0], sem.at[1])
        cb = lax.broadcasted_iota(jnp.int32, (past, nblk), 0) >> 5
        j2 = 2 * lax.broadcasted_iota(jnp.int32, (past, nblk), 1)
        pool_s[0] = _bf(jnp.where(cb == j2, 1.0, 0.0))
        pool_s[1] = _bf(jnp.where(cb == j2 + 1, 1.0, 0.0))
        ej = lax.broadcasted_iota(jnp.int32, (nblk, past), 0)
        ec = lax.broadcasted_iota(jnp.int32, (nblk, past), 1) >> 6
        exp_s[...] = _bf(jnp.where(ej == ec, 1.0, 0.0))
        pad_s[...] = jnp.zeros(pad_s.shape, F32)

    slot = b % 2
    slc = slc_s.at[slot]

    @pl.when(b + 1 < nb)
    def _():
        start_gather(b + 1, 1, slc_s.at[1 - slot], sem.at[2 - slot])

    wait_gather(b, 0, cmp_s, sem.at[0])
    wait_gather(b, 1, slc, sem.at[1 + slot])

    q_all = q_ref[0] * (NSA_DH ** -0.5)
    qq = _bf(jnp.concatenate([_heads_to_rows(q_all, g, nt) for g in range(NSA_KV)], axis=0))
    tr = lax.broadcasted_iota(jnp.int32, (rows, 1), 0) & (nt - 1)

    wp = wpool_ref[...]
    ks = _bf(cmp_s[0:LANES, :] * wp[0:1, :])
    vs = _bf(cmp_s[LANES:half, :] * wp[1:2, :])
    kce = _bf(_dot(ks, pool_s[0]))
    kco = _bf(_dot(ks, pool_s[1]))
    vce = _bf(_dot(vs, pool_s[0]))
    vco = _bf(_dot(vs, pool_s[1]))

    @pl.when(b + 1 < nb)
    def _():
        start_gather(b + 1, 0, cmp_s, sem.at[0])

    new = rows_ref[0]
    wnew = wnew_ref[0]
    pad_s[0, 0:nt, :] = new[:, 2 * LANES:3 * LANES]
    pad_s[1, 0:nt, :] = new[:, 3 * LANES:4 * LANES]
    pad_s[2, 0:nt, :] = wnew[:, 0:LANES]
    pad_s[3, 0:nt, :] = wnew[:, LANES:2 * LANES]

    wb = wbuf_ref[0, 0]
    s_all = _dot(qq, _bf(slc[0:LANES, :]))
    s_new = _dot_nt(qq, _bf(pad_s[0])) + bnew_ref[...]
    s_win = _dot(qq, _bf(wb[0:LANES, :])) + bwin_ref[...]
    s_wnew = _dot_nt(qq, _bf(pad_s[2])) + bwnew_ref[...]

    se = _dot(qq, kce) + bce_ref[...]
    so = _dot(qq, kco) + bco_ref[...]
    mx = jnp.maximum(jnp.max(se, axis=1, keepdims=True), jnp.max(so, axis=1, keepdims=True))
    ee = jnp.exp(se - mx)
    eo = jnp.exp(so - mx)
    inv = 1.0 / (jnp.sum(ee, axis=1, keepdims=True) + jnp.sum(eo, axis=1, keepdims=True))
    pe = ee * inv
    po = eo * inv
    o_c = _dot_nt(_bf(pe), vce) + _dot_nt(_bf(po), vco)

    def head_sum(pr):
        return jnp.concatenate(
            [pr[g * 4 * nt:g * 4 * nt + nt] + pr[g * 4 * nt + nt:g * 4 * nt + 2 * nt]
             + pr[g * 4 * nt + 2 * nt:g * 4 * nt + 3 * nt] + pr[g * 4 * nt + 3 * nt:g * 4 * nt + 4 * nt]
             for g in range(NSA_KV)], axis=0)

    jcol = lax.broadcasted_iota(jnp.int32, (NSA_KV * nt, nblk), 1)
    forced = (jcol == 0) | (jcol == nblk - 1)
    score = (head_sum(pe) + head_sum(po)) + jnp.where(forced, FORCE_BONUS, 0.0)
    rank = jnp.where(FORCE_BONUS > score, 1.0, 0.0)
    for j in range(nblk):
        sj = score[:, j:j + 1]
        ahead = (sj > score) | ((sj == score) & (jcol > j))
        rank = rank + jnp.where(ahead, 1.0, 0.0)
    sel = jnp.where(rank < float(N_SEL), 1.0, 0.0)
    sel_rows = jnp.concatenate([sel[g * nt:(g + 1) * nt] for g in range(NSA_KV) for _ in range(NSA_GQ)], axis=0)

    tc = lax.broadcasted_iota(jnp.int32, (rows, LANES), 1)
    mnew = (tc <= tr) & (tc < n_new)

    mk = _dot(_bf(sel_rows), exp_s[...]) > 0.5
    far = past - PAGE
    p_far, p_last, p_new = _masked_softmax_parts(
        [s_all[:, :far], s_all[:, far:] + blast_ref[...], s_new], [mk[:, :far], mk[:, far:], mnew], 1)
    o_s = _dot_nt(_bf(jnp.concatenate([p_far, p_last], axis=1)), _bf(slc[LANES:half, :])) \
        + _dot(_bf(p_new), _bf(pad_s[1]))

    cw = lax.broadcasted_iota(jnp.int32, (rows, WINDOW), 1)
    pw, pn = _masked_softmax_parts([s_win, s_wnew], [cw > tr, mnew], 1)
    o_w = _dot_nt(_bf(pw), _bf(wb[LANES:half, :])) + _dot(_bf(pn), _bf(pad_s[3]))

    gl = gl_ref[0]
    gate_all = [_sigmoid(_pick_columns(gl, rep_ref[br])) for br in range(3)]
    gates = [jnp.concatenate([_heads_to_rows(ga, g, nt) for g in range(NSA_KV)], axis=0) for ga in gate_all]
    comb = gates[0] * o_c + gates[1] * o_s + gates[2] * o_w
    for g in range(NSA_KV):
        blocks = _rows_to_heads(comb[g * 4 * nt:(g + 1) * 4 * nt], g, nt)
        o_ref[0, :, (2 * g) * LANES:(2 * g + 1) * LANES] = blocks[0]
        o_ref[0, :, (2 * g + 1) * LANES:(2 * g + 2) * LANES] = blocks[1]


def _nsa_sample(layer, cache_t, page_table, u, wbuf_t, w_pos, tables, n_new):
    bsz, n_pages = page_table.shape
    nt = u.shape[1]
    past = n_pages * PAGE
    wpool = jnp.tile(w_pos, (1, past // L_CMP))
    rep = _gate_rep()
    full = lambda a: pl.BlockSpec(a.shape, lambda b, pt, nd=a.ndim: (0,) * nd)
    ucols = lambda width, off: pl.BlockSpec((1, nt, width), lambda b, pt: (b, 0, off // width))
    kern = functools.partial(_nsa_sample_kernel, layer, n_pages, n_new)
    grid_spec = pltpu.PrefetchScalarGridSpec(
        num_scalar_prefetch=1,
        grid=(bsz,),
        in_specs=[pl.BlockSpec(memory_space=pl.ANY),
                  ucols(512, U_Q), ucols(512, U_ROWS), ucols(2 * LANES, U_WIN),
                  pl.BlockSpec((1, 1) + wbuf_t.shape[2:], lambda b, pt: (layer, b, 0, 0)),
                  ucols(LANES, U_GL), full(rep), full(wpool)] + [full(t) for t in tables],
        out_specs=pl.BlockSpec((1, nt, 512), lambda b, pt: (b, 0, 0)),
        scratch_shapes=[pltpu.VMEM((2 * LANES, past), F32), pltpu.VMEM((2, 2 * LANES, past), F32),
                        pltpu.VMEM((2, past, 2 * n_pages), BF16), pltpu.VMEM((2 * n_pages, past), BF16),
                        pltpu.VMEM((4, LANES, LANES), F32), pltpu.SemaphoreType.DMA((3,))],
    )
    return pl.pallas_call(
        kern,
        grid_spec=grid_spec,
        out_shape=jax.ShapeDtypeStruct((bsz, nt, 512), F32),
        compiler_params=pltpu.CompilerParams(dimension_semantics=("arbitrary",), vmem_limit_bytes=BIG_VMEM_LIMIT),
        name="nsa_sample",
    )(page_table, cache_t, u, u, u, wbuf_t, u, rep, wpool, *tables)


def _mixout_kernel(x_ref, g_ref, ca_ref, ob_ref, oc_ref, wg_ref, wpa_ref, wpb_ref, wpc_ref, wo_ref, o_ref):
    x = x_ref[...]
    h = _bf(x * lax.rsqrt(jnp.mean(x * x, axis=-1, keepdims=True) + EPS) * g_ref[...])
    gate = lambda k: _sigmoid(_dot(h, wg_ref[:, k * D_MODEL:(k + 1) * D_MODEL]))
    y = gate(0) * _dot(_bf(ca_ref[...]), wpa_ref[...])
    y = y + gate(1) * _dot(_bf(ob_ref[...]), wpb_ref[...])
    y = y + gate(2) * _dot(_bf(oc_ref[...]), wpc_ref[...])
    o_ref[...] = x + _dot(_bf(y), wo_ref[...])


def _mixout(x, g, ca, ob, oc, wg, wpa, wpb, wpc, wo):
    m = x.shape[0]
    tm = min(m, 512)
    rowblk = lambda n: pl.BlockSpec((tm, n), lambda i: (i, 0))
    full = lambda a: pl.BlockSpec(a.shape, lambda i: (0, 0))
    return pl.pallas_call(
        _mixout_kernel,
        grid=(m // tm,),
        in_specs=[rowblk(D_MODEL), pl.BlockSpec((1, D_MODEL), lambda i: (0, 0)),
                  rowblk(512), rowblk(512), rowblk(512), full(wg), full(wpa), full(wpb), full(wpc), full(wo)],
        out_specs=rowblk(D_MODEL),
        out_shape=jax.ShapeDtypeStruct((m, D_MODEL), F32),
        compiler_params=_params("parallel"),
        name="mixer_out",
    )(x, g.reshape(1, D_MODEL), ca, ob, oc, wg, wpa, wpb, wpc, wo)


def _xattn_kernel(x_ref, g_ref, kv_ref, wq_ref, wo_ref, o_ref):
    nb, tt, d = x_ref.shape
    x = x_ref[...].reshape(nb * tt, d)
    h = _bf(x * lax.rsqrt(jnp.mean(x * x, axis=-1, keepdims=True) + EPS) * g_ref[...])
    q = _dot(h, wq_ref[...])

    def mem_head(sq, which, hd):
        return jnp.concatenate(
            [kv_ref[0, sq, pl.ds(which * 8 + half * X_HEADS + hd, N_MEM, stride=MEM_ROWS), :]
             for half in range(2)], axis=1)

    seqs = []
    for sq in range(nb):
        outs = []
        for hd in range(X_HEADS):
            qh = _bf(q[sq * tt:(sq + 1) * tt, hd * X_DH:(hd + 1) * X_DH])
            kh = _bf(mem_head(sq, 0, hd))
            vh = _bf(mem_head(sq, 1, hd))
            s = _dot_nt(qh, kh) * (X_DH ** -0.5)
            e = jnp.exp(s - jnp.max(s, axis=-1, keepdims=True))
            pr = e * (1.0 / jnp.sum(e, axis=-1, keepdims=True))
            outs.append(_dot(_bf(pr), vh))
        seqs.append(jnp.concatenate(outs, axis=1))
    o = seqs[0] if nb == 1 else jnp.concatenate(seqs, axis=0)
    o_ref[...] = (x + _dot(_bf(o), wo_ref[...])).reshape(nb, tt, d)


def _cross_attn(x, g, mem_kv, layer, wq, wo):
    bsz, t_len, d = x.shape
    tt = min(t_len, 512)
    nb = 4 if (t_len <= 8 and bsz % 4 == 0) else 1
    full = lambda a: pl.BlockSpec(a.shape, lambda b, t: (0, 0))
    return pl.pallas_call(
        _xattn_kernel,
        grid=(bsz // nb, t_len // tt),
        in_specs=[pl.BlockSpec((nb, tt, d), lambda b, t: (b, t, 0)),
                  pl.BlockSpec((1, d), lambda b, t: (0, 0)),
                  pl.BlockSpec((1, nb) + mem_kv.shape[2:], lambda b, t: (layer, b, 0, 0)),
                  full(wq), full(wo)],
        out_specs=pl.BlockSpec((nb, tt, d), lambda b, t: (b, t, 0)),
        out_shape=jax.ShapeDtypeStruct((bsz, t_len, d), F32),
        compiler_params=_params("parallel", "parallel"),
        name="cross_attn",
    )(x, g.reshape(1, d), mem_kv, wq, wo)


FF_CHUNK = 1024


def _mlp_kernel(n_k, final_norm, x_ref, g_ref, gf_ref, w1_ref, w2_ref, o_ref, h_s, acc_s):
    k = pl.program_id(1)

    @pl.when(k == 0)
    def _():
        x = x_ref[...]
        h_s[...] = _bf(x * lax.rsqrt(jnp.mean(x * x, axis=-1, keepdims=True) + EPS) * g_ref[...])
        acc_s[...] = x

    a = jnp.maximum(_dot(h_s[...], w1_ref[...]), 0.0)
    acc_s[...] += _dot(_bf(a * a), w2_ref[...])

    @pl.when(k == n_k - 1)
    def _():
        y = acc_s[...]
        if final_norm:
            y = y * lax.rsqrt(jnp.mean(y * y, axis=-1, keepdims=True) + EPS) * gf_ref[...]
        o_ref[...] = y


def _mlp(x, g, w1, w2, final_g=None):
    m, d = x.shape
    tm = min(m, 1024)
    n_k = D_FF // FF_CHUNK
    gf = g if final_g is None else final_g
    return pl.pallas_call(
        functools.partial(_mlp_kernel, n_k, final_g is not None),
        grid=(m // tm, n_k),
        in_specs=[pl.BlockSpec((tm, d), lambda i, k: (i, 0)),
                  pl.BlockSpec((1, d), lambda i, k: (0, 0)),
                  pl.BlockSpec((1, d), lambda i, k: (0, 0)),
                  pl.BlockSpec((d, FF_CHUNK), lambda i, k: (0, k)),
                  pl.BlockSpec((FF_CHUNK, d), lambda i, k: (k, 0))],
        out_specs=pl.BlockSpec((tm, d), lambda i, k: (i, 0)),
        out_shape=jax.ShapeDtypeStruct((m, d), F32),
        scratch_shapes=[pltpu.VMEM((tm, d), BF16), pltpu.VMEM((tm, d), F32)],
        compiler_params=_params("parallel", "arbitrary"),
        name="sq_relu_mlp",
    )(x, g.reshape(1, d), gf.reshape(1, d), w1, w2)


A_COLS = 2 * CONV_CH
B0 = A_COLS
Z0 = B0 + GDN_QKV
AB0 = Z0 + GDN_HEADS * GDN_D
C0 = AB0 + 2 * GDN_HEADS
KV0 = C0 + NSA_HEADS * NSA_DH
GL0 = KV0 + 6 * NSA_KV * NSA_DH
G0 = GL0 + 3 * NSA_HEADS
N_IN = G0 + 3 * D_MODEL


def _layer_weights(l, w_in, w_pa, w_pb, w_pc, w_o, w_xq, w_xk, w_xv, w_xo, w_ff1, w_ff2):
    w = w_in[l]
    lane_pad = lambda cols: jnp.pad(cols, ((0, 0), (0, LANES - cols.shape[1])))
    groups = [(U_A, w[:, 0:A_COLS]), (U_Q, w[:, C0:KV0]), (U_ROWS, w[:, KV0:KV0 + 4 * LANES]),
              (U_QKV, w[:, B0:Z0]), (U_Z, w[:, Z0:AB0]),
              (U_WIN, w[:, KV0 + 4 * LANES:GL0]), (U_AB, lane_pad(w[:, AB0:C0])), (U_GL, lane_pad(w[:, GL0:G0]))]
    off = 0
    for start, cols in groups:
        assert start == off
        off += cols.shape[1]
    assert off == U_N
    return {
        "in": _bf(jnp.concatenate([cols for _, cols in groups], axis=1)),
        "g": _bf(w[:, G0:N_IN]),
        "pa": _bf(w_pa[l]), "pb": _bf(w_pb[l]), "pc": _bf(w_pc[l]), "o": _bf(w_o[l]),
        "xq": _bf(w_xq[l]), "xo": _bf(w_xo[l]),
        "xkv": _bf(jnp.concatenate([w_xk[l], w_xv[l]], axis=1)),
        "ff1": _bf(w_ff1[l]), "ff2": _bf(w_ff2[l]),
    }


def _mixers(x, lw, p, l, conv_state_pad, qkv_state_pad, s0, n_valid, gdn_len, nsa_fn):
    bsz, t_len, d = x.shape
    m = bsz * t_len
    x2 = x.reshape(m, d)
    u2, rows, win = _proj_in(x2, p["norm_mix"][l], lw["in"])
    u = u2.reshape(bsz, t_len, U_N)
    rows = rows.reshape(bsz, t_len, 4 * LANES)
    win = win.reshape(bsz, t_len, 2 * LANES)
    qkv_tail = u[:, max(n_valid - 3, 0):n_valid, U_QKV:U_QKV + GDN_QKV]

    ca, conv_new = _conformer(u, conv_state_pad, p["conv_a_w"][l], p["conv_a_b"][l], p["ln_a_g"][l],
                              p["ln_a_b"][l], n_valid if n_valid < t_len else min(t_len, CONV_TILE))
    w_conv_pad = jnp.pad(p["gdn_conv_w"][l], ((0, 4), (0, 0)))
    gdn_args = (qkv_state_pad, s0, w_conv_pad, p["gdn_a_log"][l], p["gdn_dt_bias"][l], p["gdn_norm_g"][l])
    if gdn_len == t_len:
        ob, s_new = _gated_deltanet(u, U_QKV // LANES, U_Z // LANES, U_AB // LANES, *gdn_args, gdn_len)
    else:
        ug = jnp.concatenate([u[:, :, U_QKV:U_WIN], u[:, :, U_AB:U_AB + LANES]], axis=-1)
        ug = jnp.pad(ug, ((0, 0), (0, gdn_len - t_len), (0, 0)))
        ob, s_new = _gated_deltanet(ug, 0, (U_Z - U_QKV) // LANES, (U_WIN - U_QKV) // LANES, *gdn_args, n_valid)
        ob = ob[:, :t_len]
    oc = nsa_fn(u)
    x_new = _mixout(x2, p["norm_mix"][l], ca.reshape(m, -1), ob.reshape(m, -1), oc.reshape(m, -1),
                    lw["g"], lw["pa"], lw["pb"], lw["pc"], lw["o"])
    return x_new.reshape(bsz, t_len, d), conv_new[:, HALO - (CONV_W - 1):], qkv_tail, s_new, rows, win


def kernel(x_prompt, x_sample, cache_nsa_kv, cache_win_kv, state_conv_a, state_conv_qkv, state_gdn, cache_mem_kv,
           page_table, mem_prompt, rel_bias, norm_mix, w_in, conv_a_w, conv_a_b, ln_a_g, ln_a_b, w_pa, gdn_conv_w,
           gdn_a_log, gdn_dt_bias, gdn_norm_g, w_pb, nsa_cmp_w, w_pc, w_o, norm_x, w_xq, w_xk, w_xv, w_xo,
           norm_mlp, w_ff1, w_ff2, norm_final):
    p = {"norm_mix": norm_mix, "conv_a_w": conv_a_w, "conv_a_b": conv_a_b, "ln_a_g": ln_a_g, "ln_a_b": ln_a_b,
         "gdn_conv_w": gdn_conv_w, "gdn_a_log": gdn_a_log, "gdn_dt_bias": gdn_dt_bias, "gdn_norm_g": gdn_norm_g}
    depth = w_in.shape[0]
    bp, tp, d = x_prompt.shape
    bs, ts, _ = x_sample.shape
    ts_pad = 8
    n_pages = page_table.shape[1]
    wb = cache_win_kv.shape[2]
    xp = x_prompt
    xs = jnp.pad(x_sample, ((0, 0), (0, ts_pad - ts), (0, 0)))
    ptab, stab = _nsa_tables(rel_bias, tp, n_pages, ts_pad)
    cache_t = jnp.transpose(cache_nsa_kv, (0, 1, 3, 4, 5, 2)).reshape(depth, -1, 4 * LANES, PAGE)
    wbuf_t = jnp.transpose(cache_win_kv, (0, 1, 3, 4, 5, 2)).reshape(depth, bs, 2 * LANES, wb)
    mem_t = cache_mem_kv.reshape(depth, bs, N_MEM, 2, X_HEADS, 2, LANES)
    mem_t = jnp.transpose(mem_t, (0, 1, 2, 3, 5, 4, 6)).reshape(depth, bs, N_MEM * MEM_ROWS, LANES)
    outs = {k: [] for k in ("p_rows", "p_win", "p_conv", "p_qkv", "p_gdn", "p_mem",
                            "s_rows", "s_win", "s_conv", "s_qkv", "s_gdn")}
    for l in range(depth):
        lw = _layer_weights(l, w_in, w_pa, w_pb, w_pc, w_o, w_xq, w_xk, w_xv, w_xo, w_ff1, w_ff2)
        nsa_p = lambda u: _nsa_prompt(u, nsa_cmp_w[l], ptab)
        xp, conv_n, qkv_tail, s_n, rows, win = _mixers(
            xp, lw, p, l, jnp.zeros((bp, HALO, CONV_CH), F32), jnp.zeros((bp, 8, GDN_QKV), F32),
            jnp.zeros((bp, GDN_HEADS, GDN_D, GDN_D), F32), tp, tp, nsa_p)
        mem_kv = _mem_kv(_bf(mem_prompt.reshape(bp * N_MEM, d)), lw["xkv"]).reshape(1, bp, N_MEM * MEM_ROWS, LANES)
        xp = _cross_attn(xp, norm_x[l], mem_kv, 0, lw["xq"], lw["xo"])
        final_g = norm_final if l == depth - 1 else None
        xp = _mlp(xp.reshape(bp * tp, d), norm_mlp[l], lw["ff1"], lw["ff2"], final_g).reshape(bp, tp, d)
        outs["p_rows"].append(rows.reshape(bp, tp, 4, NSA_KV, NSA_DH))
        outs["p_win"].append(win[:, tp - min(WINDOW, tp):].reshape(bp, min(WINDOW, tp), 2, NSA_KV, NSA_DH))
        outs["p_conv"].append(conv_n)
        outs["p_qkv"].append(qkv_tail)
        outs["p_gdn"].append(s_n)
        mem_out = mem_kv.reshape(bp, N_MEM, 2, 2, X_HEADS, LANES)
        outs["p_mem"].append(jnp.swapaxes(mem_out, 3, 4).reshape(bp, N_MEM, 2, X_HEADS, X_DH))
        nsa_s = lambda u: _nsa_sample(l, cache_t, page_table, u, wbuf_t, nsa_cmp_w[l], stab, ts)
        conv_pad = jnp.pad(state_conv_a[l], ((0, 0), (HALO - (CONV_W - 1), 0), (0, 0)))
        qkv_pad = jnp.pad(state_conv_qkv[l], ((0, 0), (5, 0), (0, 0)))
        xs, conv_n, qkv_tail, s_n, rows, win = _mixers(
            xs, lw, p, l, conv_pad, qkv_pad, state_gdn[l], ts, GDN_CHUNK, nsa_s)
        xs = _cross_attn(xs, norm_x[l], mem_t, l, lw["xq"], lw["xo"])
        xs = _mlp(xs.reshape(bs * ts_pad, d), norm_mlp[l], lw["ff1"], lw["ff2"], final_g).reshape(bs, ts_pad, d)
        outs["s_rows"].append(rows[:, :ts].reshape(bs, ts, 4, NSA_KV, NSA_DH))
        win_new = win[:, :ts].reshape(bs, ts, 2, NSA_KV, NSA_DH)
        outs["s_win"].append(jnp.concatenate([cache_win_kv[l], win_new], axis=1)[:, ts:])
        outs["s_conv"].append(conv_n)
        outs["s_qkv"].append(qkv_tail)
        outs["s_gdn"].append(s_n)
    st = lambda k: jnp.stack(outs[k], axis=0)
    return (xp, xs[:, :ts], st("p_rows"), st("p_win"), st("p_conv"), st("p_qkv"), st("p_gdn"), st("p_mem"),
            st("s_rows"), st("s_win"), st("s_conv"), st("s_qkv"), st("s_gdn"))
```
